```python
import math, functools
import jax, jax.numpy as jnp
from jax import lax
import numpy as np

D_MODEL = 1024
BATCH = 8
SEQ = 2048
DEPTH = 2
DEC_BATCH = 128
DEC_SEQ = 8
PAST_LEN = 8192
PAGE_SIZE = 128

N_AB_LAYERS = (DEPTH + 1) // 2
N_C_LAYERS = DEPTH // 2
H_A = 4
DK_A = 128
DV_A = 128
CONV_W = 4
QKV_A = H_A * (2 * DK_A + DV_A)
H_B = 8
D_NOPE = 64
D_ROPE = 32
DV_B = 64
Q_LORA = 256
KV_LORA = 256
ROPE_BASE = 10000.0
MLA_SCALE = (D_NOPE + D_ROPE) ** -0.5
H_C = 8
DK_C = 128
DV_C = D_MODEL // H_C
D_FF = 4 * D_MODEL
CHUNK = 64
Q_BLOCK = 128
EPS = 1e-6
SPLIT_AB = (QKV_A,
            QKV_A + H_A * DV_A,
            QKV_A + H_A * DV_A + H_A,
            QKV_A + H_A * DV_A + 2 * H_A,
            QKV_A + H_A * DV_A + 2 * H_A + Q_LORA,
            QKV_A + H_A * DV_A + 2 * H_A + Q_LORA + KV_LORA)
IN_AB = SPLIT_AB[-1] + D_ROPE
OUT_AB = H_A * DV_A + H_B * DV_B
IN_C = H_C * (2 * DK_C + 2 * DV_C)
SPLIT_C = (H_C * DK_C, 2 * H_C * DK_C, 2 * H_C * DK_C + H_C * DV_C)

kernel_name = 'hybrid_gdn_mla_hgrn2_step'


def rmsnorm(x, g):
    xf = x.astype(jnp.float32)
    y = xf * lax.rsqrt(jnp.mean(xf * xf, axis=-1, keepdims=True) + EPS)
    return (y * g.astype(jnp.float32)).astype(x.dtype)


def l2norm(x):
    xf = x.astype(jnp.float32)
    return xf * lax.rsqrt(jnp.sum(xf * xf, axis=-1, keepdims=True) + EPS)


def rope(x, pos):
    half = D_ROPE // 2
    inv = ROPE_BASE ** (-jnp.arange(half, dtype=jnp.float32) / half)
    ang = pos.astype(jnp.float32)[:, None] * inv
    shape = (ang.shape[0],) + (1,) * (x.ndim - 3) + (half,)
    cos, sin = jnp.cos(ang).reshape(shape), jnp.sin(ang).reshape(shape)
    x1, x2 = x[..., :half].astype(jnp.float32), x[..., half:].astype(jnp.float32)
    return jnp.concatenate([x1 * cos - x2 * sin, x2 * cos + x1 * sin], axis=-1).astype(x.dtype)


def to_chunks(a, c):
    b, t = a.shape[:2]
    a = a.astype(jnp.float32).reshape((b, t // c, c) + a.shape[2:])
    return jnp.swapaxes(jnp.moveaxis(a, 1, 0), 2, 3)


def from_chunks(o):
    n, b, h, c, d = o.shape
    return jnp.moveaxis(jnp.swapaxes(o, 2, 3), 0, 1).reshape(b, n * c, h, d)


def gated_delta_rule(q, k, v, g, beta, s0):
    c = math.gcd(q.shape[1], CHUNK)
    q = to_chunks(q, c) * (DK_A ** -0.5)
    k, v, g, beta = (to_chunks(a, c) for a in (k, v, g, beta))
    gc = jnp.cumsum(g, axis=-1)
    incl = jnp.tril(jnp.ones((c, c), dtype=bool))
    strict = jnp.tril(jnp.ones((c, c), dtype=bool), -1)
    diff = gc[..., :, None] - gc[..., None, :]
    decay = jnp.where(incl, jnp.exp(jnp.where(incl, diff, 0.0)), 0.0)
    kb = k * beta[..., None]
    lower = jnp.where(strict, jnp.einsum('nbhid,nbhjd->nbhij', kb, k) * decay, 0.0)
    eye = jnp.eye(c, dtype=jnp.float32)
    t_inv = lax.linalg.triangular_solve(eye + lower, jnp.broadcast_to(eye, lower.shape),
                                        left_side=True, lower=True, unit_diagonal=True)
    u = jnp.einsum('nbhij,nbhje->nbhie', t_inv, v * beta[..., None])
    w = jnp.einsum('nbhij,nbhjd->nbhid', t_inv, kb * jnp.exp(gc)[..., None])

    def step(s, xs):
        qc, kc, uc, wc, gcc, dc = xs
        v_new = uc - jnp.einsum('bhid,bhde->bhie', wc, s)
        intra = jnp.einsum('bhid,bhjd->bhij', qc, kc) * dc
        o = (jnp.einsum('bhid,bhde->bhie', qc * jnp.exp(gcc)[..., None], s)
             + jnp.einsum('bhij,bhje->bhie', intra, v_new))
        g_last = gcc[..., -1]
        s = (s * jnp.exp(g_last)[..., None, None]
             + jnp.einsum('bhid,bhie->bhde', kc * jnp.exp(g_last[..., None] - gcc)[..., None], v_new))
        return s, o

    s, o = lax.scan(step, s0.astype(jnp.float32), (q, k, u, w, gc, decay))
    return from_chunks(o), s


def gated_linear_recurrence(q, k, v, logf, s0):
    c = math.gcd(q.shape[1], CHUNK)
    q = to_chunks(q, c) * (DK_C ** -0.5)
    k, v, logf = (to_chunks(a, c) for a in (k, v, logf))
    gc = jnp.cumsum(logf, axis=-2)
    incl = jnp.tril(jnp.ones((c, c), dtype=bool))[:, :, None]

    def step(s, xs):
        qc, kc, vc, gcc = xs
        diff = gcc[:, :, :, None, :] - gcc[:, :, None, :, :]
        dec = jnp.where(incl, jnp.exp(jnp.where(incl, diff, 0.0)), 0.0)
        intra = jnp.einsum('bhid,bhjd,bhijd->bhij', qc, kc, dec)
        o = (jnp.einsum('bhid,bhde->bhie', qc * jnp.exp(gcc), s)
             + jnp.einsum('bhij,bhje->bhie', intra, vc))
        g_last = gcc[:, :, -1]
        s = (s * jnp.exp(g_last)[..., None]
             + jnp.einsum('bhid,bhie->bhde', kc * jnp.exp(g_last[:, :, None] - gcc), vc))
        return s, o

    s, o = lax.scan(step, s0.astype(jnp.float32), (q, k, v, gc))
    return from_chunks(o), s


def mla_prompt(q_nope, q_pe, c_kv, k_pe, w_uk, w_uv):
    b, s, h, _ = q_nope.shape
    qb = math.gcd(s, Q_BLOCK)
    k_nope = jnp.einsum('bsl,lhd->bshd', c_kv, w_uk)
    v = jnp.einsum('bsl,lhd->bshd', c_kv, w_uv)
    kpos = jnp.arange(s)

    def block(args):
        i, qn, qp = args
        sc = (jnp.einsum('bqhd,bkhd->bhqk', qn, k_nope)
              + jnp.einsum('bqhd,bkd->bhqk', qp, k_pe)).astype(jnp.float32) * MLA_SCALE
        qpos = i * qb + jnp.arange(qb)
        sc = jnp.where(qpos[:, None] >= kpos[None, :], sc, -jnp.inf)
        p = jax.nn.softmax(sc, axis=-1).astype(v.dtype)
        return jnp.einsum('bhqk,bkhd->bqhd', p, v)

    def split(a):
        return jnp.swapaxes(a.reshape((b, s // qb, qb) + a.shape[2:]), 0, 1)

    o = lax.map(block, (jnp.arange(s // qb), split(q_nope), split(q_pe)))
    return jnp.swapaxes(o, 0, 1).reshape(b, s, h, DV_B)


def mla_sample(q_nope, q_pe, c_kv, k_pe, w_uk, w_uv, cache_lat, cache_kpe, page_table):
    b, t, h, _ = q_nope.shape
    lat = jnp.concatenate([cache_lat[page_table].reshape(b, -1, KV_LORA), c_kv], axis=1)
    kpe = jnp.concatenate([cache_kpe[page_table].reshape(b, -1, D_ROPE), k_pe], axis=1)
    past = lat.shape[1] - t
    q_lat = jnp.einsum('bthd,lhd->bthl', q_nope, w_uk)
    sc = (jnp.einsum('bthl,bkl->bhtk', q_lat, lat)
          + jnp.einsum('bthd,bkd->bhtk', q_pe, kpe)).astype(jnp.float32) * MLA_SCALE
    mask = jnp.arange(past + t)[None, :] <= past + jnp.arange(t)[:, None]
    sc = jnp.where(mask, sc, -jnp.inf)
    p = jax.nn.softmax(sc, axis=-1).astype(lat.dtype)
    o_lat = jnp.einsum('bhtk,bkl->bthl', p, lat)
    return jnp.einsum('bthl,lhd->bthd', o_lat, w_uv)


def gdn_mla_mixer(h, pos, conv_buf, s0, attend, w_in, conv_w, a_log, dt_bias, gdn_norm,
                  q_norm, w_uq, kv_norm, w_uk, w_uv, w_out):
    b, t, _ = h.shape
    qkv, z, b_raw, a_raw, cq, ckv, kpe = jnp.split(h @ w_in, SPLIT_AB, axis=-1)
    ext = jnp.concatenate([conv_buf.astype(qkv.dtype), qkv], axis=1)
    new_buf = ext[:, -(CONV_W - 1):]
    qkv = jax.nn.silu(sum(ext[:, i:i + t] * conv_w[i] for i in range(CONV_W)))
    q, k, v = jnp.split(qkv, (H_A * DK_A, 2 * H_A * DK_A), axis=-1)
    q = l2norm(q.reshape(b, t, H_A, DK_A))
    k = l2norm(k.reshape(b, t, H_A, DK_A))
    beta = jax.nn.sigmoid(b_raw.astype(jnp.float32))
    g = -jnp.exp(a_log.astype(jnp.float32)) * jax.nn.softplus(a_raw.astype(jnp.float32) + dt_bias)
    o_a, s_new = gated_delta_rule(q, k, v.reshape(b, t, H_A, DV_A), g, beta, s0)
    o_a = rmsnorm(o_a.astype(h.dtype), gdn_norm) * jax.nn.silu(z.reshape(b, t, H_A, DV_A))
    qh = jnp.einsum('btl,lhd->bthd', rmsnorm(cq, q_norm), w_uq)
    q_nope, q_pe = qh[..., :D_NOPE], rope(qh[..., D_NOPE:], pos)
    ckv = rmsnorm(ckv, kv_norm)
    kpe = rope(kpe, pos)
    o_b = attend(q_nope, q_pe, ckv, kpe, w_uk, w_uv)
    o = jnp.concatenate([o_a.reshape(b, t, -1), o_b.reshape(b, t, -1)], axis=-1)
    return o @ w_out, new_buf, s_new, ckv, kpe


def hgrn2_mixer(h, s0, lb, w_in, g_norm, w_out):
    b, t, _ = h.shape
    q, f, i, gate = jnp.split(h @ w_in, SPLIT_C, axis=-1)
    f = lb + (1.0 - lb) * jax.nn.sigmoid(f.astype(jnp.float32))

    def heads(a, d):
        return a.reshape(b, t, H_C, d)

    o, s_new = gated_linear_recurrence(heads(jax.nn.silu(q), DK_C), heads(1.0 - f, DK_C),
                                       heads(i, DV_C), heads(jnp.log(f), DK_C), s0)
    o = rmsnorm(o.astype(h.dtype), g_norm) * jax.nn.silu(heads(gate, DV_C))
    return o.reshape(b, t, -1) @ w_out, s_new


def sq_relu_mlp(h, w_up, w_down):
    return jnp.square(jax.nn.relu(h @ w_up)) @ w_down


def setup_inputs(seed: int = 0) -> dict:
    key = jax.random.key(seed)
    ks = iter(jax.random.split(key, 32))

    def nrm(shape, scale=1.0):
        return jax.random.normal(next(ks), shape, jnp.float32) * scale

    def gain(shape):
        return 1.0 + nrm(shape, 0.02)

    n_pages = PAST_LEN // PAGE_SIZE
    n_pool = (DEC_BATCH * n_pages * 5) // 4
    page_table = jax.random.permutation(next(ks), n_pool)[:DEC_BATCH * n_pages]
    page_table = page_table.reshape(DEC_BATCH, n_pages).astype(jnp.int32)
    dt = jnp.exp(jax.random.uniform(next(ks), (N_AB_LAYERS, H_A), jnp.float32,
                                    math.log(1e-3), math.log(1e-1)))
    a_log = jnp.log(jax.random.uniform(next(ks), (N_AB_LAYERS, H_A), jnp.float32, 1.0, 16.0))
    return {
        'x_prompt': nrm((BATCH, SEQ, D_MODEL)),
        'x_sample': nrm((DEC_BATCH, DEC_SEQ, D_MODEL)),
        'state_gdn': nrm((N_AB_LAYERS, DEC_BATCH, H_A, DK_A, DV_A), 0.5),
        'state_gdn_conv': nrm((N_AB_LAYERS, DEC_BATCH, CONV_W - 1, QKV_A)),
        'cache_mla_latent': nrm((N_AB_LAYERS, n_pool, PAGE_SIZE, KV_LORA)),
        'cache_mla_krope': nrm((N_AB_LAYERS, n_pool, PAGE_SIZE, D_ROPE)),
        'state_hgrn': nrm((N_C_LAYERS, DEC_BATCH, H_C, DK_C, DV_C), 0.5),
        'page_table': page_table,
        'mix_norm': gain((DEPTH, D_MODEL)),
        'mlp_norm': gain((DEPTH, D_MODEL)),
        'final_norm': gain((D_MODEL,)),
        'w_up': nrm((DEPTH, D_MODEL, D_FF), D_MODEL ** -0.5),
        'w_down': nrm((DEPTH, D_FF, D_MODEL), 0.5 * D_FF ** -0.5),
        'w_in_ab': nrm((N_AB_LAYERS, D_MODEL, IN_AB), D_MODEL ** -0.5),
        'conv_w_ab': nrm((N_AB_LAYERS, CONV_W, QKV_A), CONV_W ** -0.5),
        'a_log_ab': a_log,
        'dt_bias_ab': dt + jnp.log(-jnp.expm1(-dt)),
        'gdn_norm_ab': gain((N_AB_LAYERS, DV_A)),
        'q_norm_ab': gain((N_AB_LAYERS, Q_LORA)),
        'w_uq_ab': nrm((N_AB_LAYERS, Q_LORA, H_B, D_NOPE + D_ROPE), Q_LORA ** -0.5),
        'kv_norm_ab': gain((N_AB_LAYERS, KV_LORA)),
        'w_uk_ab': nrm((N_AB_LAYERS, KV_LORA, H_B, D_NOPE), KV_LORA ** -0.5),
        'w_uv_ab': nrm((N_AB_LAYERS, KV_LORA, H_B, DV_B), KV_LORA ** -0.5),
        'w_out_ab': nrm((N_AB_LAYERS, OUT_AB, D_MODEL), OUT_AB ** -0.5),
        'w_in_c': nrm((N_C_LAYERS, D_MODEL, IN_C), D_MODEL ** -0.5),
        'lb_logits_c': nrm((DEPTH, H_C * DK_C), 0.5),
        'g_norm_c': gain((N_C_LAYERS, DV_C)),
        'w_out_c': nrm((N_C_LAYERS, H_C * DV_C, D_MODEL), (H_C * DV_C) ** -0.5),
    }


def reference(x_prompt, x_sample, state_gdn, state_gdn_conv, cache_mla_latent, cache_mla_krope,
              state_hgrn, page_table, mix_norm, mlp_norm, final_norm, w_up, w_down,
              w_in_ab, conv_w_ab, a_log_ab, dt_bias_ab, gdn_norm_ab, q_norm_ab, w_uq_ab,
              kv_norm_ab, w_uk_ab, w_uv_ab, w_out_ab, w_in_c, lb_logits_c, g_norm_c, w_out_c):
    f32 = jnp.float32
    p_lb = jax.nn.softmax(lb_logits_c.astype(f32), axis=0)
    lower_bounds = jnp.cumsum(p_lb, axis=0) - p_lb[0]

    def trunk(x, pos, conv0, gdn0, hgrn0, attend):
        convs, gdns, lats, kpes, hgrns = [], [], [], [], []
        for layer in range(DEPTH):
            j = layer // 2
            h = rmsnorm(x, mix_norm[layer])
            if layer % 2 == 0:
                o, buf, s, ckv, kpe = gdn_mla_mixer(
                    h, pos, conv0[j], gdn0[j], functools.partial(attend, j), w_in_ab[j], conv_w_ab[j],
                    a_log_ab[j], dt_bias_ab[j], gdn_norm_ab[j], q_norm_ab[j], w_uq_ab[j],
                    kv_norm_ab[j], w_uk_ab[j], w_uv_ab[j], w_out_ab[j])
                convs.append(buf)
                gdns.append(s)
                lats.append(ckv)
                kpes.append(kpe)
            else:
                o, s = hgrn2_mixer(h, hgrn0[j], lower_bounds[layer], w_in_c[j], g_norm_c[j], w_out_c[j])
                hgrns.append(s)
            x = x + o
            x = x + sq_relu_mlp(rmsnorm(x, mlp_norm[layer]), w_up[layer], w_down[layer])
        return (rmsnorm(x, final_norm), jnp.stack(gdns), jnp.stack(convs), jnp.stack(lats),
                jnp.stack(kpes), jnp.stack(hgrns))

    def attend_prompt(j, q_nope, q_pe, c_kv, k_pe, w_uk, w_uv):
        return mla_prompt(q_nope, q_pe, c_kv, k_pe, w_uk, w_uv)

    def attend_sample(j, q_nope, q_pe, c_kv, k_pe, w_uk, w_uv):
        return mla_sample(q_nope, q_pe, c_kv, k_pe, w_uk, w_uv,
                          cache_mla_latent[j], cache_mla_krope[j], page_table)

    b, s_len = x_prompt.shape[:2]
    y_prompt, gdn_p, conv_p, lat_p, kpe_p, hgrn_p = trunk(
        x_prompt, jnp.arange(s_len),
        jnp.zeros((N_AB_LAYERS, b, CONV_W - 1, QKV_A), x_prompt.dtype),
        jnp.zeros((N_AB_LAYERS, b, H_A, DK_A, DV_A), f32),
        jnp.zeros((N_C_LAYERS, b, H_C, DK_C, DV_C), f32),
        attend_prompt)
    past_len = page_table.shape[1] * cache_mla_latent.shape[2]
    y_sample, gdn_s, conv_s, lat_s, kpe_s, hgrn_s = trunk(
        x_sample, past_len + jnp.arange(x_sample.shape[1]),
        state_gdn_conv, state_gdn, state_hgrn, attend_sample)
    return (y_prompt, y_sample, gdn_p, conv_p, lat_p, kpe_p, hgrn_p, gdn_s, conv_s, lat_s, kpe_s, hgrn_s)
```

```python
import functools
import math

import jax
import jax.numpy as jnp
from jax import lax
from jax.experimental import pallas as pl
from jax.experimental.pallas import tpu as pltpu

F32 = jnp.float32
BF16 = jnp.bfloat16
HI = lax.Precision.HIGHEST
EPS = 1e-6
NEG_INF = float("-inf")

H_A, DK_A, DV_A, CONV_W = 4, 128, 128, 4
QKV_A = H_A * (2 * DK_A + DV_A)
H_B, D_NOPE, D_ROPE, DV_B = 8, 64, 32, 64
Q_LORA, KV_LORA = 256, 256
ROPE_BASE = 10000.0
MLA_SCALE = (D_NOPE + D_ROPE) ** -0.5
H_C, DK_C, DV_C = 8, 128, 128
CHUNK = 64
SUB = 16
PAGE = 128
QW = 256
IN_AB_PAD = 2816
V7X_VMEM_LIMIT = 48 * 1024 * 1024


def _cparams(sem):
    return pltpu.CompilerParams(dimension_semantics=sem, vmem_limit_bytes=V7X_VMEM_LIMIT)


def _const_spec(shape):
    nd = len(shape)
    return pl.BlockSpec(shape, lambda *_: (0,) * nd)


def _dot(a, b):
    return jnp.dot(a.astype(BF16), b.astype(BF16), preferred_element_type=F32)


def _dot_nt(a, b):
    return lax.dot_general(a.astype(BF16), b.astype(BF16), (((1,), (1,)), ((), ())),
                           preferred_element_type=F32)


def _dot_tn(a, b):
    return lax.dot_general(a.astype(BF16), b.astype(BF16), (((0,), (0,)), ((), ())),
                           preferred_element_type=F32)


def _dot_nt_f32(a, b):
    return lax.dot_general(a, b, (((1,), (1,)), ((), ())), precision=HI, preferred_element_type=F32)


def _rms(x, g):
    return x * lax.rsqrt(jnp.mean(x * x, axis=-1, keepdims=True) + EPS) * g


def _silu(x):
    return x * jax.nn.sigmoid(x)


def _cumsum_rows(x):
    n = x.shape[0]
    row = lax.broadcasted_iota(jnp.int32, x.shape, 0)
    s = 1
    while s < n:
        x = x + jnp.where(row >= s, pltpu.roll(x, s, axis=0), 0.0)
        s *= 2
    return x


def _proj_kernel(x_ref, g_ref, w_ref, o_ref):
    h = _rms(x_ref[...], g_ref[...]).astype(BF16)
    o_ref[...] = jnp.dot(h, w_ref[...], preferred_element_type=F32)


def _proj(x2, g, w, tm):
    n, d = x2.shape
    nout = w.shape[1]
    return pl.pallas_call(
        _proj_kernel,
        grid=(n // tm,),
        in_specs=[pl.BlockSpec((tm, d), lambda i: (i, 0)), _const_spec((1, d)), _const_spec((d, nout))],
        out_specs=pl.BlockSpec((tm, nout), lambda i: (i, 0)),
        out_shape=jax.ShapeDtypeStruct((n, nout), F32),
        compiler_params=_cparams(("parallel",)),
        name="proj",
    )(x2, g.reshape(1, d), w)


def _outproj_kernel(*refs, n_in):
    r_ref = refs[0]
    a_refs = refs[1:1 + n_in]
    w_refs = refs[1 + n_in:1 + 2 * n_in]
    o_ref = refs[1 + 2 * n_in]
    acc = r_ref[...]
    for a_ref, w_ref in zip(a_refs, w_refs):
        acc = acc + jnp.dot(a_ref[...], w_ref[...], preferred_element_type=F32)
    o_ref[...] = acc


def _outproj(res, acts, ws, tm):
    n, d = res.shape
    n_in = len(acts)
    in_specs = [pl.BlockSpec((tm, d), lambda i: (i, 0))]
    in_specs += [pl.BlockSpec((tm, a.shape[1]), lambda i: (i, 0)) for a in acts]
    in_specs += [_const_spec(w.shape) for w in ws]
    return pl.pallas_call(
        functools.partial(_outproj_kernel, n_in=n_in),
        grid=(n // tm,),
        in_specs=in_specs,
        out_specs=pl.BlockSpec((tm, d), lambda i: (i, 0)),
        out_shape=jax.ShapeDtypeStruct((n, d), F32),
        compiler_params=_cparams(("parallel",)),
        name="outproj",
    )(res, *acts, *ws)


def _mlp_kernel(x_ref, g_ref, gf_ref, wu_ref, wd_ref, o_ref, h_ref, *, final_norm):
    j = pl.program_id(1)

    @pl.when(j == 0)
    def _():
        x = x_ref[...]
        h_ref[...] = _rms(x, g_ref[...]).astype(BF16)
        o_ref[...] = x

    a = jnp.dot(h_ref[...], wu_ref[...], preferred_element_type=F32)
    a = jnp.square(jnp.maximum(a, 0.0))
    o_ref[...] += jnp.dot(a.astype(BF16), wd_ref[...], preferred_element_type=F32)

    if final_norm:
        @pl.when(j == pl.num_programs(1) - 1)
        def _():
            o_ref[...] = _rms(o_ref[...], gf_ref[...])


def _mlp(x2, g, w_up, w_down, gf, final_norm, tm, tf):
    n, d = x2.shape
    ff = w_up.shape[1]
    return pl.pallas_call(
        functools.partial(_mlp_kernel, final_norm=final_norm),
        grid=(n // tm, ff // tf),
        in_specs=[pl.BlockSpec((tm, d), lambda i, j: (i, 0)), _const_spec((1, d)), _const_spec((1, d)),
                  pl.BlockSpec((d, tf), lambda i, j: (0, j)), pl.BlockSpec((tf, d), lambda i, j: (j, 0))],
        out_specs=pl.BlockSpec((tm, d), lambda i, j: (i, 0)),
        out_shape=jax.ShapeDtypeStruct((n, d), F32),
        scratch_shapes=[pltpu.VMEM((tm, d), BF16)],
        compiler_params=_cparams(("parallel", "arbitrary")),
        name="mlp",
    )(x2, g.reshape(1, d), gf.reshape(1, d), w_up, w_down)


def _gdn_head(q, k, v, z, b_col, a_col, alog, dtb, gnorm, s_prev, c):
    q = q * lax.rsqrt(jnp.sum(q * q, axis=-1, keepdims=True) + EPS) * (DK_A ** -0.5)
    k = k * lax.rsqrt(jnp.sum(k * k, axis=-1, keepdims=True) + EPS)
    beta = jax.nn.sigmoid(b_col)
    g = -jnp.exp(alog) * jax.nn.softplus(a_col + dtb)
    gc_b = _cumsum_rows(jnp.broadcast_to(g, (c, 128)))
    gc = gc_b[:, 0:1]
    diff = gc_b[:, :c] - gc_b.T[:c, :]
    row = lax.broadcasted_iota(jnp.int32, (c, c), 0)
    col = lax.broadcasted_iota(jnp.int32, (c, c), 1)
    incl = row >= col
    decay = jnp.where(incl, jnp.exp(jnp.where(incl, diff, 0.0)), 0.0)
    kb = k * beta
    lower = jnp.where(row > col, _dot_nt_f32(kb, k) * decay, 0.0)
    rhs = jnp.concatenate([v * beta, kb * jnp.exp(gc)], axis=1)
    for j in range(c - 1):
        rhs = rhs - lower[:, j:j + 1] * rhs[j:j + 1, :]
    u, w = rhs[:, :DV_A], rhs[:, DV_A:]
    v_new = u - _dot(w, s_prev)
    intra = _dot_nt_f32(q, k) * decay
    o = _dot(q * jnp.exp(gc), s_prev) + _dot(intra, v_new)
    g_last = gc[c - 1:c, :]
    s_new = s_prev * jnp.exp(g_last) + _dot_tn(k * jnp.exp(g_last - gc), v_new)
    o = _rms(o, gnorm) * _silu(z)
    return o, s_new


def _gdn_kernel(*refs, c, n_chunks, has_state):
    (q_ref, k_ref, v_ref, z_ref, gt_ref, cwq_ref, cwk_ref, cwv_ref, alog_ref, dtb_ref, gn_ref) = refs[:11]
    if has_state:
        cq_ref, ck_ref, cv_ref, s0_ref = refs[11:15]
        o_ref, sout_ref, s_scr, tail_scr = refs[15:]
    else:
        o_ref, sout_ref, s_scr, tail_scr = refs[11:]
    n = pl.program_id(1)

    @pl.when(n == 0)
    def _():
        if has_state:
            s_scr[...] = s0_ref[0]
            tail_scr[0] = cq_ref[0]
            tail_scr[1] = ck_ref[0]
            tail_scr[2] = cv_ref[0]
        else:
            s_scr[...] = jnp.zeros_like(s_scr)
            tail_scr[...] = jnp.zeros_like(tail_scr)

    def chunk(ci, carry):
        r0 = ci * c if isinstance(ci, int) else pl.multiple_of(ci * c, c)
        rows = pl.ds(r0, c)
        conv = []
        for idx, (x_ref, cw_ref) in enumerate(((q_ref, cwq_ref), (k_ref, cwk_ref), (v_ref, cwv_ref))):
            x = x_ref[0, rows, :]
            ext = jnp.concatenate([tail_scr[idx], x], axis=0)
            cw = cw_ref[...]
            y = x * cw[CONV_W - 1:CONV_W, :]
            for sh in range(1, CONV_W):
                y = y + pltpu.roll(ext, sh, axis=0)[8:] * cw[CONV_W - 1 - sh:CONV_W - sh, :]
            tail_scr[idx] = x[c - 8:, :]
            conv.append(_silu(y))
        z = z_ref[0, rows, :]
        gt = gt_ref[0, rows, :]
        outs = []
        for h in range(H_A):
            hs = slice(h * 128, (h + 1) * 128)
            o, s_new = _gdn_head(conv[0][:, hs], conv[1][:, hs], conv[2][:, hs], z[:, hs],
                                 gt[:, h:h + 1], gt[:, H_A + h:H_A + h + 1],
                                 alog_ref[:, h:h + 1], dtb_ref[:, h:h + 1], gn_ref[...], s_scr[h], c)
            s_scr[h] = s_new
            outs.append(o)
        o_ref[0, rows, :] = jnp.concatenate(outs, axis=1).astype(o_ref.dtype)
        return carry

    if n_chunks == 1:
        chunk(0, 0)
    else:
        lax.fori_loop(0, n_chunks, chunk, 0)

    @pl.when(n == pl.num_programs(1) - 1)
    def _():
        sout_ref[0] = s_scr[...]


def _gdn(p3, conv_w, a_log, dt_bias, gdn_norm, conv0, s0, tb):
    b, t, _ = p3.shape
    c = math.gcd(t, CHUNK)
    has_state = s0 is not None
    hw = H_A * 128
    cw = conv_w
    blk = lambda j: pl.BlockSpec((1, tb, hw), lambda i, n: (i, n, j))
    in_specs = [blk(0), blk(1), blk(2), blk(3),
                pl.BlockSpec((1, tb, 128), lambda i, n: (i, n, (IN_AB_PAD - 128) // 128)),
                pl.BlockSpec((CONV_W, hw), lambda i, n: (0, 0)), pl.BlockSpec((CONV_W, hw), lambda i, n: (0, 1)),
                pl.BlockSpec((CONV_W, hw), lambda i, n: (0, 2)),
                _const_spec((1, H_A)), _const_spec((1, H_A)), _const_spec((1, DV_A))]
    args = [p3, p3, p3, p3, p3, cw, cw, cw, a_log.reshape(1, H_A), dt_bias.reshape(1, H_A),
            gdn_norm.reshape(1, DV_A)]
    if has_state:
        conv8 = jnp.pad(conv0, ((0, 0), (8 - (CONV_W - 1), 0), (0, 0)))
        in_specs += [pl.BlockSpec((1, 8, hw), lambda i, n, j=j: (i, 0, j)) for j in range(3)]
        in_specs += [pl.BlockSpec((1, H_A, DK_A, DV_A), lambda i, n: (i, 0, 0, 0))]
        args += [conv8, conv8, conv8, s0]
    return pl.pallas_call(
        functools.partial(_gdn_kernel, c=c, n_chunks=tb // c, has_state=has_state),
        grid=(b, t // tb),
        in_specs=in_specs,
        out_specs=[pl.BlockSpec((1, tb, hw), lambda i, n: (i, n, 0)),
                   pl.BlockSpec((1, H_A, DK_A, DV_A), lambda i, n: (i, 0, 0, 0))],
        out_shape=[jax.ShapeDtypeStruct((b, t, hw), BF16),
                   jax.ShapeDtypeStruct((b, H_A, DK_A, DV_A), F32)],
        scratch_shapes=[pltpu.VMEM((H_A, DK_A, DV_A), F32), pltpu.VMEM((3, 8, hw), F32)],
        compiler_params=_cparams(("parallel", "arbitrary")),
        name="gdn",
    )(*args)


def _hgrn_chunk(qr, fr, v, lb, s_prev, c):
    sub = min(SUB, c)
    q = _silu(qr) * (DK_C ** -0.5)
    f = lb + (1.0 - lb) * jax.nn.sigmoid(fr)
    k = 1.0 - f
    gc = _cumsum_rows(jnp.log(f))
    o_inter = _dot(q * jnp.exp(gc), s_prev)
    row = lax.broadcasted_iota(jnp.int32, (sub, sub), 0)
    col = lax.broadcasted_iota(jnp.int32, (sub, sub), 1)
    row_d = lax.broadcasted_iota(jnp.int32, (sub, 128), 0)
    outs = []
    for blk in range(c // sub):
        r0 = blk * sub
        g_i = gc[r0:r0 + sub]
        q_i = q[r0:r0 + sub]
        a_diag = jnp.zeros((sub, sub), F32)
        for j in range(sub):
            jj = r0 + j
            e = jnp.exp(jnp.where(row_d >= j, g_i - gc[jj:jj + 1, :], 0.0))
            col_j = jnp.sum(q_i * k[jj:jj + 1, :] * e, axis=-1, keepdims=True)
            a_diag = jnp.where(col == j, col_j, a_diag)
        a_diag = jnp.where(row >= col, a_diag, 0.0)
        o_i = _dot(a_diag, v[r0:r0 + sub])
        if blk > 0:
            g_ref = gc[r0 - 1:r0, :]
            a_off = _dot_nt(q_i * jnp.exp(g_i - g_ref), k[:r0] * jnp.exp(g_ref - gc[:r0]))
            o_i = o_i + _dot(a_off, v[:r0])
        outs.append(o_i)
    o = o_inter + (jnp.concatenate(outs, axis=0) if len(outs) > 1 else outs[0])
    g_last = gc[c - 1:c, :]
    gl_col = gc.T[:, c - 1:c]
    s_new = s_prev * jnp.exp(gl_col) + _dot_tn(k * jnp.exp(g_last - gc), v)
    return o, s_new


def _hgrn_kernel(*refs, c, n_chunks, layer, has_state):
    q_ref, f_ref, i_ref, gate_ref, lbl_ref, gn_ref = refs[:6]
    if has_state:
        s0_ref = refs[6]
        o_ref, sout_ref, s_scr = refs[7:]
    else:
        o_ref, sout_ref, s_scr = refs[6:]
    n = pl.program_id(2)

    @pl.when(n == 0)
    def _():
        if has_state:
            s_scr[...] = s0_ref[0, 0]
        else:
            s_scr[...] = jnp.zeros_like(s_scr)

    lbl = lbl_ref[...]
    ex = jnp.exp(lbl - jnp.max(lbl, axis=0, keepdims=True))
    p = ex / jnp.sum(ex, axis=0, keepdims=True)
    lb = jnp.sum(p[:layer + 1], axis=0, keepdims=True) - p[0:1]

    def chunk(ci, carry):
        r0 = ci * c if isinstance(ci, int) else pl.multiple_of(ci * c, c)
        rows = pl.ds(r0, c)
        o, s_new = _hgrn_chunk(q_ref[0, rows, :], f_ref[0, rows, :], i_ref[0, rows, :], lb, s_scr[...], c)
        s_scr[...] = s_new
        o_ref[0, rows, :] = (_rms(o, gn_ref[...]) * _silu(gate_ref[0, rows, :])).astype(o_ref.dtype)
        return carry

    if n_chunks == 1:
        chunk(0, 0)
    else:
        lax.fori_loop(0, n_chunks, chunk, 0)

    @pl.when(n == pl.num_programs(2) - 1)
    def _():
        sout_ref[0, 0] = s_scr[...]


def _hgrn(pc3, lb_logits, layer, g_norm, s0, tb):
    b, t, _ = pc3.shape
    c = math.gcd(t, CHUNK)
    has_state = s0 is not None
    depth = lb_logits.shape[0]
    blk = lambda j: pl.BlockSpec((1, tb, 128), lambda i, h, n: (i, n, j * H_C + h))
    in_specs = [blk(0), blk(1), blk(2), blk(3),
                pl.BlockSpec((depth, 128), lambda i, h, n: (0, h)), _const_spec((1, DV_C))]
    args = [pc3, pc3, pc3, pc3, lb_logits, g_norm.reshape(1, DV_C)]
    if has_state:
        in_specs.append(pl.BlockSpec((1, 1, DK_C, DV_C), lambda i, h, n: (i, h, 0, 0)))
        args.append(s0)
    return pl.pallas_call(
        functools.partial(_hgrn_kernel, c=c, n_chunks=tb // c, layer=layer, has_state=has_state),
        grid=(b, H_C, t // tb),
        in_specs=in_specs,
        out_specs=[pl.BlockSpec((1, tb, 128), lambda i, h, n: (i, n, h)),
                   pl.BlockSpec((1, 1, DK_C, DV_C), lambda i, h, n: (i, h, 0, 0))],
        out_shape=[jax.ShapeDtypeStruct((b, t, H_C * DV_C), BF16),
                   jax.ShapeDtypeStruct((b, H_C, DK_C, DV_C), F32)],
        scratch_shapes=[pltpu.VMEM((DK_C, DV_C), F32)],
        compiler_params=_cparams(("parallel", "parallel", "arbitrary")),
        name="hgrn",
    )(*args)


def _mla_prep_kernel(*refs, absorb):
    cq_ref, ckv_ref, kp_ref, qn_ref, kvn_ref, tq_ref, tk_ref, wq_ref = refs[:8]
    if absorb:
        wukt_ref, lat_ref, kr_ref, qlat_ref, qpe_ref = refs[8:]
    else:
        wkv_ref, lat_ref, kr_ref, q_ref, kv_ref, kp4_ref = refs[8:]
    lat = _rms(ckv_ref[...], kvn_ref[...])
    lat_ref[...] = lat
    kp4 = kp_ref[...] * tk_ref[...]
    kr_ref[...] = kp4[:, 0:D_ROPE] + kp4[:, D_ROPE:2 * D_ROPE]
    cqn = _rms(cq_ref[...], qn_ref[...]).astype(BF16)
    tq = tq_ref[...]
    if absorb:
        for h in range(H_B):
            qh = jnp.dot(cqn, wq_ref[:, h * QW:(h + 1) * QW], preferred_element_type=F32) * tq
            qlat_ref[:, h * KV_LORA:(h + 1) * KV_LORA] = _dot(qh[:, 0:D_NOPE], wukt_ref[h])
            qpe_ref[:, h * D_ROPE:(h + 1) * D_ROPE] = (qh[:, 128:128 + D_ROPE]
                                                      + qh[:, 128 + 2 * D_ROPE:128 + 3 * D_ROPE])
    else:
        for h in range(H_B):
            qh = jnp.dot(cqn, wq_ref[:, h * QW:(h + 1) * QW], preferred_element_type=F32) * tq
            q_ref[:, h * QW:(h + 1) * QW] = qh.astype(BF16)
        kv_ref[...] = jnp.dot(lat.astype(BF16), wkv_ref[...], preferred_element_type=F32).astype(BF16)
        kp4_ref[...] = kp4.astype(BF16)


def _mla_prep(p2, q_norm, kv_norm, tabq, tabk, wq, w_extra, absorb, tm):
    n = p2.shape[0]
    nt = tabq.shape[0] // tm
    in_specs = [pl.BlockSpec((tm, Q_LORA), lambda i: (i, 2048 // 256)),
                pl.BlockSpec((tm, KV_LORA), lambda i: (i, 2304 // 256)),
                pl.BlockSpec((tm, 128), lambda i: (i, 2560 // 128)),
                _const_spec((1, Q_LORA)), _const_spec((1, KV_LORA)),
                pl.BlockSpec((tm, QW), lambda i: (i % nt, 0)), pl.BlockSpec((tm, 128), lambda i: (i % nt, 0)),
                _const_spec(wq.shape), _const_spec(w_extra.shape)]
    row = lambda w: pl.BlockSpec((tm, w), lambda i: (i, 0))
    out_specs = [row(KV_LORA), row(D_ROPE)]
    out_shape = [jax.ShapeDtypeStruct((n, KV_LORA), F32), jax.ShapeDtypeStruct((n, D_ROPE), F32)]
    if absorb:
        out_specs += [row(H_B * KV_LORA), row(H_B * D_ROPE)]
        out_shape += [jax.ShapeDtypeStruct((n, H_B * KV_LORA), F32), jax.ShapeDtypeStruct((n, H_B * D_ROPE), F32)]
    else:
        out_specs += [row(H_B * QW), row(H_B * 128), row(128)]
        out_shape += [jax.ShapeDtypeStruct((n, H_B * QW), BF16), jax.ShapeDtypeStruct((n, H_B * 128), BF16),
                      jax.ShapeDtypeStruct((n, 128), BF16)]
    return pl.pallas_call(
        functools.partial(_mla_prep_kernel, absorb=absorb),
        grid=(n // tm,),
        in_specs=in_specs, out_specs=out_specs, out_shape=out_shape,
        compiler_params=_cparams(("parallel",)),
        name="mla_prep",
    )(p2, p2, p2, q_norm.reshape(1, Q_LORA), kv_norm.reshape(1, KV_LORA), tabq, tabk, wq, w_extra)


def _flash_kernel(q_ref, kv_ref, kp_ref, o_ref, *, tq):
    qi = pl.program_id(1)
    row = lax.broadcasted_iota(jnp.int32, (tq, tq), 0)
    col = lax.broadcasted_iota(jnp.int32, (tq, tq), 1)
    outs = []
    for h in range(H_B):
        q = q_ref[0, :, h * QW:(h + 1) * QW]

        def tile(k0, carry, masked):
            m, l, acc = carry
            kvh = kv_ref[0, pl.ds(k0, tq), h * 128:(h + 1) * 128]
            keys = jnp.concatenate([kvh, kp_ref[0, pl.ds(k0, tq), :]], axis=1)
            s = lax.dot_general(q, keys, (((1,), (1,)), ((), ())), preferred_element_type=F32)
            if masked:
                s = jnp.where(row >= col, s, NEG_INF)
            m_new = jnp.maximum(m, jnp.max(s, axis=-1, keepdims=True))
            alpha = jnp.exp(m - m_new)
            p = jnp.exp(s - m_new)
            l = alpha * l + jnp.sum(p, axis=-1, keepdims=True)
            acc = alpha * acc + jnp.dot(p.astype(BF16), kvh, preferred_element_type=F32)
            return m_new, l, acc

        init = (jnp.full((tq, 1), NEG_INF, F32), jnp.zeros((tq, 1), F32), jnp.zeros((tq, 128), F32))
        carry = lax.fori_loop(0, qi, lambda kt, cr: tile(pl.multiple_of(kt * tq, tq), cr, False), init)
        m, l, acc = tile(pl.multiple_of(qi * tq, tq), carry, True)
        outs.append((acc / l)[:, D_NOPE:D_NOPE + DV_B])
    o_ref[0] = jnp.concatenate(outs, axis=1).astype(o_ref.dtype)


def _flash(q3, kv3, kp3, tq):
    b, t, _ = q3.shape
    return pl.pallas_call(
        functools.partial(_flash_kernel, tq=tq),
        grid=(b, t // tq),
        in_specs=[pl.BlockSpec((1, tq, H_B * QW), lambda i, j: (i, j, 0)),
                  pl.BlockSpec((1, t, H_B * 128), lambda i, j: (i, 0, 0)),
                  pl.BlockSpec((1, t, 128), lambda i, j: (i, 0, 0))],
        out_specs=pl.BlockSpec((1, tq, H_B * DV_B), lambda i, j: (i, j, 0)),
        out_shape=jax.ShapeDtypeStruct((b, t, H_B * DV_B), BF16),
        compiler_params=_cparams(("parallel", "arbitrary")),
        name="mla_flash",
    )(q3, kv3, kp3)


def _decode_kernel(pt_ref, qlat_ref, qpe_ref, latn_ref, krn_ref, wuv_ref, *refs, pp, t_new):
    lat_refs = refs[:pp]
    kc_refs = refs[pp:2 * pp]
    o_ref = refs[2 * pp]
    m_scr, l_scr, acc_scr = refs[2 * pp + 1:]
    g = pl.program_id(1)
    nq = H_B * t_new

    @pl.when(g == 0)
    def _():
        m_scr[...] = jnp.full_like(m_scr, NEG_INF)
        l_scr[...] = jnp.zeros_like(l_scr)
        acc_scr[...] = jnp.zeros_like(acc_scr)

    ql = jnp.concatenate([qlat_ref[0, :, h * KV_LORA:(h + 1) * KV_LORA] for h in range(H_B)], axis=0).astype(BF16)
    qp = jnp.concatenate([qpe_ref[0, :, h * D_ROPE:(h + 1) * D_ROPE] for h in range(H_B)], axis=0).astype(BF16)

    def update(s, values):
        m = m_scr[...]
        m_new = jnp.maximum(m, jnp.max(s, axis=-1, keepdims=True))
        alpha = jnp.exp(m - m_new)
        p = jnp.exp(s - m_new)
        l_scr[...] = alpha * l_scr[...] + jnp.sum(p, axis=-1, keepdims=True)
        w = 0
        pv = None
        for val in values:
            d = jnp.dot(p[:, w:w + val.shape[0]].astype(BF16), val, preferred_element_type=F32)
            pv = d if pv is None else pv + d
            w += val.shape[0]
        acc_scr[...] = alpha * acc_scr[...] + pv
        m_scr[...] = m_new

    lats = [lat_refs[i][0].astype(BF16) for i in range(pp)]
    s_parts = []
    for i in range(pp):
        kc = kc_refs[i][0].astype(BF16)
        s_parts.append(lax.dot_general(ql, lats[i], (((1,), (1,)), ((), ())), preferred_element_type=F32)
                       + lax.dot_general(qp, kc, (((1,), (1,)), ((), ())), preferred_element_type=F32))
    update(jnp.concatenate(s_parts, axis=1) if pp > 1 else s_parts[0], lats)

    @pl.when(g == pl.num_programs(1) - 1)
    def _():
        latn = latn_ref[0].astype(BF16)
        krn = krn_ref[0].astype(BF16)
        s = (lax.dot_general(ql, latn, (((1,), (1,)), ((), ())), preferred_element_type=F32)
             + lax.dot_general(qp, krn, (((1,), (1,)), ((), ())), preferred_element_type=F32))
        tok = lax.broadcasted_iota(jnp.int32, (nq, t_new), 0) % t_new
        key = lax.broadcasted_iota(jnp.int32, (nq, t_new), 1)
        update(jnp.where(key <= tok, s, NEG_INF), [latn])
        o_lat = acc_scr[...] / l_scr[...]
        outs = [_dot(o_lat[h * t_new:(h + 1) * t_new], wuv_ref[h]) for h in range(H_B)]
        o_ref[0] = jnp.concatenate(outs, axis=1).astype(o_ref.dtype)


def _decode(page_table, qlat3, qpe3, latn3, krn3, wuv, cache_lat, cache_kpe, pp):
    b, t_new, _ = qlat3.shape
    n_pages = page_table.shape[1]
    in_specs = [pl.BlockSpec((1, t_new, H_B * KV_LORA), lambda i, g, pt: (i, 0, 0)),
                pl.BlockSpec((1, t_new, H_B * D_ROPE), lambda i, g, pt: (i, 0, 0)),
                pl.BlockSpec((1, t_new, KV_LORA), lambda i, g, pt: (i, 0, 0)),
                pl.BlockSpec((1, t_new, D_ROPE), lambda i, g, pt: (i, 0, 0)),
                pl.BlockSpec(wuv.shape, lambda i, g, pt: (0, 0, 0))]
    in_specs += [pl.BlockSpec((1, PAGE, KV_LORA), lambda i, g, pt, j=j: (pt[i, g * pp + j], 0, 0)) for j in range(pp)]
    in_specs += [pl.BlockSpec((1, PAGE, D_ROPE), lambda i, g, pt, j=j: (pt[i, g * pp + j], 0, 0)) for j in range(pp)]
    nq = H_B * t_new
    return pl.pallas_call(
        functools.partial(_decode_kernel, pp=pp, t_new=t_new),
        grid_spec=pltpu.PrefetchScalarGridSpec(
            num_scalar_prefetch=1,
            grid=(b, n_pages // pp),
            in_specs=in_specs,
            out_specs=pl.BlockSpec((1, t_new, H_B * DV_B), lambda i, g, pt: (i, 0, 0)),
            scratch_shapes=[pltpu.VMEM((nq, 1), F32), pltpu.VMEM((nq, 1), F32), pltpu.VMEM((nq, KV_LORA), F32)]),
        out_shape=jax.ShapeDtypeStruct((b, t_new, H_B * DV_B), BF16),
        compiler_params=_cparams(("parallel", "arbitrary")),
        name="mla_decode",
    )(page_table, qlat3, qpe3, latn3, krn3, wuv, *([cache_lat] * pp), *([cache_kpe] * pp))


def _rot_cols(w):
    half = D_ROPE // 2
    return jnp.concatenate([-w[..., half:], w[..., :half]], axis=-1)


def _layout_w_in_ab(w):
    d = w.shape[0]
    o = QKV_A + H_A * DV_A
    qkv_z, b_w, a_w = w[:, :o], w[:, o:o + H_A], w[:, o + H_A:o + 2 * H_A]
    o += 2 * H_A
    cq, ckv, kpe = w[:, o:o + Q_LORA], w[:, o + Q_LORA:o + Q_LORA + KV_LORA], w[:, o + Q_LORA + KV_LORA:]
    kr = _rot_cols(kpe)
    pad = jnp.zeros((d, 128 - 2 * H_A), w.dtype)
    return jnp.concatenate([qkv_z, cq, ckv, kpe, kr, kpe, kr, b_w, a_w, pad], axis=1).astype(BF16)


def _layout_w_uq(w_uq):
    nope, pe = w_uq[..., :D_NOPE], w_uq[..., D_NOPE:]
    rot = _rot_cols(pe)
    z = jnp.zeros_like(nope)
    return jnp.concatenate([nope, z, pe, pe, rot, rot], axis=-1).reshape(w_uq.shape[0], H_B * QW).astype(BF16)


def _rope_tables(pos):
    half = D_ROPE // 2
    inv = ROPE_BASE ** (-jnp.arange(half, dtype=F32) / half)
    ang = pos.astype(F32)[:, None] * inv
    c = jnp.concatenate([jnp.cos(ang), jnp.cos(ang)], axis=1)
    s = jnp.concatenate([jnp.sin(ang), jnp.sin(ang)], axis=1)
    one = jnp.ones((pos.shape[0], 128), F32)
    tabq = jnp.concatenate([one, c, c, s, s], axis=1) * MLA_SCALE
    tabk = jnp.concatenate([c, s, c, s], axis=1)
    return tabq, tabk


def _row_tile(n):
    for tm in (512, 256, 128, 64, 32, 16, 8):
        if n % tm == 0:
            return tm
    raise ValueError(n)


def _trunk(x, pos, conv0, gdn0, hgrn0, paged, wts):
    b, t, d = x.shape
    n = b * t
    tm = min(256, _row_tile(n))
    tb = min(256, t)
    x2 = x.reshape(n, d)

    p2 = _proj(x2, wts["mix_norm"][0], wts["w_in_ab"], tm)
    p3 = p2.reshape(b, t, IN_AB_PAD)
    conv_new = p3[:, t - (CONV_W - 1):, :QKV_A]
    o_a, gdn_new = _gdn(p3, wts["conv_w"], wts["a_log"], wts["dt_bias"], wts["gdn_norm"], conv0, gdn0, tb)
    tabq, tabk = _rope_tables(pos)
    if paged is None:
        tmp = math.gcd(tm, t)
        lat, kr, q, kv, kp4 = _mla_prep(p2, wts["q_norm"], wts["kv_norm"], tabq, tabk, wts["w_uq"], wts["w_kv"],
                                        False, tmp)
        o_b = _flash(q.reshape(b, t, -1), kv.reshape(b, t, -1), kp4.reshape(b, t, -1), min(256, t))
    else:
        cache_lat, cache_kpe, page_table = paged
        reps = tm // t
        lat, kr, qlat, qpe = _mla_prep(p2, wts["q_norm"], wts["kv_norm"], jnp.tile(tabq, (reps, 1)),
                                       jnp.tile(tabk, (reps, 1)), wts["w_uq"], wts["w_ukt"], True, tm)
        pp = math.gcd(page_table.shape[1], 8)
        o_b = _decode(page_table, qlat.reshape(b, t, -1), qpe.reshape(b, t, -1), lat.reshape(b, t, -1),
                      kr.reshape(b, t, -1), wts["w_uv"], cache_lat, cache_kpe, pp)
    x2 = _outproj(x2, [o_a.reshape(n, -1), o_b.reshape(n, -1)], [wts["w_out_a"], wts["w_out_b"]], tm)
    tmm = min(512, _row_tile(n))
    x2 = _mlp(x2, wts["mlp_norm"][0], wts["w_up"][0], wts["w_down"][0], wts["final_norm"], False, tmm, 1024)

    pc = _proj(x2, wts["mix_norm"][1], wts["w_in_c"], tm)
    o_c, hgrn_new = _hgrn(pc.reshape(b, t, -1), wts["lb_logits"], 1, wts["g_norm_c"], hgrn0, tb)
    x2 = _outproj(x2, [o_c.reshape(n, -1)], [wts["w_out_c"]], tm)
    y2 = _mlp(x2, wts["mlp_norm"][1], wts["w_up"][1], wts["w_down"][1], wts["final_norm"], True, tmm, 1024)
    return (y2.reshape(b, t, d), gdn_new[None], conv_new[None], lat.reshape(b, t, -1)[None],
            kr.reshape(b, t, -1)[None], hgrn_new[None])


def kernel(x_prompt, x_sample, state_gdn, state_gdn_conv, cache_mla_latent, cache_mla_krope, state_hgrn,
           page_table, mix_norm, mlp_norm, final_norm, w_up, w_down, w_in_ab, conv_w_ab, a_log_ab, dt_bias_ab,
           gdn_norm_ab, q_norm_ab, w_uq_ab, kv_norm_ab, w_uk_ab, w_uv_ab, w_out_ab, w_in_c, lb_logits_c,
           g_norm_c, w_out_c):
    assert mix_norm.shape[0] == 2 and w_in_ab.shape[0] == 1 and w_in_c.shape[0] == 1
    assert a_log_ab.shape == (1, H_A) and conv_w_ab.shape == (1, CONV_W, QKV_A)
    assert w_uq_ab.shape[1:] == (Q_LORA, H_B, D_NOPE + D_ROPE) and w_uk_ab.shape[1:] == (KV_LORA, H_B, D_NOPE)
    assert cache_mla_latent.shape[2] == PAGE and x_prompt.shape[1] >= CONV_W - 1 and x_sample.shape[1] >= CONV_W - 1
    w_uk, w_uv = w_uk_ab[0], w_uv_ab[0]
    wts = {
        "mix_norm": mix_norm, "mlp_norm": mlp_norm, "final_norm": final_norm,
        "w_up": w_up.astype(BF16), "w_down": w_down.astype(BF16),
        "w_in_ab": _layout_w_in_ab(w_in_ab[0]), "conv_w": conv_w_ab[0], "a_log": a_log_ab[0],
        "dt_bias": dt_bias_ab[0], "gdn_norm": gdn_norm_ab[0], "q_norm": q_norm_ab[0], "kv_norm": kv_norm_ab[0],
        "w_uq": _layout_w_uq(w_uq_ab[0]),
        "w_kv": jnp.concatenate([w_uk, w_uv], axis=-1).reshape(KV_LORA, H_B * 128).astype(BF16),
        "w_ukt": jnp.transpose(w_uk, (1, 2, 0)).astype(BF16),
        "w_uv": jnp.transpose(w_uv, (1, 0, 2)).astype(BF16),
        "w_out_a": w_out_ab[0, :H_A * DV_A].astype(BF16), "w_out_b": w_out_ab[0, H_A * DV_A:].astype(BF16),
        "w_in_c": w_in_c[0].astype(BF16), "lb_logits": lb_logits_c, "g_norm_c": g_norm_c[0],
        "w_out_c": w_out_c[0].astype(BF16),
    }
    s_len = x_prompt.shape[1]
    outs_p = _trunk(x_prompt, jnp.arange(s_len), None, None, None, None, wts)
    past_len = page_table.shape[1] * cache_mla_latent.shape[2]
    outs_s = _trunk(x_sample, past_len + jnp.arange(x_sample.shape[1]),
                    state_gdn_conv.reshape(state_gdn_conv.shape[1:]), state_gdn.reshape(state_gdn.shape[1:]),
                    state_hgrn.reshape(state_hgrn.shape[1:]),
                    (cache_mla_latent.reshape(cache_mla_latent.shape[1:]),
                     cache_mla_krope.reshape(cache_mla_krope.shape[1:]), page_table), wts)
    return (outs_p[0], outs_s[0]) + outs_p[1:] + outs_s[1:]
```

```python
import functools
import math

import jax
import jax.numpy as jnp
from jax import lax
from jax.experimental import pallas as pl
from jax.experimental.pallas import tpu as pltpu

F32 = jnp.float32
BF16 = jnp.bfloat16
HI = lax.Precision.HIGHEST
EPS = 1e-6
NEG_INF = float("-inf")

H_A, DK_A, DV_A, CONV_W = 4, 128, 128, 4
QKV_A = H_A * (2 * DK_A + DV_A)
H_B, D_NOPE, D_ROPE, DV_B = 8, 64, 32, 64
Q_LORA, KV_LORA = 256, 256
ROPE_BASE = 10000.0
MLA_SCALE = (D_NOPE + D_ROPE) ** -0.5
H_C, DK_C, DV_C = 8, 128, 128
CHUNK = 64
SUB = 16
PAGE = 128
QW = 256
IN_AB_PAD = 2816
V7X_VMEM_LIMIT = 48 * 1024 * 1024


def _cparams(sem):
    return pltpu.CompilerParams(dimension_semantics=sem, vmem_limit_bytes=V7X_VMEM_LIMIT)


def _const_spec(shape):
    nd = len(shape)
    return pl.BlockSpec(shape, lambda *_: (0,) * nd)


def _dot(a, b):
    return jnp.dot(a.astype(BF16), b.astype(BF16), preferred_element_type=F32)


def _dot_nt(a, b):
    return lax.dot_general(a.astype(BF16), b.astype(BF16), (((1,), (1,)), ((), ())),
                           preferred_element_type=F32)


def _dot_tn(a, b):
    return lax.dot_general(a.astype(BF16), b.astype(BF16), (((0,), (0,)), ((), ())),
                           preferred_element_type=F32)


def _dot_nt_f32(a, b):
    return lax.dot_general(a, b, (((1,), (1,)), ((), ())), precision=HI, preferred_element_type=F32)


def _rms(x, g):
    return x * lax.rsqrt(jnp.mean(x * x, axis=-1, keepdims=True) + EPS) * g


def _silu(x):
    return x * jax.nn.sigmoid(x)


def _cumsum_rows(x):
    n = x.shape[0]
    row = lax.broadcasted_iota(jnp.int32, x.shape, 0)
    s = 1
    while s < n:
        x = x + jnp.where(row >= s, pltpu.roll(x, s, axis=0), 0.0)
        s *= 2
    return x


def _proj_kernel(x_ref, g_ref, w_ref, o_ref):
    h = _rms(x_ref[...], g_ref[...]).astype(BF16)
    o_ref[...] = jnp.dot(h, w_ref[...], preferred_element_type=F32)


def _proj(x2, g, w, tm):
    n, d = x2.shape
    nout = w.shape[1]
    return pl.pallas_call(
        _proj_kernel,
        grid=(n // tm,),
        in_specs=[pl.BlockSpec((tm, d), lambda i: (i, 0)), _const_spec((1, d)), _const_spec((d, nout))],
        out_specs=pl.BlockSpec((tm, nout), lambda i: (i, 0)),
        out_shape=jax.ShapeDtypeStruct((n, nout), F32),
        compiler_params=_cparams(("parallel",)),
        name="proj",
    )(x2, g.reshape(1, d), w)


def _outproj_kernel(*refs, n_in):
    r_ref = refs[0]
    a_refs = refs[1:1 + n_in]
    w_refs = refs[1 + n_in:1 + 2 * n_in]
    o_ref = refs[1 + 2 * n_in]
    acc = r_ref[...]
    for a_ref, w_ref in zip(a_refs, w_refs):
        acc = acc + jnp.dot(a_ref[...], w_ref[...], preferred_element_type=F32)
    o_ref[...] = acc


def _outproj(res, acts, ws, tm):
    n, d = res.shape
    n_in = len(acts)
    in_specs = [pl.BlockSpec((tm, d), lambda i: (i, 0))]
    in_specs += [pl.BlockSpec((tm, a.shape[1]), lambda i: (i, 0)) for a in acts]
    in_specs += [_const_spec(w.shape) for w in ws]
    return pl.pallas_call(
        functools.partial(_outproj_kernel, n_in=n_in),
        grid=(n // tm,),
        in_specs=in_specs,
        out_specs=pl.BlockSpec((tm, d), lambda i: (i, 0)),
        out_shape=jax.ShapeDtypeStruct((n, d), F32),
        compiler_params=_cparams(("parallel",)),
        name="outproj",
    )(res, *acts, *ws)


def _mlp_kernel(x_ref, g_ref, gf_ref, wu_ref, wd_ref, o_ref, h_ref, *, final_norm):
    j = pl.program_id(1)

    @pl.when(j == 0)
    def _():
        x = x_ref[...]
        h_ref[...] = _rms(x, g_ref[...]).astype(BF16)
        o_ref[...] = x

    a = jnp.dot(h_ref[...], wu_ref[...], preferred_element_type=F32)
    a = jnp.square(jnp.maximum(a, 0.0))
    o_ref[...] += jnp.dot(a.astype(BF16), wd_ref[...], preferred_element_type=F32)

    if final_norm:
        @pl.when(j == pl.num_programs(1) - 1)
        def _():
            o_ref[...] = _rms(o_ref[...], gf_ref[...])


def _mlp(x2, g, w_up, w_down, gf, final_norm, tm, tf):
    n, d = x2.shape
    ff = w_up.shape[1]
    return pl.pallas_call(
        functools.partial(_mlp_kernel, final_norm=final_norm),
        grid=(n // tm, ff // tf),
        in_specs=[pl.BlockSpec((tm, d), lambda i, j: (i, 0)), _const_spec((1, d)), _const_spec((1, d)),
                  pl.BlockSpec((d, tf), lambda i, j: (0, j)), pl.BlockSpec((tf, d), lambda i, j: (j, 0))],
        out_specs=pl.BlockSpec((tm, d), lambda i, j: (i, 0)),
        out_shape=jax.ShapeDtypeStruct((n, d), F32),
        scratch_shapes=[pltpu.VMEM((tm, d), BF16)],
        compiler_params=_cparams(("parallel", "arbitrary")),
        name="mlp",
    )(x2, g.reshape(1, d), gf.reshape(1, d), w_up, w_down)


def _dot_f32(a, b):
    return jnp.dot(a, b, precision=HI, preferred_element_type=F32)


def _split3_dot(a, b):
    a_hi = a.astype(BF16)
    a_lo = (a - a_hi.astype(F32)).astype(BF16)
    b_hi = b.astype(BF16)
    b_lo = (b - b_hi.astype(F32)).astype(BF16)
    return jnp.dot(jnp.concatenate([a_hi, a_lo, a_hi], axis=1), jnp.concatenate([b_hi, b_hi, b_lo], axis=0),
                   preferred_element_type=F32)


def _solve_unit_lower(lowers, rhss, c):
    nh = len(lowers)
    s = 8
    nb = c // s
    w = nh * c
    row = lax.broadcasted_iota(jnp.int32, (c, w), 0)
    col = lax.broadcasted_iota(jnp.int32, (c, w), 1) % c
    if nb == 1:
        diag_cols = lowers
    else:
        r1 = lax.broadcasted_iota(jnp.int32, (c, c), 0)
        c1 = lax.broadcasted_iota(jnp.int32, (c, c), 1)
        pick = (lax.broadcasted_iota(jnp.int32, (c, s), 0) % s
                == lax.broadcasted_iota(jnp.int32, (c, s), 1)).astype(F32)
        diag_cols = [_dot_f32(jnp.where(r1 // s == c1 // s, lo, 0.0), pick) for lo in lowers]
    x = (row == col).astype(F32)
    for j in range(s - 1):
        mult = jnp.concatenate([jnp.broadcast_to(dc[:, j:j + 1], (c, c)) for dc in diag_cols], axis=1)
        pivot_rows = jnp.broadcast_to(x.reshape(nb, s, w)[:, j:j + 1, :], (nb, s, w)).reshape(c, w)
        x = x - mult * pivot_rows

    def block_diag(m):
        r2 = lax.broadcasted_iota(jnp.int32, (w, w), 0)
        c2 = lax.broadcasted_iota(jnp.int32, (w, w), 1)
        return jnp.where(r2 // c == c2 // c, jnp.concatenate([m] * nh, axis=0), 0.0)

    lower_cat = jnp.concatenate(lowers, axis=1)
    size = s
    while size < c:
        off = jnp.where((row // (2 * size) == col // (2 * size)) & (row // size != col // size) & (row > col),
                        lower_cat, 0.0)
        x = x - _split3_dot(_split3_dot(x, block_diag(off)), block_diag(x))
        size *= 2
    out = _split3_dot(block_diag(x), jnp.concatenate(rhss, axis=0))
    return [out[h * c:(h + 1) * c] for h in range(nh)]


def _gdn_head_pre(q, k, v, b_col, a_col, alog, dtb, c):
    q = q * lax.rsqrt(jnp.sum(q * q, axis=-1, keepdims=True) + EPS) * (DK_A ** -0.5)
    k = k * lax.rsqrt(jnp.sum(k * k, axis=-1, keepdims=True) + EPS)
    beta = jax.nn.sigmoid(b_col)
    g = -jnp.exp(alog) * jax.nn.softplus(a_col + dtb)
    gc_b = _cumsum_rows(jnp.broadcast_to(g, (c, 128)))
    gc = gc_b[:, 0:1]
    diff = gc_b[:, :c] - gc_b.T[:c, :]
    row = lax.broadcasted_iota(jnp.int32, (c, c), 0)
    col = lax.broadcasted_iota(jnp.int32, (c, c), 1)
    incl = row >= col
    decay = jnp.where(incl, jnp.exp(jnp.where(incl, diff, 0.0)), 0.0)
    kb = k * beta
    lower = jnp.where(row > col, _dot_nt_f32(kb, k) * decay, 0.0)
    rhs = jnp.concatenate([v * beta, kb * jnp.exp(gc)], axis=1)
    intra = _dot_nt_f32(q, k) * decay
    return q, k, gc, lower, rhs, intra


def _gdn_head_post(q, k, gc, uw, intra, z, gnorm, s_prev, c):
    u, w = uw[:, :DV_A], uw[:, DV_A:]
    v_new = u - _dot(w, s_prev)
    o = _dot(q * jnp.exp(gc), s_prev) + _dot(intra, v_new)
    g_last = gc[c - 1:c, :]
    s_new = s_prev * jnp.exp(g_last) + _dot_tn(k * jnp.exp(g_last - gc), v_new)
    o = _rms(o, gnorm) * _silu(z)
    return o, s_new


def _gdn_kernel(*refs, c, n_chunks, has_state):
    (q_ref, k_ref, v_ref, z_ref, gt_ref, cwq_ref, cwk_ref, cwv_ref, alog_ref, dtb_ref, gn_ref) = refs[:11]
    if has_state:
        cq_ref, ck_ref, cv_ref, s0_ref = refs[11:15]
        o_ref, sout_ref, s_scr, tail_scr = refs[15:]
    else:
        o_ref, sout_ref, s_scr, tail_scr = refs[11:]
    n = pl.program_id(1)

    @pl.when(n == 0)
    def _():
        if has_state:
            s_scr[...] = s0_ref[0]
            tail_scr[0] = cq_ref[0]
            tail_scr[1] = ck_ref[0]
            tail_scr[2] = cv_ref[0]
        else:
            s_scr[...] = jnp.zeros_like(s_scr)
            tail_scr[...] = jnp.zeros_like(tail_scr)

    def chunk(ci, carry):
        r0 = ci * c if isinstance(ci, int) else pl.multiple_of(ci * c, c)
        rows = pl.ds(r0, c)
        conv = []
        for idx, (x_ref, cw_ref) in enumerate(((q_ref, cwq_ref), (k_ref, cwk_ref), (v_ref, cwv_ref))):
            x = x_ref[0, rows, :]
            ext = jnp.concatenate([tail_scr[idx], x], axis=0)
            cw = cw_ref[...]
            y = x * cw[CONV_W - 1:CONV_W, :]
            for sh in range(1, CONV_W):
                y = y + pltpu.roll(ext, sh, axis=0)[8:] * cw[CONV_W - 1 - sh:CONV_W - sh, :]
            tail_scr[idx] = x[c - 8:, :]
            conv.append(_silu(y))
        z = z_ref[0, rows, :]
        gt = gt_ref[0, rows, :]
        pre = []
        for h in range(H_A):
            hs = slice(h * 128, (h + 1) * 128)
            pre.append(_gdn_head_pre(conv[0][:, hs], conv[1][:, hs], conv[2][:, hs],
                                     gt[:, h:h + 1], gt[:, H_A + h:H_A + h + 1],
                                     alog_ref[:, h:h + 1], dtb_ref[:, h:h + 1], c))
        uws = _solve_unit_lower([p[3] for p in pre], [p[4] for p in pre], c)
        outs = []
        for h in range(H_A):
            q, k, gc, _, _, intra = pre[h]
            o, s_new = _gdn_head_post(q, k, gc, uws[h], intra, z[:, h * 128:(h + 1) * 128], gn_ref[...],
                                      s_scr[h], c)
            s_scr[h] = s_new
            outs.append(o)
        o_ref[0, rows, :] = jnp.concatenate(outs, axis=1).astype(o_ref.dtype)
        return carry

    if n_chunks == 1:
        chunk(0, 0)
    else:
        lax.fori_loop(0, n_chunks, chunk, 0)

    @pl.when(n == pl.num_programs(1) - 1)
    def _():
        sout_ref[0] = s_scr[...]


def _gdn(p3, conv_w, a_log, dt_bias, gdn_norm, conv0, s0, tb):
    b, t, _ = p3.shape
    c = math.gcd(t, CHUNK)
    has_state = s0 is not None
    hw = H_A * 128
    cw = conv_w
    blk = lambda j: pl.BlockSpec((1, tb, hw), lambda i, n: (i, n, j))
    in_specs = [blk(0), blk(1), blk(2), blk(3),
                pl.BlockSpec((1, tb, 128), lambda i, n: (i, n, (IN_AB_PAD - 128) // 128)),
                pl.BlockSpec((CONV_W, hw), lambda i, n: (0, 0)), pl.BlockSpec((CONV_W, hw), lambda i, n: (0, 1)),
                pl.BlockSpec((CONV_W, hw), lambda i, n: (0, 2)),
                _const_spec((1, H_A)), _const_spec((1, H_A)), _const_spec((1, DV_A))]
    args = [p3, p3, p3, p3, p3, cw, cw, cw, a_log.reshape(1, H_A), dt_bias.reshape(1, H_A),
            gdn_norm.reshape(1, DV_A)]
    if has_state:
        conv8 = jnp.pad(conv0, ((0, 0), (8 - (CONV_W - 1), 0), (0, 0)))
        in_specs += [pl.BlockSpec((1, 8, hw), lambda i, n, j=j: (i, 0, j)) for j in range(3)]
        in_specs += [pl.BlockSpec((1, H_A, DK_A, DV_A), lambda i, n: (i, 0, 0, 0))]
        args += [conv8, conv8, conv8, s0]
    return pl.pallas_call(
        functools.partial(_gdn_kernel, c=c, n_chunks=tb // c, has_state=has_state),
        grid=(b, t // tb),
        in_specs=in_specs,
        out_specs=[pl.BlockSpec((1, tb, hw), lambda i, n: (i, n, 0)),
                   pl.BlockSpec((1, H_A, DK_A, DV_A), lambda i, n: (i, 0, 0, 0))],
        out_shape=[jax.ShapeDtypeStruct((b, t, hw), BF16),
                   jax.ShapeDtypeStruct((b, H_A, DK_A, DV_A), F32)],
        scratch_shapes=[pltpu.VMEM((H_A, DK_A, DV_A), F32), pltpu.VMEM((3, 8, hw), F32)],
        compiler_params=_cparams(("parallel", "arbitrary")),
        name="gdn",
    )(*args)


def _hgrn_chunk(qr, fr, v, lb, s_prev, c):
    sub = min(SUB, c)
    q = _silu(qr) * (DK_C ** -0.5)
    f = lb + (1.0 - lb) * jax.nn.sigmoid(fr)
    k = 1.0 - f
    gc = _cumsum_rows(jnp.log(f))
    o_inter = _dot(q * jnp.exp(gc), s_prev)
    row = lax.broadcasted_iota(jnp.int32, (sub, sub), 0)
    col = lax.broadcasted_iota(jnp.int32, (sub, sub), 1)
    row_d = lax.broadcasted_iota(jnp.int32, (sub, 128), 0)
    outs = []
    for blk in range(c // sub):
        r0 = blk * sub
        g_i = gc[r0:r0 + sub]
        q_i = q[r0:r0 + sub]
        a_diag = jnp.zeros((sub, sub), F32)
        for j in range(sub):
            jj = r0 + j
            e = jnp.exp(jnp.where(row_d >= j, g_i - gc[jj:jj + 1, :], 0.0))
            col_j = jnp.sum(q_i * k[jj:jj + 1, :] * e, axis=-1, keepdims=True)
            a_diag = jnp.where(col == j, col_j, a_diag)
        a_diag = jnp.where(row >= col, a_diag, 0.0)
        o_i = _dot(a_diag, v[r0:r0 + sub])
        if blk > 0:
            g_ref = gc[r0 - 1:r0, :]
            a_off = _dot_nt(q_i * jnp.exp(g_i - g_ref), k[:r0] * jnp.exp(g_ref - gc[:r0]))
            o_i = o_i + _dot(a_off, v[:r0])
        outs.append(o_i)
    o = o_inter + (jnp.concatenate(outs, axis=0) if len(outs) > 1 else outs[0])
    g_last = gc[c - 1:c, :]
    gl_col = gc.T[:, c - 1:c]
    s_new = s_prev * jnp.exp(gl_col) + _dot_tn(k * jnp.exp(g_last - gc), v)
    return o, s_new


def _hgrn_kernel(*refs, c, n_chunks, layer, has_state, hg):
    q_ref, f_ref, i_ref, gate_ref, lbl_ref, gn_ref = refs[:6]
    if has_state:
        s0_ref = refs[6]
        o_ref, sout_ref, s_scr = refs[7:]
    else:
        o_ref, sout_ref, s_scr = refs[6:]
    n = pl.program_id(2)

    @pl.when(n == 0)
    def _():
        if has_state:
            s_scr[...] = s0_ref[0]
        else:
            s_scr[...] = jnp.zeros_like(s_scr)

    lbl = lbl_ref[...]
    ex = jnp.exp(lbl - jnp.max(lbl, axis=0, keepdims=True))
    p = ex / jnp.sum(ex, axis=0, keepdims=True)
    lb = jnp.sum(p[:layer + 1], axis=0, keepdims=True) - p[0:1]

    def chunk(ci, carry):
        r0 = ci * c if isinstance(ci, int) else pl.multiple_of(ci * c, c)
        rows = pl.ds(r0, c)
        for h in range(hg):
            hs = slice(h * 128, (h + 1) * 128)
            o, s_new = _hgrn_chunk(q_ref[0, rows, hs], f_ref[0, rows, hs], i_ref[0, rows, hs], lb[:, hs],
                                   s_scr[h], c)
            s_scr[h] = s_new
            o_ref[0, rows, hs] = (_rms(o, gn_ref[...]) * _silu(gate_ref[0, rows, hs])).astype(o_ref.dtype)
        return carry

    if n_chunks == 1:
        chunk(0, 0)
    else:
        lax.fori_loop(0, n_chunks, chunk, 0)

    @pl.when(n == pl.num_programs(2) - 1)
    def _():
        sout_ref[0] = s_scr[...]


def _hgrn(pc3, lb_logits, layer, g_norm, s0, tb, hg):
    b, t, _ = pc3.shape
    c = math.gcd(t, CHUNK)
    has_state = s0 is not None
    depth = lb_logits.shape[0]
    ng = H_C // hg
    blk = lambda j: pl.BlockSpec((1, tb, hg * 128), lambda i, h, n: (i, n, j * ng + h))
    in_specs = [blk(0), blk(1), blk(2), blk(3),
                pl.BlockSpec((depth, hg * 128), lambda i, h, n: (0, h)), _const_spec((1, DV_C))]
    args = [pc3, pc3, pc3, pc3, lb_logits, g_norm.reshape(1, DV_C)]
    if has_state:
        in_specs.append(pl.BlockSpec((1, hg, DK_C, DV_C), lambda i, h, n: (i, h, 0, 0)))
        args.append(s0)
    return pl.pallas_call(
        functools.partial(_hgrn_kernel, c=c, n_chunks=tb // c, layer=layer, has_state=has_state, hg=hg),
        grid=(b, ng, t // tb),
        in_specs=in_specs,
        out_specs=[pl.BlockSpec((1, tb, hg * 128), lambda i, h, n: (i, n, h)),
                   pl.BlockSpec((1, hg, DK_C, DV_C), lambda i, h, n: (i, h, 0, 0))],
        out_shape=[jax.ShapeDtypeStruct((b, t, H_C * DV_C), BF16),
                   jax.ShapeDtypeStruct((b, H_C, DK_C, DV_C), F32)],
        scratch_shapes=[pltpu.VMEM((hg, DK_C, DV_C), F32)],
        compiler_params=_cparams(("parallel", "parallel", "arbitrary")),
        name="hgrn",
    )(*args)


def _mla_prep_kernel(*refs, absorb):
    cq_ref, ckv_ref, kp_ref, qn_ref, kvn_ref, tq_ref, tk_ref, wq_ref = refs[:8]
    if absorb:
        wukt_ref, lat_ref, kr_ref, qlat_ref, qpe_ref = refs[8:]
    else:
        wkv_ref, lat_ref, kr_ref, q_ref, kv_ref, kp4_ref = refs[8:]
    lat = _rms(ckv_ref[...], kvn_ref[...])
    lat_ref[...] = lat
    kp4 = kp_ref[...] * tk_ref[...]
    kr_ref[...] = kp4[:, 0:D_ROPE] + kp4[:, D_ROPE:2 * D_ROPE]
    cqn = _rms(cq_ref[...], qn_ref[...]).astype(BF16)
    tq = tq_ref[...]
    if absorb:
        for h in range(H_B):
            qh = jnp.dot(cqn, wq_ref[:, h * QW:(h + 1) * QW], preferred_element_type=F32) * tq
            qlat_ref[:, h * KV_LORA:(h + 1) * KV_LORA] = _dot(qh[:, 0:D_NOPE], wukt_ref[h])
            qpe_ref[:, h * D_ROPE:(h + 1) * D_ROPE] = (qh[:, 128:128 + D_ROPE]
                                                      + qh[:, 128 + 2 * D_ROPE:128 + 3 * D_ROPE])
    else:
        for h in range(H_B):
            qh = jnp.dot(cqn, wq_ref[:, h * QW:(h + 1) * QW], preferred_element_type=F32) * tq
            q_ref[:, h * QW:(h + 1) * QW] = qh.astype(BF16)
        kv_ref[...] = jnp.dot(lat.astype(BF16), wkv_ref[...], preferred_element_type=F32).astype(BF16)
        kp4_ref[...] = kp4.astype(BF16)


def _mla_prep(p2, q_norm, kv_norm, tabq, tabk, wq, w_extra, absorb, tm):
    n = p2.shape[0]
    nt = tabq.shape[0] // tm
    in_specs = [pl.BlockSpec((tm, Q_LORA), lambda i: (i, 2048 // 256)),
                pl.BlockSpec((tm, KV_LORA), lambda i: (i, 2304 // 256)),
                pl.BlockSpec((tm, 128), lambda i: (i, 2560 // 128)),
                _const_spec((1, Q_LORA)), _const_spec((1, KV_LORA)),
                pl.BlockSpec((tm, QW), lambda i: (i % nt, 0)), pl.BlockSpec((tm, 128), lambda i: (i % nt, 0)),
                _const_spec(wq.shape), _const_spec(w_extra.shape)]
    row = lambda w: pl.BlockSpec((tm, w), lambda i: (i, 0))
    out_specs = [row(KV_LORA), row(D_ROPE)]
    out_shape = [jax.ShapeDtypeStruct((n, KV_LORA), F32), jax.ShapeDtypeStruct((n, D_ROPE), F32)]
    if absorb:
        out_specs += [row(H_B * KV_LORA), row(H_B * D_ROPE)]
        out_shape += [jax.ShapeDtypeStruct((n, H_B * KV_LORA), F32), jax.ShapeDtypeStruct((n, H_B * D_ROPE), F32)]
    else:
        out_specs += [row(H_B * QW), row(H_B * 128), row(128)]
        out_shape += [jax.ShapeDtypeStruct((n, H_B * QW), BF16), jax.ShapeDtypeStruct((n, H_B * 128), BF16),
                      jax.ShapeDtypeStruct((n, 128), BF16)]
    return pl.pallas_call(
        functools.partial(_mla_prep_kernel, absorb=absorb),
        grid=(n // tm,),
        in_specs=in_specs, out_specs=out_specs, out_shape=out_shape,
        compiler_params=_cparams(("parallel",)),
        name="mla_prep",
    )(p2, p2, p2, q_norm.reshape(1, Q_LORA), kv_norm.reshape(1, KV_LORA), tabq, tabk, wq, w_extra)


def _flash_kernel(q_ref, kv_ref, kp_ref, o_ref, m_scr, l_scr, acc_scr, *, tq):
    qi = pl.program_id(1)
    row = lax.broadcasted_iota(jnp.int32, (tq, tq), 0)
    col = lax.broadcasted_iota(jnp.int32, (tq, tq), 1)
    m_scr[...] = jnp.full_like(m_scr, NEG_INF)
    l_scr[...] = jnp.zeros_like(l_scr)
    acc_scr[...] = jnp.zeros_like(acc_scr)

    def key_tile(k0, masked):
        kp = kp_ref[0, pl.ds(k0, tq), :]
        for h in range(H_B):
            kvh = kv_ref[0, pl.ds(k0, tq), h * 128:(h + 1) * 128]
            keys = jnp.concatenate([kvh, kp], axis=1)
            s = lax.dot_general(q_ref[0, :, h * QW:(h + 1) * QW], keys, (((1,), (1,)), ((), ())),
                                preferred_element_type=F32)
            if masked:
                s = jnp.where(row >= col, s, NEG_INF)
            m = m_scr[h]
            m_new = jnp.maximum(m, jnp.max(s, axis=-1, keepdims=True))
            alpha = jnp.exp(m - m_new)
            p = jnp.exp(s - m_new)
            l_scr[h] = alpha * l_scr[h] + jnp.sum(p, axis=-1, keepdims=True)
            acc_scr[h] = alpha * acc_scr[h] + jnp.dot(p.astype(BF16), kvh, preferred_element_type=F32)
            m_scr[h] = m_new

    def body(kt, carry):
        key_tile(pl.multiple_of(kt * tq, tq), False)
        return carry

    lax.fori_loop(0, qi, body, 0)
    key_tile(pl.multiple_of(qi * tq, tq), True)
    outs = [(acc_scr[h] / l_scr[h])[:, D_NOPE:D_NOPE + DV_B] for h in range(H_B)]
    o_ref[0] = jnp.concatenate(outs, axis=1).astype(o_ref.dtype)


def _flash(q3, kv3, kp3, tq):
    b, t, _ = q3.shape
    return pl.pallas_call(
        functools.partial(_flash_kernel, tq=tq),
        grid=(b, t // tq),
        in_specs=[pl.BlockSpec((1, tq, H_B * QW), lambda i, j: (i, j, 0)),
                  pl.BlockSpec((1, t, H_B * 128), lambda i, j: (i, 0, 0)),
                  pl.BlockSpec((1, t, 128), lambda i, j: (i, 0, 0))],
        out_specs=pl.BlockSpec((1, tq, H_B * DV_B), lambda i, j: (i, j, 0)),
        out_shape=jax.ShapeDtypeStruct((b, t, H_B * DV_B), BF16),
        scratch_shapes=[pltpu.VMEM((H_B, tq, 1), F32), pltpu.VMEM((H_B, tq, 1), F32),
                        pltpu.VMEM((H_B, tq, 128), F32)],
        compiler_params=_cparams(("parallel", "arbitrary")),
        name="mla_flash",
    )(q3, kv3, kp3)


def _decode_kernel(pt_ref, qlat_ref, qpe_ref, latn_ref, krn_ref, wuv_ref, *refs, pp, grp, t_new):
    lat_refs = refs[:pp]
    kct_refs = refs[pp:2 * pp]
    o_ref = refs[2 * pp]
    m_scr, l_scr, acc_scr = refs[2 * pp + 1:]
    g = pl.program_id(1)
    nq = H_B * t_new

    @pl.when(g == 0)
    def _():
        m_scr[...] = jnp.full_like(m_scr, NEG_INF)
        l_scr[...] = jnp.zeros_like(l_scr)
        acc_scr[...] = jnp.zeros_like(acc_scr)

    ql = jnp.concatenate([qlat_ref[0, :, h * KV_LORA:(h + 1) * KV_LORA] for h in range(H_B)], axis=0).astype(BF16)
    qp = jnp.concatenate([qpe_ref[0, :, h * D_ROPE:(h + 1) * D_ROPE] for h in range(H_B)], axis=0).astype(BF16)

    def partial_softmax(s, values):
        m = jnp.max(s, axis=-1, keepdims=True)
        p = jnp.exp(s - m)
        w = 0
        pv = None
        for val in values:
            d = jnp.dot(p[:, w:w + val.shape[0]].astype(BF16), val, preferred_element_type=F32)
            pv = d if pv is None else pv + d
            w += val.shape[0]
        return m, jnp.sum(p, axis=-1, keepdims=True), pv

    def merge(parts):
        m_old = m_scr[...]
        m_new = m_old
        for m, _, _ in parts:
            m_new = jnp.maximum(m_new, m)
        alpha = jnp.exp(m_old - m_new)
        l = alpha * l_scr[...]
        acc = alpha * acc_scr[...]
        for m, l_g, pv in parts:
            w_g = jnp.exp(m - m_new)
            l = l + w_g * l_g
            acc = acc + w_g * pv
        m_scr[...] = m_new
        l_scr[...] = l
        acc_scr[...] = acc

    parts = []
    for g0 in range(0, pp, grp):
        lats = [lat_refs[i][0].astype(BF16) for i in range(g0, g0 + grp)]
        s_parts = [lax.dot_general(ql, lats[i], (((1,), (1,)), ((), ())), preferred_element_type=F32)
                   + jnp.dot(qp, kct_refs[g0 + i][0].astype(BF16), preferred_element_type=F32)
                   for i in range(grp)]
        parts.append(partial_softmax(jnp.concatenate(s_parts, axis=1) if grp > 1 else s_parts[0], lats))
    merge(parts)

    @pl.when(g == pl.num_programs(1) - 1)
    def _():
        latn = latn_ref[0].astype(BF16)
        krn = krn_ref[0].astype(BF16)
        s = (lax.dot_general(ql, latn, (((1,), (1,)), ((), ())), preferred_element_type=F32)
             + lax.dot_general(qp, krn, (((1,), (1,)), ((), ())), preferred_element_type=F32))
        tok = lax.broadcasted_iota(jnp.int32, (nq, t_new), 0) % t_new
        key = lax.broadcasted_iota(jnp.int32, (nq, t_new), 1)
        merge([partial_softmax(jnp.where(key <= tok, s, NEG_INF), [latn])])
        o_lat = acc_scr[...] / l_scr[...]
        outs = [_dot(o_lat[h * t_new:(h + 1) * t_new], wuv_ref[h]) for h in range(H_B)]
        o_ref[0] = jnp.concatenate(outs, axis=1).astype(o_ref.dtype)


def _decode(page_table, qlat3, qpe3, latn3, krn3, wuv, cache_lat, cache_kpe_t, pp, grp):
    b, t_new, _ = qlat3.shape
    n_pages = page_table.shape[1]
    in_specs = [pl.BlockSpec((1, t_new, H_B * KV_LORA), lambda i, g, pt: (i, 0, 0)),
                pl.BlockSpec((1, t_new, H_B * D_ROPE), lambda i, g, pt: (i, 0, 0)),
                pl.BlockSpec((1, t_new, KV_LORA), lambda i, g, pt: (i, 0, 0)),
                pl.BlockSpec((1, t_new, D_ROPE), lambda i, g, pt: (i, 0, 0)),
                pl.BlockSpec(wuv.shape, lambda i, g, pt: (0, 0, 0))]
    in_specs += [pl.BlockSpec((1, PAGE, KV_LORA), lambda i, g, pt, j=j: (pt[i, g * pp + j], 0, 0)) for j in range(pp)]
    in_specs += [pl.BlockSpec((1, D_ROPE, PAGE), lambda i, g, pt, j=j: (pt[i, g * pp + j], 0, 0)) for j in range(pp)]
    nq = H_B * t_new
    return pl.pallas_call(
        functools.partial(_decode_kernel, pp=pp, grp=grp, t_new=t_new),
        grid_spec=pltpu.PrefetchScalarGridSpec(
            num_scalar_prefetch=1,
            grid=(b, n_pages // pp),
            in_specs=in_specs,
            out_specs=pl.BlockSpec((1, t_new, H_B * DV_B), lambda i, g, pt: (i, 0, 0)),
            scratch_shapes=[pltpu.VMEM((nq, 1), F32), pltpu.VMEM((nq, 1), F32), pltpu.VMEM((nq, KV_LORA), F32)]),
        out_shape=jax.ShapeDtypeStruct((b, t_new, H_B * DV_B), BF16),
        compiler_params=_cparams(("parallel", "arbitrary")),
        name="mla_decode",
    )(page_table, qlat3, qpe3, latn3, krn3, wuv, *([cache_lat] * pp), *([cache_kpe_t] * pp))


def _rot_cols(w):
    half = D_ROPE // 2
    return jnp.concatenate([-w[..., half:], w[..., :half]], axis=-1)


def _layout_w_in_ab(w):
    d = w.shape[0]
    o = QKV_A + H_A * DV_A
    qkv_z, b_w, a_w = w[:, :o], w[:, o:o + H_A], w[:, o + H_A:o + 2 * H_A]
    o += 2 * H_A
    cq, ckv, kpe = w[:, o:o + Q_LORA], w[:, o + Q_LORA:o + Q_LORA + KV_LORA], w[:, o + Q_LORA + KV_LORA:]
    kr = _rot_cols(kpe)
    pad = jnp.zeros((d, 128 - 2 * H_A), w.dtype)
    return jnp.concatenate([qkv_z, cq, ckv, kpe, kr, kpe, kr, b_w, a_w, pad], axis=1).astype(BF16)


def _layout_w_uq(w_uq):
    nope, pe = w_uq[..., :D_NOPE], w_uq[..., D_NOPE:]
    rot = _rot_cols(pe)
    z = jnp.zeros_like(nope)
    return jnp.concatenate([nope, z, pe, pe, rot, rot], axis=-1).reshape(w_uq.shape[0], H_B * QW).astype(BF16)


def _rope_tables(pos):
    half = D_ROPE // 2
    inv = ROPE_BASE ** (-jnp.arange(half, dtype=F32) / half)
    ang = pos.astype(F32)[:, None] * inv
    c = jnp.concatenate([jnp.cos(ang), jnp.cos(ang)], axis=1)
    s = jnp.concatenate([jnp.sin(ang), jnp.sin(ang)], axis=1)
    one = jnp.ones((pos.shape[0], 128), F32)
    tabq = jnp.concatenate([one, c, c, s, s], axis=1) * MLA_SCALE
    tabk = jnp.concatenate([c, s, c, s], axis=1)
    return tabq, tabk


def _row_tile(n):
    for tm in (512, 256, 128, 64, 32, 16, 8):
        if n % tm == 0:
            return tm
    raise ValueError(n)


def _trunk(x, pos, conv0, gdn0, hgrn0, paged, wts):
    b, t, d = x.shape
    n = b * t
    tm = min(256, _row_tile(n))
    tb = min(256, t)
    x2 = x.reshape(n, d)

    p2 = _proj(x2, wts["mix_norm"][0], wts["w_in_ab"], tm)
    p3 = p2.reshape(b, t, IN_AB_PAD)
    conv_new = p3[:, t - (CONV_W - 1):, :QKV_A]
    o_a, gdn_new = _gdn(p3, wts["conv_w"], wts["a_log"], wts["dt_bias"], wts["gdn_norm"], conv0, gdn0, tb)
    tabq, tabk = _rope_tables(pos)
    if paged is None:
        tmp = math.gcd(tm, t)
        lat, kr, q, kv, kp4 = _mla_prep(p2, wts["q_norm"], wts["kv_norm"], tabq, tabk, wts["w_uq"], wts["w_kv"],
                                        False, tmp)
        o_b = _flash(q.reshape(b, t, -1), kv.reshape(b, t, -1), kp4.reshape(b, t, -1), min(256, t))
    else:
        cache_lat, cache_kpe, page_table = paged
        reps = tm // t
        lat, kr, qlat, qpe = _mla_prep(p2, wts["q_norm"], wts["kv_norm"], jnp.tile(tabq, (reps, 1)),
                                       jnp.tile(tabk, (reps, 1)), wts["w_uq"], wts["w_ukt"], True, tm)
        pp = math.gcd(page_table.shape[1], 16)
        grp = max(1, pp // 4) if pp >= 8 else max(1, pp // 2)
        o_b = _decode(page_table, qlat.reshape(b, t, -1), qpe.reshape(b, t, -1), lat.reshape(b, t, -1),
                      kr.reshape(b, t, -1), wts["w_uv"], cache_lat, jnp.swapaxes(cache_kpe, 1, 2), pp, grp)
    x2 = _outproj(x2, [o_a.reshape(n, -1), o_b.reshape(n, -1)], [wts["w_out_a"], wts["w_out_b"]], tm)
    tmm = min(512, _row_tile(n))
    x2 = _mlp(x2, wts["mlp_norm"][0], wts["w_up"][0], wts["w_down"][0], wts["final_norm"], False, tmm, 1024)

    pc = _proj(x2, wts["mix_norm"][1], wts["w_in_c"], tm)
    o_c, hgrn_new = _hgrn(pc.reshape(b, t, -1), wts["lb_logits"], 1, wts["g_norm_c"], hgrn0, tb,
                          H_C if hgrn0 is not None else 2)
    x2 = _outproj(x2, [o_c.reshape(n, -1)], [wts["w_out_c"]], tm)
    y2 = _mlp(x2, wts["mlp_norm"][1], wts["w_up"][1], wts["w_down"][1], wts["final_norm"], True, tmm, 1024)
    return (y2.reshape(b, t, d), gdn_new[None], conv_new[None], lat.reshape(b, t, -1)[None],
            kr.reshape(b, t, -1)[None], hgrn_new[None])


def kernel(x_prompt, x_sample, state_gdn, state_gdn_conv, cache_mla_latent, cache_mla_krope, state_hgrn,
           page_table, mix_norm, mlp_norm, final_norm, w_up, w_down, w_in_ab, conv_w_ab, a_log_ab, dt_bias_ab,
           gdn_norm_ab, q_norm_ab, w_uq_ab, kv_norm_ab, w_uk_ab, w_uv_ab, w_out_ab, w_in_c, lb_logits_c,
           g_norm_c, w_out_c):
    assert mix_norm.shape[0] == 2 and w_in_ab.shape[0] == 1 and w_in_c.shape[0] == 1
    assert a_log_ab.shape == (1, H_A) and conv_w_ab.shape == (1, CONV_W, QKV_A)
    assert w_uq_ab.shape[1:] == (Q_LORA, H_B, D_NOPE + D_ROPE) and w_uk_ab.shape[1:] == (KV_LORA, H_B, D_NOPE)
    assert cache_mla_latent.shape[2] == PAGE and x_prompt.shape[1] >= CONV_W - 1 and x_sample.shape[1] >= CONV_W - 1
    w_uk, w_uv = w_uk_ab[0], w_uv_ab[0]
    wts = {
        "mix_norm": mix_norm, "mlp_norm": mlp_norm, "final_norm": final_norm,
        "w_up": w_up.astype(BF16), "w_down": w_down.astype(BF16),
        "w_in_ab": _layout_w_in_ab(w_in_ab[0]), "conv_w": conv_w_ab[0], "a_log": a_log_ab[0],
        "dt_bias": dt_bias_ab[0], "gdn_norm": gdn_norm_ab[0], "q_norm": q_norm_ab[0], "kv_norm": kv_norm_ab[0],
        "w_uq": _layout_w_uq(w_uq_ab[0]),
        "w_kv": jnp.concatenate([w_uk, w_uv], axis=-1).reshape(KV_LORA, H_B * 128).astype(BF16),
        "w_ukt": jnp.transpose(w_uk, (1, 2, 0)).astype(BF16),
        "w_uv": jnp.transpose(w_uv, (1, 0, 2)).astype(BF16),
        "w_out_a": w_out_ab[0, :H_A * DV_A].astype(BF16), "w_out_b": w_out_ab[0, H_A * DV_A:].astype(BF16),
        "w_in_c": w_in_c[0].astype(BF16), "lb_logits": lb_logits_c, "g_norm_c": g_norm_c[0],
        "w_out_c": w_out_c[0].astype(BF16),
    }
    s_len = x_prompt.shape[1]
    outs_p = _trunk(x_prompt, jnp.arange(s_len), None, None, None, None, wts)
    past_len = page_table.shape[1] * cache_mla_latent.shape[2]
    outs_s = _trunk(x_sample, past_len + jnp.arange(x_sample.shape[1]),
                    state_gdn_conv.reshape(state_gdn_conv.shape[1:]), state_gdn.reshape(state_gdn.shape[1:]),
                    state_hgrn.reshape(state_hgrn.shape[1:]),
                    (cache_mla_latent.reshape(cache_mla_latent.shape[1:]),
                     cache_mla_krope.reshape(cache_mla_krope.shape[1:]), page_table), wts)
    return (outs_p[0], outs_s[0]) + outs_p[1:] + outs_s[1:]
```

```python
import functools
import math

import jax
import jax.numpy as jnp
from jax import lax
from jax.experimental import pallas as pl
from jax.experimental.pallas import tpu as pltpu

F32 = jnp.float32
BF16 = jnp.bfloat16
HI = lax.Precision.HIGHEST
EPS = 1e-6
NEG_INF = float("-inf")

H_A, DK_A, DV_A, CONV_W = 4, 128, 128, 4
QKV_A = H_A * (2 * DK_A + DV_A)
H_B, D_NOPE, D_ROPE, DV_B = 8, 64, 32, 64
Q_LORA, KV_LORA = 256, 256
ROPE_BASE = 10000.0
MLA_SCALE = (D_NOPE + D_ROPE) ** -0.5
H_C, DK_C, DV_C = 8, 128, 128
CHUNK = 64
SUB = 16
PAGE = 128
QW = 256
IN_AB_PAD = 2816
V7X_VMEM_LIMIT = 48 * 1024 * 1024


def _cparams(sem):
    return pltpu.CompilerParams(dimension_semantics=sem, vmem_limit_bytes=V7X_VMEM_LIMIT)


def _const_spec(shape):
    nd = len(shape)
    return pl.BlockSpec(shape, lambda *_: (0,) * nd)


def _dot(a, b):
    return jnp.dot(a.astype(BF16), b.astype(BF16), preferred_element_type=F32)


def _dot_nt(a, b):
    return lax.dot_general(a.astype(BF16), b.astype(BF16), (((1,), (1,)), ((), ())),
                           preferred_element_type=F32)


def _dot_tn(a, b):
    return lax.dot_general(a.astype(BF16), b.astype(BF16), (((0,), (0,)), ((), ())),
                           preferred_element_type=F32)


def _dot_nt_f32(a, b):
    return lax.dot_general(a, b, (((1,), (1,)), ((), ())), precision=HI, preferred_element_type=F32)


def _rms(x, g):
    return x * lax.rsqrt(jnp.mean(x * x, axis=-1, keepdims=True) + EPS) * g


def _silu(x):
    return x * jax.nn.sigmoid(x)


def _cumsum_rows(x):
    n = x.shape[0]
    row = lax.broadcasted_iota(jnp.int32, x.shape, 0)
    s = 1
    while s < n:
        x = x + jnp.where(row >= s, pltpu.roll(x, s, axis=0), 0.0)
        s *= 2
    return x


def _proj_kernel(x_ref, g_ref, w_ref, o_ref):
    h = _rms(x_ref[...], g_ref[...]).astype(BF16)
    o_ref[...] = jnp.dot(h, w_ref[...], preferred_element_type=F32)


def _proj(x2, g, w, tm):
    n, d = x2.shape
    nout = w.shape[1]
    return pl.pallas_call(
        _proj_kernel,
        grid=(n // tm,),
        in_specs=[pl.BlockSpec((tm, d), lambda i: (i, 0)), _const_spec((1, d)), _const_spec((d, nout))],
        out_specs=pl.BlockSpec((tm, nout), lambda i: (i, 0)),
        out_shape=jax.ShapeDtypeStruct((n, nout), F32),
        compiler_params=_cparams(("parallel",)),
        name="proj",
    )(x2, g.reshape(1, d), w)


def _outproj_kernel(*refs, n_in):
    r_ref = refs[0]
    a_refs = refs[1:1 + n_in]
    w_refs = refs[1 + n_in:1 + 2 * n_in]
    o_ref = refs[1 + 2 * n_in]
    acc = r_ref[...]
    for a_ref, w_ref in zip(a_refs, w_refs):
        acc = acc + jnp.dot(a_ref[...], w_ref[...], preferred_element_type=F32)
    o_ref[...] = acc


def _outproj(res, acts, ws, tm):
    n, d = res.shape
    n_in = len(acts)
    in_specs = [pl.BlockSpec((tm, d), lambda i: (i, 0))]
    in_specs += [pl.BlockSpec((tm, a.shape[1]), lambda i: (i, 0)) for a in acts]
    in_specs += [_const_spec(w.shape) for w in ws]
    return pl.pallas_call(
        functools.partial(_outproj_kernel, n_in=n_in),
        grid=(n // tm,),
        in_specs=in_specs,
        out_specs=pl.BlockSpec((tm, d), lambda i: (i, 0)),
        out_shape=jax.ShapeDtypeStruct((n, d), F32),
        compiler_params=_cparams(("parallel",)),
        name="outproj",
    )(res, *acts, *ws)


def _mlp_kernel(x_ref, g_ref, gf_ref, wu_ref, wd_ref, o_ref, h_ref, *, final_norm):
    j = pl.program_id(1)

    @pl.when(j == 0)
    def _():
        x = x_ref[...]
        h_ref[...] = _rms(x, g_ref[...]).astype(BF16)
        o_ref[...] = x

    a = jnp.dot(h_ref[...], wu_ref[...], preferred_element_type=F32)
    a = jnp.square(jnp.maximum(a, 0.0))
    o_ref[...] += jnp.dot(a.astype(BF16), wd_ref[...], preferred_element_type=F32)

    if final_norm:
        @pl.when(j == pl.num_programs(1) - 1)
        def _():
            o_ref[...] = _rms(o_ref[...], gf_ref[...])


def _mlp(x2, g, w_up, w_down, gf, final_norm, tm, tf):
    n, d = x2.shape
    ff = w_up.shape[1]
    return pl.pallas_call(
        functools.partial(_mlp_kernel, final_norm=final_norm),
        grid=(n // tm, ff // tf),
        in_specs=[pl.BlockSpec((tm, d), lambda i, j: (i, 0)), _const_spec((1, d)), _const_spec((1, d)),
                  pl.BlockSpec((d, tf), lambda i, j: (0, j)), pl.BlockSpec((tf, d), lambda i, j: (j, 0))],
        out_specs=pl.BlockSpec((tm, d), lambda i, j: (i, 0)),
        out_shape=jax.ShapeDtypeStruct((n, d), F32),
        scratch_shapes=[pltpu.VMEM((tm, d), BF16)],
        compiler_params=_cparams(("parallel", "arbitrary")),
        name="mlp",
    )(x2, g.reshape(1, d), gf.reshape(1, d), w_up, w_down)


def _dot_f32(a, b):
    return jnp.dot(a, b, precision=HI, preferred_element_type=F32)


def _split3_dot(a, b):
    a_hi = a.astype(BF16)
    a_lo = (a - a_hi.astype(F32)).astype(BF16)
    b_hi = b.astype(BF16)
    b_lo = (b - b_hi.astype(F32)).astype(BF16)
    return jnp.dot(jnp.concatenate([a_hi, a_lo, a_hi], axis=1), jnp.concatenate([b_hi, b_hi, b_lo], axis=0),
                   preferred_element_type=F32)


def _solve_unit_lower(lowers, rhss, c):
    nh = len(lowers)
    s = 8
    nb = c // s
    w = nh * c
    row = lax.broadcasted_iota(jnp.int32, (c, w), 0)
    col = lax.broadcasted_iota(jnp.int32, (c, w), 1) % c
    if nb == 1:
        diag_cols = lowers
    else:
        r1 = lax.broadcasted_iota(jnp.int32, (c, c), 0)
        c1 = lax.broadcasted_iota(jnp.int32, (c, c), 1)
        pick = (lax.broadcasted_iota(jnp.int32, (c, s), 0) % s
                == lax.broadcasted_iota(jnp.int32, (c, s), 1)).astype(F32)
        diag_cols = [_dot_f32(jnp.where(r1 // s == c1 // s, lo, 0.0), pick) for lo in lowers]
    x = (row == col).astype(F32)
    for j in range(s - 1):
        mult = jnp.concatenate([jnp.broadcast_to(dc[:, j:j + 1], (c, c)) for dc in diag_cols], axis=1)
        pivot_rows = jnp.broadcast_to(x.reshape(nb, s, w)[:, j:j + 1, :], (nb, s, w)).reshape(c, w)
        x = x - mult * pivot_rows

    def block_diag(m):
        r2 = lax.broadcasted_iota(jnp.int32, (w, w), 0)
        c2 = lax.broadcasted_iota(jnp.int32, (w, w), 1)
        return jnp.where(r2 // c == c2 // c, jnp.concatenate([m] * nh, axis=0), 0.0)

    lower_cat = jnp.concatenate(lowers, axis=1)
    size = s
    while size < c:
        off = jnp.where((row // (2 * size) == col // (2 * size)) & (row // size != col // size) & (row > col),
                        lower_cat, 0.0)
        x = x - _split3_dot(_split3_dot(x, block_diag(off)), block_diag(x))
        size *= 2
    out = _split3_dot(block_diag(x), jnp.concatenate(rhss, axis=0))
    return [out[h * c:(h + 1) * c] for h in range(nh)]


def _gdn_head_pre(q, k, v, b_col, a_col, alog, dtb, c):
    q = q * lax.rsqrt(jnp.sum(q * q, axis=-1, keepdims=True) + EPS) * (DK_A ** -0.5)
    k = k * lax.rsqrt(jnp.sum(k * k, axis=-1, keepdims=True) + EPS)
    beta = jax.nn.sigmoid(b_col)
    g = -jnp.exp(alog) * jax.nn.softplus(a_col + dtb)
    gc_b = _cumsum_rows(jnp.broadcast_to(g, (c, 128)))
    gc = gc_b[:, 0:1]
    diff = gc_b[:, :c] - gc_b.T[:c, :]
    row = lax.broadcasted_iota(jnp.int32, (c, c), 0)
    col = lax.broadcasted_iota(jnp.int32, (c, c), 1)
    incl = row >= col
    decay = jnp.where(incl, jnp.exp(jnp.where(incl, diff, 0.0)), 0.0)
    kb = k * beta
    lower = jnp.where(row > col, _dot_nt_f32(kb, k) * decay, 0.0)
    rhs = jnp.concatenate([v * beta, kb * jnp.exp(gc)], axis=1)
    intra = _dot_nt_f32(q, k) * decay
    return q, k, gc, lower, rhs, intra


def _gdn_head_post(q, k, gc, uw, intra, z, gnorm, s_prev, c):
    u, w = uw[:, :DV_A], uw[:, DV_A:]
    v_new = u - _dot(w, s_prev)
    o = _dot(q * jnp.exp(gc), s_prev) + _dot(intra, v_new)
    g_last = gc[c - 1:c, :]
    s_new = s_prev * jnp.exp(g_last) + _dot_tn(k * jnp.exp(g_last - gc), v_new)
    o = _rms(o, gnorm) * _silu(z)
    return o, s_new


def _gdn_kernel(*refs, c, n_chunks, has_state):
    (q_ref, k_ref, v_ref, z_ref, gt_ref, cwq_ref, cwk_ref, cwv_ref, alog_ref, dtb_ref, gn_ref) = refs[:11]
    if has_state:
        cq_ref, ck_ref, cv_ref, s0_ref = refs[11:15]
        o_ref, sout_ref, s_scr, tail_scr = refs[15:]
    else:
        o_ref, sout_ref, s_scr, tail_scr = refs[11:]
    n = pl.program_id(1)

    @pl.when(n == 0)
    def _():
        if has_state:
            s_scr[...] = s0_ref[0]
            tail_scr[0] = cq_ref[0]
            tail_scr[1] = ck_ref[0]
            tail_scr[2] = cv_ref[0]
        else:
            s_scr[...] = jnp.zeros_like(s_scr)
            tail_scr[...] = jnp.zeros_like(tail_scr)

    def chunk(ci, carry):
        r0 = ci * c if isinstance(ci, int) else pl.multiple_of(ci * c, c)
        rows = pl.ds(r0, c)
        conv = []
        for idx, (x_ref, cw_ref) in enumerate(((q_ref, cwq_ref), (k_ref, cwk_ref), (v_ref, cwv_ref))):
            x = x_ref[0, rows, :]
            ext = jnp.concatenate([tail_scr[idx], x], axis=0)
            cw = cw_ref[...]
            y = x * cw[CONV_W - 1:CONV_W, :]
            for sh in range(1, CONV_W):
                y = y + pltpu.roll(ext, sh, axis=0)[8:] * cw[CONV_W - 1 - sh:CONV_W - sh, :]
            tail_scr[idx] = x[c - 8:, :]
            conv.append(_silu(y))
        z = z_ref[0, rows, :]
        gt = gt_ref[0, rows, :]
        pre = []
        for h in range(H_A):
            hs = slice(h * 128, (h + 1) * 128)
            pre.append(_gdn_head_pre(conv[0][:, hs], conv[1][:, hs], conv[2][:, hs],
                                     gt[:, h:h + 1], gt[:, H_A + h:H_A + h + 1],
                                     alog_ref[:, h:h + 1], dtb_ref[:, h:h + 1], c))
        uws = _solve_unit_lower([p[3] for p in pre], [p[4] for p in pre], c)
        outs = []
        for h in range(H_A):
            q, k, gc, _, _, intra = pre[h]
            o, s_new = _gdn_head_post(q, k, gc, uws[h], intra, z[:, h * 128:(h + 1) * 128], gn_ref[...],
                                      s_scr[h], c)
            s_scr[h] = s_new
            outs.append(o)
        o_ref[0, rows, :] = jnp.concatenate(outs, axis=1).astype(o_ref.dtype)
        return carry

    if n_chunks == 1:
        chunk(0, 0)
    else:
        lax.fori_loop(0, n_chunks, chunk, 0)

    @pl.when(n == pl.num_programs(1) - 1)
    def _():
        sout_ref[0] = s_scr[...]


def _gdn(p3, conv_w, a_log, dt_bias, gdn_norm, conv0, s0, tb):
    b, t, _ = p3.shape
    c = math.gcd(t, CHUNK)
    has_state = s0 is not None
    hw = H_A * 128
    cw = conv_w
    blk = lambda j: pl.BlockSpec((1, tb, hw), lambda i, n: (i, n, j))
    in_specs = [blk(0), blk(1), blk(2), blk(3),
                pl.BlockSpec((1, tb, 128), lambda i, n: (i, n, (IN_AB_PAD - 128) // 128)),
                pl.BlockSpec((CONV_W, hw), lambda i, n: (0, 0)), pl.BlockSpec((CONV_W, hw), lambda i, n: (0, 1)),
                pl.BlockSpec((CONV_W, hw), lambda i, n: (0, 2)),
                _const_spec((1, H_A)), _const_spec((1, H_A)), _const_spec((1, DV_A))]
    args = [p3, p3, p3, p3, p3, cw, cw, cw, a_log.reshape(1, H_A), dt_bias.reshape(1, H_A),
            gdn_norm.reshape(1, DV_A)]
    if has_state:
        conv8 = jnp.pad(conv0, ((0, 0), (8 - (CONV_W - 1), 0), (0, 0)))
        in_specs += [pl.BlockSpec((1, 8, hw), lambda i, n, j=j: (i, 0, j)) for j in range(3)]
        in_specs += [pl.BlockSpec((1, H_A, DK_A, DV_A), lambda i, n: (i, 0, 0, 0))]
        args += [conv8, conv8, conv8, s0]
    return pl.pallas_call(
        functools.partial(_gdn_kernel, c=c, n_chunks=tb // c, has_state=has_state),
        grid=(b, t // tb),
        in_specs=in_specs,
        out_specs=[pl.BlockSpec((1, tb, hw), lambda i, n: (i, n, 0)),
                   pl.BlockSpec((1, H_A, DK_A, DV_A), lambda i, n: (i, 0, 0, 0))],
        out_shape=[jax.ShapeDtypeStruct((b, t, hw), BF16),
                   jax.ShapeDtypeStruct((b, H_A, DK_A, DV_A), F32)],
        scratch_shapes=[pltpu.VMEM((H_A, DK_A, DV_A), F32), pltpu.VMEM((3, 8, hw), F32)],
        compiler_params=_cparams(("parallel", "arbitrary")),
        name="gdn",
    )(*args)


def _hgrn_chunk(qr, fr, v, lb, s_prev, c):
    sub = min(SUB, c)
    q = _silu(qr) * (DK_C ** -0.5)
    f = lb + (1.0 - lb) * jax.nn.sigmoid(fr)
    k = 1.0 - f
    gc = _cumsum_rows(jnp.log(f))
    o_inter = _dot(q * jnp.exp(gc), s_prev)
    row = lax.broadcasted_iota(jnp.int32, (sub, sub), 0)
    col = lax.broadcasted_iota(jnp.int32, (sub, sub), 1)
    row_d = lax.broadcasted_iota(jnp.int32, (sub, 128), 0)
    outs = []
    for blk in range(c // sub):
        r0 = blk * sub
        g_i = gc[r0:r0 + sub]
        q_i = q[r0:r0 + sub]
        a_diag = jnp.zeros((sub, sub), F32)
        for j in range(sub):
            jj = r0 + j
            e = jnp.exp(jnp.where(row_d >= j, g_i - gc[jj:jj + 1, :], 0.0))
            col_j = jnp.sum(q_i * k[jj:jj + 1, :] * e, axis=-1, keepdims=True)
            a_diag = jnp.where(col == j, col_j, a_diag)
        a_diag = jnp.where(row >= col, a_diag, 0.0)
        o_i = _dot(a_diag, v[r0:r0 + sub])
        if blk > 0:
            g_ref = gc[r0 - 1:r0, :]
            a_off = _dot_nt(q_i * jnp.exp(g_i - g_ref), k[:r0] * jnp.exp(g_ref - gc[:r0]))
            o_i = o_i + _dot(a_off, v[:r0])
        outs.append(o_i)
    o = o_inter + (jnp.concatenate(outs, axis=0) if len(outs) > 1 else outs[0])
    g_last = gc[c - 1:c, :]
    gl_col = gc.T[:, c - 1:c]
    s_new = s_prev * jnp.exp(gl_col) + _dot_tn(k * jnp.exp(g_last - gc), v)
    return o, s_new


def _hgrn_kernel(*refs, c, n_chunks, layer, has_state, hg):
    q_ref, f_ref, i_ref, gate_ref, lbl_ref, gn_ref = refs[:6]
    if has_state:
        s0_ref = refs[6]
        o_ref, sout_ref, s_scr = refs[7:]
    else:
        o_ref, sout_ref, s_scr = refs[6:]
    n = pl.program_id(2)

    @pl.when(n == 0)
    def _():
        if has_state:
            s_scr[...] = s0_ref[0]
        else:
            s_scr[...] = jnp.zeros_like(s_scr)

    lbl = lbl_ref[...]
    ex = jnp.exp(lbl - jnp.max(lbl, axis=0, keepdims=True))
    p = ex / jnp.sum(ex, axis=0, keepdims=True)
    lb = jnp.sum(p[:layer + 1], axis=0, keepdims=True) - p[0:1]

    def chunk(ci, carry):
        r0 = ci * c if isinstance(ci, int) else pl.multiple_of(ci * c, c)
        rows = pl.ds(r0, c)
        for h in range(hg):
            hs = slice(h * 128, (h + 1) * 128)
            o, s_new = _hgrn_chunk(q_ref[0, rows, hs], f_ref[0, rows, hs], i_ref[0, rows, hs], lb[:, hs],
                                   s_scr[h], c)
            s_scr[h] = s_new
            o_ref[0, rows, hs] = (_rms(o, gn_ref[...]) * _silu(gate_ref[0, rows, hs])).astype(o_ref.dtype)
        return carry

    if n_chunks == 1:
        chunk(0, 0)
    else:
        lax.fori_loop(0, n_chunks, chunk, 0)

    @pl.when(n == pl.num_programs(2) - 1)
    def _():
        sout_ref[0] = s_scr[...]


def _hgrn(pc3, lb_logits, layer, g_norm, s0, tb, hg):
    b, t, _ = pc3.shape
    c = math.gcd(t, CHUNK)
    has_state = s0 is not None
    depth = lb_logits.shape[0]
    ng = H_C // hg
    blk = lambda j: pl.BlockSpec((1, tb, hg * 128), lambda i, h, n: (i, n, j * ng + h))
    in_specs = [blk(0), blk(1), blk(2), blk(3),
                pl.BlockSpec((depth, hg * 128), lambda i, h, n: (0, h)), _const_spec((1, DV_C))]
    args = [pc3, pc3, pc3, pc3, lb_logits, g_norm.reshape(1, DV_C)]
    if has_state:
        in_specs.append(pl.BlockSpec((1, hg, DK_C, DV_C), lambda i, h, n: (i, h, 0, 0)))
        args.append(s0)
    return pl.pallas_call(
        functools.partial(_hgrn_kernel, c=c, n_chunks=tb // c, layer=layer, has_state=has_state, hg=hg),
        grid=(b, ng, t // tb),
        in_specs=in_specs,
        out_specs=[pl.BlockSpec((1, tb, hg * 128), lambda i, h, n: (i, n, h)),
                   pl.BlockSpec((1, hg, DK_C, DV_C), lambda i, h, n: (i, h, 0, 0))],
        out_shape=[jax.ShapeDtypeStruct((b, t, H_C * DV_C), BF16),
                   jax.ShapeDtypeStruct((b, H_C, DK_C, DV_C), F32)],
        scratch_shapes=[pltpu.VMEM((hg, DK_C, DV_C), F32)],
        compiler_params=_cparams(("parallel", "parallel", "arbitrary")),
        name="hgrn",
    )(*args)


def _mla_prep_kernel(*refs, absorb):
    cq_ref, ckv_ref, kp_ref, qn_ref, kvn_ref, tq_ref, tk_ref, wq_ref = refs[:8]
    if absorb:
        wukt_ref, lat_ref, kr_ref, qlat_ref, qpe_ref = refs[8:]
    else:
        wkv_ref, wkvt_ref, lat_ref, kr_ref, qt_ref, kv_ref, kvt_ref, kp4_ref = refs[8:]
    lat = _rms(ckv_ref[...], kvn_ref[...])
    lat_ref[...] = lat
    kp4 = kp_ref[...] * tk_ref[...]
    kr_ref[...] = kp4[:, 0:D_ROPE] + kp4[:, D_ROPE:2 * D_ROPE]
    cqn = _rms(cq_ref[...], qn_ref[...]).astype(BF16)
    tq = tq_ref[...]
    if absorb:
        for h in range(H_B):
            qh = jnp.dot(cqn, wq_ref[:, h * QW:(h + 1) * QW], preferred_element_type=F32) * tq
            qlat_ref[:, h * KV_LORA:(h + 1) * KV_LORA] = _dot(qh[:, 0:D_NOPE], wukt_ref[h])
            qpe_ref[:, h * D_ROPE:(h + 1) * D_ROPE] = (qh[:, 128:128 + D_ROPE]
                                                      + qh[:, 128 + 2 * D_ROPE:128 + 3 * D_ROPE])
    else:
        nt_dims = (((1,), (1,)), ((), ()))
        for h in range(H_B):
            qh = lax.dot_general(wq_ref[h * QW:(h + 1) * QW, :], cqn, nt_dims, preferred_element_type=F32) * tq
            qt_ref[0, h * QW:(h + 1) * QW, :] = qh.astype(BF16)
        lat_b = lat.astype(BF16)
        kv_ref[...] = jnp.dot(lat_b, wkv_ref[...], preferred_element_type=F32).astype(BF16)
        kvt_ref[0] = lax.dot_general(wkvt_ref[...], lat_b, nt_dims, preferred_element_type=F32).astype(BF16)
        kp4_ref[...] = kp4.astype(BF16)


def _mla_prep(p2, q_norm, kv_norm, tabq, tabk, wq, w_extra, absorb, tm, seq):
    n = p2.shape[0]
    nt = seq // tm if not absorb else 1
    tabq_spec = (pl.BlockSpec((tm, QW), lambda i: (0, 0)) if absorb
                 else pl.BlockSpec((QW, tm), lambda i: (0, i % nt)))
    in_specs = [pl.BlockSpec((tm, Q_LORA), lambda i: (i, 2048 // 256)),
                pl.BlockSpec((tm, KV_LORA), lambda i: (i, 2304 // 256)),
                pl.BlockSpec((tm, 128), lambda i: (i, 2560 // 128)),
                _const_spec((1, Q_LORA)), _const_spec((1, KV_LORA)),
                tabq_spec, pl.BlockSpec((tm, 128), lambda i: (i % nt, 0)),
                _const_spec(wq.shape)] + [_const_spec(w.shape) for w in w_extra]
    row = lambda w: pl.BlockSpec((tm, w), lambda i: (i, 0))
    out_specs = [row(KV_LORA), row(D_ROPE)]
    out_shape = [jax.ShapeDtypeStruct((n, KV_LORA), F32), jax.ShapeDtypeStruct((n, D_ROPE), F32)]
    if absorb:
        out_specs += [row(H_B * KV_LORA), row(H_B * D_ROPE)]
        out_shape += [jax.ShapeDtypeStruct((n, H_B * KV_LORA), F32), jax.ShapeDtypeStruct((n, H_B * D_ROPE), F32)]
    else:
        b = n // seq
        col = lambda w: pl.BlockSpec((1, w, tm), lambda i: (i // nt, 0, i % nt))
        out_specs += [col(H_B * QW), row(H_B * 128), col(H_B * 128), row(128)]
        out_shape += [jax.ShapeDtypeStruct((b, H_B * QW, seq), BF16), jax.ShapeDtypeStruct((n, H_B * 128), BF16),
                      jax.ShapeDtypeStruct((b, H_B * 128, seq), BF16), jax.ShapeDtypeStruct((n, 128), BF16)]
    return pl.pallas_call(
        functools.partial(_mla_prep_kernel, absorb=absorb),
        grid=(n // tm,),
        in_specs=in_specs, out_specs=out_specs, out_shape=out_shape,
        compiler_params=_cparams(("parallel",)),
        name="mla_prep",
    )(p2, p2, p2, q_norm.reshape(1, Q_LORA), kv_norm.reshape(1, KV_LORA), tabq, tabk, wq, *w_extra)


def _flash_kernel(qt_ref, kv_ref, kvt_ref, kp_ref, o_ref, m_scr, l_scr, acc_scr, *, tq):
    qi = pl.program_id(1)
    key = lax.broadcasted_iota(jnp.int32, (tq, tq), 0)
    qry = lax.broadcasted_iota(jnp.int32, (tq, tq), 1)
    m_scr[...] = jnp.full_like(m_scr, NEG_INF)
    l_scr[...] = jnp.zeros_like(l_scr)
    acc_scr[...] = jnp.zeros_like(acc_scr)

    def key_tile(k0, masked):
        kp = kp_ref[0, pl.ds(k0, tq), :]
        for h in range(H_B):
            keys = jnp.concatenate([kv_ref[0, pl.ds(k0, tq), h * 128:(h + 1) * 128], kp], axis=1)
            s = jnp.dot(keys, qt_ref[0, h * QW:(h + 1) * QW, :], preferred_element_type=F32)
            if masked:
                s = jnp.where(key <= qry, s, NEG_INF)
            m = m_scr[h]
            m_new = jnp.maximum(m, jnp.max(s, axis=0, keepdims=True))
            alpha = jnp.exp(m - m_new)
            p = jnp.exp(s - m_new)
            l_scr[h] = alpha * l_scr[h] + jnp.sum(p, axis=0, keepdims=True)
            acc_scr[h] = alpha * acc_scr[h] + jnp.dot(kvt_ref[0, h * 128:(h + 1) * 128, pl.ds(k0, tq)],
                                                      p.astype(BF16), preferred_element_type=F32)
            m_scr[h] = m_new

    def body(kt, carry):
        key_tile(pl.multiple_of(kt * tq, tq), False)
        return carry

    lax.fori_loop(0, qi, body, 0)
    key_tile(pl.multiple_of(qi * tq, tq), True)
    o_t = jnp.concatenate([(acc_scr[h] / l_scr[h])[D_NOPE:D_NOPE + DV_B] for h in range(H_B)], axis=0)
    o_ref[0] = o_t.T.astype(o_ref.dtype)


def _flash(qt3, kv3, kvt3, kp3, tq):
    b, t, _ = kv3.shape
    return pl.pallas_call(
        functools.partial(_flash_kernel, tq=tq),
        grid=(b, t // tq),
        in_specs=[pl.BlockSpec((1, H_B * QW, tq), lambda i, j: (i, 0, j)),
                  pl.BlockSpec((1, t, H_B * 128), lambda i, j: (i, 0, 0)),
                  pl.BlockSpec((1, H_B * 128, t), lambda i, j: (i, 0, 0)),
                  pl.BlockSpec((1, t, 128), lambda i, j: (i, 0, 0))],
        out_specs=pl.BlockSpec((1, tq, H_B * DV_B), lambda i, j: (i, j, 0)),
        out_shape=jax.ShapeDtypeStruct((b, t, H_B * DV_B), BF16),
        scratch_shapes=[pltpu.VMEM((H_B, 1, tq), F32), pltpu.VMEM((H_B, 1, tq), F32),
                        pltpu.VMEM((H_B, 128, tq), F32)],
        compiler_params=_cparams(("parallel", "arbitrary")),
        name="mla_flash",
    )(qt3, kv3, kvt3, kp3)


def _decode_kernel(pt_ref, qlat_ref, qpe_ref, latn_ref, krn_ref, wuv_ref, lat_hbm, kct_hbm, o_ref,
                   lat_buf, kct_buf, lat_sem, kct_sem, *, gp, n_groups, t_new):
    b = pl.program_id(0)
    nb = pl.num_programs(0)
    nq = H_B * t_new

    def page_copies(bi, g, slot):
        copies = []
        for j in range(gp):
            page = pt_ref[bi, g * gp + j]
            copies.append(pltpu.make_async_copy(lat_hbm.at[page], lat_buf.at[slot, pl.ds(j * PAGE, PAGE), :],
                                                lat_sem.at[slot]))
            copies.append(pltpu.make_async_copy(kct_hbm.at[page], kct_buf.at[slot, j], kct_sem.at[slot]))
        return copies

    def start_group(bi, g, slot):
        for cp in page_copies(bi, g, slot):
            cp.start()

    def wait_group(bi, g, slot):
        for cp in page_copies(bi, g, slot):
            cp.wait()

    @pl.when(b == 0)
    def _():
        start_group(0, 0, 0)

    ql = jnp.concatenate([qlat_ref[0, :, h * KV_LORA:(h + 1) * KV_LORA] for h in range(H_B)], axis=0).astype(BF16)
    qp = jnp.concatenate([qpe_ref[0, :, h * D_ROPE:(h + 1) * D_ROPE] for h in range(H_B)], axis=0).astype(BF16)

    def merge(state, s, values):
        m_old, l_old, acc_old = state
        m_g = jnp.max(s, axis=-1, keepdims=True)
        p = jnp.exp(s - m_g)
        l_g = jnp.sum(p, axis=-1, keepdims=True)
        pv = jnp.dot(p.astype(BF16), values, preferred_element_type=F32)
        m_new = jnp.maximum(m_old, m_g)
        a_old = jnp.exp(m_old - m_new)
        a_g = jnp.exp(m_g - m_new)
        return m_new, a_old * l_old + a_g * l_g, a_old * acc_old + a_g * pv

    state = (jnp.full((nq, 1), NEG_INF, F32), jnp.zeros((nq, 1), F32), jnp.zeros((nq, KV_LORA), F32))
    for g in range(n_groups):
        slot = g % 2
        wait_group(b, g, slot)
        if g + 1 < n_groups:
            start_group(b, g + 1, 1 - slot)
        else:
            @pl.when(b + 1 < nb)
            def _():
                start_group(b + 1, 0, 1 - slot)
        lat = lat_buf[slot].astype(BF16)
        kct = jnp.concatenate([kct_buf[slot, j] for j in range(gp)], axis=1).astype(BF16)
        s = (lax.dot_general(ql, lat, (((1,), (1,)), ((), ())), preferred_element_type=F32)
             + jnp.dot(qp, kct, preferred_element_type=F32))
        state = merge(state, s, lat)

    latn = latn_ref[0].astype(BF16)
    krn = krn_ref[0].astype(BF16)
    s = (lax.dot_general(ql, latn, (((1,), (1,)), ((), ())), preferred_element_type=F32)
         + lax.dot_general(qp, krn, (((1,), (1,)), ((), ())), preferred_element_type=F32))
    tok = lax.broadcasted_iota(jnp.int32, (nq, t_new), 0) % t_new
    key = lax.broadcasted_iota(jnp.int32, (nq, t_new), 1)
    _, l, acc = merge(state, jnp.where(key <= tok, s, NEG_INF), latn)
    o_lat = acc / l
    outs = [_dot(o_lat[h * t_new:(h + 1) * t_new], wuv_ref[h]) for h in range(H_B)]
    o_ref[0] = jnp.concatenate(outs, axis=1).astype(o_ref.dtype)


def _decode(page_table, qlat3, qpe3, latn3, krn3, wuv, cache_lat, cache_kpe_t, gp):
    b, t_new, _ = qlat3.shape
    n_pages = page_table.shape[1]
    n_groups = n_pages // gp
    assert n_groups % 2 == 0
    per_batch = lambda w: pl.BlockSpec((1, t_new, w), lambda i, pt: (i, 0, 0))
    in_specs = [per_batch(H_B * KV_LORA), per_batch(H_B * D_ROPE), per_batch(KV_LORA), per_batch(D_ROPE),
                pl.BlockSpec(wuv.shape, lambda i, pt: (0, 0, 0)),
                pl.BlockSpec(memory_space=pl.ANY), pl.BlockSpec(memory_space=pl.ANY)]
    return pl.pallas_call(
        functools.partial(_decode_kernel, gp=gp, n_groups=n_groups, t_new=t_new),
        grid_spec=pltpu.PrefetchScalarGridSpec(
            num_scalar_prefetch=1,
            grid=(b,),
            in_specs=in_specs,
            out_specs=pl.BlockSpec((1, t_new, H_B * DV_B), lambda i, pt: (i, 0, 0)),
            scratch_shapes=[pltpu.VMEM((2, gp * PAGE, KV_LORA), F32), pltpu.VMEM((2, gp, D_ROPE, PAGE), F32),
                            pltpu.SemaphoreType.DMA((2,)), pltpu.SemaphoreType.DMA((2,))]),
        out_shape=jax.ShapeDtypeStruct((b, t_new, H_B * DV_B), BF16),
        compiler_params=_cparams(("arbitrary",)),
        name="mla_decode",
    )(page_table, qlat3, qpe3, latn3, krn3, wuv, cache_lat, cache_kpe_t)


def _rot_cols(w):
    half = D_ROPE // 2
    return jnp.concatenate([-w[..., half:], w[..., :half]], axis=-1)


def _layout_w_in_ab(w):
    d = w.shape[0]
    o = QKV_A + H_A * DV_A
    qkv_z, b_w, a_w = w[:, :o], w[:, o:o + H_A], w[:, o + H_A:o + 2 * H_A]
    o += 2 * H_A
    cq, ckv, kpe = w[:, o:o + Q_LORA], w[:, o + Q_LORA:o + Q_LORA + KV_LORA], w[:, o + Q_LORA + KV_LORA:]
    kr = _rot_cols(kpe)
    pad = jnp.zeros((d, 128 - 2 * H_A), w.dtype)
    return jnp.concatenate([qkv_z, cq, ckv, kpe, kr, kpe, kr, b_w, a_w, pad], axis=1).astype(BF16)


def _layout_w_uq(w_uq):
    nope, pe = w_uq[..., :D_NOPE], w_uq[..., D_NOPE:]
    rot = _rot_cols(pe)
    z = jnp.zeros_like(nope)
    return jnp.concatenate([nope, z, pe, pe, rot, rot], axis=-1).reshape(w_uq.shape[0], H_B * QW).astype(BF16)


def _rope_tables(pos):
    half = D_ROPE // 2
    inv = ROPE_BASE ** (-jnp.arange(half, dtype=F32) / half)
    ang = pos.astype(F32)[:, None] * inv
    c = jnp.concatenate([jnp.cos(ang), jnp.cos(ang)], axis=1)
    s = jnp.concatenate([jnp.sin(ang), jnp.sin(ang)], axis=1)
    one = jnp.ones((pos.shape[0], 128), F32)
    tabq = jnp.concatenate([one, c, c, s, s], axis=1) * MLA_SCALE
    tabk = jnp.concatenate([c, s, c, s], axis=1)
    return tabq, tabk


def _row_tile(n):
    for tm in (512, 256, 128, 64, 32, 16, 8):
        if n % tm == 0:
            return tm
    raise ValueError(n)


def _trunk(x, pos, conv0, gdn0, hgrn0, paged, wts):
    b, t, d = x.shape
    n = b * t
    tm = min(256, _row_tile(n))
    tb = min(256, t)
    x2 = x.reshape(n, d)

    p2 = _proj(x2, wts["mix_norm"][0], wts["w_in_ab"], tm)
    p3 = p2.reshape(b, t, IN_AB_PAD)
    conv_new = p3[:, t - (CONV_W - 1):, :QKV_A]
    o_a, gdn_new = _gdn(p3, wts["conv_w"], wts["a_log"], wts["dt_bias"], wts["gdn_norm"], conv0, gdn0, tb)
    tabq, tabk = _rope_tables(pos)
    if paged is None:
        tmp = math.gcd(tm, t)
        lat, kr, qt, kv, kvt, kp4 = _mla_prep(p2, wts["q_norm"], wts["kv_norm"], tabq.T, tabk, wts["w_uq"].T,
                                              [wts["w_kv"], wts["w_kv"].T], False, tmp, t)
        o_b = _flash(qt, kv.reshape(b, t, -1), kvt, kp4.reshape(b, t, -1), min(256, t))
    else:
        cache_lat, cache_kpe, page_table = paged
        reps = tm // t
        lat, kr, qlat, qpe = _mla_prep(p2, wts["q_norm"], wts["kv_norm"], jnp.tile(tabq, (reps, 1)),
                                       jnp.tile(tabk, (reps, 1)), wts["w_uq"], [wts["w_ukt"]], True, tm, t)
        n_pages = page_table.shape[1]
        gp = math.gcd(n_pages // 2, 16)
        o_b = _decode(page_table, qlat.reshape(b, t, -1), qpe.reshape(b, t, -1), lat.reshape(b, t, -1),
                      kr.reshape(b, t, -1), wts["w_uv"], cache_lat, jnp.swapaxes(cache_kpe, 1, 2), gp)
    x2 = _outproj(x2, [o_a.reshape(n, -1), o_b.reshape(n, -1)], [wts["w_out_a"], wts["w_out_b"]], tm)
    tmm = min(512, _row_tile(n))
    x2 = _mlp(x2, wts["mlp_norm"][0], wts["w_up"][0], wts["w_down"][0], wts["final_norm"], False, tmm, 1024)

    pc = _proj(x2, wts["mix_norm"][1], wts["w_in_c"], tm)
    o_c, hgrn_new = _hgrn(pc.reshape(b, t, -1), wts["lb_logits"], 1, wts["g_norm_c"], hgrn0, tb,
                          H_C if hgrn0 is not None else 2)
    x2 = _outproj(x2, [o_c.reshape(n, -1)], [wts["w_out_c"]], tm)
    y2 = _mlp(x2, wts["mlp_norm"][1], wts["w_up"][1], wts["w_down"][1], wts["final_norm"], True, tmm, 1024)
    return (y2.reshape(b, t, d), gdn_new[None], conv_new[None], lat.reshape(b, t, -1)[None],
            kr.reshape(b, t, -1)[None], hgrn_new[None])


def kernel(x_prompt, x_sample, state_gdn, state_gdn_conv, cache_mla_latent, cache_mla_krope, state_hgrn,
           page_table, mix_norm, mlp_norm, final_norm, w_up, w_down, w_in_ab, conv_w_ab, a_log_ab, dt_bias_ab,
           gdn_norm_ab, q_norm_ab, w_uq_ab, kv_norm_ab, w_uk_ab, w_uv_ab, w_out_ab, w_in_c, lb_logits_c,
           g_norm_c, w_out_c):
    assert mix_norm.shape[0] == 2 and w_in_ab.shape[0] == 1 and w_in_c.shape[0] == 1
    assert a_log_ab.shape == (1, H_A) and conv_w_ab.shape == (1, CONV_W, QKV_A)
    assert w_uq_ab.shape[1:] == (Q_LORA, H_B, D_NOPE + D_ROPE) and w_uk_ab.shape[1:] == (KV_LORA, H_B, D_NOPE)
    assert cache_mla_latent.shape[2] == PAGE and x_prompt.shape[1] >= CONV_W - 1 and x_sample.shape[1] >= CONV_W - 1
    w_uk, w_uv = w_uk_ab[0], w_uv_ab[0]
    wts = {
        "mix_norm": mix_norm, "mlp_norm": mlp_norm, "final_norm": final_norm,
        "w_up": w_up.astype(BF16), "w_down": w_down.astype(BF16),
        "w_in_ab": _layout_w_in_ab(w_in_ab[0]), "conv_w": conv_w_ab[0], "a_log": a_log_ab[0],
        "dt_bias": dt_bias_ab[0], "gdn_norm": gdn_norm_ab[0], "q_norm": q_norm_ab[0], "kv_norm": kv_norm_ab[0],
        "w_uq": _layout_w_uq(w_uq_ab[0]),
        "w_kv": jnp.concatenate([w_uk, w_uv], axis=-1).reshape(KV_LORA, H_B * 128).astype(BF16),
        "w_ukt": jnp.transpose(w_uk, (1, 2, 0)).astype(BF16),
        "w_uv": jnp.transpose(w_uv, (1, 0, 2)).astype(BF16),
        "w_out_a": w_out_ab[0, :H_A * DV_A].astype(BF16), "w_out_b": w_out_ab[0, H_A * DV_A:].astype(BF16),
        "w_in_c": w_in_c[0].astype(BF16), "lb_logits": lb_logits_c, "g_norm_c": g_norm_c[0],
        "w_out_c": w_out_c[0].astype(BF16),
    }
    s_len = x_prompt.shape[1]
    outs_p = _trunk(x_prompt, jnp.arange(s_len), None, None, None, None, wts)
    past_len = page_table.shape[1] * cache_mla_latent.shape[2]
    outs_s = _trunk(x_sample, past_len + jnp.arange(x_sample.shape[1]),
                    state_gdn_conv.reshape(state_gdn_conv.shape[1:]), state_gdn.reshape(state_gdn.shape[1:]),
                    state_hgrn.reshape(state_hgrn.shape[1:]),
                    (cache_mla_latent.reshape(cache_mla_latent.shape[1:]),
                     cache_mla_krope.reshape(cache_mla_krope.shape[1:]), page_table), wts)
    return (outs_p[0], outs_s[0]) + outs_p[1:] + outs_s[1:]
```

```python
import functools
import math

import jax
import jax.numpy as jnp
from jax import lax
from jax.experimental import pallas as pl
from jax.experimental.pallas import tpu as pltpu

F32 = jnp.float32
BF16 = jnp.bfloat16
HI = lax.Precision.HIGHEST
EPS = 1e-6
NEG_INF = float("-inf")

H_A, DK_A, DV_A, CONV_W = 4, 128, 128, 4
QKV_A = H_A * (2 * DK_A + DV_A)
H_B, D_NOPE, D_ROPE, DV_B = 8, 64, 32, 64
Q_LORA, KV_LORA = 256, 256
ROPE_BASE = 10000.0
MLA_SCALE = (D_NOPE + D_ROPE) ** -0.5
H_C, DK_C, DV_C = 8, 128, 128
CHUNK = 64
SUB = 16
PAGE = 128
QW = 256
IN_AB_PAD = 2816
V7X_VMEM_LIMIT = 48 * 1024 * 1024


def _cparams(sem):
    return pltpu.CompilerParams(dimension_semantics=sem, vmem_limit_bytes=V7X_VMEM_LIMIT)


def _const_spec(shape):
    nd = len(shape)
    return pl.BlockSpec(shape, lambda *_: (0,) * nd)


def _dot(a, b):
    return jnp.dot(a.astype(BF16), b.astype(BF16), preferred_element_type=F32)


def _dot_nt(a, b):
    return lax.dot_general(a.astype(BF16), b.astype(BF16), (((1,), (1,)), ((), ())),
                           preferred_element_type=F32)


def _dot_tn(a, b):
    return lax.dot_general(a.astype(BF16), b.astype(BF16), (((0,), (0,)), ((), ())),
                           preferred_element_type=F32)


def _dot_nt_f32(a, b):
    return lax.dot_general(a, b, (((1,), (1,)), ((), ())), precision=HI, preferred_element_type=F32)


def _rms(x, g):
    return x * lax.rsqrt(jnp.mean(x * x, axis=-1, keepdims=True) + EPS) * g


def _silu(x):
    return x * jax.nn.sigmoid(x)


def _cumsum_rows(x):
    n = x.shape[0]
    row = lax.broadcasted_iota(jnp.int32, x.shape, 0)
    s = 1
    while s < n:
        x = x + jnp.where(row >= s, pltpu.roll(x, s, axis=0), 0.0)
        s *= 2
    return x


def _proj_kernel(x_ref, g_ref, w_ref, o_ref):
    h = _rms(x_ref[...], g_ref[...]).astype(BF16)
    o_ref[...] = jnp.dot(h, w_ref[...], preferred_element_type=F32)


def _proj(x2, g, w, tm):
    n, d = x2.shape
    nout = w.shape[1]
    return pl.pallas_call(
        _proj_kernel,
        grid=(n // tm,),
        in_specs=[pl.BlockSpec((tm, d), lambda i: (i, 0)), _const_spec((1, d)), _const_spec((d, nout))],
        out_specs=pl.BlockSpec((tm, nout), lambda i: (i, 0)),
        out_shape=jax.ShapeDtypeStruct((n, nout), F32),
        compiler_params=_cparams(("parallel",)),
        name="proj",
    )(x2, g.reshape(1, d), w)


def _outproj_kernel(*refs, n_in):
    r_ref = refs[0]
    a_refs = refs[1:1 + n_in]
    w_refs = refs[1 + n_in:1 + 2 * n_in]
    o_ref = refs[1 + 2 * n_in]
    acc = r_ref[...]
    for a_ref, w_ref in zip(a_refs, w_refs):
        acc = acc + jnp.dot(a_ref[...], w_ref[...], preferred_element_type=F32)
    o_ref[...] = acc


def _outproj(res, acts, ws, tm):
    n, d = res.shape
    n_in = len(acts)
    in_specs = [pl.BlockSpec((tm, d), lambda i: (i, 0))]
    in_specs += [pl.BlockSpec((tm, a.shape[1]), lambda i: (i, 0)) for a in acts]
    in_specs += [_const_spec(w.shape) for w in ws]
    return pl.pallas_call(
        functools.partial(_outproj_kernel, n_in=n_in),
        grid=(n // tm,),
        in_specs=in_specs,
        out_specs=pl.BlockSpec((tm, d), lambda i: (i, 0)),
        out_shape=jax.ShapeDtypeStruct((n, d), F32),
        compiler_params=_cparams(("parallel",)),
        name="outproj",
    )(res, *acts, *ws)


def _mlp_kernel(x_ref, g_ref, gf_ref, wu_ref, wd_ref, o_ref, h_ref, *, final_norm):
    j = pl.program_id(1)

    @pl.when(j == 0)
    def _():
        x = x_ref[...]
        h_ref[...] = _rms(x, g_ref[...]).astype(BF16)
        o_ref[...] = x

    a = jnp.dot(h_ref[...], wu_ref[...], preferred_element_type=F32)
    a = jnp.square(jnp.maximum(a, 0.0))
    o_ref[...] += jnp.dot(a.astype(BF16), wd_ref[...], preferred_element_type=F32)

    if final_norm:
        @pl.when(j == pl.num_programs(1) - 1)
        def _():
            o_ref[...] = _rms(o_ref[...], gf_ref[...])


def _mlp(x2, g, w_up, w_down, gf, final_norm, tm, tf):
    n, d = x2.shape
    ff = w_up.shape[1]
    return pl.pallas_call(
        functools.partial(_mlp_kernel, final_norm=final_norm),
        grid=(n // tm, ff // tf),
        in_specs=[pl.BlockSpec((tm, d), lambda i, j: (i, 0)), _const_spec((1, d)), _const_spec((1, d)),
                  pl.BlockSpec((d, tf), lambda i, j: (0, j)), pl.BlockSpec((tf, d), lambda i, j: (j, 0))],
        out_specs=pl.BlockSpec((tm, d), lambda i, j: (i, 0)),
        out_shape=jax.ShapeDtypeStruct((n, d), F32),
        scratch_shapes=[pltpu.VMEM((tm, d), BF16)],
        compiler_params=_cparams(("parallel", "arbitrary")),
        name="mlp",
    )(x2, g.reshape(1, d), gf.reshape(1, d), w_up, w_down)


def _dot_f32(a, b):
    return jnp.dot(a, b, precision=HI, preferred_element_type=F32)


def _split3_dot(a, b):
    a_hi = a.astype(BF16)
    a_lo = (a - a_hi.astype(F32)).astype(BF16)
    b_hi = b.astype(BF16)
    b_lo = (b - b_hi.astype(F32)).astype(BF16)
    return jnp.dot(jnp.concatenate([a_hi, a_lo, a_hi], axis=1), jnp.concatenate([b_hi, b_hi, b_lo], axis=0),
                   preferred_element_type=F32)


def _solve_unit_lower(lowers, rhss, c):
    nh = len(lowers)
    s = 8
    nb = c // s
    w = nh * c
    row = lax.broadcasted_iota(jnp.int32, (c, w), 0)
    col = lax.broadcasted_iota(jnp.int32, (c, w), 1) % c
    if nb == 1:
        diag_cols = lowers
    else:
        r1 = lax.broadcasted_iota(jnp.int32, (c, c), 0)
        c1 = lax.broadcasted_iota(jnp.int32, (c, c), 1)
        pick = (lax.broadcasted_iota(jnp.int32, (c, s), 0) % s
                == lax.broadcasted_iota(jnp.int32, (c, s), 1)).astype(F32)
        diag_cols = [_dot_f32(jnp.where(r1 // s == c1 // s, lo, 0.0), pick) for lo in lowers]
    x = (row == col).astype(F32)
    for j in range(s - 1):
        mult = jnp.concatenate([jnp.broadcast_to(dc[:, j:j + 1], (c, c)) for dc in diag_cols], axis=1)
        pivot_rows = jnp.broadcast_to(x.reshape(nb, s, w)[:, j:j + 1, :], (nb, s, w)).reshape(c, w)
        x = x - mult * pivot_rows

    def block_diag(m):
        r2 = lax.broadcasted_iota(jnp.int32, (w, w), 0)
        c2 = lax.broadcasted_iota(jnp.int32, (w, w), 1)
        return jnp.where(r2 // c == c2 // c, jnp.concatenate([m] * nh, axis=0), 0.0)

    lower_cat = jnp.concatenate(lowers, axis=1)
    size = s
    while size < c:
        off = jnp.where((row // (2 * size) == col // (2 * size)) & (row // size != col // size) & (row > col),
                        lower_cat, 0.0)
        x = x - _split3_dot(_split3_dot(x, block_diag(off)), block_diag(x))
        size *= 2
    out = _split3_dot(block_diag(x), jnp.concatenate(rhss, axis=0))
    return [out[h * c:(h + 1) * c] for h in range(nh)]


def _gdn_head_pre(q, k, v, b_col, a_col, alog, dtb, c):
    q = q * lax.rsqrt(jnp.sum(q * q, axis=-1, keepdims=True) + EPS) * (DK_A ** -0.5)
    k = k * lax.rsqrt(jnp.sum(k * k, axis=-1, keepdims=True) + EPS)
    beta = jax.nn.sigmoid(b_col)
    g = -jnp.exp(alog) * jax.nn.softplus(a_col + dtb)
    gc_b = _cumsum_rows(jnp.broadcast_to(g, (c, 128)))
    gc = gc_b[:, 0:1]
    diff = gc_b[:, :c] - gc_b.T[:c, :]
    row = lax.broadcasted_iota(jnp.int32, (c, c), 0)
    col = lax.broadcasted_iota(jnp.int32, (c, c), 1)
    incl = row >= col
    decay = jnp.where(incl, jnp.exp(jnp.where(incl, diff, 0.0)), 0.0)
    kb = k * beta
    lower = jnp.where(row > col, _dot_nt_f32(kb, k) * decay, 0.0)
    rhs = jnp.concatenate([v * beta, kb * jnp.exp(gc)], axis=1)
    intra = _dot_nt_f32(q, k) * decay
    return q, k, gc, lower, rhs, intra


def _gdn_head_post(q, k, gc, uw, intra, z, gnorm, s_prev, c):
    u, w = uw[:, :DV_A], uw[:, DV_A:]
    v_new = u - _dot(w, s_prev)
    o = _dot(q * jnp.exp(gc), s_prev) + _dot(intra, v_new)
    g_last = gc[c - 1:c, :]
    s_new = s_prev * jnp.exp(g_last) + _dot_tn(k * jnp.exp(g_last - gc), v_new)
    o = _rms(o, gnorm) * _silu(z)
    return o, s_new


def _gdn_kernel(*refs, c, n_chunks, has_state):
    (q_ref, k_ref, v_ref, z_ref, gt_ref, cwq_ref, cwk_ref, cwv_ref, alog_ref, dtb_ref, gn_ref) = refs[:11]
    if has_state:
        cq_ref, ck_ref, cv_ref, s0_ref = refs[11:15]
        o_ref, sout_ref, s_scr, tail_scr = refs[15:]
    else:
        o_ref, sout_ref, s_scr, tail_scr = refs[11:]
    n = pl.program_id(1)

    @pl.when(n == 0)
    def _():
        if has_state:
            s_scr[...] = s0_ref[0]
            tail_scr[0] = cq_ref[0]
            tail_scr[1] = ck_ref[0]
            tail_scr[2] = cv_ref[0]
        else:
            s_scr[...] = jnp.zeros_like(s_scr)
            tail_scr[...] = jnp.zeros_like(tail_scr)

    def chunk(ci, carry):
        r0 = ci * c if isinstance(ci, int) else pl.multiple_of(ci * c, c)
        rows = pl.ds(r0, c)
        conv = []
        for idx, (x_ref, cw_ref) in enumerate(((q_ref, cwq_ref), (k_ref, cwk_ref), (v_ref, cwv_ref))):
            x = x_ref[0, rows, :]
            ext = jnp.concatenate([tail_scr[idx], x], axis=0)
            cw = cw_ref[...]
            y = x * cw[CONV_W - 1:CONV_W, :]
            for sh in range(1, CONV_W):
                y = y + pltpu.roll(ext, sh, axis=0)[8:] * cw[CONV_W - 1 - sh:CONV_W - sh, :]
            tail_scr[idx] = x[c - 8:, :]
            conv.append(_silu(y))
        z = z_ref[0, rows, :]
        gt = gt_ref[0, rows, :]
        pre = []
        for h in range(H_A):
            hs = slice(h * 128, (h + 1) * 128)
            pre.append(_gdn_head_pre(conv[0][:, hs], conv[1][:, hs], conv[2][:, hs],
                                     gt[:, h:h + 1], gt[:, H_A + h:H_A + h + 1],
                                     alog_ref[:, h:h + 1], dtb_ref[:, h:h + 1], c))
        uws = _solve_unit_lower([p[3] for p in pre], [p[4] for p in pre], c)
        outs = []
        for h in range(H_A):
            q, k, gc, _, _, intra = pre[h]
            o, s_new = _gdn_head_post(q, k, gc, uws[h], intra, z[:, h * 128:(h + 1) * 128], gn_ref[...],
                                      s_scr[h], c)
            s_scr[h] = s_new
            outs.append(o)
        o_ref[0, rows, :] = jnp.concatenate(outs, axis=1).astype(o_ref.dtype)
        return carry

    for ci in range(n_chunks):
        chunk(ci, 0)

    @pl.when(n == pl.num_programs(1) - 1)
    def _():
        sout_ref[0] = s_scr[...]


def _gdn(p3, conv_w, a_log, dt_bias, gdn_norm, conv0, s0, tb):
    b, t, _ = p3.shape
    c = math.gcd(t, CHUNK)
    has_state = s0 is not None
    hw = H_A * 128
    cw = conv_w
    blk = lambda j: pl.BlockSpec((1, tb, hw), lambda i, n: (i, n, j))
    in_specs = [blk(0), blk(1), blk(2), blk(3),
                pl.BlockSpec((1, tb, 128), lambda i, n: (i, n, (IN_AB_PAD - 128) // 128)),
                pl.BlockSpec((CONV_W, hw), lambda i, n: (0, 0)), pl.BlockSpec((CONV_W, hw), lambda i, n: (0, 1)),
                pl.BlockSpec((CONV_W, hw), lambda i, n: (0, 2)),
                _const_spec((1, H_A)), _const_spec((1, H_A)), _const_spec((1, DV_A))]
    args = [p3, p3, p3, p3, p3, cw, cw, cw, a_log.reshape(1, H_A), dt_bias.reshape(1, H_A),
            gdn_norm.reshape(1, DV_A)]
    if has_state:
        conv8 = jnp.pad(conv0, ((0, 0), (8 - (CONV_W - 1), 0), (0, 0)))
        in_specs += [pl.BlockSpec((1, 8, hw), lambda i, n, j=j: (i, 0, j)) for j in range(3)]
        in_specs += [pl.BlockSpec((1, H_A, DK_A, DV_A), lambda i, n: (i, 0, 0, 0))]
        args += [conv8, conv8, conv8, s0]
    return pl.pallas_call(
        functools.partial(_gdn_kernel, c=c, n_chunks=tb // c, has_state=has_state),
        grid=(b, t // tb),
        in_specs=in_specs,
        out_specs=[pl.BlockSpec((1, tb, hw), lambda i, n: (i, n, 0)),
                   pl.BlockSpec((1, H_A, DK_A, DV_A), lambda i, n: (i, 0, 0, 0))],
        out_shape=[jax.ShapeDtypeStruct((b, t, hw), BF16),
                   jax.ShapeDtypeStruct((b, H_A, DK_A, DV_A), F32)],
        scratch_shapes=[pltpu.VMEM((H_A, DK_A, DV_A), F32), pltpu.VMEM((3, 8, hw), F32)],
        compiler_params=_cparams(("parallel", "arbitrary")),
        name="gdn",
    )(*args)


def _hgrn_chunk(qr, fr, v, lb, s_prev, c):
    sub = min(SUB, c)
    q = _silu(qr) * (DK_C ** -0.5)
    f = lb + (1.0 - lb) * jax.nn.sigmoid(fr)
    k = 1.0 - f
    gc = _cumsum_rows(jnp.log(f))
    o_inter = _dot(q * jnp.exp(gc), s_prev)
    row = lax.broadcasted_iota(jnp.int32, (sub, sub), 0)
    col = lax.broadcasted_iota(jnp.int32, (sub, sub), 1)
    row_d = lax.broadcasted_iota(jnp.int32, (sub, 128), 0)
    outs = []
    for blk in range(c // sub):
        r0 = blk * sub
        g_i = gc[r0:r0 + sub]
        q_i = q[r0:r0 + sub]
        a_parts = [jnp.zeros((8, sub), F32) for _ in range(sub // 8)]
        for j in range(sub):
            jj = r0 + j
            lo = 8 * (j // 8)
            e = jnp.exp(g_i[lo:] - gc[jj:jj + 1, :])
            col_j = jnp.sum(q_i[lo:] * k[jj:jj + 1, :] * e, axis=-1, keepdims=True)
            for p in range(j // 8, sub // 8):
                a_parts[p] = jnp.where(col[:8] == j, col_j[8 * p - lo:8 * p - lo + 8], a_parts[p])
        a_diag = jnp.concatenate(a_parts, axis=0) if len(a_parts) > 1 else a_parts[0]
        a_diag = jnp.where(row >= col, a_diag, 0.0)
        o_i = _dot(a_diag, v[r0:r0 + sub])
        if blk > 0:
            g_ref = gc[r0 - 1:r0, :]
            a_off = _dot_nt(q_i * jnp.exp(g_i - g_ref), k[:r0] * jnp.exp(g_ref - gc[:r0]))
            o_i = o_i + _dot(a_off, v[:r0])
        outs.append(o_i)
    o = o_inter + (jnp.concatenate(outs, axis=0) if len(outs) > 1 else outs[0])
    g_last = gc[c - 1:c, :]
    gl_col = gc.T[:, c - 1:c]
    s_new = s_prev * jnp.exp(gl_col) + _dot_tn(k * jnp.exp(g_last - gc), v)
    return o, s_new


def _hgrn_kernel(*refs, c, n_chunks, layer, has_state, hg):
    q_ref, f_ref, i_ref, gate_ref, lbl_ref, gn_ref = refs[:6]
    if has_state:
        s0_ref = refs[6]
        o_ref, sout_ref, s_scr = refs[7:]
    else:
        o_ref, sout_ref, s_scr = refs[6:]
    n = pl.program_id(2)

    @pl.when(n == 0)
    def _():
        if has_state:
            s_scr[...] = s0_ref[0]
        else:
            s_scr[...] = jnp.zeros_like(s_scr)

    lbl = lbl_ref[...]
    ex = jnp.exp(lbl - jnp.max(lbl, axis=0, keepdims=True))
    p = ex / jnp.sum(ex, axis=0, keepdims=True)
    lb = jnp.sum(p[:layer + 1], axis=0, keepdims=True) - p[0:1]

    def chunk(ci, carry):
        r0 = ci * c if isinstance(ci, int) else pl.multiple_of(ci * c, c)
        rows = pl.ds(r0, c)
        for h in range(hg):
            hs = slice(h * 128, (h + 1) * 128)
            o, s_new = _hgrn_chunk(q_ref[0, rows, hs], f_ref[0, rows, hs], i_ref[0, rows, hs], lb[:, hs],
                                   s_scr[h], c)
            s_scr[h] = s_new
            o_ref[0, rows, hs] = (_rms(o, gn_ref[...]) * _silu(gate_ref[0, rows, hs])).astype(o_ref.dtype)
        return carry

    if n_chunks == 1:
        chunk(0, 0)
    else:
        lax.fori_loop(0, n_chunks, chunk, 0)

    @pl.when(n == pl.num_programs(2) - 1)
    def _():
        sout_ref[0] = s_scr[...]


def _hgrn(pc3, lb_logits, layer, g_norm, s0, tb, hg):
    b, t, _ = pc3.shape
    c = math.gcd(t, CHUNK)
    has_state = s0 is not None
    depth = lb_logits.shape[0]
    ng = H_C // hg
    blk = lambda j: pl.BlockSpec((1, tb, hg * 128), lambda i, h, n: (i, n, j * ng + h))
    in_specs = [blk(0), blk(1), blk(2), blk(3),
                pl.BlockSpec((depth, hg * 128), lambda i, h, n: (0, h)), _const_spec((1, DV_C))]
    args = [pc3, pc3, pc3, pc3, lb_logits, g_norm.reshape(1, DV_C)]
    if has_state:
        in_specs.append(pl.BlockSpec((1, hg, DK_C, DV_C), lambda i, h, n: (i, h, 0, 0)))
        args.append(s0)
    return pl.pallas_call(
        functools.partial(_hgrn_kernel, c=c, n_chunks=tb // c, layer=layer, has_state=has_state, hg=hg),
        grid=(b, ng, t // tb),
        in_specs=in_specs,
        out_specs=[pl.BlockSpec((1, tb, hg * 128), lambda i, h, n: (i, n, h)),
                   pl.BlockSpec((1, hg, DK_C, DV_C), lambda i, h, n: (i, h, 0, 0))],
        out_shape=[jax.ShapeDtypeStruct((b, t, H_C * DV_C), BF16),
                   jax.ShapeDtypeStruct((b, H_C, DK_C, DV_C), F32)],
        scratch_shapes=[pltpu.VMEM((hg, DK_C, DV_C), F32)],
        compiler_params=_cparams(("parallel", "parallel", "arbitrary")),
        name="hgrn",
    )(*args)


def _mla_prep_kernel(*refs, absorb):
    cq_ref, ckv_ref, kp_ref, qn_ref, kvn_ref, tq_ref, tk_ref, wq_ref = refs[:8]
    if absorb:
        wukt_ref, lat_ref, kr_ref, qlat_ref, qpe_ref = refs[8:]
    else:
        wkv_ref, wkvt_ref, lat_ref, kr_ref, qt_ref, kv_ref, kvt_ref, kp4_ref = refs[8:]
    lat = _rms(ckv_ref[...], kvn_ref[...])
    lat_ref[...] = lat
    kp4 = kp_ref[...] * tk_ref[...]
    kr_ref[...] = kp4[:, 0:D_ROPE] + kp4[:, D_ROPE:2 * D_ROPE]
    cqn = _rms(cq_ref[...], qn_ref[...]).astype(BF16)
    tq = tq_ref[...]
    if absorb:
        for h in range(H_B):
            qh = jnp.dot(cqn, wq_ref[:, h * QW:(h + 1) * QW], preferred_element_type=F32) * tq
            qlat_ref[:, h * KV_LORA:(h + 1) * KV_LORA] = _dot(qh[:, 0:D_NOPE], wukt_ref[h])
            qpe_ref[:, h * D_ROPE:(h + 1) * D_ROPE] = (qh[:, 128:128 + D_ROPE]
                                                      + qh[:, 128 + 2 * D_ROPE:128 + 3 * D_ROPE])
    else:
        nt_dims = (((1,), (1,)), ((), ()))
        for h in range(H_B):
            qh = lax.dot_general(wq_ref[h * QW:(h + 1) * QW, :], cqn, nt_dims, preferred_element_type=F32) * tq
            qt_ref[0, h * QW:(h + 1) * QW, :] = qh.astype(BF16)
        lat_b = lat.astype(BF16)
        kv_ref[...] = jnp.dot(lat_b, wkv_ref[...], preferred_element_type=F32).astype(BF16)
        kvt_ref[0] = lax.dot_general(wkvt_ref[...], lat_b, nt_dims, preferred_element_type=F32).astype(BF16)
        kp4_ref[...] = kp4.astype(BF16)


def _mla_prep(p2, q_norm, kv_norm, tabq, tabk, wq, w_extra, absorb, tm, seq):
    n = p2.shape[0]
    nt = seq // tm if not absorb else 1
    tabq_spec = (pl.BlockSpec((tm, QW), lambda i: (0, 0)) if absorb
                 else pl.BlockSpec((QW, tm), lambda i: (0, i % nt)))
    in_specs = [pl.BlockSpec((tm, Q_LORA), lambda i: (i, 2048 // 256)),
                pl.BlockSpec((tm, KV_LORA), lambda i: (i, 2304 // 256)),
                pl.BlockSpec((tm, 128), lambda i: (i, 2560 // 128)),
                _const_spec((1, Q_LORA)), _const_spec((1, KV_LORA)),
                tabq_spec, pl.BlockSpec((tm, 128), lambda i: (i % nt, 0)),
                _const_spec(wq.shape)] + [_const_spec(w.shape) for w in w_extra]
    row = lambda w: pl.BlockSpec((tm, w), lambda i: (i, 0))
    out_specs = [row(KV_LORA), row(D_ROPE)]
    out_shape = [jax.ShapeDtypeStruct((n, KV_LORA), F32), jax.ShapeDtypeStruct((n, D_ROPE), F32)]
    if absorb:
        out_specs += [row(H_B * KV_LORA), row(H_B * D_ROPE)]
        out_shape += [jax.ShapeDtypeStruct((n, H_B * KV_LORA), F32), jax.ShapeDtypeStruct((n, H_B * D_ROPE), F32)]
    else:
        b = n // seq
        col = lambda w: pl.BlockSpec((1, w, tm), lambda i: (i // nt, 0, i % nt))
        out_specs += [col(H_B * QW), row(H_B * 128), col(H_B * 128), row(128)]
        out_shape += [jax.ShapeDtypeStruct((b, H_B * QW, seq), BF16), jax.ShapeDtypeStruct((n, H_B * 128), BF16),
                      jax.ShapeDtypeStruct((b, H_B * 128, seq), BF16), jax.ShapeDtypeStruct((n, 128), BF16)]
    return pl.pallas_call(
        functools.partial(_mla_prep_kernel, absorb=absorb),
        grid=(n // tm,),
        in_specs=in_specs, out_specs=out_specs, out_shape=out_shape,
        compiler_params=_cparams(("parallel",)),
        name="mla_prep",
    )(p2, p2, p2, q_norm.reshape(1, Q_LORA), kv_norm.reshape(1, KV_LORA), tabq, tabk, wq, *w_extra)


def _flash_kernel(qt_ref, kv_ref, kvt_ref, kp_ref, o_ref, m_scr, l_scr, acc_scr, *, tq):
    qi = pl.program_id(1)
    key = lax.broadcasted_iota(jnp.int32, (tq, tq), 0)
    qry = lax.broadcasted_iota(jnp.int32, (tq, tq), 1)
    m_scr[...] = jnp.full_like(m_scr, NEG_INF)
    l_scr[...] = jnp.zeros_like(l_scr)
    acc_scr[...] = jnp.zeros_like(acc_scr)

    def key_tile(k0, masked):
        kp = kp_ref[0, pl.ds(k0, tq), :]
        for h in range(H_B):
            keys = jnp.concatenate([kv_ref[0, pl.ds(k0, tq), h * 128:(h + 1) * 128], kp], axis=1)
            s = jnp.dot(keys, qt_ref[0, h * QW:(h + 1) * QW, :], preferred_element_type=F32)
            if masked:
                s = jnp.where(key <= qry, s, NEG_INF)
            m = m_scr[h]
            m_new = jnp.maximum(m, jnp.max(s, axis=0, keepdims=True))
            alpha = jnp.exp(m - m_new)
            p = jnp.exp(s - m_new)
            l_scr[h] = alpha * l_scr[h] + jnp.sum(p, axis=0, keepdims=True)
            acc_scr[h] = alpha * acc_scr[h] + jnp.dot(kvt_ref[0, h * 128:(h + 1) * 128, pl.ds(k0, tq)],
                                                      p.astype(BF16), preferred_element_type=F32)
            m_scr[h] = m_new

    def body(kt, carry):
        key_tile(pl.multiple_of(kt * tq, tq), False)
        return carry

    lax.fori_loop(0, qi, body, 0)
    key_tile(pl.multiple_of(qi * tq, tq), True)
    o_t = jnp.concatenate([(acc_scr[h] / l_scr[h])[D_NOPE:D_NOPE + DV_B] for h in range(H_B)], axis=0)
    o_ref[0] = o_t.T.astype(o_ref.dtype)


def _flash(qt3, kv3, kvt3, kp3, tq):
    b, t, _ = kv3.shape
    return pl.pallas_call(
        functools.partial(_flash_kernel, tq=tq),
        grid=(b, t // tq),
        in_specs=[pl.BlockSpec((1, H_B * QW, tq), lambda i, j: (i, 0, j)),
                  pl.BlockSpec((1, t, H_B * 128), lambda i, j: (i, 0, 0)),
                  pl.BlockSpec((1, H_B * 128, t), lambda i, j: (i, 0, 0)),
                  pl.BlockSpec((1, t, 128), lambda i, j: (i, 0, 0))],
        out_specs=pl.BlockSpec((1, tq, H_B * DV_B), lambda i, j: (i, j, 0)),
        out_shape=jax.ShapeDtypeStruct((b, t, H_B * DV_B), BF16),
        scratch_shapes=[pltpu.VMEM((H_B, 1, tq), F32), pltpu.VMEM((H_B, 1, tq), F32),
                        pltpu.VMEM((H_B, 128, tq), F32)],
        compiler_params=_cparams(("parallel", "arbitrary")),
        name="mla_flash",
    )(qt3, kv3, kvt3, kp3)


def _decode_kernel(pt_ref, qlat_ref, qpe_ref, latn_ref, krn_ref, wuv_ref, lat_hbm, kct_hbm, o_ref,
                   lat_buf, kct_buf, lat_sem, kct_sem, *, gp, n_groups, t_new):
    b = pl.program_id(0)
    nb = pl.num_programs(0)
    nq = H_B * t_new

    def page_copies(bi, g, slot):
        copies = []
        for j in range(gp):
            page = pt_ref[bi, g * gp + j]
            copies.append(pltpu.make_async_copy(lat_hbm.at[page], lat_buf.at[slot, pl.ds(j * PAGE, PAGE), :],
                                                lat_sem.at[slot]))
            copies.append(pltpu.make_async_copy(kct_hbm.at[page], kct_buf.at[slot, j], kct_sem.at[slot]))
        return copies

    def start_group(bi, g, slot):
        for cp in page_copies(bi, g, slot):
            cp.start()

    def wait_group(bi, g, slot):
        for cp in page_copies(bi, g, slot):
            cp.wait()

    @pl.when(b == 0)
    def _():
        start_group(0, 0, 0)

    ql = jnp.concatenate([qlat_ref[0, :, h * KV_LORA:(h + 1) * KV_LORA] for h in range(H_B)], axis=0).astype(BF16)
    qp = jnp.concatenate([qpe_ref[0, :, h * D_ROPE:(h + 1) * D_ROPE] for h in range(H_B)], axis=0).astype(BF16)

    def group_softmax(s):
        m_g = jnp.max(s, axis=-1, keepdims=True)
        p = jnp.exp(s - m_g)
        return p.astype(BF16), m_g, jnp.sum(p, axis=-1, keepdims=True)

    def merge(state, group):
        m_old, l_old, acc_old = state
        p, m_g, l_g, values = group
        pv = jnp.dot(p, values, preferred_element_type=F32)
        m_new = jnp.maximum(m_old, m_g)
        a_old = jnp.exp(m_old - m_new)
        a_g = jnp.exp(m_g - m_new)
        return m_new, a_old * l_old + a_g * l_g, a_old * acc_old + a_g * pv

    state = (jnp.full((nq, 1), NEG_INF, F32), jnp.zeros((nq, 1), F32), jnp.zeros((nq, KV_LORA), F32))
    pending = None
    for g in range(n_groups):
        slot = g % 2
        wait_group(b, g, slot)
        if g + 1 < n_groups:
            start_group(b, g + 1, 1 - slot)
        else:
            @pl.when(b + 1 < nb)
            def _():
                start_group(b + 1, 0, 1 - slot)
        lat = lat_buf[slot].astype(BF16)
        kct = jnp.concatenate([kct_buf[slot, j] for j in range(gp)], axis=1).astype(BF16)
        s = (lax.dot_general(ql, lat, (((1,), (1,)), ((), ())), preferred_element_type=F32)
             + jnp.dot(qp, kct, preferred_element_type=F32))
        if pending is not None:
            state = merge(state, pending)
        pending = group_softmax(s) + (lat,)
    state = merge(state, pending)

    latn = latn_ref[0].astype(BF16)
    krn = krn_ref[0].astype(BF16)
    s = (lax.dot_general(ql, latn, (((1,), (1,)), ((), ())), preferred_element_type=F32)
         + lax.dot_general(qp, krn, (((1,), (1,)), ((), ())), preferred_element_type=F32))
    tok = lax.broadcasted_iota(jnp.int32, (nq, t_new), 0) % t_new
    key = lax.broadcasted_iota(jnp.int32, (nq, t_new), 1)
    _, l, acc = merge(state, group_softmax(jnp.where(key <= tok, s, NEG_INF)) + (latn,))
    o_lat = acc / l
    outs = [_dot(o_lat[h * t_new:(h + 1) * t_new], wuv_ref[h]) for h in range(H_B)]
    o_ref[0] = jnp.concatenate(outs, axis=1).astype(o_ref.dtype)


def _decode(page_table, qlat3, qpe3, latn3, krn3, wuv, cache_lat, cache_kpe_t, gp):
    b, t_new, _ = qlat3.shape
    n_pages = page_table.shape[1]
    n_groups = n_pages // gp
    assert n_groups % 2 == 0
    per_batch = lambda w: pl.BlockSpec((1, t_new, w), lambda i, pt: (i, 0, 0))
    in_specs = [per_batch(H_B * KV_LORA), per_batch(H_B * D_ROPE), per_batch(KV_LORA), per_batch(D_ROPE),
                pl.BlockSpec(wuv.shape, lambda i, pt: (0, 0, 0)),
                pl.BlockSpec(memory_space=pl.ANY), pl.BlockSpec(memory_space=pl.ANY)]
    return pl.pallas_call(
        functools.partial(_decode_kernel, gp=gp, n_groups=n_groups, t_new=t_new),
        grid_spec=pltpu.PrefetchScalarGridSpec(
            num_scalar_prefetch=1,
            grid=(b,),
            in_specs=in_specs,
            out_specs=pl.BlockSpec((1, t_new, H_B * DV_B), lambda i, pt: (i, 0, 0)),
            scratch_shapes=[pltpu.VMEM((2, gp * PAGE, KV_LORA), F32), pltpu.VMEM((2, gp, D_ROPE, PAGE), F32),
                            pltpu.SemaphoreType.DMA((2,)), pltpu.SemaphoreType.DMA((2,))]),
        out_shape=jax.ShapeDtypeStruct((b, t_new, H_B * DV_B), BF16),
        compiler_params=_cparams(("arbitrary",)),
        name="mla_decode",
    )(page_table, qlat3, qpe3, latn3, krn3, wuv, cache_lat, cache_kpe_t)


def _rot_cols(w):
    half = D_ROPE // 2
    return jnp.concatenate([-w[..., half:], w[..., :half]], axis=-1)


def _layout_w_in_ab(w):
    d = w.shape[0]
    o = QKV_A + H_A * DV_A
    qkv_z, b_w, a_w = w[:, :o], w[:, o:o + H_A], w[:, o + H_A:o + 2 * H_A]
    o += 2 * H_A
    cq, ckv, kpe = w[:, o:o + Q_LORA], w[:, o + Q_LORA:o + Q_LORA + KV_LORA], w[:, o + Q_LORA + KV_LORA:]
    kr = _rot_cols(kpe)
    pad = jnp.zeros((d, 128 - 2 * H_A), w.dtype)
    return jnp.concatenate([qkv_z, cq, ckv, kpe, kr, kpe, kr, b_w, a_w, pad], axis=1).astype(BF16)


def _layout_w_uq(w_uq):
    nope, pe = w_uq[..., :D_NOPE], w_uq[..., D_NOPE:]
    rot = _rot_cols(pe)
    z = jnp.zeros_like(nope)
    return jnp.concatenate([nope, z, pe, pe, rot, rot], axis=-1).reshape(w_uq.shape[0], H_B * QW).astype(BF16)


def _rope_tables(pos):
    half = D_ROPE // 2
    inv = ROPE_BASE ** (-jnp.arange(half, dtype=F32) / half)
    ang = pos.astype(F32)[:, None] * inv
    c = jnp.concatenate([jnp.cos(ang), jnp.cos(ang)], axis=1)
    s = jnp.concatenate([jnp.sin(ang), jnp.sin(ang)], axis=1)
    one = jnp.ones((pos.shape[0], 128), F32)
    tabq = jnp.concatenate([one, c, c, s, s], axis=1) * MLA_SCALE
    tabk = jnp.concatenate([c, s, c, s], axis=1)
    return tabq, tabk


def _row_tile(n):
    for tm in (512, 256, 128, 64, 32, 16, 8):
        if n % tm == 0:
            return tm
    raise ValueError(n)


def _trunk(x, pos, conv0, gdn0, hgrn0, paged, wts):
    b, t, d = x.shape
    n = b * t
    tm = min(256, _row_tile(n))
    tb = min(256, t)
    x2 = x.reshape(n, d)

    p2 = _proj(x2, wts["mix_norm"][0], wts["w_in_ab"], tm)
    p3 = p2.reshape(b, t, IN_AB_PAD)
    conv_new = p3[:, t - (CONV_W - 1):, :QKV_A]
    o_a, gdn_new = _gdn(p3, wts["conv_w"], wts["a_log"], wts["dt_bias"], wts["gdn_norm"], conv0, gdn0, tb)
    tabq, tabk = _rope_tables(pos)
    if paged is None:
        tmp = math.gcd(tm, t)
        lat, kr, qt, kv, kvt, kp4 = _mla_prep(p2, wts["q_norm"], wts["kv_norm"], tabq.T, tabk, wts["w_uq"].T,
                                              [wts["w_kv"], wts["w_kv"].T], False, tmp, t)
        o_b = _flash(qt, kv.reshape(b, t, -1), kvt, kp4.reshape(b, t, -1), min(512, t))
    else:
        cache_lat, cache_kpe, page_table = paged
        reps = tm // t
        lat, kr, qlat, qpe = _mla_prep(p2, wts["q_norm"], wts["kv_norm"], jnp.tile(tabq, (reps, 1)),
                                       jnp.tile(tabk, (reps, 1)), wts["w_uq"], [wts["w_ukt"]], True, tm, t)
        n_pages = page_table.shape[1]
        gp = math.gcd(n_pages // 2, 16)
        o_b = _decode(page_table, qlat.reshape(b, t, -1), qpe.reshape(b, t, -1), lat.reshape(b, t, -1),
                      kr.reshape(b, t, -1), wts["w_uv"], cache_lat, jnp.swapaxes(cache_kpe, 1, 2), gp)
    x2 = _outproj(x2, [o_a.reshape(n, -1), o_b.reshape(n, -1)], [wts["w_out_a"], wts["w_out_b"]], tm)
    tmm = min(512, _row_tile(n))
    x2 = _mlp(x2, wts["mlp_norm"][0], wts["w_up"][0], wts["w_down"][0], wts["final_norm"], False, tmm, 1024)

    pc = _proj(x2, wts["mix_norm"][1], wts["w_in_c"], tm)
    o_c, hgrn_new = _hgrn(pc.reshape(b, t, -1), wts["lb_logits"], 1, wts["g_norm_c"], hgrn0, tb,
                          H_C if hgrn0 is not None else 2)
    x2 = _outproj(x2, [o_c.reshape(n, -1)], [wts["w_out_c"]], tm)
    y2 = _mlp(x2, wts["mlp_norm"][1], wts["w_up"][1], wts["w_down"][1], wts["final_norm"], True, tmm, 1024)
    return (y2.reshape(b, t, d), gdn_new[None], conv_new[None], lat.reshape(b, t, -1)[None],
            kr.reshape(b, t, -1)[None], hgrn_new[None])


def kernel(x_prompt, x_sample, state_gdn, state_gdn_conv, cache_mla_latent, cache_mla_krope, state_hgrn,
           page_table, mix_norm, mlp_norm, final_norm, w_up, w_down, w_in_ab, conv_w_ab, a_log_ab, dt_bias_ab,
           gdn_norm_ab, q_norm_ab, w_uq_ab, kv_norm_ab, w_uk_ab, w_uv_ab, w_out_ab, w_in_c, lb_logits_c,
           g_norm_c, w_out_c):
    assert mix_norm.shape[0] == 2 and w_in_ab.shape[0] == 1 and w_in_c.shape[0] == 1
    assert a_log_ab.shape == (1, H_A) and conv_w_ab.shape == (1, CONV_W, QKV_A)
    assert w_uq_ab.shape[1:] == (Q_LORA, H_B, D_NOPE + D_ROPE) and w_uk_ab.shape[1:] == (KV_LORA, H_B, D_NOPE)
    assert cache_mla_latent.shape[2] == PAGE and x_prompt.shape[1] >= CONV_W - 1 and x_sample.shape[1] >= CONV_W - 1
    w_uk, w_uv = w_uk_ab[0], w_uv_ab[0]
    wts = {
        "mix_norm": mix_norm, "mlp_norm": mlp_norm, "final_norm": final_norm,
        "w_up": w_up.astype(BF16), "w_down": w_down.astype(BF16),
        "w_in_ab": _layout_w_in_ab(w_in_ab[0]), "conv_w": conv_w_ab[0], "a_log": a_log_ab[0],
        "dt_bias": dt_bias_ab[0], "gdn_norm": gdn_norm_ab[0], "q_norm": q_norm_ab[0], "kv_norm": kv_norm_ab[0],
        "w_uq": _layout_w_uq(w_uq_ab[0]),
        "w_kv": jnp.concatenate([w_uk, w_uv], axis=-1).reshape(KV_LORA, H_B * 128).astype(BF16),
        "w_ukt": jnp.transpose(w_uk, (1, 2, 0)).astype(BF16),
        "w_uv": jnp.transpose(w_uv, (1, 0, 2)).astype(BF16),
        "w_out_a": w_out_ab[0, :H_A * DV_A].astype(BF16), "w_out_b": w_out_ab[0, H_A * DV_A:].astype(BF16),
        "w_in_c": w_in_c[0].astype(BF16), "lb_logits": lb_logits_c, "g_norm_c": g_norm_c[0],
        "w_out_c": w_out_c[0].astype(BF16),
    }
    s_len = x_prompt.shape[1]
    outs_p = _trunk(x_prompt, jnp.arange(s_len), None, None, None, None, wts)
    past_len = page_table.shape[1] * cache_mla_latent.shape[2]
    outs_s = _trunk(x_sample, past_len + jnp.arange(x_sample.shape[1]),
                    state_gdn_conv.reshape(state_gdn_conv.shape[1:]), state_gdn.reshape(state_gdn.shape[1:]),
                    state_hgrn.reshape(state_hgrn.shape[1:]),
                    (cache_mla_latent.reshape(cache_mla_latent.shape[1:]),
                     cache_mla_krope.reshape(cache_mla_krope.shape[1:]), page_table), wts)
    return (outs_p[0], outs_s[0]) + outs_p[1:] + outs_s[1:]
```

```python
import functools
import math

import jax
import jax.numpy as jnp
from jax import lax
from jax.experimental import pallas as pl
from jax.experimental.pallas import tpu as pltpu

F32 = jnp.float32
BF16 = jnp.bfloat16
EPS = 1e-6
NEG_INF = float("-inf")

H_A, DK_A, DV_A, CONV_W = 4, 128, 128, 4
QKV_A = H_A * (2 * DK_A + DV_A)
H_B, D_NOPE, D_ROPE, DV_B = 8, 64, 32, 64
Q_LORA, KV_LORA = 256, 256
ROPE_BASE = 10000.0
MLA_SCALE = (D_NOPE + D_ROPE) ** -0.5
H_C, DK_C, DV_C = 8, 128, 128
CHUNK = 64
SUB = 16
PAGE = 128
QW = 256
IN_AB_PAD = 2816
V7X_VMEM_LIMIT = 48 * 1024 * 1024


def _cparams(sem):
    return pltpu.CompilerParams(dimension_semantics=sem, vmem_limit_bytes=V7X_VMEM_LIMIT)


def _const_spec(shape):
    nd = len(shape)
    return pl.BlockSpec(shape, lambda *_: (0,) * nd)


def _dot(a, b):
    return jnp.dot(a.astype(BF16), b.astype(BF16), preferred_element_type=F32)


def _dot_nt(a, b):
    return lax.dot_general(a.astype(BF16), b.astype(BF16), (((1,), (1,)), ((), ())),
                           preferred_element_type=F32)


def _dot_tn(a, b):
    return lax.dot_general(a.astype(BF16), b.astype(BF16), (((0,), (0,)), ((), ())),
                           preferred_element_type=F32)


def _rms(x, g):
    return x * lax.rsqrt(jnp.mean(x * x, axis=-1, keepdims=True) + EPS) * g


def _silu(x):
    return x * jax.nn.sigmoid(x)


def _cumsum_rows(x):
    n = x.shape[0]
    row = lax.broadcasted_iota(jnp.int32, x.shape, 0)
    s = 1
    while s < n:
        x = x + jnp.where(row >= s, pltpu.roll(x, s, axis=0), 0.0)
        s *= 2
    return x


def _proj_kernel(x_ref, g_ref, w_ref, o_ref):
    h = _rms(x_ref[...], g_ref[...]).astype(BF16)
    o_ref[...] = jnp.dot(h, w_ref[...], preferred_element_type=F32)


def _proj(x2, g, w, tm):
    n, d = x2.shape
    nout = w.shape[1]
    return pl.pallas_call(
        _proj_kernel,
        grid=(n // tm,),
        in_specs=[pl.BlockSpec((tm, d), lambda i: (i, 0)), _const_spec((1, d)), _const_spec((d, nout))],
        out_specs=pl.BlockSpec((tm, nout), lambda i: (i, 0)),
        out_shape=jax.ShapeDtypeStruct((n, nout), F32),
        compiler_params=_cparams(("parallel",)),
        name="proj",
    )(x2, g.reshape(1, d), w)


def _outproj_kernel(*refs, n_in):
    r_ref = refs[0]
    a_refs = refs[1:1 + n_in]
    w_refs = refs[1 + n_in:1 + 2 * n_in]
    o_ref = refs[1 + 2 * n_in]
    acc = r_ref[...]
    for a_ref, w_ref in zip(a_refs, w_refs):
        acc = acc + jnp.dot(a_ref[...], w_ref[...], preferred_element_type=F32)
    o_ref[...] = acc


def _outproj(res, acts, ws, tm):
    n, d = res.shape
    n_in = len(acts)
    in_specs = [pl.BlockSpec((tm, d), lambda i: (i, 0))]
    in_specs += [pl.BlockSpec((tm, a.shape[1]), lambda i: (i, 0)) for a in acts]
    in_specs += [_const_spec(w.shape) for w in ws]
    return pl.pallas_call(
        functools.partial(_outproj_kernel, n_in=n_in),
        grid=(n // tm,),
        in_specs=in_specs,
        out_specs=pl.BlockSpec((tm, d), lambda i: (i, 0)),
        out_shape=jax.ShapeDtypeStruct((n, d), F32),
        compiler_params=_cparams(("parallel",)),
        name="outproj",
    )(res, *acts, *ws)


def _mlp_kernel(x_ref, g_ref, gf_ref, wu_ref, wd_ref, o_ref, h_ref, *, final_norm):
    j = pl.program_id(1)

    @pl.when(j == 0)
    def _():
        x = x_ref[...]
        h_ref[...] = _rms(x, g_ref[...]).astype(BF16)
        o_ref[...] = x

    a = jnp.dot(h_ref[...], wu_ref[...], preferred_element_type=F32)
    a = jnp.square(jnp.maximum(a, 0.0))
    o_ref[...] += jnp.dot(a.astype(BF16), wd_ref[...], preferred_element_type=F32)

    if final_norm:
        @pl.when(j == pl.num_programs(1) - 1)
        def _():
            o_ref[...] = _rms(o_ref[...], gf_ref[...])


def _mlp(x2, g, w_up, w_down, gf, final_norm, tm, tf):
    n, d = x2.shape
    ff = w_up.shape[1]
    return pl.pallas_call(
        functools.partial(_mlp_kernel, final_norm=final_norm),
        grid=(n // tm, ff // tf),
        in_specs=[pl.BlockSpec((tm, d), lambda i, j: (i, 0)), _const_spec((1, d)), _const_spec((1, d)),
                  pl.BlockSpec((d, tf), lambda i, j: (0, j)), pl.BlockSpec((tf, d), lambda i, j: (j, 0))],
        out_specs=pl.BlockSpec((tm, d), lambda i, j: (i, 0)),
        out_shape=jax.ShapeDtypeStruct((n, d), F32),
        scratch_shapes=[pltpu.VMEM((tm, d), BF16)],
        compiler_params=_cparams(("parallel", "arbitrary")),
        name="mlp",
    )(x2, g.reshape(1, d), gf.reshape(1, d), w_up, w_down)


def _split3_dot(a, b):
    a_hi = a.astype(BF16)
    a_lo = (a - a_hi.astype(F32)).astype(BF16)
    b_hi = b.astype(BF16)
    b_lo = (b - b_hi.astype(F32)).astype(BF16)
    return jnp.dot(jnp.concatenate([a_hi, a_lo, a_hi], axis=1), jnp.concatenate([b_hi, b_hi, b_lo], axis=0),
                   preferred_element_type=F32)


def _pick_dot(a, pick):
    a_hi = a.astype(BF16)
    a_lo = (a - a_hi.astype(F32)).astype(BF16)
    pick = pick.astype(BF16)
    return jnp.dot(jnp.concatenate([a_hi, a_lo], axis=1), jnp.concatenate([pick, pick], axis=0),
                   preferred_element_type=F32)


def _solve_unit_lower(lowers, rhss, c):
    nh = len(lowers)
    s = 8
    nb = c // s
    w = nh * c
    row = lax.broadcasted_iota(jnp.int32, (c, w), 0)
    col = lax.broadcasted_iota(jnp.int32, (c, w), 1) % c
    if nb == 1:
        diag_cols = lowers
    else:
        r1 = lax.broadcasted_iota(jnp.int32, (c, c), 0)
        c1 = lax.broadcasted_iota(jnp.int32, (c, c), 1)
        pick = (lax.broadcasted_iota(jnp.int32, (c, s), 0) % s
                == lax.broadcasted_iota(jnp.int32, (c, s), 1)).astype(F32)
        diag_cols = [_pick_dot(jnp.where(r1 // s == c1 // s, lo, 0.0), pick) for lo in lowers]
    x = (row == col).astype(F32)
    for j in range(s - 1):
        mult = jnp.concatenate([jnp.broadcast_to(dc[:, j:j + 1], (c, c)) for dc in diag_cols], axis=1)
        pivot_rows = jnp.broadcast_to(x.reshape(nb, s, w)[:, j:j + 1, :], (nb, s, w)).reshape(c, w)
        x = x - mult * pivot_rows

    def block_diag(m):
        r2 = lax.broadcasted_iota(jnp.int32, (w, w), 0)
        c2 = lax.broadcasted_iota(jnp.int32, (w, w), 1)
        return jnp.where(r2 // c == c2 // c, jnp.concatenate([m] * nh, axis=0), 0.0)

    lower_cat = jnp.concatenate(lowers, axis=1)
    size = s
    while size < c:
        off = jnp.where((row // (2 * size) == col // (2 * size)) & (row // size != col // size) & (row > col),
                        lower_cat, 0.0)
        x = x - _split3_dot(_split3_dot(x, block_diag(off)), block_diag(x))
        size *= 2
    out = _split3_dot(block_diag(x), jnp.concatenate(rhss, axis=0))
    return [out[h * c:(h + 1) * c] for h in range(nh)]


def _gdn_head_pre(q, k, v, b_col, a_col, alog, dtb, c):
    q = q * lax.rsqrt(jnp.sum(q * q, axis=-1, keepdims=True) + EPS) * (DK_A ** -0.5)
    k = k * lax.rsqrt(jnp.sum(k * k, axis=-1, keepdims=True) + EPS)
    beta = jax.nn.sigmoid(b_col)
    g = -jnp.exp(alog) * jax.nn.softplus(a_col + dtb)
    gc_b = _cumsum_rows(jnp.broadcast_to(g, (c, 128)))
    gc = gc_b[:, 0:1]
    diff = gc_b[:, :c] - gc_b.T[:c, :]
    row = lax.broadcasted_iota(jnp.int32, (c, c), 0)
    col = lax.broadcasted_iota(jnp.int32, (c, c), 1)
    incl = row >= col
    decay = jnp.where(incl, jnp.exp(jnp.where(incl, diff, 0.0)), 0.0)
    kb = k * beta
    lower = jnp.where(row > col, _dot_nt(kb, k) * decay, 0.0)
    rhs = jnp.concatenate([v * beta, kb * jnp.exp(gc)], axis=1)
    intra = _dot_nt(q, k) * decay
    return q, k, gc, lower, rhs, intra


def _gdn_head_post(q, k, gc, uw, intra, z, gnorm, s_prev, c):
    u, w = uw[:, :DV_A], uw[:, DV_A:]
    v_new = u - _dot(w, s_prev)
    o = _dot(q * jnp.exp(gc), s_prev) + _dot(intra, v_new)
    g_last = gc[c - 1:c, :]
    s_new = s_prev * jnp.exp(g_last) + _dot_tn(k * jnp.exp(g_last - gc), v_new)
    o = _rms(o, gnorm) * _silu(z)
    return o, s_new


def _gdn_kernel(*refs, c, n_chunks, has_state, bg):
    (q_ref, k_ref, v_ref, z_ref, gt_ref, cwq_ref, cwk_ref, cwv_ref, alog_ref, dtb_ref, gn_ref) = refs[:11]
    if has_state:
        cq_ref, ck_ref, cv_ref, s0_ref = refs[11:15]
        o_ref, sout_ref, s_scr, tail_scr = refs[15:]
    else:
        o_ref, sout_ref, s_scr, tail_scr = refs[11:]
    n = pl.program_id(1)

    @pl.when(n == 0)
    def _():
        if has_state:
            s_scr[...] = s0_ref[...]
            tail_scr[0] = cq_ref[...]
            tail_scr[1] = ck_ref[...]
            tail_scr[2] = cv_ref[...]
        else:
            s_scr[...] = jnp.zeros_like(s_scr)
            tail_scr[...] = jnp.zeros_like(tail_scr)

    def chunk(ci):
        rows = pl.ds(ci * c, c)
        pre = []
        zs = []
        for bi in range(bg):
            conv = []
            for idx, (x_ref, cw_ref) in enumerate(((q_ref, cwq_ref), (k_ref, cwk_ref), (v_ref, cwv_ref))):
                x = x_ref[bi, rows, :]
                ext = jnp.concatenate([tail_scr[idx, bi], x], axis=0)
                cw = cw_ref[...]
                y = x * cw[CONV_W - 1:CONV_W, :]
                for sh in range(1, CONV_W):
                    y = y + pltpu.roll(ext, sh, axis=0)[8:] * cw[CONV_W - 1 - sh:CONV_W - sh, :]
                tail_scr[idx, bi] = x[c - 8:, :]
                conv.append(_silu(y))
            zs.append(z_ref[bi, rows, :])
            gt = gt_ref[bi, rows, :]
            for h in range(H_A):
                hs = slice(h * 128, (h + 1) * 128)
                pre.append(_gdn_head_pre(conv[0][:, hs], conv[1][:, hs], conv[2][:, hs],
                                         gt[:, h:h + 1], gt[:, H_A + h:H_A + h + 1],
                                         alog_ref[:, h:h + 1], dtb_ref[:, h:h + 1], c))
        uws = _solve_unit_lower([p[3] for p in pre], [p[4] for p in pre], c)
        for bi in range(bg):
            outs = []
            for h in range(H_A):
                q, k, gc, _, _, intra = pre[bi * H_A + h]
                o, s_new = _gdn_head_post(q, k, gc, uws[bi * H_A + h], intra, zs[bi][:, h * 128:(h + 1) * 128],
                                          gn_ref[...], s_scr[bi, h], c)
                s_scr[bi, h] = s_new
                outs.append(o)
            o_ref[bi, rows, :] = jnp.concatenate(outs, axis=1).astype(o_ref.dtype)

    for ci in range(n_chunks):
        chunk(ci)

    @pl.when(n == pl.num_programs(1) - 1)
    def _():
        sout_ref[...] = s_scr[...]


def _gdn(p3, conv_w, a_log, dt_bias, gdn_norm, conv0, s0, tb, bg):
    b, t, _ = p3.shape
    c = math.gcd(t, CHUNK)
    has_state = s0 is not None
    hw = H_A * 128
    cw = conv_w
    blk = lambda j: pl.BlockSpec((bg, tb, hw), lambda i, n: (i, n, j))
    in_specs = [blk(0), blk(1), blk(2), blk(3),
                pl.BlockSpec((bg, tb, 128), lambda i, n: (i, n, (IN_AB_PAD - 128) // 128)),
                pl.BlockSpec((CONV_W, hw), lambda i, n: (0, 0)), pl.BlockSpec((CONV_W, hw), lambda i, n: (0, 1)),
                pl.BlockSpec((CONV_W, hw), lambda i, n: (0, 2)),
                _const_spec((1, H_A)), _const_spec((1, H_A)), _const_spec((1, DV_A))]
    args = [p3, p3, p3, p3, p3, cw, cw, cw, a_log.reshape(1, H_A), dt_bias.reshape(1, H_A),
            gdn_norm.reshape(1, DV_A)]
    if has_state:
        conv8 = jnp.pad(conv0, ((0, 0), (8 - (CONV_W - 1), 0), (0, 0)))
        in_specs += [pl.BlockSpec((bg, 8, hw), lambda i, n, j=j: (i, 0, j)) for j in range(3)]
        in_specs += [pl.BlockSpec((bg, H_A, DK_A, DV_A), lambda i, n: (i, 0, 0, 0))]
        args += [conv8, conv8, conv8, s0]
    return pl.pallas_call(
        functools.partial(_gdn_kernel, c=c, n_chunks=tb // c, has_state=has_state, bg=bg),
        grid=(b // bg, t // tb),
        in_specs=in_specs,
        out_specs=[pl.BlockSpec((bg, tb, hw), lambda i, n: (i, n, 0)),
                   pl.BlockSpec((bg, H_A, DK_A, DV_A), lambda i, n: (i, 0, 0, 0))],
        out_shape=[jax.ShapeDtypeStruct((b, t, hw), BF16),
                   jax.ShapeDtypeStruct((b, H_A, DK_A, DV_A), F32)],
        scratch_shapes=[pltpu.VMEM((bg, H_A, DK_A, DV_A), F32), pltpu.VMEM((3, bg, 8, hw), F32)],
        compiler_params=_cparams(("parallel", "arbitrary")),
        name="gdn",
    )(*args)


def _hgrn_chunk(qr, fr, v, lb, s_prev, c):
    sub = min(SUB, c)
    q = _silu(qr) * (DK_C ** -0.5)
    f = lb + (1.0 - lb) * jax.nn.sigmoid(fr)
    k = 1.0 - f
    gc = _cumsum_rows(jnp.log(f))
    o_inter = _dot(q * jnp.exp(gc), s_prev)
    row = lax.broadcasted_iota(jnp.int32, (sub, sub), 0)
    col = lax.broadcasted_iota(jnp.int32, (sub, sub), 1)
    outs = []
    for blk in range(c // sub):
        r0 = blk * sub
        g_i = gc[r0:r0 + sub]
        q_i = q[r0:r0 + sub]
        a_parts = [jnp.zeros((8, sub), F32) for _ in range(sub // 8)]
        for j in range(sub):
            jj = r0 + j
            lo = 8 * (j // 8)
            e = jnp.exp(g_i[lo:] - gc[jj:jj + 1, :])
            col_j = jnp.sum(q_i[lo:] * k[jj:jj + 1, :] * e, axis=-1, keepdims=True)
            for p in range(j // 8, sub // 8):
                a_parts[p] = jnp.where(col[:8] == j, col_j[8 * p - lo:8 * p - lo + 8], a_parts[p])
        a_diag = jnp.concatenate(a_parts, axis=0) if len(a_parts) > 1 else a_parts[0]
        a_diag = jnp.where(row >= col, a_diag, 0.0)
        o_i = _dot(a_diag, v[r0:r0 + sub])
        if blk > 0:
            g_ref = gc[r0 - 1:r0, :]
            a_off = _dot_nt(q_i * jnp.exp(g_i - g_ref), k[:r0] * jnp.exp(g_ref - gc[:r0]))
            o_i = o_i + _dot(a_off, v[:r0])
        outs.append(o_i)
    o = o_inter + (jnp.concatenate(outs, axis=0) if len(outs) > 1 else outs[0])
    g_last = gc[c - 1:c, :]
    gl_col = gc.T[:, c - 1:c]
    s_new = s_prev * jnp.exp(gl_col) + _dot_tn(k * jnp.exp(g_last - gc), v)
    return o, s_new


def _hgrn_kernel(*refs, c, n_chunks, layer, has_state, hg):
    q_ref, f_ref, i_ref, gate_ref, lbl_ref, gn_ref = refs[:6]
    if has_state:
        s0_ref = refs[6]
        o_ref, sout_ref, s_scr = refs[7:]
    else:
        o_ref, sout_ref, s_scr = refs[6:]
    n = pl.program_id(2)

    @pl.when(n == 0)
    def _():
        if has_state:
            s_scr[...] = s0_ref[0]
        else:
            s_scr[...] = jnp.zeros_like(s_scr)

    lbl = lbl_ref[...]
    ex = jnp.exp(lbl - jnp.max(lbl, axis=0, keepdims=True))
    p = ex / jnp.sum(ex, axis=0, keepdims=True)
    lb = jnp.sum(p[:layer + 1], axis=0, keepdims=True) - p[0:1]

    def chunk(ci, carry):
        r0 = ci * c if isinstance(ci, int) else pl.multiple_of(ci * c, c)
        rows = pl.ds(r0, c)
        for h in range(hg):
            hs = slice(h * 128, (h + 1) * 128)
            o, s_new = _hgrn_chunk(q_ref[0, rows, hs], f_ref[0, rows, hs], i_ref[0, rows, hs], lb[:, hs],
                                   s_scr[h], c)
            s_scr[h] = s_new
            o_ref[0, rows, hs] = (_rms(o, gn_ref[...]) * _silu(gate_ref[0, rows, hs])).astype(o_ref.dtype)
        return carry

    if n_chunks == 1:
        chunk(0, 0)
    else:
        lax.fori_loop(0, n_chunks, chunk, 0)

    @pl.when(n == pl.num_programs(2) - 1)
    def _():
        sout_ref[0] = s_scr[...]


def _hgrn(pc3, lb_logits, layer, g_norm, s0, tb, hg):
    b, t, _ = pc3.shape
    c = math.gcd(t, CHUNK)
    has_state = s0 is not None
    depth = lb_logits.shape[0]
    ng = H_C // hg
    blk = lambda j: pl.BlockSpec((1, tb, hg * 128), lambda i, h, n: (i, n, j * ng + h))
    in_specs = [blk(0), blk(1), blk(2), blk(3),
                pl.BlockSpec((depth, hg * 128), lambda i, h, n: (0, h)), _const_spec((1, DV_C))]
    args = [pc3, pc3, pc3, pc3, lb_logits, g_norm.reshape(1, DV_C)]
    if has_state:
        in_specs.append(pl.BlockSpec((1, hg, DK_C, DV_C), lambda i, h, n: (i, h, 0, 0)))
        args.append(s0)
    return pl.pallas_call(
        functools.partial(_hgrn_kernel, c=c, n_chunks=tb // c, layer=layer, has_state=has_state, hg=hg),
        grid=(b, ng, t // tb),
        in_specs=in_specs,
        out_specs=[pl.BlockSpec((1, tb, hg * 128), lambda i, h, n: (i, n, h)),
                   pl.BlockSpec((1, hg, DK_C, DV_C), lambda i, h, n: (i, h, 0, 0))],
        out_shape=[jax.ShapeDtypeStruct((b, t, H_C * DV_C), BF16),
                   jax.ShapeDtypeStruct((b, H_C, DK_C, DV_C), F32)],
        scratch_shapes=[pltpu.VMEM((hg, DK_C, DV_C), F32)],
        compiler_params=_cparams(("parallel", "parallel", "arbitrary")),
        name="hgrn",
    )(*args)


def _mla_prep_kernel(*refs, absorb):
    cq_ref, ckv_ref, kp_ref, qn_ref, kvn_ref, tq_ref, tk_ref, wq_ref = refs[:8]
    if absorb:
        wukt_ref, lat_ref, kr_ref, qlat_ref, qpe_ref = refs[8:]
    else:
        wkv_ref, wkvt_ref, lat_ref, kr_ref, qt_ref, kv_ref, kvt_ref, kp4_ref = refs[8:]
    lat = _rms(ckv_ref[...], kvn_ref[...])
    lat_ref[...] = lat
    kp4 = kp_ref[...] * tk_ref[...]
    kr_ref[...] = kp4[:, 0:D_ROPE] + kp4[:, D_ROPE:2 * D_ROPE]
    cqn = _rms(cq_ref[...], qn_ref[...]).astype(BF16)
    tq = tq_ref[...]
    if absorb:
        for h in range(H_B):
            qh = jnp.dot(cqn, wq_ref[:, h * QW:(h + 1) * QW], preferred_element_type=F32) * tq
            qlat_ref[:, h * KV_LORA:(h + 1) * KV_LORA] = _dot(qh[:, 0:D_NOPE], wukt_ref[h])
            qpe_ref[:, h * D_ROPE:(h + 1) * D_ROPE] = (qh[:, 128:128 + D_ROPE]
                                                      + qh[:, 128 + 2 * D_ROPE:128 + 3 * D_ROPE])
    else:
        nt_dims = (((1,), (1,)), ((), ()))
        for h in range(H_B):
            qh = lax.dot_general(wq_ref[h * QW:(h + 1) * QW, :], cqn, nt_dims, preferred_element_type=F32) * tq
            qt_ref[0, h * QW:(h + 1) * QW, :] = qh.astype(BF16)
        lat_b = lat.astype(BF16)
        kv_ref[...] = jnp.dot(lat_b, wkv_ref[...], preferred_element_type=F32).astype(BF16)
        kvt_ref[0] = lax.dot_general(wkvt_ref[...], lat_b, nt_dims, preferred_element_type=F32).astype(BF16)
        kp4_ref[...] = kp4.astype(BF16)


def _mla_prep(p2, q_norm, kv_norm, tabq, tabk, wq, w_extra, absorb, tm, seq):
    n = p2.shape[0]
    nt = seq // tm if not absorb else 1
    tabq_spec = (pl.BlockSpec((tm, QW), lambda i: (0, 0)) if absorb
                 else pl.BlockSpec((QW, tm), lambda i: (0, i % nt)))
    in_specs = [pl.BlockSpec((tm, Q_LORA), lambda i: (i, 2048 // 256)),
                pl.BlockSpec((tm, KV_LORA), lambda i: (i, 2304 // 256)),
                pl.BlockSpec((tm, 128), lambda i: (i, 2560 // 128)),
                _const_spec((1, Q_LORA)), _const_spec((1, KV_LORA)),
                tabq_spec, pl.BlockSpec((tm, 128), lambda i: (i % nt, 0)),
                _const_spec(wq.shape)] + [_const_spec(w.shape) for w in w_extra]
    row = lambda w: pl.BlockSpec((tm, w), lambda i: (i, 0))
    out_specs = [row(KV_LORA), row(D_ROPE)]
    out_shape = [jax.ShapeDtypeStruct((n, KV_LORA), F32), jax.ShapeDtypeStruct((n, D_ROPE), F32)]
    if absorb:
        out_specs += [row(H_B * KV_LORA), row(H_B * D_ROPE)]
        out_shape += [jax.ShapeDtypeStruct((n, H_B * KV_LORA), F32), jax.ShapeDtypeStruct((n, H_B * D_ROPE), F32)]
    else:
        b = n // seq
        col = lambda w: pl.BlockSpec((1, w, tm), lambda i: (i // nt, 0, i % nt))
        out_specs += [col(H_B * QW), row(H_B * 128), col(H_B * 128), row(128)]
        out_shape += [jax.ShapeDtypeStruct((b, H_B * QW, seq), BF16), jax.ShapeDtypeStruct((n, H_B * 128), BF16),
                      jax.ShapeDtypeStruct((b, H_B * 128, seq), BF16), jax.ShapeDtypeStruct((n, 128), BF16)]
    return pl.pallas_call(
        functools.partial(_mla_prep_kernel, absorb=absorb),
        grid=(n // tm,),
        in_specs=in_specs, out_specs=out_specs, out_shape=out_shape,
        compiler_params=_cparams(("parallel",)),
        name="mla_prep",
    )(p2, p2, p2, q_norm.reshape(1, Q_LORA), kv_norm.reshape(1, KV_LORA), tabq, tabk, wq, *w_extra)


def _flash_kernel(qt_ref, kv_ref, kvt_ref, kp_ref, o_ref, m_scr, l_scr, acc_scr, *, tq):
    qi = pl.program_id(1)
    key = lax.broadcasted_iota(jnp.int32, (tq, tq), 0)
    qry = lax.broadcasted_iota(jnp.int32, (tq, tq), 1)
    m_scr[...] = jnp.full_like(m_scr, NEG_INF)
    l_scr[...] = jnp.zeros_like(l_scr)
    acc_scr[...] = jnp.zeros_like(acc_scr)

    def key_tile(k0, masked):
        kp = kp_ref[0, pl.ds(k0, tq), :]
        for h in range(H_B):
            keys = jnp.concatenate([kv_ref[0, pl.ds(k0, tq), h * 128:(h + 1) * 128], kp], axis=1)
            s = jnp.dot(keys, qt_ref[0, h * QW:(h + 1) * QW, :], preferred_element_type=F32)
            if masked:
                s = jnp.where(key <= qry, s, NEG_INF)
            m = m_scr[h]
            m_new = jnp.maximum(m, jnp.max(s, axis=0, keepdims=True))
            alpha = jnp.exp(m - m_new)
            p = jnp.exp(s - m_new)
            l_scr[h] = alpha * l_scr[h] + jnp.sum(p, axis=0, keepdims=True)
            acc_scr[h] = alpha * acc_scr[h] + jnp.dot(kvt_ref[0, h * 128:(h + 1) * 128, pl.ds(k0, tq)],
                                                      p.astype(BF16), preferred_element_type=F32)
            m_scr[h] = m_new

    def body(kt, carry):
        key_tile(pl.multiple_of(kt * tq, tq), False)
        return carry

    lax.fori_loop(0, qi, body, 0)
    key_tile(pl.multiple_of(qi * tq, tq), True)
    o_t = jnp.concatenate([(acc_scr[h] / l_scr[h])[D_NOPE:D_NOPE + DV_B] for h in range(H_B)], axis=0)
    o_ref[0] = o_t.T.astype(o_ref.dtype)


def _flash(qt3, kv3, kvt3, kp3, tq):
    b, t, _ = kv3.shape
    return pl.pallas_call(
        functools.partial(_flash_kernel, tq=tq),
        grid=(b, t // tq),
        in_specs=[pl.BlockSpec((1, H_B * QW, tq), lambda i, j: (i, 0, j)),
                  pl.BlockSpec((1, t, H_B * 128), lambda i, j: (i, 0, 0)),
                  pl.BlockSpec((1, H_B * 128, t), lambda i, j: (i, 0, 0)),
                  pl.BlockSpec((1, t, 128), lambda i, j: (i, 0, 0))],
        out_specs=pl.BlockSpec((1, tq, H_B * DV_B), lambda i, j: (i, j, 0)),
        out_shape=jax.ShapeDtypeStruct((b, t, H_B * DV_B), BF16),
        scratch_shapes=[pltpu.VMEM((H_B, 1, tq), F32), pltpu.VMEM((H_B, 1, tq), F32),
                        pltpu.VMEM((H_B, 128, tq), F32)],
        compiler_params=_cparams(("parallel", "arbitrary")),
        name="mla_flash",
    )(qt3, kv3, kvt3, kp3)


def _decode_kernel(pt_ref, qlat_ref, qpe_ref, latn_ref, krn_ref, wuv_ref, lat_hbm, kct_hbm, o_ref,
                   lat_buf, kct_buf, lat_sem, kct_sem, *, gp, n_groups, n_slots, t_new):
    b = pl.program_id(0)
    nb = pl.num_programs(0)
    nq = H_B * t_new

    def page_copies(bi, g, slot):
        copies = []
        for j in range(gp):
            page = pt_ref[bi, g * gp + j]
            copies.append(pltpu.make_async_copy(lat_hbm.at[page], lat_buf.at[slot, pl.ds(j * PAGE, PAGE), :],
                                                lat_sem.at[slot]))
            copies.append(pltpu.make_async_copy(kct_hbm.at[page], kct_buf.at[slot, j], kct_sem.at[slot]))
        return copies

    def start_group(bi, g, slot):
        for cp in page_copies(bi, g, slot):
            cp.start()

    def wait_group(bi, g, slot):
        for cp in page_copies(bi, g, slot):
            cp.wait()

    ahead = n_slots // 2

    @pl.when(b == 0)
    def _():
        for g in range(ahead):
            start_group(0, g, g)

    ql = jnp.concatenate([qlat_ref[0, :, h * KV_LORA:(h + 1) * KV_LORA] for h in range(H_B)], axis=0).astype(BF16)
    qp = jnp.concatenate([qpe_ref[0, :, h * D_ROPE:(h + 1) * D_ROPE] for h in range(H_B)], axis=0).astype(BF16)

    def group_softmax(s):
        m_g = jnp.max(s, axis=-1, keepdims=True)
        p = jnp.exp(s - m_g)
        return p.astype(BF16), m_g, jnp.sum(p, axis=-1, keepdims=True)

    def merge(state, group):
        m_old, l_old, acc_old = state
        p, m_g, l_g, values = group
        pv = jnp.dot(p, values, preferred_element_type=F32)
        m_new = jnp.maximum(m_old, m_g)
        a_old = jnp.exp(m_old - m_new)
        a_g = jnp.exp(m_g - m_new)
        return m_new, a_old * l_old + a_g * l_g, a_old * acc_old + a_g * pv

    state = (jnp.full((nq, 1), NEG_INF, F32), jnp.zeros((nq, 1), F32), jnp.zeros((nq, KV_LORA), F32))
    pending = None
    for g in range(n_groups):
        slot = g % n_slots
        wait_group(b, g, slot)
        nxt = g + ahead
        if nxt < n_groups:
            start_group(b, nxt, nxt % n_slots)
        else:
            @pl.when(b + 1 < nb)
            def _():
                start_group(b + 1, nxt - n_groups, nxt % n_slots)
        lat = lat_buf[slot].astype(BF16)
        kct = jnp.concatenate([kct_buf[slot, j] for j in range(gp)], axis=1).astype(BF16)
        s = (lax.dot_general(ql, lat, (((1,), (1,)), ((), ())), preferred_element_type=F32)
             + jnp.dot(qp, kct, preferred_element_type=F32))
        if pending is not None:
            state = merge(state, pending)
        pending = group_softmax(s) + (lat,)
    state = merge(state, pending)

    latn = latn_ref[0].astype(BF16)
    krn = krn_ref[0].astype(BF16)
    s = (lax.dot_general(ql, latn, (((1,), (1,)), ((), ())), preferred_element_type=F32)
         + lax.dot_general(qp, krn, (((1,), (1,)), ((), ())), preferred_element_type=F32))
    tok = lax.broadcasted_iota(jnp.int32, (nq, t_new), 0) % t_new
    key = lax.broadcasted_iota(jnp.int32, (nq, t_new), 1)
    _, l, acc = merge(state, group_softmax(jnp.where(key <= tok, s, NEG_INF)) + (latn,))
    o_lat = acc / l
    outs = [_dot(o_lat[h * t_new:(h + 1) * t_new], wuv_ref[h]) for h in range(H_B)]
    o_ref[0] = jnp.concatenate(outs, axis=1).astype(o_ref.dtype)


def _decode(page_table, qlat3, qpe3, latn3, krn3, wuv, cache_lat, cache_kpe_t, gp):
    b, t_new, _ = qlat3.shape
    n_pages = page_table.shape[1]
    n_groups = n_pages // gp
    n_slots = min(4, n_groups)
    assert n_slots >= 2 and n_groups % n_slots == 0
    per_batch = lambda w: pl.BlockSpec((1, t_new, w), lambda i, pt: (i, 0, 0))
    in_specs = [per_batch(H_B * KV_LORA), per_batch(H_B * D_ROPE), per_batch(KV_LORA), per_batch(D_ROPE),
                pl.BlockSpec(wuv.shape, lambda i, pt: (0, 0, 0)),
                pl.BlockSpec(memory_space=pl.ANY), pl.BlockSpec(memory_space=pl.ANY)]
    return pl.pallas_call(
        functools.partial(_decode_kernel, gp=gp, n_groups=n_groups, n_slots=n_slots, t_new=t_new),
        grid_spec=pltpu.PrefetchScalarGridSpec(
            num_scalar_prefetch=1,
            grid=(b,),
            in_specs=in_specs,
            out_specs=pl.BlockSpec((1, t_new, H_B * DV_B), lambda i, pt: (i, 0, 0)),
            scratch_shapes=[pltpu.VMEM((n_slots, gp * PAGE, KV_LORA), F32),
                            pltpu.VMEM((n_slots, gp, D_ROPE, PAGE), F32),
                            pltpu.SemaphoreType.DMA((n_slots,)), pltpu.SemaphoreType.DMA((n_slots,))]),
        out_shape=jax.ShapeDtypeStruct((b, t_new, H_B * DV_B), BF16),
        compiler_params=_cparams(("arbitrary",)),
        name="mla_decode",
    )(page_table, qlat3, qpe3, latn3, krn3, wuv, cache_lat, cache_kpe_t)


def _rot_cols(w):
    half = D_ROPE // 2
    return jnp.concatenate([-w[..., half:], w[..., :half]], axis=-1)


def _layout_w_in_ab(w):
    d = w.shape[0]
    o = QKV_A + H_A * DV_A
    qkv_z, b_w, a_w = w[:, :o], w[:, o:o + H_A], w[:, o + H_A:o + 2 * H_A]
    o += 2 * H_A
    cq, ckv, kpe = w[:, o:o + Q_LORA], w[:, o + Q_LORA:o + Q_LORA + KV_LORA], w[:, o + Q_LORA + KV_LORA:]
    kr = _rot_cols(kpe)
    pad = jnp.zeros((d, 128 - 2 * H_A), w.dtype)
    return jnp.concatenate([qkv_z, cq, ckv, kpe, kr, kpe, kr, b_w, a_w, pad], axis=1).astype(BF16)


def _layout_w_uq(w_uq):
    nope, pe = w_uq[..., :D_NOPE], w_uq[..., D_NOPE:]
    rot = _rot_cols(pe)
    z = jnp.zeros_like(nope)
    return jnp.concatenate([nope, z, pe, pe, rot, rot], axis=-1).reshape(w_uq.shape[0], H_B * QW).astype(BF16)


def _rope_tables(pos):
    half = D_ROPE // 2
    inv = ROPE_BASE ** (-jnp.arange(half, dtype=F32) / half)
    ang = pos.astype(F32)[:, None] * inv
    c = jnp.concatenate([jnp.cos(ang), jnp.cos(ang)], axis=1)
    s = jnp.concatenate([jnp.sin(ang), jnp.sin(ang)], axis=1)
    one = jnp.ones((pos.shape[0], 128), F32)
    tabq = jnp.concatenate([one, c, c, s, s], axis=1) * MLA_SCALE
    tabk = jnp.concatenate([c, s, c, s], axis=1)
    return tabq, tabk


def _row_tile(n):
    for tm in (512, 256, 128, 64, 32, 16, 8):
        if n % tm == 0:
            return tm
    raise ValueError(n)


def _trunk(x, pos, conv0, gdn0, hgrn0, paged, wts):
    b, t, d = x.shape
    n = b * t
    tm = min(256, _row_tile(n))
    tb = min(256, t)
    x2 = x.reshape(n, d)

    p2 = _proj(x2, wts["mix_norm"][0], wts["w_in_ab"], tm)
    p3 = p2.reshape(b, t, IN_AB_PAD)
    conv_new = p3[:, t - (CONV_W - 1):, :QKV_A]
    o_a, gdn_new = _gdn(p3, wts["conv_w"], wts["a_log"], wts["dt_bias"], wts["gdn_norm"], conv0, gdn0, tb,
                        1 if gdn0 is None else math.gcd(b, 4))
    tabq, tabk = _rope_tables(pos)
    if paged is None:
        tmp = math.gcd(tm, t)
        lat, kr, qt, kv, kvt, kp4 = _mla_prep(p2, wts["q_norm"], wts["kv_norm"], tabq.T, tabk, wts["w_uq"].T,
                                              [wts["w_kv"], wts["w_kv"].T], False, tmp, t)
        o_b = _flash(qt, kv.reshape(b, t, -1), kvt, kp4.reshape(b, t, -1), min(512, t))
    else:
        cache_lat, cache_kpe, page_table = paged
        reps = tm // t
        lat, kr, qlat, qpe = _mla_prep(p2, wts["q_norm"], wts["kv_norm"], jnp.tile(tabq, (reps, 1)),
                                       jnp.tile(tabk, (reps, 1)), wts["w_uq"], [wts["w_ukt"]], True, tm, t)
        n_pages = page_table.shape[1]
        gp = math.gcd(n_pages // 4, 16)
        o_b = _decode(page_table, qlat.reshape(b, t, -1), qpe.reshape(b, t, -1), lat.reshape(b, t, -1),
                      kr.reshape(b, t, -1), wts["w_uv"], cache_lat, jnp.swapaxes(cache_kpe, 1, 2), gp)
    x2 = _outproj(x2, [o_a.reshape(n, -1), o_b.reshape(n, -1)], [wts["w_out_a"], wts["w_out_b"]], tm)
    tmm = min(512, _row_tile(n))
    x2 = _mlp(x2, wts["mlp_norm"][0], wts["w_up"][0], wts["w_down"][0], wts["final_norm"], False, tmm, 1024)

    pc = _proj(x2, wts["mix_norm"][1], wts["w_in_c"], tm)
    o_c, hgrn_new = _hgrn(pc.reshape(b, t, -1), wts["lb_logits"], 1, wts["g_norm_c"], hgrn0, tb,
                          H_C if hgrn0 is not None else 2)
    x2 = _outproj(x2, [o_c.reshape(n, -1)], [wts["w_out_c"]], tm)
    y2 = _mlp(x2, wts["mlp_norm"][1], wts["w_up"][1], wts["w_down"][1], wts["final_norm"], True, tmm, 1024)
    return (y2.reshape(b, t, d), gdn_new[None], conv_new[None], lat.reshape(b, t, -1)[None],
            kr.reshape(b, t, -1)[None], hgrn_new[None])


def kernel(x_prompt, x_sample, state_gdn, state_gdn_conv, cache_mla_latent, cache_mla_krope, state_hgrn,
           page_table, mix_norm, mlp_norm, final_norm, w_up, w_down, w_in_ab, conv_w_ab, a_log_ab, dt_bias_ab,
           gdn_norm_ab, q_norm_ab, w_uq_ab, kv_norm_ab, w_uk_ab, w_uv_ab, w_out_ab, w_in_c, lb_logits_c,
           g_norm_c, w_out_c):
    assert mix_norm.shape[0] == 2 and w_in_ab.shape[0] == 1 and w_in_c.shape[0] == 1
    assert a_log_ab.shape == (1, H_A) and conv_w_ab.shape == (1, CONV_W, QKV_A)
    assert w_uq_ab.shape[1:] == (Q_LORA, H_B, D_NOPE + D_ROPE) and w_uk_ab.shape[1:] == (KV_LORA, H_B, D_NOPE)
    assert cache_mla_latent.shape[2] == PAGE and x_prompt.shape[1] >= CONV_W - 1 and x_sample.shape[1] >= CONV_W - 1
    w_uk, w_uv = w_uk_ab[0], w_uv_ab[0]
    wts = {
        "mix_norm": mix_norm, "mlp_norm": mlp_norm, "final_norm": final_norm,
        "w_up": w_up.astype(BF16), "w_down": w_down.astype(BF16),
        "w_in_ab": _layout_w_in_ab(w_in_ab[0]), "conv_w": conv_w_ab[0], "a_log": a_log_ab[0],
        "dt_bias": dt_bias_ab[0], "gdn_norm": gdn_norm_ab[0], "q_norm": q_norm_ab[0], "kv_norm": kv_norm_ab[0],
        "w_uq": _layout_w_uq(w_uq_ab[0]),
        "w_kv": jnp.concatenate([w_uk, w_uv], axis=-1).reshape(KV_LORA, H_B * 128).astype(BF16),
        "w_ukt": jnp.transpose(w_uk, (1, 2, 0)).astype(BF16),
        "w_uv": jnp.transpose(w_uv, (1, 0, 2)).astype(BF16),
        "w_out_a": w_out_ab[0, :H_A * DV_A].astype(BF16), "w_out_b": w_out_ab[0, H_A * DV_A:].astype(BF16),
        "w_in_c": w_in_c[0].astype(BF16), "lb_logits": lb_logits_c, "g_norm_c": g_norm_c[0],
        "w_out_c": w_out_c[0].astype(BF16),
    }
    s_len = x_prompt.shape[1]
    outs_p = _trunk(x_prompt, jnp.arange(s_len), None, None, None, None, wts)
    past_len = page_table.shape[1] * cache_mla_latent.shape[2]
    outs_s = _trunk(x_sample, past_len + jnp.arange(x_sample.shape[1]),
                    state_gdn_conv.reshape(state_gdn_conv.shape[1:]), state_gdn.reshape(state_gdn.shape[1:]),
                    state_hgrn.reshape(state_hgrn.shape[1:]),
                    (cache_mla_latent.reshape(cache_mla_latent.shape[1:]),
                     cache_mla_krope.reshape(cache_mla_krope.shape[1:]), page_table), wts)
    return (outs_p[0], outs_s[0]) + outs_p[1:] + outs_s[1:]
```

```python
import functools
import math

import jax
import jax.numpy as jnp
from jax import lax
from jax.experimental import pallas as pl
from jax.experimental.pallas import tpu as pltpu

F32 = jnp.float32
BF16 = jnp.bfloat16
EPS = 1e-6
NEG_INF = float("-inf")

H_A, DK_A, DV_A, CONV_W = 4, 128, 128, 4
QKV_A = H_A * (2 * DK_A + DV_A)
H_B, D_NOPE, D_ROPE, DV_B = 8, 64, 32, 64
Q_LORA, KV_LORA = 256, 256
ROPE_BASE = 10000.0
MLA_SCALE = (D_NOPE + D_ROPE) ** -0.5
H_C, DK_C, DV_C = 8, 128, 128
CHUNK = 64
SUB = 16
PAGE = 128
QW = 256
IN_AB_PAD = 2816
V7X_VMEM_LIMIT = 48 * 1024 * 1024


def _cparams(sem):
    return pltpu.CompilerParams(dimension_semantics=sem, vmem_limit_bytes=V7X_VMEM_LIMIT)


def _const_spec(shape):
    nd = len(shape)
    return pl.BlockSpec(shape, lambda *_: (0,) * nd)


def _dot(a, b):
    return jnp.dot(a.astype(BF16), b.astype(BF16), preferred_element_type=F32)


def _dot_nt(a, b):
    return lax.dot_general(a.astype(BF16), b.astype(BF16), (((1,), (1,)), ((), ())),
                           preferred_element_type=F32)


def _dot_tn(a, b):
    return lax.dot_general(a.astype(BF16), b.astype(BF16), (((0,), (0,)), ((), ())),
                           preferred_element_type=F32)


def _rms(x, g):
    return x * lax.rsqrt(jnp.mean(x * x, axis=-1, keepdims=True) + EPS) * g


def _silu(x):
    return x * jax.nn.sigmoid(x)


def _cumsum_rows(x):
    n = x.shape[0]
    row = lax.broadcasted_iota(jnp.int32, x.shape, 0)
    s = 1
    while s < n:
        x = x + jnp.where(row >= s, pltpu.roll(x, s, axis=0), 0.0)
        s *= 2
    return x


def _proj_kernel(x_ref, g_ref, w_ref, o_ref):
    h = _rms(x_ref[...], g_ref[...]).astype(BF16)
    o_ref[...] = jnp.dot(h, w_ref[...], preferred_element_type=F32)


def _proj(x2, g, w, tm):
    n, d = x2.shape
    nout = w.shape[1]
    return pl.pallas_call(
        _proj_kernel,
        grid=(n // tm,),
        in_specs=[pl.BlockSpec((tm, d), lambda i: (i, 0)), _const_spec((1, d)), _const_spec((d, nout))],
        out_specs=pl.BlockSpec((tm, nout), lambda i: (i, 0)),
        out_shape=jax.ShapeDtypeStruct((n, nout), F32),
        compiler_params=_cparams(("parallel",)),
        name="proj",
    )(x2, g.reshape(1, d), w)


def _outproj_kernel(*refs, n_in):
    r_ref = refs[0]
    a_refs = refs[1:1 + n_in]
    w_refs = refs[1 + n_in:1 + 2 * n_in]
    o_ref = refs[1 + 2 * n_in]
    acc = r_ref[...]
    for a_ref, w_ref in zip(a_refs, w_refs):
        acc = acc + jnp.dot(a_ref[...], w_ref[...], preferred_element_type=F32)
    o_ref[...] = acc


def _outproj(res, acts, ws, tm):
    n, d = res.shape
    n_in = len(acts)
    in_specs = [pl.BlockSpec((tm, d), lambda i: (i, 0))]
    in_specs += [pl.BlockSpec((tm, a.shape[1]), lambda i: (i, 0)) for a in acts]
    in_specs += [_const_spec(w.shape) for w in ws]
    return pl.pallas_call(
        functools.partial(_outproj_kernel, n_in=n_in),
        grid=(n // tm,),
        in_specs=in_specs,
        out_specs=pl.BlockSpec((tm, d), lambda i: (i, 0)),
        out_shape=jax.ShapeDtypeStruct((n, d), F32),
        compiler_params=_cparams(("parallel",)),
        name="outproj",
    )(res, *acts, *ws)


def _mlp_kernel(x_ref, g_ref, gf_ref, wu_ref, wd_ref, o_ref, h_ref, *, final_norm):
    j = pl.program_id(1)

    @pl.when(j == 0)
    def _():
        x = x_ref[...]
        h_ref[...] = _rms(x, g_ref[...]).astype(BF16)
        o_ref[...] = x

    a = jnp.dot(h_ref[...], wu_ref[...], preferred_element_type=F32)
    a = jnp.square(jnp.maximum(a, 0.0))
    o_ref[...] += jnp.dot(a.astype(BF16), wd_ref[...], preferred_element_type=F32)

    if final_norm:
        @pl.when(j == pl.num_programs(1) - 1)
        def _():
            o_ref[...] = _rms(o_ref[...], gf_ref[...])


def _mlp(x2, g, w_up, w_down, gf, final_norm, tm, tf):
    n, d = x2.shape
    ff = w_up.shape[1]
    return pl.pallas_call(
        functools.partial(_mlp_kernel, final_norm=final_norm),
        grid=(n // tm, ff // tf),
        in_specs=[pl.BlockSpec((tm, d), lambda i, j: (i, 0)), _const_spec((1, d)), _const_spec((1, d)),
                  pl.BlockSpec((d, tf), lambda i, j: (0, j)), pl.BlockSpec((tf, d), lambda i, j: (j, 0))],
        out_specs=pl.BlockSpec((tm, d), lambda i, j: (i, 0)),
        out_shape=jax.ShapeDtypeStruct((n, d), F32),
        scratch_shapes=[pltpu.VMEM((tm, d), BF16)],
        compiler_params=_cparams(("parallel", "arbitrary")),
        name="mlp",
    )(x2, g.reshape(1, d), gf.reshape(1, d), w_up, w_down)


def _split3_dot(a, b):
    a_hi = a.astype(BF16)
    a_lo = (a - a_hi.astype(F32)).astype(BF16)
    b_hi = b.astype(BF16)
    b_lo = (b - b_hi.astype(F32)).astype(BF16)
    return jnp.dot(jnp.concatenate([a_hi, a_lo, a_hi], axis=1), jnp.concatenate([b_hi, b_hi, b_lo], axis=0),
                   preferred_element_type=F32)


def _pick_dot(a, pick):
    a_hi = a.astype(BF16)
    a_lo = (a - a_hi.astype(F32)).astype(BF16)
    pick = pick.astype(BF16)
    return jnp.dot(jnp.concatenate([a_hi, a_lo], axis=1), jnp.concatenate([pick, pick], axis=0),
                   preferred_element_type=F32)


def _solve_unit_lower(lowers, rhss, c):
    nh = len(lowers)
    s = 8
    nb = c // s
    w = nh * c
    row = lax.broadcasted_iota(jnp.int32, (c, w), 0)
    col = lax.broadcasted_iota(jnp.int32, (c, w), 1) % c
    if nb == 1:
        diag_cols = lowers
    else:
        r1 = lax.broadcasted_iota(jnp.int32, (c, c), 0)
        c1 = lax.broadcasted_iota(jnp.int32, (c, c), 1)
        pick = (lax.broadcasted_iota(jnp.int32, (c, s), 0) % s
                == lax.broadcasted_iota(jnp.int32, (c, s), 1)).astype(F32)
        diag_cols = [_pick_dot(jnp.where(r1 // s == c1 // s, lo, 0.0), pick) for lo in lowers]
    x = (row == col).astype(F32)
    for j in range(s - 1):
        mult = jnp.concatenate([jnp.broadcast_to(dc[:, j:j + 1], (c, c)) for dc in diag_cols], axis=1)
        pivot_rows = jnp.broadcast_to(x.reshape(nb, s, w)[:, j:j + 1, :], (nb, s, w)).reshape(c, w)
        x = x - mult * pivot_rows

    def block_diag(m):
        r2 = lax.broadcasted_iota(jnp.int32, (w, w), 0)
        c2 = lax.broadcasted_iota(jnp.int32, (w, w), 1)
        return jnp.where(r2 // c == c2 // c, jnp.concatenate([m] * nh, axis=0), 0.0)

    lower_cat = jnp.concatenate(lowers, axis=1)
    size = s
    while size < c:
        off = jnp.where((row // (2 * size) == col // (2 * size)) & (row // size != col // size) & (row > col),
                        lower_cat, 0.0)
        x = x - _split3_dot(_split3_dot(x, block_diag(off)), block_diag(x))
        size *= 2
    out = _split3_dot(block_diag(x), jnp.concatenate(rhss, axis=0))
    return [out[h * c:(h + 1) * c] for h in range(nh)]


def _gdn_head_pre(q, k, v, b_col, a_col, alog, dtb, c):
    q = q * lax.rsqrt(jnp.sum(q * q, axis=-1, keepdims=True) + EPS) * (DK_A ** -0.5)
    k = k * lax.rsqrt(jnp.sum(k * k, axis=-1, keepdims=True) + EPS)
    beta = jax.nn.sigmoid(b_col)
    g = -jnp.exp(alog) * jax.nn.softplus(a_col + dtb)
    gc_b = _cumsum_rows(jnp.broadcast_to(g, (c, 128)))
    gc = gc_b[:, 0:1]
    diff = gc_b[:, :c] - gc_b.T[:c, :]
    row = lax.broadcasted_iota(jnp.int32, (c, c), 0)
    col = lax.broadcasted_iota(jnp.int32, (c, c), 1)
    incl = row >= col
    decay = jnp.where(incl, jnp.exp(jnp.where(incl, diff, 0.0)), 0.0)
    kb = k * beta
    lower = jnp.where(row > col, _dot_nt(kb, k) * decay, 0.0)
    rhs = jnp.concatenate([v * beta, kb * jnp.exp(gc)], axis=1)
    intra = _dot_nt(q, k) * decay
    return q, k, gc, lower, rhs, intra


def _gdn_head_post(q, k, gc, uw, intra, z, gnorm, s_prev, c):
    u, w = uw[:, :DV_A], uw[:, DV_A:]
    v_new = u - _dot(w, s_prev)
    o = _dot(q * jnp.exp(gc), s_prev) + _dot(intra, v_new)
    g_last = gc[c - 1:c, :]
    s_new = s_prev * jnp.exp(g_last) + _dot_tn(k * jnp.exp(g_last - gc), v_new)
    o = _rms(o, gnorm) * _silu(z)
    return o, s_new


def _gdn_kernel(*refs, c, n_chunks, has_state, bg):
    (q_ref, k_ref, v_ref, z_ref, gt_ref, cwq_ref, cwk_ref, cwv_ref, alog_ref, dtb_ref, gn_ref) = refs[:11]
    if has_state:
        cq_ref, ck_ref, cv_ref, s0_ref = refs[11:15]
        o_ref, sout_ref, s_scr, tail_scr = refs[15:]
    else:
        o_ref, sout_ref, s_scr, tail_scr = refs[11:]
    n = pl.program_id(1)

    @pl.when(n == 0)
    def _():
        if has_state:
            s_scr[...] = s0_ref[...]
            tail_scr[0] = cq_ref[...]
            tail_scr[1] = ck_ref[...]
            tail_scr[2] = cv_ref[...]
        else:
            s_scr[...] = jnp.zeros_like(s_scr)
            tail_scr[...] = jnp.zeros_like(tail_scr)

    def chunk(ci):
        rows = pl.ds(ci * c, c)
        pre = []
        zs = []
        for bi in range(bg):
            conv = []
            for idx, (x_ref, cw_ref) in enumerate(((q_ref, cwq_ref), (k_ref, cwk_ref), (v_ref, cwv_ref))):
                x = x_ref[bi, rows, :]
                ext = jnp.concatenate([tail_scr[idx, bi], x], axis=0)
                cw = cw_ref[...]
                y = x * cw[CONV_W - 1:CONV_W, :]
                for sh in range(1, CONV_W):
                    y = y + pltpu.roll(ext, sh, axis=0)[8:] * cw[CONV_W - 1 - sh:CONV_W - sh, :]
                tail_scr[idx, bi] = x[c - 8:, :]
                conv.append(_silu(y))
            zs.append(z_ref[bi, rows, :])
            gt = gt_ref[bi, rows, :]
            for h in range(H_A):
                hs = slice(h * 128, (h + 1) * 128)
                pre.append(_gdn_head_pre(conv[0][:, hs], conv[1][:, hs], conv[2][:, hs],
                                         gt[:, h:h + 1], gt[:, H_A + h:H_A + h + 1],
                                         alog_ref[:, h:h + 1], dtb_ref[:, h:h + 1], c))
        per_solve = max(H_A, min(bg * H_A, 128 // c))
        uws = []
        for g0 in range(0, bg * H_A, per_solve):
            grp = pre[g0:g0 + per_solve]
            uws += _solve_unit_lower([p[3] for p in grp], [p[4] for p in grp], c)
        for bi in range(bg):
            outs = []
            for h in range(H_A):
                q, k, gc, _, _, intra = pre[bi * H_A + h]
                o, s_new = _gdn_head_post(q, k, gc, uws[bi * H_A + h], intra, zs[bi][:, h * 128:(h + 1) * 128],
                                          gn_ref[...], s_scr[bi, h], c)
                s_scr[bi, h] = s_new
                outs.append(o)
            o_ref[bi, rows, :] = jnp.concatenate(outs, axis=1).astype(o_ref.dtype)

    for ci in range(n_chunks):
        chunk(ci)

    @pl.when(n == pl.num_programs(1) - 1)
    def _():
        sout_ref[...] = s_scr[...]


def _gdn(p3, conv_w, a_log, dt_bias, gdn_norm, conv0, s0, tb, bg):
    b, t, _ = p3.shape
    c = math.gcd(t, CHUNK)
    has_state = s0 is not None
    hw = H_A * 128
    cw = conv_w
    blk = lambda j: pl.BlockSpec((bg, tb, hw), lambda i, n: (i, n, j))
    in_specs = [blk(0), blk(1), blk(2), blk(3),
                pl.BlockSpec((bg, tb, 128), lambda i, n: (i, n, (IN_AB_PAD - 128) // 128)),
                pl.BlockSpec((CONV_W, hw), lambda i, n: (0, 0)), pl.BlockSpec((CONV_W, hw), lambda i, n: (0, 1)),
                pl.BlockSpec((CONV_W, hw), lambda i, n: (0, 2)),
                _const_spec((1, H_A)), _const_spec((1, H_A)), _const_spec((1, DV_A))]
    args = [p3, p3, p3, p3, p3, cw, cw, cw, a_log.reshape(1, H_A), dt_bias.reshape(1, H_A),
            gdn_norm.reshape(1, DV_A)]
    if has_state:
        conv8 = jnp.pad(conv0, ((0, 0), (8 - (CONV_W - 1), 0), (0, 0)))
        in_specs += [pl.BlockSpec((bg, 8, hw), lambda i, n, j=j: (i, 0, j)) for j in range(3)]
        in_specs += [pl.BlockSpec((bg, H_A, DK_A, DV_A), lambda i, n: (i, 0, 0, 0))]
        args += [conv8, conv8, conv8, s0]
    return pl.pallas_call(
        functools.partial(_gdn_kernel, c=c, n_chunks=tb // c, has_state=has_state, bg=bg),
        grid=(b // bg, t // tb),
        in_specs=in_specs,
        out_specs=[pl.BlockSpec((bg, tb, hw), lambda i, n: (i, n, 0)),
                   pl.BlockSpec((bg, H_A, DK_A, DV_A), lambda i, n: (i, 0, 0, 0))],
        out_shape=[jax.ShapeDtypeStruct((b, t, hw), BF16),
                   jax.ShapeDtypeStruct((b, H_A, DK_A, DV_A), F32)],
        scratch_shapes=[pltpu.VMEM((bg, H_A, DK_A, DV_A), F32), pltpu.VMEM((3, bg, 8, hw), F32)],
        compiler_params=_cparams(("parallel", "arbitrary")),
        name="gdn",
    )(*args)


def _hgrn_chunk(qr, fr, v, lb, s_prev, c):
    sub = min(SUB, c)
    q = _silu(qr) * (DK_C ** -0.5)
    f = lb + (1.0 - lb) * jax.nn.sigmoid(fr)
    k = 1.0 - f
    gc = _cumsum_rows(jnp.log(f))
    o_inter = _dot(q * jnp.exp(gc), s_prev)
    row = lax.broadcasted_iota(jnp.int32, (sub, sub), 0)
    col = lax.broadcasted_iota(jnp.int32, (sub, sub), 1)
    outs = []
    for blk in range(c // sub):
        r0 = blk * sub
        g_i = gc[r0:r0 + sub]
        q_i = q[r0:r0 + sub]
        a_parts = [jnp.zeros((8, sub), F32) for _ in range(sub // 8)]
        for j in range(sub):
            jj = r0 + j
            lo = 8 * (j // 8)
            e = jnp.exp(g_i[lo:] - gc[jj:jj + 1, :])
            col_j = jnp.sum(q_i[lo:] * k[jj:jj + 1, :] * e, axis=-1, keepdims=True)
            for p in range(j // 8, sub // 8):
                a_parts[p] = jnp.where(col[:8] == j, col_j[8 * p - lo:8 * p - lo + 8], a_parts[p])
        a_diag = jnp.concatenate(a_parts, axis=0) if len(a_parts) > 1 else a_parts[0]
        a_diag = jnp.where(row >= col, a_diag, 0.0)
        o_i = _dot(a_diag, v[r0:r0 + sub])
        if blk > 0:
            g_ref = gc[r0 - 1:r0, :]
            a_off = _dot_nt(q_i * jnp.exp(g_i - g_ref), k[:r0] * jnp.exp(g_ref - gc[:r0]))
            o_i = o_i + _dot(a_off, v[:r0])
        outs.append(o_i)
    o = o_inter + (jnp.concatenate(outs, axis=0) if len(outs) > 1 else outs[0])
    g_last = gc[c - 1:c, :]
    gl_col = gc.T[:, c - 1:c]
    s_new = s_prev * jnp.exp(gl_col) + _dot_tn(k * jnp.exp(g_last - gc), v)
    return o, s_new


def _hgrn_kernel(*refs, c, n_chunks, layer, has_state, hg, bg):
    q_ref, f_ref, i_ref, gate_ref, lbl_ref, gn_ref = refs[:6]
    if has_state:
        s0_ref = refs[6]
        o_ref, sout_ref, s_scr = refs[7:]
    else:
        o_ref, sout_ref, s_scr = refs[6:]
    n = pl.program_id(2)

    @pl.when(n == 0)
    def _():
        if has_state:
            s_scr[...] = s0_ref[...]
        else:
            s_scr[...] = jnp.zeros_like(s_scr)

    lbl = lbl_ref[...]
    ex = jnp.exp(lbl - jnp.max(lbl, axis=0, keepdims=True))
    p = ex / jnp.sum(ex, axis=0, keepdims=True)
    lb = jnp.sum(p[:layer + 1], axis=0, keepdims=True) - p[0:1]

    def chunk(ci, carry):
        r0 = ci * c if isinstance(ci, int) else pl.multiple_of(ci * c, c)
        rows = pl.ds(r0, c)
        for bi in range(bg):
            for h in range(hg):
                hs = slice(h * 128, (h + 1) * 128)
                o, s_new = _hgrn_chunk(q_ref[bi, rows, hs], f_ref[bi, rows, hs], i_ref[bi, rows, hs], lb[:, hs],
                                       s_scr[bi, h], c)
                s_scr[bi, h] = s_new
                o_ref[bi, rows, hs] = (_rms(o, gn_ref[...]) * _silu(gate_ref[bi, rows, hs])).astype(o_ref.dtype)
        return carry

    if n_chunks == 1:
        chunk(0, 0)
    else:
        lax.fori_loop(0, n_chunks, chunk, 0)

    @pl.when(n == pl.num_programs(2) - 1)
    def _():
        sout_ref[...] = s_scr[...]


def _hgrn(pc3, lb_logits, layer, g_norm, s0, tb, hg, bg):
    b, t, _ = pc3.shape
    c = math.gcd(t, CHUNK)
    has_state = s0 is not None
    depth = lb_logits.shape[0]
    ng = H_C // hg
    blk = lambda j: pl.BlockSpec((bg, tb, hg * 128), lambda i, h, n: (i, n, j * ng + h))
    in_specs = [blk(0), blk(1), blk(2), blk(3),
                pl.BlockSpec((depth, hg * 128), lambda i, h, n: (0, h)), _const_spec((1, DV_C))]
    args = [pc3, pc3, pc3, pc3, lb_logits, g_norm.reshape(1, DV_C)]
    if has_state:
        in_specs.append(pl.BlockSpec((bg, hg, DK_C, DV_C), lambda i, h, n: (i, h, 0, 0)))
        args.append(s0)
    return pl.pallas_call(
        functools.partial(_hgrn_kernel, c=c, n_chunks=tb // c, layer=layer, has_state=has_state, hg=hg, bg=bg),
        grid=(b // bg, ng, t // tb),
        in_specs=in_specs,
        out_specs=[pl.BlockSpec((bg, tb, hg * 128), lambda i, h, n: (i, n, h)),
                   pl.BlockSpec((bg, hg, DK_C, DV_C), lambda i, h, n: (i, h, 0, 0))],
        out_shape=[jax.ShapeDtypeStruct((b, t, H_C * DV_C), BF16),
                   jax.ShapeDtypeStruct((b, H_C, DK_C, DV_C), F32)],
        scratch_shapes=[pltpu.VMEM((bg, hg, DK_C, DV_C), F32)],
        compiler_params=_cparams(("parallel", "parallel", "arbitrary")),
        name="hgrn",
    )(*args)


def _mla_prep_kernel(*refs, absorb):
    cq_ref, ckv_ref, kp_ref, qn_ref, kvn_ref, tq_ref, tk_ref, wq_ref = refs[:8]
    if absorb:
        wukt_ref, lat_ref, kr_ref, qlat_ref, qpe_ref = refs[8:]
    else:
        wkv_ref, wkvt_ref, lat_ref, kr_ref, qt_ref, kv_ref, kvt_ref, kp4_ref = refs[8:]
    lat = _rms(ckv_ref[...], kvn_ref[...])
    lat_ref[...] = lat
    kp4 = kp_ref[...] * tk_ref[...]
    kr_ref[...] = kp4[:, 0:D_ROPE] + kp4[:, D_ROPE:2 * D_ROPE]
    cqn = _rms(cq_ref[...], qn_ref[...]).astype(BF16)
    tq = tq_ref[...]
    if absorb:
        for h in range(H_B):
            qh = jnp.dot(cqn, wq_ref[:, h * QW:(h + 1) * QW], preferred_element_type=F32) * tq
            qlat_ref[:, h * KV_LORA:(h + 1) * KV_LORA] = _dot(qh[:, 0:D_NOPE], wukt_ref[h])
            qpe_ref[:, h * D_ROPE:(h + 1) * D_ROPE] = (qh[:, 128:128 + D_ROPE]
                                                      + qh[:, 128 + 2 * D_ROPE:128 + 3 * D_ROPE])
    else:
        nt_dims = (((1,), (1,)), ((), ()))
        for h in range(H_B):
            qh = lax.dot_general(wq_ref[h * QW:(h + 1) * QW, :], cqn, nt_dims, preferred_element_type=F32) * tq
            qt_ref[0, h * QW:(h + 1) * QW, :] = qh.astype(BF16)
        lat_b = lat.astype(BF16)
        kv_ref[...] = jnp.dot(lat_b, wkv_ref[...], preferred_element_type=F32).astype(BF16)
        kvt_ref[0] = lax.dot_general(wkvt_ref[...], lat_b, nt_dims, preferred_element_type=F32).astype(BF16)
        kp4_ref[...] = kp4.astype(BF16)


def _mla_prep(p2, q_norm, kv_norm, tabq, tabk, wq, w_extra, absorb, tm, seq):
    n = p2.shape[0]
    nt = seq // tm if not absorb else 1
    tabq_spec = (pl.BlockSpec((tm, QW), lambda i: (0, 0)) if absorb
                 else pl.BlockSpec((QW, tm), lambda i: (0, i % nt)))
    in_specs = [pl.BlockSpec((tm, Q_LORA), lambda i: (i, 2048 // 256)),
                pl.BlockSpec((tm, KV_LORA), lambda i: (i, 2304 // 256)),
                pl.BlockSpec((tm, 128), lambda i: (i, 2560 // 128)),
                _const_spec((1, Q_LORA)), _const_spec((1, KV_LORA)),
                tabq_spec, pl.BlockSpec((tm, 128), lambda i: (i % nt, 0)),
                _const_spec(wq.shape)] + [_const_spec(w.shape) for w in w_extra]
    row = lambda w: pl.BlockSpec((tm, w), lambda i: (i, 0))
    out_specs = [row(KV_LORA), row(D_ROPE)]
    out_shape = [jax.ShapeDtypeStruct((n, KV_LORA), F32), jax.ShapeDtypeStruct((n, D_ROPE), F32)]
    if absorb:
        out_specs += [row(H_B * KV_LORA), row(H_B * D_ROPE)]
        out_shape += [jax.ShapeDtypeStruct((n, H_B * KV_LORA), F32), jax.ShapeDtypeStruct((n, H_B * D_ROPE), F32)]
    else:
        b = n // seq
        col = lambda w: pl.BlockSpec((1, w, tm), lambda i: (i // nt, 0, i % nt))
        out_specs += [col(H_B * QW), row(H_B * 128), col(H_B * 128), row(128)]
        out_shape += [jax.ShapeDtypeStruct((b, H_B * QW, seq), BF16), jax.ShapeDtypeStruct((n, H_B * 128), BF16),
                      jax.ShapeDtypeStruct((b, H_B * 128, seq), BF16), jax.ShapeDtypeStruct((n, 128), BF16)]
    return pl.pallas_call(
        functools.partial(_mla_prep_kernel, absorb=absorb),
        grid=(n // tm,),
        in_specs=in_specs, out_specs=out_specs, out_shape=out_shape,
        compiler_params=_cparams(("parallel",)),
        name="mla_prep",
    )(p2, p2, p2, q_norm.reshape(1, Q_LORA), kv_norm.reshape(1, KV_LORA), tabq, tabk, wq, *w_extra)


def _flash_kernel(qt_ref, kv_ref, kvt_ref, kp_ref, o_ref, m_scr, l_scr, acc_scr, *, tq):
    qi = pl.program_id(1)
    key = lax.broadcasted_iota(jnp.int32, (tq, tq), 0)
    qry = lax.broadcasted_iota(jnp.int32, (tq, tq), 1)
    m_scr[...] = jnp.full_like(m_scr, NEG_INF)
    l_scr[...] = jnp.zeros_like(l_scr)
    acc_scr[...] = jnp.zeros_like(acc_scr)

    def key_tile(k0, masked):
        kp = kp_ref[0, pl.ds(k0, tq), :]
        for h in range(H_B):
            keys = jnp.concatenate([kv_ref[0, pl.ds(k0, tq), h * 128:(h + 1) * 128], kp], axis=1)
            s = jnp.dot(keys, qt_ref[0, h * QW:(h + 1) * QW, :], preferred_element_type=F32)
            if masked:
                s = jnp.where(key <= qry, s, NEG_INF)
            m = m_scr[h]
            m_new = jnp.maximum(m, jnp.max(s, axis=0, keepdims=True))
            alpha = jnp.exp(m - m_new)
            p = jnp.exp(s - m_new)
            l_scr[h] = alpha * l_scr[h] + jnp.sum(p, axis=0, keepdims=True)
            acc_scr[h] = alpha * acc_scr[h] + jnp.dot(kvt_ref[0, h * 128:(h + 1) * 128, pl.ds(k0, tq)],
                                                      p.astype(BF16), preferred_element_type=F32)
            m_scr[h] = m_new

    def body(kt, carry):
        key_tile(pl.multiple_of(kt * tq, tq), False)
        return carry

    lax.fori_loop(0, qi, body, 0)
    key_tile(pl.multiple_of(qi * tq, tq), True)
    o_t = jnp.concatenate([(acc_scr[h] / l_scr[h])[D_NOPE:D_NOPE + DV_B] for h in range(H_B)], axis=0)
    o_ref[0] = o_t.T.astype(o_ref.dtype)


def _flash(qt3, kv3, kvt3, kp3, tq):
    b, t, _ = kv3.shape
    return pl.pallas_call(
        functools.partial(_flash_kernel, tq=tq),
        grid=(b, t // tq),
        in_specs=[pl.BlockSpec((1, H_B * QW, tq), lambda i, j: (i, 0, j)),
                  pl.BlockSpec((1, t, H_B * 128), lambda i, j: (i, 0, 0)),
                  pl.BlockSpec((1, H_B * 128, t), lambda i, j: (i, 0, 0)),
                  pl.BlockSpec((1, t, 128), lambda i, j: (i, 0, 0))],
        out_specs=pl.BlockSpec((1, tq, H_B * DV_B), lambda i, j: (i, j, 0)),
        out_shape=jax.ShapeDtypeStruct((b, t, H_B * DV_B), BF16),
        scratch_shapes=[pltpu.VMEM((H_B, 1, tq), F32), pltpu.VMEM((H_B, 1, tq), F32),
                        pltpu.VMEM((H_B, 128, tq), F32)],
        compiler_params=_cparams(("parallel", "arbitrary")),
        name="mla_flash",
    )(qt3, kv3, kvt3, kp3)


def _decode_kernel(pt_ref, qlat_ref, qpe_ref, latn_ref, krn_ref, wuv_ref, lat_hbm, kct_hbm, o_ref,
                   lat_buf, kct_buf, lat_sem, kct_sem, *, gp, n_groups, n_slots, t_new):
    b = pl.program_id(0)
    nb = pl.num_programs(0)
    nq = H_B * t_new

    def page_copies(bi, g, slot):
        copies = []
        for j in range(gp):
            page = pt_ref[bi, g * gp + j]
            copies.append(pltpu.make_async_copy(lat_hbm.at[page], lat_buf.at[slot, pl.ds(j * PAGE, PAGE), :],
                                                lat_sem.at[slot]))
            copies.append(pltpu.make_async_copy(kct_hbm.at[page], kct_buf.at[slot, j], kct_sem.at[slot]))
        return copies

    def start_group(bi, g, slot):
        for cp in page_copies(bi, g, slot):
            cp.start()

    def wait_group(bi, g, slot):
        for cp in page_copies(bi, g, slot):
            cp.wait()

    ahead = n_slots // 2

    @pl.when(b == 0)
    def _():
        for g in range(ahead):
            start_group(0, g, g)

    ql = jnp.concatenate([qlat_ref[0, :, h * KV_LORA:(h + 1) * KV_LORA] for h in range(H_B)], axis=0).astype(BF16)
    qp = jnp.concatenate([qpe_ref[0, :, h * D_ROPE:(h + 1) * D_ROPE] for h in range(H_B)], axis=0).astype(BF16)

    def group_softmax(s):
        m_g = jnp.max(s, axis=-1, keepdims=True)
        p = jnp.exp(s - m_g)
        return p.astype(BF16), m_g, jnp.sum(p, axis=-1, keepdims=True)

    def merge(state, group):
        m_old, l_old, acc_old = state
        p, m_g, l_g, values = group
        pv = jnp.dot(p, values, preferred_element_type=F32)
        m_new = jnp.maximum(m_old, m_g)
        a_old = jnp.exp(m_old - m_new)
        a_g = jnp.exp(m_g - m_new)
        return m_new, a_old * l_old + a_g * l_g, a_old * acc_old + a_g * pv

    state = (jnp.full((nq, 1), NEG_INF, F32), jnp.zeros((nq, 1), F32), jnp.zeros((nq, KV_LORA), F32))
    pending = None
    for g in range(n_groups):
        slot = g % n_slots
        wait_group(b, g, slot)
        nxt = g + ahead
        if nxt < n_groups:
            start_group(b, nxt, nxt % n_slots)
        else:
            @pl.when(b + 1 < nb)
            def _():
                start_group(b + 1, nxt - n_groups, nxt % n_slots)
        lat = lat_buf[slot].astype(BF16)
        kct = jnp.concatenate([kct_buf[slot, j] for j in range(gp)], axis=1).astype(BF16)
        s = (lax.dot_general(ql, lat, (((1,), (1,)), ((), ())), preferred_element_type=F32)
             + jnp.dot(qp, kct, preferred_element_type=F32))
        if pending is not None:
            state = merge(state, pending)
        pending = group_softmax(s) + (lat,)
    state = merge(state, pending)

    latn = latn_ref[0].astype(BF16)
    krn = krn_ref[0].astype(BF16)
    s = (lax.dot_general(ql, latn, (((1,), (1,)), ((), ())), preferred_element_type=F32)
         + lax.dot_general(qp, krn, (((1,), (1,)), ((), ())), preferred_element_type=F32))
    tok = lax.broadcasted_iota(jnp.int32, (nq, t_new), 0) % t_new
    key = lax.broadcasted_iota(jnp.int32, (nq, t_new), 1)
    _, l, acc = merge(state, group_softmax(jnp.where(key <= tok, s, NEG_INF)) + (latn,))
    o_lat = acc / l
    outs = [_dot(o_lat[h * t_new:(h + 1) * t_new], wuv_ref[h]) for h in range(H_B)]
    o_ref[0] = jnp.concatenate(outs, axis=1).astype(o_ref.dtype)


def _decode(page_table, qlat3, qpe3, latn3, krn3, wuv, cache_lat, cache_kpe_t, gp):
    b, t_new, _ = qlat3.shape
    n_pages = page_table.shape[1]
    n_groups = n_pages // gp
    n_slots = min(4, n_groups)
    assert n_slots >= 2 and n_groups % n_slots == 0
    per_batch = lambda w: pl.BlockSpec((1, t_new, w), lambda i, pt: (i, 0, 0))
    in_specs = [per_batch(H_B * KV_LORA), per_batch(H_B * D_ROPE), per_batch(KV_LORA), per_batch(D_ROPE),
                pl.BlockSpec(wuv.shape, lambda i, pt: (0, 0, 0)),
                pl.BlockSpec(memory_space=pl.ANY), pl.BlockSpec(memory_space=pl.ANY)]
    return pl.pallas_call(
        functools.partial(_decode_kernel, gp=gp, n_groups=n_groups, n_slots=n_slots, t_new=t_new),
        grid_spec=pltpu.PrefetchScalarGridSpec(
            num_scalar_prefetch=1,
            grid=(b,),
            in_specs=in_specs,
            out_specs=pl.BlockSpec((1, t_new, H_B * DV_B), lambda i, pt: (i, 0, 0)),
            scratch_shapes=[pltpu.VMEM((n_slots, gp * PAGE, KV_LORA), F32),
                            pltpu.VMEM((n_slots, gp, D_ROPE, PAGE), F32),
                            pltpu.SemaphoreType.DMA((n_slots,)), pltpu.SemaphoreType.DMA((n_slots,))]),
        out_shape=jax.ShapeDtypeStruct((b, t_new, H_B * DV_B), BF16),
        compiler_params=_cparams(("arbitrary",)),
        name="mla_decode",
    )(page_table, qlat3, qpe3, latn3, krn3, wuv, cache_lat, cache_kpe_t)


def _rot_cols(w):
    half = D_ROPE // 2
    return jnp.concatenate([-w[..., half:], w[..., :half]], axis=-1)


def _layout_w_in_ab(w):
    d = w.shape[0]
    o = QKV_A + H_A * DV_A
    qkv_z, b_w, a_w = w[:, :o], w[:, o:o + H_A], w[:, o + H_A:o + 2 * H_A]
    o += 2 * H_A
    cq, ckv, kpe = w[:, o:o + Q_LORA], w[:, o + Q_LORA:o + Q_LORA + KV_LORA], w[:, o + Q_LORA + KV_LORA:]
    kr = _rot_cols(kpe)
    pad = jnp.zeros((d, 128 - 2 * H_A), w.dtype)
    return jnp.concatenate([qkv_z, cq, ckv, kpe, kr, kpe, kr, b_w, a_w, pad], axis=1).astype(BF16)


def _layout_w_uq(w_uq):
    nope, pe = w_uq[..., :D_NOPE], w_uq[..., D_NOPE:]
    rot = _rot_cols(pe)
    z = jnp.zeros_like(nope)
    return jnp.concatenate([nope, z, pe, pe, rot, rot], axis=-1).reshape(w_uq.shape[0], H_B * QW).astype(BF16)


def _rope_tables(pos):
    half = D_ROPE // 2
    inv = ROPE_BASE ** (-jnp.arange(half, dtype=F32) / half)
    ang = pos.astype(F32)[:, None] * inv
    c = jnp.concatenate([jnp.cos(ang), jnp.cos(ang)], axis=1)
    s = jnp.concatenate([jnp.sin(ang), jnp.sin(ang)], axis=1)
    one = jnp.ones((pos.shape[0], 128), F32)
    tabq = jnp.concatenate([one, c, c, s, s], axis=1) * MLA_SCALE
    tabk = jnp.concatenate([c, s, c, s], axis=1)
    return tabq, tabk


def _row_tile(n):
    for tm in (512, 256, 128, 64, 32, 16, 8):
        if n % tm == 0:
            return tm
    raise ValueError(n)


def _trunk(x, pos, conv0, gdn0, hgrn0, paged, wts):
    b, t, d = x.shape
    n = b * t
    tm = min(256, _row_tile(n))
    tb = min(256, t)
    x2 = x.reshape(n, d)

    p2 = _proj(x2, wts["mix_norm"][0], wts["w_in_ab"], tm)
    p3 = p2.reshape(b, t, IN_AB_PAD)
    conv_new = p3[:, t - (CONV_W - 1):, :QKV_A]
    o_a, gdn_new = _gdn(p3, wts["conv_w"], wts["a_log"], wts["dt_bias"], wts["gdn_norm"], conv0, gdn0, tb,
                        math.gcd(b, 2 if gdn0 is None else 4))
    tabq, tabk = _rope_tables(pos)
    if paged is None:
        tmp = math.gcd(tm, t)
        lat, kr, qt, kv, kvt, kp4 = _mla_prep(p2, wts["q_norm"], wts["kv_norm"], tabq.T, tabk, wts["w_uq"].T,
                                              [wts["w_kv"], wts["w_kv"].T], False, tmp, t)
        o_b = _flash(qt, kv.reshape(b, t, -1), kvt, kp4.reshape(b, t, -1), min(512, t))
    else:
        cache_lat, cache_kpe, page_table = paged
        reps = tm // t
        lat, kr, qlat, qpe = _mla_prep(p2, wts["q_norm"], wts["kv_norm"], jnp.tile(tabq, (reps, 1)),
                                       jnp.tile(tabk, (reps, 1)), wts["w_uq"], [wts["w_ukt"]], True, tm, t)
        n_pages = page_table.shape[1]
        gp = math.gcd(n_pages // 4, 16)
        o_b = _decode(page_table, qlat.reshape(b, t, -1), qpe.reshape(b, t, -1), lat.reshape(b, t, -1),
                      kr.reshape(b, t, -1), wts["w_uv"], cache_lat, jnp.swapaxes(cache_kpe, 1, 2), gp)
    x2 = _outproj(x2, [o_a.reshape(n, -1), o_b.reshape(n, -1)], [wts["w_out_a"], wts["w_out_b"]], tm)
    tmm = min(512, _row_tile(n))
    x2 = _mlp(x2, wts["mlp_norm"][0], wts["w_up"][0], wts["w_down"][0], wts["final_norm"], False, tmm, 1024)

    pc = _proj(x2, wts["mix_norm"][1], wts["w_in_c"], tm)
    o_c, hgrn_new = _hgrn(pc.reshape(b, t, -1), wts["lb_logits"], 1, wts["g_norm_c"], hgrn0, tb,
                          H_C, 1)
    x2 = _outproj(x2, [o_c.reshape(n, -1)], [wts["w_out_c"]], tm)
    y2 = _mlp(x2, wts["mlp_norm"][1], wts["w_up"][1], wts["w_down"][1], wts["final_norm"], True, tmm, 1024)
    return (y2.reshape(b, t, d), gdn_new[None], conv_new[None], lat.reshape(b, t, -1)[None],
            kr.reshape(b, t, -1)[None], hgrn_new[None])


def kernel(x_prompt, x_sample, state_gdn, state_gdn_conv, cache_mla_latent, cache_mla_krope, state_hgrn,
           page_table, mix_norm, mlp_norm, final_norm, w_up, w_down, w_in_ab, conv_w_ab, a_log_ab, dt_bias_ab,
           gdn_norm_ab, q_norm_ab, w_uq_ab, kv_norm_ab, w_uk_ab, w_uv_ab, w_out_ab, w_in_c, lb_logits_c,
           g_norm_c, w_out_c):
    assert mix_norm.shape[0] == 2 and w_in_ab.shape[0] == 1 and w_in_c.shape[0] == 1
    assert a_log_ab.shape == (1, H_A) and conv_w_ab.shape == (1, CONV_W, QKV_A)
    assert w_uq_ab.shape[1:] == (Q_LORA, H_B, D_NOPE + D_ROPE) and w_uk_ab.shape[1:] == (KV_LORA, H_B, D_NOPE)
    assert cache_mla_latent.shape[2] == PAGE and x_prompt.shape[1] >= CONV_W - 1 and x_sample.shape[1] >= CONV_W - 1
    w_uk, w_uv = w_uk_ab[0], w_uv_ab[0]
    wts = {
        "mix_norm": mix_norm, "mlp_norm": mlp_norm, "final_norm": final_norm,
        "w_up": w_up.astype(BF16), "w_down": w_down.astype(BF16),
        "w_in_ab": _layout_w_in_ab(w_in_ab[0]), "conv_w": conv_w_ab[0], "a_log": a_log_ab[0],
        "dt_bias": dt_bias_ab[0], "gdn_norm": gdn_norm_ab[0], "q_norm": q_norm_ab[0], "kv_norm": kv_norm_ab[0],
        "w_uq": _layout_w_uq(w_uq_ab[0]),
        "w_kv": jnp.concatenate([w_uk, w_uv], axis=-1).reshape(KV_LORA, H_B * 128).astype(BF16),
        "w_ukt": jnp.transpose(w_uk, (1, 2, 0)).astype(BF16),
        "w_uv": jnp.transpose(w_uv, (1, 0, 2)).astype(BF16),
        "w_out_a": w_out_ab[0, :H_A * DV_A].astype(BF16), "w_out_b": w_out_ab[0, H_A * DV_A:].astype(BF16),
        "w_in_c": w_in_c[0].astype(BF16), "lb_logits": lb_logits_c, "g_norm_c": g_norm_c[0],
        "w_out_c": w_out_c[0].astype(BF16),
    }
    s_len = x_prompt.shape[1]
    outs_p = _trunk(x_prompt, jnp.arange(s_len), None, None, None, None, wts)
    past_len = page_table.shape[1] * cache_mla_latent.shape[2]
    outs_s = _trunk(x_sample, past_len + jnp.arange(x_sample.shape[1]),
                    state_gdn_conv.reshape(state_gdn_conv.shape[1:]), state_gdn.reshape(state_gdn.shape[1:]),
                    state_hgrn.reshape(state_hgrn.shape[1:]),
                    (cache_mla_latent.reshape(cache_mla_latent.shape[1:]),
                     cache_mla_krope.reshape(cache_mla_krope.shape[1:]), page_table), wts)
    return (outs_p[0], outs_s[0]) + outs_p[1:] + outs_s[1:]
```

```python
import functools
import math

import jax
import jax.numpy as jnp
from jax import lax
from jax.experimental import pallas as pl
from jax.experimental.pallas import tpu as pltpu

F32 = jnp.float32
BF16 = jnp.bfloat16
EPS = 1e-6
NEG_INF = float("-inf")

H_A, DK_A, DV_A, CONV_W = 4, 128, 128, 4
QKV_A = H_A * (2 * DK_A + DV_A)
H_B, D_NOPE, D_ROPE, DV_B = 8, 64, 32, 64
Q_LORA, KV_LORA = 256, 256
ROPE_BASE = 10000.0
MLA_SCALE = (D_NOPE + D_ROPE) ** -0.5
H_C, DK_C, DV_C = 8, 128, 128
CHUNK = 64
SUB = 16
PAGE = 128
QW = 256
IN_AB_PAD = 2816
V7X_VMEM_LIMIT = 48 * 1024 * 1024


def _cparams(sem):
    return pltpu.CompilerParams(dimension_semantics=sem, vmem_limit_bytes=V7X_VMEM_LIMIT)


def _const_spec(shape):
    nd = len(shape)
    return pl.BlockSpec(shape, lambda *_: (0,) * nd)


def _dot(a, b):
    return jnp.dot(a.astype(BF16), b.astype(BF16), preferred_element_type=F32)


def _dot_nt(a, b):
    return lax.dot_general(a.astype(BF16), b.astype(BF16), (((1,), (1,)), ((), ())),
                           preferred_element_type=F32)


def _dot_tn(a, b):
    return lax.dot_general(a.astype(BF16), b.astype(BF16), (((0,), (0,)), ((), ())),
                           preferred_element_type=F32)


def _rms(x, g):
    return x * lax.rsqrt(jnp.mean(x * x, axis=-1, keepdims=True) + EPS) * g


def _silu(x):
    return x * jax.nn.sigmoid(x)


def _cumsum_rows(x):
    n = x.shape[0]
    row = lax.broadcasted_iota(jnp.int32, x.shape, 0)
    s = 1
    while s < n:
        x = x + jnp.where(row >= s, pltpu.roll(x, s, axis=0), 0.0)
        s *= 2
    return x


def _proj_kernel(x_ref, g_ref, w_ref, o_ref):
    h = _rms(x_ref[...], g_ref[...]).astype(BF16)
    o_ref[...] = jnp.dot(h, w_ref[...], preferred_element_type=F32)


def _proj(x2, g, w, tm):
    n, d = x2.shape
    nout = w.shape[1]
    return pl.pallas_call(
        _proj_kernel,
        grid=(n // tm,),
        in_specs=[pl.BlockSpec((tm, d), lambda i: (i, 0)), _const_spec((1, d)), _const_spec((d, nout))],
        out_specs=pl.BlockSpec((tm, nout), lambda i: (i, 0)),
        out_shape=jax.ShapeDtypeStruct((n, nout), F32),
        compiler_params=_cparams(("parallel",)),
        name="proj",
    )(x2, g.reshape(1, d), w)


def _mix_mlp_kernel(*refs, n_in, final_norm):
    r_ref, g_ref, gf_ref = refs[:3]
    a_refs = refs[3:3 + n_in]
    w_refs = refs[3 + n_in:3 + 2 * n_in]
    wu_ref, wd_ref, o_ref = refs[3 + 2 * n_in:]
    x = r_ref[...]
    for a_ref, w_ref in zip(a_refs, w_refs):
        x = x + jnp.dot(a_ref[...], w_ref[...], preferred_element_type=F32)
    h = _rms(x, g_ref[...]).astype(BF16)
    a = jnp.square(jnp.maximum(jnp.dot(h, wu_ref[...], preferred_element_type=F32), 0.0))
    y = x + jnp.dot(a.astype(BF16), wd_ref[...], preferred_element_type=F32)
    o_ref[...] = _rms(y, gf_ref[...]) if final_norm else y


def _mix_mlp(res, acts, ws, g, w_up, w_down, gf, final_norm, tm):
    n, d = res.shape
    n_in = len(acts)
    resident = lambda w: pl.BlockSpec(w.shape, lambda i: (0,) * w.ndim, pipeline_mode=pl.Buffered(1))
    in_specs = [pl.BlockSpec((tm, d), lambda i: (i, 0)), _const_spec((1, d)), _const_spec((1, d))]
    in_specs += [pl.BlockSpec((tm, a.shape[1]), lambda i: (i, 0)) for a in acts]
    in_specs += [resident(w) for w in ws] + [resident(w_up), resident(w_down)]
    return pl.pallas_call(
        functools.partial(_mix_mlp_kernel, n_in=n_in, final_norm=final_norm),
        grid=(n // tm,),
        in_specs=in_specs,
        out_specs=pl.BlockSpec((tm, d), lambda i: (i, 0)),
        out_shape=jax.ShapeDtypeStruct((n, d), F32),
        compiler_params=_cparams(("parallel",)),
        name="mix_mlp",
    )(res, g.reshape(1, d), gf.reshape(1, d), *acts, *ws, w_up, w_down)


def _split3_dot(a, b):
    a_hi = a.astype(BF16)
    a_lo = (a - a_hi.astype(F32)).astype(BF16)
    b_hi = b.astype(BF16)
    b_lo = (b - b_hi.astype(F32)).astype(BF16)
    return jnp.dot(jnp.concatenate([a_hi, a_lo, a_hi], axis=1), jnp.concatenate([b_hi, b_hi, b_lo], axis=0),
                   preferred_element_type=F32)


def _pick_dot(a, pick):
    a_hi = a.astype(BF16)
    a_lo = (a - a_hi.astype(F32)).astype(BF16)
    pick = pick.astype(BF16)
    return jnp.dot(jnp.concatenate([a_hi, a_lo], axis=1), jnp.concatenate([pick, pick], axis=0),
                   preferred_element_type=F32)


def _solve_unit_lower(lowers, rhss, c):
    nh = len(lowers)
    s = 8
    nb = c // s
    w = nh * c
    row = lax.broadcasted_iota(jnp.int32, (c, w), 0)
    col = lax.broadcasted_iota(jnp.int32, (c, w), 1) % c
    if nb == 1:
        diag_cols = lowers
    else:
        r1 = lax.broadcasted_iota(jnp.int32, (c, c), 0)
        c1 = lax.broadcasted_iota(jnp.int32, (c, c), 1)
        pick = (lax.broadcasted_iota(jnp.int32, (c, s), 0) % s
                == lax.broadcasted_iota(jnp.int32, (c, s), 1)).astype(F32)
        diag_cols = [_pick_dot(jnp.where(r1 // s == c1 // s, lo, 0.0), pick) for lo in lowers]
    x = (row == col).astype(F32)
    for j in range(s - 1):
        mult = jnp.concatenate([jnp.broadcast_to(dc[:, j:j + 1], (c, c)) for dc in diag_cols], axis=1)
        pivot_rows = jnp.broadcast_to(x.reshape(nb, s, w)[:, j:j + 1, :], (nb, s, w)).reshape(c, w)
        x = x - mult * pivot_rows

    def block_diag(m):
        r2 = lax.broadcasted_iota(jnp.int32, (w, w), 0)
        c2 = lax.broadcasted_iota(jnp.int32, (w, w), 1)
        return jnp.where(r2 // c == c2 // c, jnp.concatenate([m] * nh, axis=0), 0.0)

    lower_cat = jnp.concatenate(lowers, axis=1)
    size = s
    while size < c:
        off = jnp.where((row // (2 * size) == col // (2 * size)) & (row // size != col // size) & (row > col),
                        lower_cat, 0.0)
        x = x - _split3_dot(_split3_dot(x, block_diag(off)), block_diag(x))
        size *= 2
    out = _split3_dot(block_diag(x), jnp.concatenate(rhss, axis=0))
    return [out[h * c:(h + 1) * c] for h in range(nh)]


def _gdn_head_pre(q, k, v, b_col, a_col, alog, dtb, c):
    q = q * lax.rsqrt(jnp.sum(q * q, axis=-1, keepdims=True) + EPS) * (DK_A ** -0.5)
    k = k * lax.rsqrt(jnp.sum(k * k, axis=-1, keepdims=True) + EPS)
    beta = jax.nn.sigmoid(b_col)
    g = -jnp.exp(alog) * jax.nn.softplus(a_col + dtb)
    gc_b = _cumsum_rows(jnp.broadcast_to(g, (c, 128)))
    gc = gc_b[:, 0:1]
    diff = gc_b[:, :c] - gc_b.T[:c, :]
    row = lax.broadcasted_iota(jnp.int32, (c, c), 0)
    col = lax.broadcasted_iota(jnp.int32, (c, c), 1)
    incl = row >= col
    decay = jnp.where(incl, jnp.exp(jnp.where(incl, diff, 0.0)), 0.0)
    kb = k * beta
    lower = jnp.where(row > col, _dot_nt(kb, k) * decay, 0.0)
    rhs = jnp.concatenate([v * beta, kb * jnp.exp(gc)], axis=1)
    intra = _dot_nt(q, k) * decay
    return q, k, gc, lower, rhs, intra


def _gdn_head_post(q, k, gc, uw, intra, z, gnorm, s_prev, c):
    u, w = uw[:, :DV_A], uw[:, DV_A:]
    v_new = u - _dot(w, s_prev)
    o = _dot(q * jnp.exp(gc), s_prev) + _dot(intra, v_new)
    g_last = gc[c - 1:c, :]
    s_new = s_prev * jnp.exp(g_last) + _dot_tn(k * jnp.exp(g_last - gc), v_new)
    o = _rms(o, gnorm) * _silu(z)
    return o, s_new


def _gdn_kernel(*refs, c, n_chunks, has_state, bg):
    (q_ref, k_ref, v_ref, z_ref, gt_ref, cwq_ref, cwk_ref, cwv_ref, alog_ref, dtb_ref, gn_ref) = refs[:11]
    if has_state:
        cq_ref, ck_ref, cv_ref, s0_ref = refs[11:15]
        o_ref, sout_ref, s_scr, tail_scr = refs[15:]
    else:
        o_ref, sout_ref, s_scr, tail_scr = refs[11:]
    n = pl.program_id(1)

    @pl.when(n == 0)
    def _():
        if has_state:
            s_scr[...] = s0_ref[...]
            tail_scr[0] = cq_ref[...]
            tail_scr[1] = ck_ref[...]
            tail_scr[2] = cv_ref[...]
        else:
            s_scr[...] = jnp.zeros_like(s_scr)
            tail_scr[...] = jnp.zeros_like(tail_scr)

    def chunk(ci):
        rows = pl.ds(ci * c, c)
        pre = []
        zs = []
        for bi in range(bg):
            conv = []
            for idx, (x_ref, cw_ref) in enumerate(((q_ref, cwq_ref), (k_ref, cwk_ref), (v_ref, cwv_ref))):
                x = x_ref[bi, rows, :]
                ext = jnp.concatenate([tail_scr[idx, bi], x], axis=0)
                cw = cw_ref[...]
                y = x * cw[CONV_W - 1:CONV_W, :]
                for sh in range(1, CONV_W):
                    y = y + pltpu.roll(ext, sh, axis=0)[8:] * cw[CONV_W - 1 - sh:CONV_W - sh, :]
                tail_scr[idx, bi] = x[c - 8:, :]
                conv.append(_silu(y))
            zs.append(z_ref[bi, rows, :])
            gt = gt_ref[bi, rows, :]
            for h in range(H_A):
                hs = slice(h * 128, (h + 1) * 128)
                pre.append(_gdn_head_pre(conv[0][:, hs], conv[1][:, hs], conv[2][:, hs],
                                         gt[:, h:h + 1], gt[:, H_A + h:H_A + h + 1],
                                         alog_ref[:, h:h + 1], dtb_ref[:, h:h + 1], c))
        per_solve = max(H_A, min(bg * H_A, 128 // c))
        uws = []
        for g0 in range(0, bg * H_A, per_solve):
            grp = pre[g0:g0 + per_solve]
            uws += _solve_unit_lower([p[3] for p in grp], [p[4] for p in grp], c)
        for bi in range(bg):
            outs = []
            for h in range(H_A):
                q, k, gc, _, _, intra = pre[bi * H_A + h]
                o, s_new = _gdn_head_post(q, k, gc, uws[bi * H_A + h], intra, zs[bi][:, h * 128:(h + 1) * 128],
                                          gn_ref[...], s_scr[bi, h], c)
                s_scr[bi, h] = s_new
                outs.append(o)
            o_ref[bi, rows, :] = jnp.concatenate(outs, axis=1).astype(o_ref.dtype)

    for ci in range(n_chunks):
        chunk(ci)

    @pl.when(n == pl.num_programs(1) - 1)
    def _():
        sout_ref[...] = s_scr[...]


def _gdn(p3, conv_w, a_log, dt_bias, gdn_norm, conv0, s0, tb, bg):
    b, t, _ = p3.shape
    c = math.gcd(t, CHUNK)
    has_state = s0 is not None
    hw = H_A * 128
    cw = conv_w
    blk = lambda j: pl.BlockSpec((bg, tb, hw), lambda i, n: (i, n, j))
    in_specs = [blk(0), blk(1), blk(2), blk(3),
                pl.BlockSpec((bg, tb, 128), lambda i, n: (i, n, (IN_AB_PAD - 128) // 128)),
                pl.BlockSpec((CONV_W, hw), lambda i, n: (0, 0)), pl.BlockSpec((CONV_W, hw), lambda i, n: (0, 1)),
                pl.BlockSpec((CONV_W, hw), lambda i, n: (0, 2)),
                _const_spec((1, H_A)), _const_spec((1, H_A)), _const_spec((1, DV_A))]
    args = [p3, p3, p3, p3, p3, cw, cw, cw, a_log.reshape(1, H_A), dt_bias.reshape(1, H_A),
            gdn_norm.reshape(1, DV_A)]
    if has_state:
        conv8 = jnp.pad(conv0, ((0, 0), (8 - (CONV_W - 1), 0), (0, 0)))
        in_specs += [pl.BlockSpec((bg, 8, hw), lambda i, n, j=j: (i, 0, j)) for j in range(3)]
        in_specs += [pl.BlockSpec((bg, H_A, DK_A, DV_A), lambda i, n: (i, 0, 0, 0))]
        args += [conv8, conv8, conv8, s0]
    return pl.pallas_call(
        functools.partial(_gdn_kernel, c=c, n_chunks=tb // c, has_state=has_state, bg=bg),
        grid=(b // bg, t // tb),
        in_specs=in_specs,
        out_specs=[pl.BlockSpec((bg, tb, hw), lambda i, n: (i, n, 0)),
                   pl.BlockSpec((bg, H_A, DK_A, DV_A), lambda i, n: (i, 0, 0, 0))],
        out_shape=[jax.ShapeDtypeStruct((b, t, hw), BF16),
                   jax.ShapeDtypeStruct((b, H_A, DK_A, DV_A), F32)],
        scratch_shapes=[pltpu.VMEM((bg, H_A, DK_A, DV_A), F32), pltpu.VMEM((3, bg, 8, hw), F32)],
        compiler_params=_cparams(("parallel", "arbitrary")),
        name="gdn",
    )(*args)


def _hgrn_chunk(qr, fr, v, lb, s_prev, c):
    sub = min(SUB, c)
    q = _silu(qr) * (DK_C ** -0.5)
    f = lb + (1.0 - lb) * jax.nn.sigmoid(fr)
    k = 1.0 - f
    gc = _cumsum_rows(jnp.log(f))
    o_inter = _dot(q * jnp.exp(gc), s_prev)
    row = lax.broadcasted_iota(jnp.int32, (sub, sub), 0)
    col = lax.broadcasted_iota(jnp.int32, (sub, sub), 1)
    outs = []
    for blk in range(c // sub):
        r0 = blk * sub
        g_i = gc[r0:r0 + sub]
        q_i = q[r0:r0 + sub]
        a_parts = [jnp.zeros((8, sub), F32) for _ in range(sub // 8)]
        for j in range(sub):
            jj = r0 + j
            lo = 8 * (j // 8)
            e = jnp.exp(g_i[lo:] - gc[jj:jj + 1, :])
            col_j = jnp.sum(q_i[lo:] * k[jj:jj + 1, :] * e, axis=-1, keepdims=True)
            for p in range(j // 8, sub // 8):
                a_parts[p] = jnp.where(col[:8] == j, col_j[8 * p - lo:8 * p - lo + 8], a_parts[p])
        a_diag = jnp.concatenate(a_parts, axis=0) if len(a_parts) > 1 else a_parts[0]
        a_diag = jnp.where(row >= col, a_diag, 0.0)
        o_i = _dot(a_diag, v[r0:r0 + sub])
        if blk > 0:
            g_ref = gc[r0 - 1:r0, :]
            a_off = _dot_nt(q_i * jnp.exp(g_i - g_ref), k[:r0] * jnp.exp(g_ref - gc[:r0]))
            o_i = o_i + _dot(a_off, v[:r0])
        outs.append(o_i)
    o = o_inter + (jnp.concatenate(outs, axis=0) if len(outs) > 1 else outs[0])
    g_last = gc[c - 1:c, :]
    gl_col = gc.T[:, c - 1:c]
    s_new = s_prev * jnp.exp(gl_col) + _dot_tn(k * jnp.exp(g_last - gc), v)
    return o, s_new


def _hgrn_kernel(*refs, c, n_chunks, layer, has_state, hg, bg):
    q_ref, f_ref, i_ref, gate_ref, lbl_ref, gn_ref = refs[:6]
    if has_state:
        s0_ref = refs[6]
        o_ref, sout_ref, s_scr = refs[7:]
    else:
        o_ref, sout_ref, s_scr = refs[6:]
    n = pl.program_id(2)

    @pl.when(n == 0)
    def _():
        if has_state:
            s_scr[...] = s0_ref[...]
        else:
            s_scr[...] = jnp.zeros_like(s_scr)

    lbl = lbl_ref[...]
    ex = jnp.exp(lbl - jnp.max(lbl, axis=0, keepdims=True))
    p = ex / jnp.sum(ex, axis=0, keepdims=True)
    lb = jnp.sum(p[:layer + 1], axis=0, keepdims=True) - p[0:1]

    def chunk(ci, carry):
        r0 = ci * c if isinstance(ci, int) else pl.multiple_of(ci * c, c)
        rows = pl.ds(r0, c)
        for bi in range(bg):
            for h in range(hg):
                hs = slice(h * 128, (h + 1) * 128)
                o, s_new = _hgrn_chunk(q_ref[bi, rows, hs], f_ref[bi, rows, hs], i_ref[bi, rows, hs], lb[:, hs],
                                       s_scr[bi, h], c)
                s_scr[bi, h] = s_new
                o_ref[bi, rows, hs] = (_rms(o, gn_ref[...]) * _silu(gate_ref[bi, rows, hs])).astype(o_ref.dtype)
        return carry

    if n_chunks == 1:
        chunk(0, 0)
    else:
        lax.fori_loop(0, n_chunks, chunk, 0)

    @pl.when(n == pl.num_programs(2) - 1)
    def _():
        sout_ref[...] = s_scr[...]


def _hgrn(pc3, lb_logits, layer, g_norm, s0, tb, hg, bg):
    b, t, _ = pc3.shape
    c = math.gcd(t, CHUNK)
    has_state = s0 is not None
    depth = lb_logits.shape[0]
    ng = H_C // hg
    blk = lambda j: pl.BlockSpec((bg, tb, hg * 128), lambda i, h, n: (i, n, j * ng + h))
    in_specs = [blk(0), blk(1), blk(2), blk(3),
                pl.BlockSpec((depth, hg * 128), lambda i, h, n: (0, h)), _const_spec((1, DV_C))]
    args = [pc3, pc3, pc3, pc3, lb_logits, g_norm.reshape(1, DV_C)]
    if has_state:
        in_specs.append(pl.BlockSpec((bg, hg, DK_C, DV_C), lambda i, h, n: (i, h, 0, 0)))
        args.append(s0)
    return pl.pallas_call(
        functools.partial(_hgrn_kernel, c=c, n_chunks=tb // c, layer=layer, has_state=has_state, hg=hg, bg=bg),
        grid=(b // bg, ng, t // tb),
        in_specs=in_specs,
        out_specs=[pl.BlockSpec((bg, tb, hg * 128), lambda i, h, n: (i, n, h)),
                   pl.BlockSpec((bg, hg, DK_C, DV_C), lambda i, h, n: (i, h, 0, 0))],
        out_shape=[jax.ShapeDtypeStruct((b, t, H_C * DV_C), BF16),
                   jax.ShapeDtypeStruct((b, H_C, DK_C, DV_C), F32)],
        scratch_shapes=[pltpu.VMEM((bg, hg, DK_C, DV_C), F32)],
        compiler_params=_cparams(("parallel", "parallel", "arbitrary")),
        name="hgrn",
    )(*args)


def _mla_prep_kernel(*refs, absorb):
    cq_ref, ckv_ref, kp_ref, qn_ref, kvn_ref, tq_ref, tk_ref, wq_ref = refs[:8]
    if absorb:
        wukt_ref, lat_ref, kr_ref, qlat_ref, qpe_ref = refs[8:]
    else:
        wkv_ref, wkvt_ref, lat_ref, kr_ref, qt_ref, kv_ref, kvt_ref, kp4_ref = refs[8:]
    lat = _rms(ckv_ref[...], kvn_ref[...])
    lat_ref[...] = lat
    kp4 = kp_ref[...] * tk_ref[...]
    kr_ref[...] = kp4[:, 0:D_ROPE] + kp4[:, D_ROPE:2 * D_ROPE]
    cqn = _rms(cq_ref[...], qn_ref[...]).astype(BF16)
    tq = tq_ref[...]
    if absorb:
        for h in range(H_B):
            qh = jnp.dot(cqn, wq_ref[:, h * QW:(h + 1) * QW], preferred_element_type=F32) * tq
            qlat_ref[:, h * KV_LORA:(h + 1) * KV_LORA] = _dot(qh[:, 0:D_NOPE], wukt_ref[h])
            qpe_ref[:, h * D_ROPE:(h + 1) * D_ROPE] = (qh[:, 128:128 + D_ROPE]
                                                      + qh[:, 128 + 2 * D_ROPE:128 + 3 * D_ROPE])
    else:
        nt_dims = (((1,), (1,)), ((), ()))
        for h in range(H_B):
            qh = lax.dot_general(wq_ref[h * QW:(h + 1) * QW, :], cqn, nt_dims, preferred_element_type=F32) * tq
            qt_ref[0, h * QW:(h + 1) * QW, :] = qh.astype(BF16)
        lat_b = lat.astype(BF16)
        kv_ref[...] = jnp.dot(lat_b, wkv_ref[...], preferred_element_type=F32).astype(BF16)
        kvt_ref[0] = lax.dot_general(wkvt_ref[...], lat_b, nt_dims, preferred_element_type=F32).astype(BF16)
        kp4_ref[...] = kp4.astype(BF16)


def _mla_prep(p2, q_norm, kv_norm, tabq, tabk, wq, w_extra, absorb, tm, seq):
    n = p2.shape[0]
    nt = seq // tm if not absorb else 1
    tabq_spec = (pl.BlockSpec((tm, QW), lambda i: (0, 0)) if absorb
                 else pl.BlockSpec((QW, tm), lambda i: (0, i % nt)))
    in_specs = [pl.BlockSpec((tm, Q_LORA), lambda i: (i, 2048 // 256)),
                pl.BlockSpec((tm, KV_LORA), lambda i: (i, 2304 // 256)),
                pl.BlockSpec((tm, 128), lambda i: (i, 2560 // 128)),
                _const_spec((1, Q_LORA)), _const_spec((1, KV_LORA)),
                tabq_spec, pl.BlockSpec((tm, 128), lambda i: (i % nt, 0)),
                _const_spec(wq.shape)] + [_const_spec(w.shape) for w in w_extra]
    row = lambda w: pl.BlockSpec((tm, w), lambda i: (i, 0))
    out_specs = [row(KV_LORA), row(D_ROPE)]
    out_shape = [jax.ShapeDtypeStruct((n, KV_LORA), F32), jax.ShapeDtypeStruct((n, D_ROPE), F32)]
    if absorb:
        out_specs += [row(H_B * KV_LORA), row(H_B * D_ROPE)]
        out_shape += [jax.ShapeDtypeStruct((n, H_B * KV_LORA), F32), jax.ShapeDtypeStruct((n, H_B * D_ROPE), F32)]
    else:
        b = n // seq
        col = lambda w: pl.BlockSpec((1, w, tm), lambda i: (i // nt, 0, i % nt))
        out_specs += [col(H_B * QW), row(H_B * 128), col(H_B * 128), row(128)]
        out_shape += [jax.ShapeDtypeStruct((b, H_B * QW, seq), BF16), jax.ShapeDtypeStruct((n, H_B * 128), BF16),
                      jax.ShapeDtypeStruct((b, H_B * 128, seq), BF16), jax.ShapeDtypeStruct((n, 128), BF16)]
    return pl.pallas_call(
        functools.partial(_mla_prep_kernel, absorb=absorb),
        grid=(n // tm,),
        in_specs=in_specs, out_specs=out_specs, out_shape=out_shape,
        compiler_params=_cparams(("parallel",)),
        name="mla_prep",
    )(p2, p2, p2, q_norm.reshape(1, Q_LORA), kv_norm.reshape(1, KV_LORA), tabq, tabk, wq, *w_extra)


def _flash_kernel(qt_ref, kv_ref, kvt_ref, kp_ref, o_ref, m_scr, l_scr, acc_scr, *, tq):
    qi = pl.program_id(1)
    key = lax.broadcasted_iota(jnp.int32, (tq, tq), 0)
    qry = lax.broadcasted_iota(jnp.int32, (tq, tq), 1)
    m_scr[...] = jnp.full_like(m_scr, NEG_INF)
    l_scr[...] = jnp.zeros_like(l_scr)
    acc_scr[...] = jnp.zeros_like(acc_scr)

    def key_tile(k0, masked):
        kp = kp_ref[0, pl.ds(k0, tq), :]
        for h in range(H_B):
            keys = jnp.concatenate([kv_ref[0, pl.ds(k0, tq), h * 128:(h + 1) * 128], kp], axis=1)
            s = jnp.dot(keys, qt_ref[0, h * QW:(h + 1) * QW, :], preferred_element_type=F32)
            if masked:
                s = jnp.where(key <= qry, s, NEG_INF)
            m = m_scr[h]
            m_new = jnp.maximum(m, jnp.max(s, axis=0, keepdims=True))
            alpha = jnp.exp(m - m_new)
            p = jnp.exp(s - m_new)
            l_scr[h] = alpha * l_scr[h] + jnp.sum(p, axis=0, keepdims=True)
            acc_scr[h] = alpha * acc_scr[h] + jnp.dot(kvt_ref[0, h * 128:(h + 1) * 128, pl.ds(k0, tq)],
                                                      p.astype(BF16), preferred_element_type=F32)
            m_scr[h] = m_new

    def body(kt, carry):
        key_tile(pl.multiple_of(kt * tq, tq), False)
        return carry

    lax.fori_loop(0, qi, body, 0)
    key_tile(pl.multiple_of(qi * tq, tq), True)
    o_t = jnp.concatenate([(acc_scr[h] / l_scr[h])[D_NOPE:D_NOPE + DV_B] for h in range(H_B)], axis=0)
    o_ref[0] = o_t.T.astype(o_ref.dtype)


def _flash(qt3, kv3, kvt3, kp3, tq):
    b, t, _ = kv3.shape
    return pl.pallas_call(
        functools.partial(_flash_kernel, tq=tq),
        grid=(b, t // tq),
        in_specs=[pl.BlockSpec((1, H_B * QW, tq), lambda i, j: (i, 0, j)),
                  pl.BlockSpec((1, t, H_B * 128), lambda i, j: (i, 0, 0)),
                  pl.BlockSpec((1, H_B * 128, t), lambda i, j: (i, 0, 0)),
                  pl.BlockSpec((1, t, 128), lambda i, j: (i, 0, 0))],
        out_specs=pl.BlockSpec((1, tq, H_B * DV_B), lambda i, j: (i, j, 0)),
        out_shape=jax.ShapeDtypeStruct((b, t, H_B * DV_B), BF16),
        scratch_shapes=[pltpu.VMEM((H_B, 1, tq), F32), pltpu.VMEM((H_B, 1, tq), F32),
                        pltpu.VMEM((H_B, 128, tq), F32)],
        compiler_params=_cparams(("parallel", "arbitrary")),
        name="mla_flash",
    )(qt3, kv3, kvt3, kp3)


def _decode_kernel(pt_ref, qlat_ref, qpe_ref, latn_ref, krn_ref, wuv_ref, lat_hbm, kct_hbm, o_ref,
                   lat_buf, kct_buf, lat_sem, kct_sem, *, gp, n_groups, n_slots, t_new):
    b = pl.program_id(0)
    nb = pl.num_programs(0)
    nq = H_B * t_new

    def page_copies(bi, g, slot):
        copies = []
        for j in range(gp):
            page = pt_ref[bi, g * gp + j]
            copies.append(pltpu.make_async_copy(lat_hbm.at[page], lat_buf.at[slot, pl.ds(j * PAGE, PAGE), :],
                                                lat_sem.at[slot]))
            copies.append(pltpu.make_async_copy(kct_hbm.at[page], kct_buf.at[slot, j], kct_sem.at[slot]))
        return copies

    def start_group(bi, g, slot):
        for cp in page_copies(bi, g, slot):
            cp.start()

    def wait_group(bi, g, slot):
        for cp in page_copies(bi, g, slot):
            cp.wait()

    ahead = n_slots // 2

    @pl.when(b == 0)
    def _():
        for g in range(ahead):
            start_group(0, g, g)

    ql = jnp.concatenate([qlat_ref[0, :, h * KV_LORA:(h + 1) * KV_LORA] for h in range(H_B)], axis=0).astype(BF16)
    qp = jnp.concatenate([qpe_ref[0, :, h * D_ROPE:(h + 1) * D_ROPE] for h in range(H_B)], axis=0).astype(BF16)

    def group_softmax(s):
        m_g = jnp.max(s, axis=-1, keepdims=True)
        p = jnp.exp(s - m_g)
        return p.astype(BF16), m_g, jnp.sum(p, axis=-1, keepdims=True)

    def merge(state, group):
        m_old, l_old, acc_old = state
        p, m_g, l_g, values = group
        pv = jnp.dot(p, values, preferred_element_type=F32)
        m_new = jnp.maximum(m_old, m_g)
        a_old = jnp.exp(m_old - m_new)
        a_g = jnp.exp(m_g - m_new)
        return m_new, a_old * l_old + a_g * l_g, a_old * acc_old + a_g * pv

    state = (jnp.full((nq, 1), NEG_INF, F32), jnp.zeros((nq, 1), F32), jnp.zeros((nq, KV_LORA), F32))
    pending = None
    for g in range(n_groups):
        slot = g % n_slots
        wait_group(b, g, slot)
        nxt = g + ahead
        if nxt < n_groups:
            start_group(b, nxt, nxt % n_slots)
        else:
            @pl.when(b + 1 < nb)
            def _():
                start_group(b + 1, nxt - n_groups, nxt % n_slots)
        lat = lat_buf[slot].astype(BF16)
        kct = jnp.concatenate([kct_buf[slot, j] for j in range(gp)], axis=1).astype(BF16)
        s = (lax.dot_general(ql, lat, (((1,), (1,)), ((), ())), preferred_element_type=F32)
             + jnp.dot(qp, kct, preferred_element_type=F32))
        if pending is not None:
            state = merge(state, pending)
        pending = group_softmax(s) + (lat,)
    state = merge(state, pending)

    latn = latn_ref[0].astype(BF16)
    krn = krn_ref[0].astype(BF16)
    s = (lax.dot_general(ql, latn, (((1,), (1,)), ((), ())), preferred_element_type=F32)
         + lax.dot_general(qp, krn, (((1,), (1,)), ((), ())), preferred_element_type=F32))
    tok = lax.broadcasted_iota(jnp.int32, (nq, t_new), 0) % t_new
    key = lax.broadcasted_iota(jnp.int32, (nq, t_new), 1)
    _, l, acc = merge(state, group_softmax(jnp.where(key <= tok, s, NEG_INF)) + (latn,))
    o_lat = acc / l
    outs = [_dot(o_lat[h * t_new:(h + 1) * t_new], wuv_ref[h]) for h in range(H_B)]
    o_ref[0] = jnp.concatenate(outs, axis=1).astype(o_ref.dtype)


def _decode(page_table, qlat3, qpe3, latn3, krn3, wuv, cache_lat, cache_kpe_t, gp):
    b, t_new, _ = qlat3.shape
    n_pages = page_table.shape[1]
    n_groups = n_pages // gp
    n_slots = min(4, n_groups)
    assert n_slots >= 2 and n_groups % n_slots == 0
    per_batch = lambda w: pl.BlockSpec((1, t_new, w), lambda i, pt: (i, 0, 0))
    in_specs = [per_batch(H_B * KV_LORA), per_batch(H_B * D_ROPE), per_batch(KV_LORA), per_batch(D_ROPE),
                pl.BlockSpec(wuv.shape, lambda i, pt: (0, 0, 0)),
                pl.BlockSpec(memory_space=pl.ANY), pl.BlockSpec(memory_space=pl.ANY)]
    return pl.pallas_call(
        functools.partial(_decode_kernel, gp=gp, n_groups=n_groups, n_slots=n_slots, t_new=t_new),
        grid_spec=pltpu.PrefetchScalarGridSpec(
            num_scalar_prefetch=1,
            grid=(b,),
            in_specs=in_specs,
            out_specs=pl.BlockSpec((1, t_new, H_B * DV_B), lambda i, pt: (i, 0, 0)),
            scratch_shapes=[pltpu.VMEM((n_slots, gp * PAGE, KV_LORA), F32),
                            pltpu.VMEM((n_slots, gp, D_ROPE, PAGE), F32),
                            pltpu.SemaphoreType.DMA((n_slots,)), pltpu.SemaphoreType.DMA((n_slots,))]),
        out_shape=jax.ShapeDtypeStruct((b, t_new, H_B * DV_B), BF16),
        compiler_params=_cparams(("arbitrary",)),
        name="mla_decode",
    )(page_table, qlat3, qpe3, latn3, krn3, wuv, cache_lat, cache_kpe_t)


def _rot_cols(w):
    half = D_ROPE // 2
    return jnp.concatenate([-w[..., half:], w[..., :half]], axis=-1)


def _layout_w_in_ab(w):
    d = w.shape[0]
    o = QKV_A + H_A * DV_A
    qkv_z, b_w, a_w = w[:, :o], w[:, o:o + H_A], w[:, o + H_A:o + 2 * H_A]
    o += 2 * H_A
    cq, ckv, kpe = w[:, o:o + Q_LORA], w[:, o + Q_LORA:o + Q_LORA + KV_LORA], w[:, o + Q_LORA + KV_LORA:]
    kr = _rot_cols(kpe)
    pad = jnp.zeros((d, 128 - 2 * H_A), w.dtype)
    return jnp.concatenate([qkv_z, cq, ckv, kpe, kr, kpe, kr, b_w, a_w, pad], axis=1).astype(BF16)


def _layout_w_uq(w_uq):
    nope, pe = w_uq[..., :D_NOPE], w_uq[..., D_NOPE:]
    rot = _rot_cols(pe)
    z = jnp.zeros_like(nope)
    return jnp.concatenate([nope, z, pe, pe, rot, rot], axis=-1).reshape(w_uq.shape[0], H_B * QW).astype(BF16)


def _rope_tables(pos):
    half = D_ROPE // 2
    inv = ROPE_BASE ** (-jnp.arange(half, dtype=F32) / half)
    ang = pos.astype(F32)[:, None] * inv
    c = jnp.concatenate([jnp.cos(ang), jnp.cos(ang)], axis=1)
    s = jnp.concatenate([jnp.sin(ang), jnp.sin(ang)], axis=1)
    one = jnp.ones((pos.shape[0], 128), F32)
    tabq = jnp.concatenate([one, c, c, s, s], axis=1) * MLA_SCALE
    tabk = jnp.concatenate([c, s, c, s], axis=1)
    return tabq, tabk


def _row_tile(n):
    for tm in (512, 256, 128, 64, 32, 16, 8):
        if n % tm == 0:
            return tm
    raise ValueError(n)


def _trunk(x, pos, conv0, gdn0, hgrn0, paged, wts):
    b, t, d = x.shape
    n = b * t
    tm = min(256, _row_tile(n))
    tb = min(256, t)
    x2 = x.reshape(n, d)

    tmd = min(512, _row_tile(n))
    p2 = _proj(x2, wts["mix_norm"][0], wts["w_in_ab"], tmd)
    p3 = p2.reshape(b, t, IN_AB_PAD)
    conv_new = p3[:, t - (CONV_W - 1):, :QKV_A]
    o_a, gdn_new = _gdn(p3, wts["conv_w"], wts["a_log"], wts["dt_bias"], wts["gdn_norm"], conv0, gdn0, tb,
                        math.gcd(b, 2 if gdn0 is None else 4))
    tabq, tabk = _rope_tables(pos)
    if paged is None:
        tmp = math.gcd(tm, t)
        lat, kr, qt, kv, kvt, kp4 = _mla_prep(p2, wts["q_norm"], wts["kv_norm"], tabq.T, tabk, wts["w_uq"].T,
                                              [wts["w_kv"], wts["w_kv"].T], False, tmp, t)
        o_b = _flash(qt, kv.reshape(b, t, -1), kvt, kp4.reshape(b, t, -1), min(512, t))
    else:
        cache_lat, cache_kpe, page_table = paged
        reps = tm // t
        lat, kr, qlat, qpe = _mla_prep(p2, wts["q_norm"], wts["kv_norm"], jnp.tile(tabq, (reps, 1)),
                                       jnp.tile(tabk, (reps, 1)), wts["w_uq"], [wts["w_ukt"]], True, tm, t)
        n_pages = page_table.shape[1]
        gp = math.gcd(n_pages // 4, 16)
        o_b = _decode(page_table, qlat.reshape(b, t, -1), qpe.reshape(b, t, -1), lat.reshape(b, t, -1),
                      kr.reshape(b, t, -1), wts["w_uv"], cache_lat, jnp.swapaxes(cache_kpe, 1, 2), gp)
    tmm = min(256, _row_tile(n))
    x2 = _mix_mlp(x2, [o_a.reshape(n, -1), o_b.reshape(n, -1)], [wts["w_out_a"], wts["w_out_b"]],
                  wts["mlp_norm"][0], wts["w_up"][0], wts["w_down"][0], wts["final_norm"], False, tmm)

    pc = _proj(x2, wts["mix_norm"][1], wts["w_in_c"], tmd)
    o_c, hgrn_new = _hgrn(pc.reshape(b, t, -1), wts["lb_logits"], 1, wts["g_norm_c"], hgrn0, tb,
                          H_C, 1)
    y2 = _mix_mlp(x2, [o_c.reshape(n, -1)], [wts["w_out_c"]],
                  wts["mlp_norm"][1], wts["w_up"][1], wts["w_down"][1], wts["final_norm"], True, tmm)
    return (y2.reshape(b, t, d), gdn_new[None], conv_new[None], lat.reshape(b, t, -1)[None],
            kr.reshape(b, t, -1)[None], hgrn_new[None])


def kernel(x_prompt, x_sample, state_gdn, state_gdn_conv, cache_mla_latent, cache_mla_krope, state_hgrn,
           page_table, mix_norm, mlp_norm, final_norm, w_up, w_down, w_in_ab, conv_w_ab, a_log_ab, dt_bias_ab,
           gdn_norm_ab, q_norm_ab, w_uq_ab, kv_norm_ab, w_uk_ab, w_uv_ab, w_out_ab, w_in_c, lb_logits_c,
           g_norm_c, w_out_c):
    assert mix_norm.shape[0] == 2 and w_in_ab.shape[0] == 1 and w_in_c.shape[0] == 1
    assert a_log_ab.shape == (1, H_A) and conv_w_ab.shape == (1, CONV_W, QKV_A)
    assert w_uq_ab.shape[1:] == (Q_LORA, H_B, D_NOPE + D_ROPE) and w_uk_ab.shape[1:] == (KV_LORA, H_B, D_NOPE)
    assert cache_mla_latent.shape[2] == PAGE and x_prompt.shape[1] >= CONV_W - 1 and x_sample.shape[1] >= CONV_W - 1
    w_uk, w_uv = w_uk_ab[0], w_uv_ab[0]
    wts = {
        "mix_norm": mix_norm, "mlp_norm": mlp_norm, "final_norm": final_norm,
        "w_up": w_up.astype(BF16), "w_down": w_down.astype(BF16),
        "w_in_ab": _layout_w_in_ab(w_in_ab[0]), "conv_w": conv_w_ab[0], "a_log": a_log_ab[0],
        "dt_bias": dt_bias_ab[0], "gdn_norm": gdn_norm_ab[0], "q_norm": q_norm_ab[0], "kv_norm": kv_norm_ab[0],
        "w_uq": _layout_w_uq(w_uq_ab[0]),
        "w_kv": jnp.concatenate([w_uk, w_uv], axis=-1).reshape(KV_LORA, H_B * 128).astype(BF16),
        "w_ukt": jnp.transpose(w_uk, (1, 2, 0)).astype(BF16),
        "w_uv": jnp.transpose(w_uv, (1, 0, 2)).astype(BF16),
        "w_out_a": w_out_ab[0, :H_A * DV_A].astype(BF16), "w_out_b": w_out_ab[0, H_A * DV_A:].astype(BF16),
        "w_in_c": w_in_c[0].astype(BF16), "lb_logits": lb_logits_c, "g_norm_c": g_norm_c[0],
        "w_out_c": w_out_c[0].astype(BF16),
    }
    s_len = x_prompt.shape[1]
    outs_p = _trunk(x_prompt, jnp.arange(s_len), None, None, None, None, wts)
    past_len = page_table.shape[1] * cache_mla_latent.shape[2]
    outs_s = _trunk(x_sample, past_len + jnp.arange(x_sample.shape[1]),
                    state_gdn_conv.reshape(state_gdn_conv.shape[1:]), state_gdn.reshape(state_gdn.shape[1:]),
                    state_hgrn.reshape(state_hgrn.shape[1:]),
                    (cache_mla_latent.reshape(cache_mla_latent.shape[1:]),
                     cache_mla_krope.reshape(cache_mla_krope.shape[1:]), page_table), wts)
    return (outs_p[0], outs_s[0]) + outs_p[1:] + outs_s[1:]
```

```python
import functools
import math

import jax
import jax.numpy as jnp
from jax import lax
from jax.experimental import pallas as pl
from jax.experimental.pallas import tpu as pltpu

F32 = jnp.float32
BF16 = jnp.bfloat16
EPS = 1e-6
NEG_INF = float("-inf")

H_A, DK_A, DV_A, CONV_W = 4, 128, 128, 4
QKV_A = H_A * (2 * DK_A + DV_A)
H_B, D_NOPE, D_ROPE, DV_B = 8, 64, 32, 64
Q_LORA, KV_LORA = 256, 256
ROPE_BASE = 10000.0
MLA_SCALE = (D_NOPE + D_ROPE) ** -0.5
H_C, DK_C, DV_C = 8, 128, 128
CHUNK = 64
SUB = 16
PAGE = 128
QW = 256
IN_AB_PAD = 2816
V7X_VMEM_LIMIT = 48 * 1024 * 1024


def _cparams(sem):
    return pltpu.CompilerParams(dimension_semantics=sem, vmem_limit_bytes=V7X_VMEM_LIMIT)


def _const_spec(shape):
    nd = len(shape)
    return pl.BlockSpec(shape, lambda *_: (0,) * nd)


def _dot(a, b):
    return jnp.dot(a.astype(BF16), b.astype(BF16), preferred_element_type=F32)


def _dot_nt(a, b):
    return lax.dot_general(a.astype(BF16), b.astype(BF16), (((1,), (1,)), ((), ())),
                           preferred_element_type=F32)


def _dot_tn(a, b):
    return lax.dot_general(a.astype(BF16), b.astype(BF16), (((0,), (0,)), ((), ())),
                           preferred_element_type=F32)


def _rms(x, g):
    return x * lax.rsqrt(jnp.mean(x * x, axis=-1, keepdims=True) + EPS) * g


def _silu(x):
    return x * jax.nn.sigmoid(x)


def _cumsum_rows(x):
    n = x.shape[0]
    row = lax.broadcasted_iota(jnp.int32, x.shape, 0)
    s = 1
    while s < n:
        x = x + jnp.where(row >= s, pltpu.roll(x, s, axis=0), 0.0)
        s *= 2
    return x


def _proj_kernel(x_ref, g_ref, w_ref, o_ref):
    h = _rms(x_ref[...], g_ref[...]).astype(BF16)
    o_ref[...] = jnp.dot(h, w_ref[...], preferred_element_type=F32)


def _proj(x2, g, w, tm):
    n, d = x2.shape
    nout = w.shape[1]
    return pl.pallas_call(
        _proj_kernel,
        grid=(n // tm,),
        in_specs=[pl.BlockSpec((tm, d), lambda i: (i, 0)), _const_spec((1, d)), _const_spec((d, nout))],
        out_specs=pl.BlockSpec((tm, nout), lambda i: (i, 0)),
        out_shape=jax.ShapeDtypeStruct((n, nout), F32),
        compiler_params=_cparams(("parallel",)),
        name="proj",
    )(x2, g.reshape(1, d), w)


def _mix_mlp_kernel(*refs, n_in, final_norm):
    r_ref, g_ref, gf_ref = refs[:3]
    a_refs = refs[3:3 + n_in]
    w_refs = refs[3 + n_in:3 + 2 * n_in]
    wu_ref, wd_ref, o_ref = refs[3 + 2 * n_in:]
    x = r_ref[...]
    for a_ref, w_ref in zip(a_refs, w_refs):
        x = x + jnp.dot(a_ref[...], w_ref[...], preferred_element_type=F32)
    h = _rms(x, g_ref[...]).astype(BF16)
    a = jnp.square(jnp.maximum(jnp.dot(h, wu_ref[...], preferred_element_type=F32), 0.0))
    y = x + jnp.dot(a.astype(BF16), wd_ref[...], preferred_element_type=F32)
    o_ref[...] = _rms(y, gf_ref[...]) if final_norm else y


def _mix_mlp(res, acts, ws, g, w_up, w_down, gf, final_norm, tm):
    n, d = res.shape
    n_in = len(acts)
    resident = lambda w: pl.BlockSpec(w.shape, lambda i: (0,) * w.ndim, pipeline_mode=pl.Buffered(1))
    in_specs = [pl.BlockSpec((tm, d), lambda i: (i, 0)), _const_spec((1, d)), _const_spec((1, d))]
    in_specs += [pl.BlockSpec((tm, a.shape[1]), lambda i: (i, 0)) for a in acts]
    in_specs += [resident(w) for w in ws] + [resident(w_up), resident(w_down)]
    return pl.pallas_call(
        functools.partial(_mix_mlp_kernel, n_in=n_in, final_norm=final_norm),
        grid=(n // tm,),
        in_specs=in_specs,
        out_specs=pl.BlockSpec((tm, d), lambda i: (i, 0)),
        out_shape=jax.ShapeDtypeStruct((n, d), F32),
        compiler_params=_cparams(("parallel",)),
        name="mix_mlp",
    )(res, g.reshape(1, d), gf.reshape(1, d), *acts, *ws, w_up, w_down)


def _split3_dot(a, b):
    a_hi = a.astype(BF16)
    a_lo = (a - a_hi.astype(F32)).astype(BF16)
    b_hi = b.astype(BF16)
    b_lo = (b - b_hi.astype(F32)).astype(BF16)
    return jnp.dot(jnp.concatenate([a_hi, a_lo, a_hi], axis=1), jnp.concatenate([b_hi, b_hi, b_lo], axis=0),
                   preferred_element_type=F32)


def _pick_dot(a, pick):
    a_hi = a.astype(BF16)
    a_lo = (a - a_hi.astype(F32)).astype(BF16)
    pick = pick.astype(BF16)
    return jnp.dot(jnp.concatenate([a_hi, a_lo], axis=1), jnp.concatenate([pick, pick], axis=0),
                   preferred_element_type=F32)


def _solve_unit_lower(lowers, rhss, c):
    nh = len(lowers)
    s = 8
    nb = c // s
    w = nh * c
    row = lax.broadcasted_iota(jnp.int32, (c, w), 0)
    col = lax.broadcasted_iota(jnp.int32, (c, w), 1) % c
    if nb == 1:
        diag_cols = lowers
    else:
        r1 = lax.broadcasted_iota(jnp.int32, (c, c), 0)
        c1 = lax.broadcasted_iota(jnp.int32, (c, c), 1)
        pick = (lax.broadcasted_iota(jnp.int32, (c, s), 0) % s
                == lax.broadcasted_iota(jnp.int32, (c, s), 1)).astype(F32)
        diag_cols = [_pick_dot(jnp.where(r1 // s == c1 // s, lo, 0.0), pick) for lo in lowers]
    x = (row == col).astype(F32)
    for j in range(s - 1):
        mult = jnp.concatenate([jnp.broadcast_to(dc[:, j:j + 1], (c, c)) for dc in diag_cols], axis=1)
        pivot_rows = jnp.broadcast_to(x.reshape(nb, s, w)[:, j:j + 1, :], (nb, s, w)).reshape(c, w)
        x = x - mult * pivot_rows

    def block_diag(m):
        r2 = lax.broadcasted_iota(jnp.int32, (w, w), 0)
        c2 = lax.broadcasted_iota(jnp.int32, (w, w), 1)
        return jnp.where(r2 // c == c2 // c, jnp.concatenate([m] * nh, axis=0), 0.0)

    lower_cat = jnp.concatenate(lowers, axis=1)
    size = s
    while size < c:
        off = jnp.where((row // (2 * size) == col // (2 * size)) & (row // size != col // size) & (row > col),
                        lower_cat, 0.0)
        x = x - _split3_dot(_split3_dot(x, block_diag(off)), block_diag(x))
        size *= 2
    out = _split3_dot(block_diag(x), jnp.concatenate(rhss, axis=0))
    return [out[h * c:(h + 1) * c] for h in range(nh)]


def _gdn_head_pre(q, k, v, b_col, a_col, alog, dtb, c):
    q = q * lax.rsqrt(jnp.sum(q * q, axis=-1, keepdims=True) + EPS) * (DK_A ** -0.5)
    k = k * lax.rsqrt(jnp.sum(k * k, axis=-1, keepdims=True) + EPS)
    beta = jax.nn.sigmoid(b_col)
    g = -jnp.exp(alog) * jax.nn.softplus(a_col + dtb)
    gc_b = _cumsum_rows(jnp.broadcast_to(g, (c, 128)))
    gc = gc_b[:, 0:1]
    diff = gc_b[:, :c] - gc_b.T[:c, :]
    row = lax.broadcasted_iota(jnp.int32, (c, c), 0)
    col = lax.broadcasted_iota(jnp.int32, (c, c), 1)
    incl = row >= col
    decay = jnp.where(incl, jnp.exp(jnp.where(incl, diff, 0.0)), 0.0)
    kb = k * beta
    lower = jnp.where(row > col, _dot_nt(kb, k) * decay, 0.0)
    rhs = jnp.concatenate([v * beta, kb * jnp.exp(gc)], axis=1)
    intra = _dot_nt(q, k) * decay
    return q, k, gc, lower, rhs, intra


def _gdn_head_post(q, k, gc, uw, intra, z, gnorm, s_prev, c):
    u, w = uw[:, :DV_A], uw[:, DV_A:]
    v_new = u - _dot(w, s_prev)
    o = _dot(q * jnp.exp(gc), s_prev) + _dot(intra, v_new)
    g_last = gc[c - 1:c, :]
    s_new = s_prev * jnp.exp(g_last) + _dot_tn(k * jnp.exp(g_last - gc), v_new)
    o = _rms(o, gnorm) * _silu(z)
    return o, s_new


def _gdn_kernel(*refs, c, n_chunks, has_state, bg):
    (q_ref, k_ref, v_ref, z_ref, gt_ref, cwq_ref, cwk_ref, cwv_ref, alog_ref, dtb_ref, gn_ref) = refs[:11]
    if has_state:
        cq_ref, ck_ref, cv_ref, s0_ref = refs[11:15]
        o_ref, sout_ref, s_scr, tail_scr = refs[15:]
    else:
        o_ref, sout_ref, s_scr, tail_scr = refs[11:]
    n = pl.program_id(1)

    @pl.when(n == 0)
    def _():
        if has_state:
            s_scr[...] = s0_ref[...]
            tail_scr[0] = cq_ref[...]
            tail_scr[1] = ck_ref[...]
            tail_scr[2] = cv_ref[...]
        else:
            s_scr[...] = jnp.zeros_like(s_scr)
            tail_scr[...] = jnp.zeros_like(tail_scr)

    def chunk(ci):
        rows = pl.ds(ci * c, c)
        pre = []
        zs = []
        for bi in range(bg):
            conv = []
            for idx, (x_ref, cw_ref) in enumerate(((q_ref, cwq_ref), (k_ref, cwk_ref), (v_ref, cwv_ref))):
                x = x_ref[bi, rows, :]
                ext = jnp.concatenate([tail_scr[idx, bi], x], axis=0)
                cw = cw_ref[...]
                y = x * cw[CONV_W - 1:CONV_W, :]
                for sh in range(1, CONV_W):
                    y = y + pltpu.roll(ext, sh, axis=0)[8:] * cw[CONV_W - 1 - sh:CONV_W - sh, :]
                tail_scr[idx, bi] = x[c - 8:, :]
                conv.append(_silu(y))
            zs.append(z_ref[bi, rows, :])
            gt = gt_ref[bi, rows, :]
            for h in range(H_A):
                hs = slice(h * 128, (h + 1) * 128)
                pre.append(_gdn_head_pre(conv[0][:, hs], conv[1][:, hs], conv[2][:, hs],
                                         gt[:, h:h + 1], gt[:, H_A + h:H_A + h + 1],
                                         alog_ref[:, h:h + 1], dtb_ref[:, h:h + 1], c))
        per_solve = max(H_A, min(bg * H_A, 128 // c))
        uws = []
        for g0 in range(0, bg * H_A, per_solve):
            grp = pre[g0:g0 + per_solve]
            uws += _solve_unit_lower([p[3] for p in grp], [p[4] for p in grp], c)
        for bi in range(bg):
            outs = []
            for h in range(H_A):
                q, k, gc, _, _, intra = pre[bi * H_A + h]
                o, s_new = _gdn_head_post(q, k, gc, uws[bi * H_A + h], intra, zs[bi][:, h * 128:(h + 1) * 128],
                                          gn_ref[...], s_scr[bi, h], c)
                s_scr[bi, h] = s_new
                outs.append(o)
            o_ref[bi, rows, :] = jnp.concatenate(outs, axis=1).astype(o_ref.dtype)

    for ci in range(n_chunks):
        chunk(ci)

    @pl.when(n == pl.num_programs(1) - 1)
    def _():
        sout_ref[...] = s_scr[...]


def _gdn(p3, conv_w, a_log, dt_bias, gdn_norm, conv0, s0, tb, bg):
    b, t, _ = p3.shape
    c = math.gcd(t, CHUNK)
    has_state = s0 is not None
    hw = H_A * 128
    cw = conv_w
    blk = lambda j: pl.BlockSpec((bg, tb, hw), lambda i, n: (i, n, j))
    in_specs = [blk(0), blk(1), blk(2), blk(3),
                pl.BlockSpec((bg, tb, 128), lambda i, n: (i, n, (IN_AB_PAD - 128) // 128)),
                pl.BlockSpec((CONV_W, hw), lambda i, n: (0, 0)), pl.BlockSpec((CONV_W, hw), lambda i, n: (0, 1)),
                pl.BlockSpec((CONV_W, hw), lambda i, n: (0, 2)),
                _const_spec((1, H_A)), _const_spec((1, H_A)), _const_spec((1, DV_A))]
    args = [p3, p3, p3, p3, p3, cw, cw, cw, a_log.reshape(1, H_A), dt_bias.reshape(1, H_A),
            gdn_norm.reshape(1, DV_A)]
    if has_state:
        conv8 = jnp.pad(conv0, ((0, 0), (8 - (CONV_W - 1), 0), (0, 0)))
        in_specs += [pl.BlockSpec((bg, 8, hw), lambda i, n, j=j: (i, 0, j)) for j in range(3)]
        in_specs += [pl.BlockSpec((bg, H_A, DK_A, DV_A), lambda i, n: (i, 0, 0, 0))]
        args += [conv8, conv8, conv8, s0]
    return pl.pallas_call(
        functools.partial(_gdn_kernel, c=c, n_chunks=tb // c, has_state=has_state, bg=bg),
        grid=(b // bg, t // tb),
        in_specs=in_specs,
        out_specs=[pl.BlockSpec((bg, tb, hw), lambda i, n: (i, n, 0)),
                   pl.BlockSpec((bg, H_A, DK_A, DV_A), lambda i, n: (i, 0, 0, 0))],
        out_shape=[jax.ShapeDtypeStruct((b, t, hw), BF16),
                   jax.ShapeDtypeStruct((b, H_A, DK_A, DV_A), F32)],
        scratch_shapes=[pltpu.VMEM((bg, H_A, DK_A, DV_A), F32), pltpu.VMEM((3, bg, 8, hw), F32)],
        compiler_params=_cparams(("parallel", "arbitrary")),
        name="gdn",
    )(*args)


def _hgrn_chunk(qr, fr, v, lb, s_prev, c):
    sub = min(SUB, c)
    q = _silu(qr) * (DK_C ** -0.5)
    f = lb + (1.0 - lb) * jax.nn.sigmoid(fr)
    k = 1.0 - f
    gc = _cumsum_rows(jnp.log(f))
    o_inter = _dot(q * jnp.exp(gc), s_prev)
    row = lax.broadcasted_iota(jnp.int32, (sub, sub), 0)
    col = lax.broadcasted_iota(jnp.int32, (sub, sub), 1)
    outs = []
    for blk in range(c // sub):
        r0 = blk * sub
        g_i = gc[r0:r0 + sub]
        q_i = q[r0:r0 + sub]
        a_parts = [jnp.zeros((8, sub), F32) for _ in range(sub // 8)]
        for j in range(sub):
            jj = r0 + j
            lo = 8 * (j // 8)
            e = jnp.exp(g_i[lo:] - gc[jj:jj + 1, :])
            col_j = jnp.sum(q_i[lo:] * k[jj:jj + 1, :] * e, axis=-1, keepdims=True)
            for p in range(j // 8, sub // 8):
                a_parts[p] = jnp.where(col[:8] == j, col_j[8 * p - lo:8 * p - lo + 8], a_parts[p])
        a_diag = jnp.concatenate(a_parts, axis=0) if len(a_parts) > 1 else a_parts[0]
        a_diag = jnp.where(row >= col, a_diag, 0.0)
        o_i = _dot(a_diag, v[r0:r0 + sub])
        if blk > 0:
            g_ref = gc[r0 - 1:r0, :]
            a_off = _dot_nt(q_i * jnp.exp(g_i - g_ref), k[:r0] * jnp.exp(g_ref - gc[:r0]))
            o_i = o_i + _dot(a_off, v[:r0])
        outs.append(o_i)
    o = o_inter + (jnp.concatenate(outs, axis=0) if len(outs) > 1 else outs[0])
    g_last = gc[c - 1:c, :]
    gl_col = gc.T[:, c - 1:c]
    s_new = s_prev * jnp.exp(gl_col) + _dot_tn(k * jnp.exp(g_last - gc), v)
    return o, s_new


def _hgrn_kernel(*refs, c, n_chunks, layer, has_state, hg, bg):
    q_ref, f_ref, i_ref, gate_ref, lbl_ref, gn_ref = refs[:6]
    if has_state:
        s0_ref = refs[6]
        o_ref, sout_ref, s_scr = refs[7:]
    else:
        o_ref, sout_ref, s_scr = refs[6:]
    n = pl.program_id(2)

    @pl.when(n == 0)
    def _():
        if has_state:
            s_scr[...] = s0_ref[...]
        else:
            s_scr[...] = jnp.zeros_like(s_scr)

    lbl = lbl_ref[...]
    ex = jnp.exp(lbl - jnp.max(lbl, axis=0, keepdims=True))
    p = ex / jnp.sum(ex, axis=0, keepdims=True)
    lb = jnp.sum(p[:layer + 1], axis=0, keepdims=True) - p[0:1]

    def chunk(ci, carry):
        r0 = ci * c if isinstance(ci, int) else pl.multiple_of(ci * c, c)
        rows = pl.ds(r0, c)
        for bi in range(bg):
            for h in range(hg):
                hs = slice(h * 128, (h + 1) * 128)
                o, s_new = _hgrn_chunk(q_ref[bi, rows, hs], f_ref[bi, rows, hs], i_ref[bi, rows, hs], lb[:, hs],
                                       s_scr[bi, h], c)
                s_scr[bi, h] = s_new
                o_ref[bi, rows, hs] = (_rms(o, gn_ref[...]) * _silu(gate_ref[bi, rows, hs])).astype(o_ref.dtype)
        return carry

    if n_chunks == 1:
        chunk(0, 0)
    else:
        lax.fori_loop(0, n_chunks, chunk, 0)

    @pl.when(n == pl.num_programs(2) - 1)
    def _():
        sout_ref[...] = s_scr[...]


def _hgrn(pc3, lb_logits, layer, g_norm, s0, tb, hg, bg):
    b, t, _ = pc3.shape
    c = math.gcd(t, CHUNK)
    has_state = s0 is not None
    depth = lb_logits.shape[0]
    ng = H_C // hg
    blk = lambda j: pl.BlockSpec((bg, tb, hg * 128), lambda i, h, n: (i, n, j * ng + h))
    in_specs = [blk(0), blk(1), blk(2), blk(3),
                pl.BlockSpec((depth, hg * 128), lambda i, h, n: (0, h)), _const_spec((1, DV_C))]
    args = [pc3, pc3, pc3, pc3, lb_logits, g_norm.reshape(1, DV_C)]
    if has_state:
        in_specs.append(pl.BlockSpec((bg, hg, DK_C, DV_C), lambda i, h, n: (i, h, 0, 0)))
        args.append(s0)
    return pl.pallas_call(
        functools.partial(_hgrn_kernel, c=c, n_chunks=tb // c, layer=layer, has_state=has_state, hg=hg, bg=bg),
        grid=(b // bg, ng, t // tb),
        in_specs=in_specs,
        out_specs=[pl.BlockSpec((bg, tb, hg * 128), lambda i, h, n: (i, n, h)),
                   pl.BlockSpec((bg, hg, DK_C, DV_C), lambda i, h, n: (i, h, 0, 0))],
        out_shape=[jax.ShapeDtypeStruct((b, t, H_C * DV_C), BF16),
                   jax.ShapeDtypeStruct((b, H_C, DK_C, DV_C), F32)],
        scratch_shapes=[pltpu.VMEM((bg, hg, DK_C, DV_C), F32)],
        compiler_params=_cparams(("parallel", "parallel", "arbitrary")),
        name="hgrn",
    )(*args)


def _mla_prep_kernel(*refs, absorb):
    cq_ref, ckv_ref, kp_ref, qn_ref, kvn_ref, tq_ref, tk_ref, wq_ref = refs[:8]
    if absorb:
        wukt_ref, lat_ref, kr_ref, qlat_ref, qpe_ref = refs[8:]
    else:
        wkv_ref, wkvt_ref, lat_ref, kr_ref, qt_ref, kv_ref, kvt_ref, kp4_ref = refs[8:]
    lat = _rms(ckv_ref[...], kvn_ref[...])
    lat_ref[...] = lat
    kp4 = kp_ref[...] * tk_ref[...]
    kr_ref[...] = kp4[:, 0:D_ROPE] + kp4[:, D_ROPE:2 * D_ROPE]
    cqn = _rms(cq_ref[...], qn_ref[...]).astype(BF16)
    tq = tq_ref[...]
    if absorb:
        for h in range(H_B):
            qh = jnp.dot(cqn, wq_ref[:, h * QW:(h + 1) * QW], preferred_element_type=F32) * tq
            qlat_ref[:, h * KV_LORA:(h + 1) * KV_LORA] = _dot(qh[:, 0:D_NOPE], wukt_ref[h])
            qpe_ref[:, h * D_ROPE:(h + 1) * D_ROPE] = (qh[:, 128:128 + D_ROPE]
                                                      + qh[:, 128 + 2 * D_ROPE:128 + 3 * D_ROPE])
    else:
        nt_dims = (((1,), (1,)), ((), ()))
        for h in range(H_B):
            qh = lax.dot_general(wq_ref[h * QW:(h + 1) * QW, :], cqn, nt_dims, preferred_element_type=F32) * tq
            qt_ref[0, h * QW:(h + 1) * QW, :] = qh.astype(BF16)
        lat_b = lat.astype(BF16)
        kv_ref[...] = jnp.dot(lat_b, wkv_ref[...], preferred_element_type=F32).astype(BF16)
        kvt_ref[0] = lax.dot_general(wkvt_ref[...], lat_b, nt_dims, preferred_element_type=F32).astype(BF16)
        kp4_ref[...] = kp4.astype(BF16)


def _mla_prep(p2, q_norm, kv_norm, tabq, tabk, wq, w_extra, absorb, tm, seq):
    n = p2.shape[0]
    nt = seq // tm if not absorb else 1
    tabq_spec = (pl.BlockSpec((tm, QW), lambda i: (0, 0)) if absorb
                 else pl.BlockSpec((QW, tm), lambda i: (0, i % nt)))
    in_specs = [pl.BlockSpec((tm, Q_LORA), lambda i: (i, 2048 // 256)),
                pl.BlockSpec((tm, KV_LORA), lambda i: (i, 2304 // 256)),
                pl.BlockSpec((tm, 128), lambda i: (i, 2560 // 128)),
                _const_spec((1, Q_LORA)), _const_spec((1, KV_LORA)),
                tabq_spec, pl.BlockSpec((tm, 128), lambda i: (i % nt, 0)),
                _const_spec(wq.shape)] + [_const_spec(w.shape) for w in w_extra]
    row = lambda w: pl.BlockSpec((tm, w), lambda i: (i, 0))
    out_specs = [row(KV_LORA), row(D_ROPE)]
    out_shape = [jax.ShapeDtypeStruct((n, KV_LORA), F32), jax.ShapeDtypeStruct((n, D_ROPE), F32)]
    if absorb:
        out_specs += [row(H_B * KV_LORA), row(H_B * D_ROPE)]
        out_shape += [jax.ShapeDtypeStruct((n, H_B * KV_LORA), F32), jax.ShapeDtypeStruct((n, H_B * D_ROPE), F32)]
    else:
        b = n // seq
        col = lambda w: pl.BlockSpec((1, w, tm), lambda i: (i // nt, 0, i % nt))
        out_specs += [col(H_B * QW), row(H_B * 128), col(H_B * 128), row(128)]
        out_shape += [jax.ShapeDtypeStruct((b, H_B * QW, seq), BF16), jax.ShapeDtypeStruct((n, H_B * 128), BF16),
                      jax.ShapeDtypeStruct((b, H_B * 128, seq), BF16), jax.ShapeDtypeStruct((n, 128), BF16)]
    return pl.pallas_call(
        functools.partial(_mla_prep_kernel, absorb=absorb),
        grid=(n // tm,),
        in_specs=in_specs, out_specs=out_specs, out_shape=out_shape,
        compiler_params=_cparams(("parallel",)),
        name="mla_prep",
    )(p2, p2, p2, q_norm.reshape(1, Q_LORA), kv_norm.reshape(1, KV_LORA), tabq, tabk, wq, *w_extra)


def _flash_kernel(qt_ref, kv_ref, kvt_ref, kp_ref, o_ref, m_scr, l_scr, acc_scr, *, tq):
    qi = pl.program_id(1)
    key = lax.broadcasted_iota(jnp.int32, (tq, tq), 0)
    qry = lax.broadcasted_iota(jnp.int32, (tq, tq), 1)
    m_scr[...] = jnp.full_like(m_scr, NEG_INF)
    l_scr[...] = jnp.zeros_like(l_scr)
    acc_scr[...] = jnp.zeros_like(acc_scr)

    def key_tile(k0, masked):
        kp = kp_ref[0, pl.ds(k0, tq), :]
        for h in range(H_B):
            keys = jnp.concatenate([kv_ref[0, pl.ds(k0, tq), h * 128:(h + 1) * 128], kp], axis=1)
            s = jnp.dot(keys, qt_ref[0, h * QW:(h + 1) * QW, :], preferred_element_type=F32)
            if masked:
                s = jnp.where(key <= qry, s, NEG_INF)
            m = m_scr[h]
            m_new = jnp.maximum(m, jnp.max(s, axis=0, keepdims=True))
            alpha = jnp.exp(m - m_new)
            p = jnp.exp(s - m_new)
            l_scr[h] = alpha * l_scr[h] + jnp.sum(p, axis=0, keepdims=True)
            acc_scr[h] = alpha * acc_scr[h] + jnp.dot(kvt_ref[0, h * 128:(h + 1) * 128, pl.ds(k0, tq)],
                                                      p.astype(BF16), preferred_element_type=F32)
            m_scr[h] = m_new

    def body(kt, carry):
        key_tile(pl.multiple_of(kt * tq, tq), False)
        return carry

    lax.fori_loop(0, qi, body, 0)
    key_tile(pl.multiple_of(qi * tq, tq), True)
    o_t = jnp.concatenate([(acc_scr[h] / l_scr[h])[D_NOPE:D_NOPE + DV_B] for h in range(H_B)], axis=0)
    o_ref[0] = o_t.T.astype(o_ref.dtype)


def _flash(qt3, kv3, kvt3, kp3, tq):
    b, t, _ = kv3.shape
    return pl.pallas_call(
        functools.partial(_flash_kernel, tq=tq),
        grid=(b, t // tq),
        in_specs=[pl.BlockSpec((1, H_B * QW, tq), lambda i, j: (i, 0, j)),
                  pl.BlockSpec((1, t, H_B * 128), lambda i, j: (i, 0, 0)),
                  pl.BlockSpec((1, H_B * 128, t), lambda i, j: (i, 0, 0)),
                  pl.BlockSpec((1, t, 128), lambda i, j: (i, 0, 0))],
        out_specs=pl.BlockSpec((1, tq, H_B * DV_B), lambda i, j: (i, j, 0)),
        out_shape=jax.ShapeDtypeStruct((b, t, H_B * DV_B), BF16),
        scratch_shapes=[pltpu.VMEM((H_B, 1, tq), F32), pltpu.VMEM((H_B, 1, tq), F32),
                        pltpu.VMEM((H_B, 128, tq), F32)],
        compiler_params=_cparams(("parallel", "arbitrary")),
        name="mla_flash",
    )(qt3, kv3, kvt3, kp3)


def _decode_kernel(pt_ref, qlat_ref, qpe_ref, latn_ref, krn_ref, wuv_ref, lat_hbm, kct_hbm, o_ref,
                   lat_buf, kct_buf, lat_sem, kct_sem, *, gp, gpe, be, n_slots, t_new):
    step = pl.program_id(0)
    n_steps = pl.num_programs(0)
    nq = H_B * t_new
    groups = [(e, g) for e in range(be) for g in range(gpe)]
    ahead = n_slots // 2

    def page_copies(bi, g, slot):
        copies = []
        for j in range(gp):
            page = pt_ref[bi, g * gp + j]
            copies.append(pltpu.make_async_copy(lat_hbm.at[page], lat_buf.at[slot, pl.ds(j * PAGE, PAGE), :],
                                                lat_sem.at[slot]))
            copies.append(pltpu.make_async_copy(kct_hbm.at[page], kct_buf.at[slot, j], kct_sem.at[slot]))
        return copies

    def start_group(st, idx):
        e, g = groups[idx % len(groups)]
        for cp in page_copies(st * be + e, g, idx % n_slots):
            cp.start()

    @pl.when(step == 0)
    def _():
        for idx in range(ahead):
            start_group(0, idx)

    def group_softmax(s):
        m_g = jnp.max(s, axis=-1, keepdims=True)
        p = jnp.exp(s - m_g)
        return p.astype(BF16), m_g, jnp.sum(p, axis=-1, keepdims=True)

    def merge(state, group):
        m_old, l_old, acc_old = state
        p, m_g, l_g, values = group
        pv = jnp.dot(p, values, preferred_element_type=F32)
        m_new = jnp.maximum(m_old, m_g)
        a_old = jnp.exp(m_old - m_new)
        a_g = jnp.exp(m_g - m_new)
        return m_new, a_old * l_old + a_g * l_g, a_old * acc_old + a_g * pv

    nt_dims = (((1,), (1,)), ((), ()))
    for idx, (e, g) in enumerate(groups):
        slot = idx % n_slots
        for cp in page_copies(step * be + e, g, slot):
            cp.wait()
        nxt = idx + ahead
        if nxt < len(groups):
            start_group(step, nxt)
        else:
            @pl.when(step + 1 < n_steps)
            def _():
                start_group(step + 1, nxt)
        if g == 0:
            ql = jnp.concatenate([qlat_ref[e, :, h * KV_LORA:(h + 1) * KV_LORA] for h in range(H_B)],
                                 axis=0).astype(BF16)
            qp = jnp.concatenate([qpe_ref[e, :, h * D_ROPE:(h + 1) * D_ROPE] for h in range(H_B)],
                                 axis=0).astype(BF16)
            state = (jnp.full((nq, 1), NEG_INF, F32), jnp.zeros((nq, 1), F32), jnp.zeros((nq, KV_LORA), F32))
            pending = None
        lat = lat_buf[slot].astype(BF16)
        kct = jnp.concatenate([kct_buf[slot, j] for j in range(gp)], axis=1).astype(BF16)
        s = lax.dot_general(ql, lat, nt_dims, preferred_element_type=F32) + jnp.dot(qp, kct, preferred_element_type=F32)
        if pending is not None:
            state = merge(state, pending)
        pending = group_softmax(s) + (lat,)
        if g == gpe - 1:
            state = merge(state, pending)
            latn = latn_ref[e].astype(BF16)
            krn = krn_ref[e].astype(BF16)
            s = (lax.dot_general(ql, latn, nt_dims, preferred_element_type=F32)
                 + lax.dot_general(qp, krn, nt_dims, preferred_element_type=F32))
            tok = lax.broadcasted_iota(jnp.int32, (nq, t_new), 0) % t_new
            key = lax.broadcasted_iota(jnp.int32, (nq, t_new), 1)
            _, l, acc = merge(state, group_softmax(jnp.where(key <= tok, s, NEG_INF)) + (latn,))
            o_lat = acc / l
            outs = [_dot(o_lat[h * t_new:(h + 1) * t_new], wuv_ref[h]) for h in range(H_B)]
            o_ref[e] = jnp.concatenate(outs, axis=1).astype(o_ref.dtype)


def _decode(page_table, qlat3, qpe3, latn3, krn3, wuv, cache_lat, cache_kpe_t, gp, be):
    b, t_new, _ = qlat3.shape
    gpe = page_table.shape[1] // gp
    n_slots = min(4, be * gpe)
    assert n_slots >= 2 and (be * gpe) % n_slots == 0 and b % be == 0
    per_step = lambda w: pl.BlockSpec((be, t_new, w), lambda i, pt: (i, 0, 0))
    in_specs = [per_step(H_B * KV_LORA), per_step(H_B * D_ROPE), per_step(KV_LORA), per_step(D_ROPE),
                pl.BlockSpec(wuv.shape, lambda i, pt: (0, 0, 0)),
                pl.BlockSpec(memory_space=pl.ANY), pl.BlockSpec(memory_space=pl.ANY)]
    return pl.pallas_call(
        functools.partial(_decode_kernel, gp=gp, gpe=gpe, be=be, n_slots=n_slots, t_new=t_new),
        grid_spec=pltpu.PrefetchScalarGridSpec(
            num_scalar_prefetch=1,
            grid=(b // be,),
            in_specs=in_specs,
            out_specs=pl.BlockSpec((be, t_new, H_B * DV_B), lambda i, pt: (i, 0, 0)),
            scratch_shapes=[pltpu.VMEM((n_slots, gp * PAGE, KV_LORA), F32),
                            pltpu.VMEM((n_slots, gp, D_ROPE, PAGE), F32),
                            pltpu.SemaphoreType.DMA((n_slots,)), pltpu.SemaphoreType.DMA((n_slots,))]),
        out_shape=jax.ShapeDtypeStruct((b, t_new, H_B * DV_B), BF16),
        compiler_params=_cparams(("arbitrary",)),
        name="mla_decode",
    )(page_table, qlat3, qpe3, latn3, krn3, wuv, cache_lat, cache_kpe_t)


def _rot_cols(w):
    half = D_ROPE // 2
    return jnp.concatenate([-w[..., half:], w[..., :half]], axis=-1)


def _layout_w_in_ab(w):
    d = w.shape[0]
    o = QKV_A + H_A * DV_A
    qkv_z, b_w, a_w = w[:, :o], w[:, o:o + H_A], w[:, o + H_A:o + 2 * H_A]
    o += 2 * H_A
    cq, ckv, kpe = w[:, o:o + Q_LORA], w[:, o + Q_LORA:o + Q_LORA + KV_LORA], w[:, o + Q_LORA + KV_LORA:]
    kr = _rot_cols(kpe)
    pad = jnp.zeros((d, 128 - 2 * H_A), w.dtype)
    return jnp.concatenate([qkv_z, cq, ckv, kpe, kr, kpe, kr, b_w, a_w, pad], axis=1).astype(BF16)


def _layout_w_uq(w_uq):
    nope, pe = w_uq[..., :D_NOPE], w_uq[..., D_NOPE:]
    rot = _rot_cols(pe)
    z = jnp.zeros_like(nope)
    return jnp.concatenate([nope, z, pe, pe, rot, rot], axis=-1).reshape(w_uq.shape[0], H_B * QW).astype(BF16)


def _rope_tables(pos):
    half = D_ROPE // 2
    inv = ROPE_BASE ** (-jnp.arange(half, dtype=F32) / half)
    ang = pos.astype(F32)[:, None] * inv
    c = jnp.concatenate([jnp.cos(ang), jnp.cos(ang)], axis=1)
    s = jnp.concatenate([jnp.sin(ang), jnp.sin(ang)], axis=1)
    one = jnp.ones((pos.shape[0], 128), F32)
    tabq = jnp.concatenate([one, c, c, s, s], axis=1) * MLA_SCALE
    tabk = jnp.concatenate([c, s, c, s], axis=1)
    return tabq, tabk


def _row_tile(n):
    for tm in (512, 256, 128, 64, 32, 16, 8):
        if n % tm == 0:
            return tm
    raise ValueError(n)


def _trunk(x, pos, conv0, gdn0, hgrn0, paged, wts):
    b, t, d = x.shape
    n = b * t
    tm = min(256, _row_tile(n))
    tb = min(256, t)
    x2 = x.reshape(n, d)

    tmd = min(512, _row_tile(n))
    p2 = _proj(x2, wts["mix_norm"][0], wts["w_in_ab"], tmd)
    p3 = p2.reshape(b, t, IN_AB_PAD)
    conv_new = p3[:, t - (CONV_W - 1):, :QKV_A]
    o_a, gdn_new = _gdn(p3, wts["conv_w"], wts["a_log"], wts["dt_bias"], wts["gdn_norm"], conv0, gdn0, tb,
                        math.gcd(b, 2 if gdn0 is None else 4))
    tabq, tabk = _rope_tables(pos)
    if paged is None:
        tmp = math.gcd(tm, t)
        lat, kr, qt, kv, kvt, kp4 = _mla_prep(p2, wts["q_norm"], wts["kv_norm"], tabq.T, tabk, wts["w_uq"].T,
                                              [wts["w_kv"], wts["w_kv"].T], False, tmp, t)
        o_b = _flash(qt, kv.reshape(b, t, -1), kvt, kp4.reshape(b, t, -1), min(512, t))
    else:
        cache_lat, cache_kpe, page_table = paged
        reps = tm // t
        lat, kr, qlat, qpe = _mla_prep(p2, wts["q_norm"], wts["kv_norm"], jnp.tile(tabq, (reps, 1)),
                                       jnp.tile(tabk, (reps, 1)), wts["w_uq"], [wts["w_ukt"]], True, tm, t)
        n_pages = page_table.shape[1]
        gp = math.gcd(n_pages // 2, 32)
        o_b = _decode(page_table, qlat.reshape(b, t, -1), qpe.reshape(b, t, -1), lat.reshape(b, t, -1),
                      kr.reshape(b, t, -1), wts["w_uv"], cache_lat, jnp.swapaxes(cache_kpe, 1, 2), gp, math.gcd(b, 2))
    tmm = min(256, _row_tile(n))
    x2 = _mix_mlp(x2, [o_a.reshape(n, -1), o_b.reshape(n, -1)], [wts["w_out_a"], wts["w_out_b"]],
                  wts["mlp_norm"][0], wts["w_up"][0], wts["w_down"][0], wts["final_norm"], False, tmm)

    pc = _proj(x2, wts["mix_norm"][1], wts["w_in_c"], tmd)
    o_c, hgrn_new = _hgrn(pc.reshape(b, t, -1), wts["lb_logits"], 1, wts["g_norm_c"], hgrn0, tb,
                          H_C, 1)
    y2 = _mix_mlp(x2, [o_c.reshape(n, -1)], [wts["w_out_c"]],
                  wts["mlp_norm"][1], wts["w_up"][1], wts["w_down"][1], wts["final_norm"], True, tmm)
    return (y2.reshape(b, t, d), gdn_new[None], conv_new[None], lat.reshape(b, t, -1)[None],
            kr.reshape(b, t, -1)[None], hgrn_new[None])


def kernel(x_prompt, x_sample, state_gdn, state_gdn_conv, cache_mla_latent, cache_mla_krope, state_hgrn,
           page_table, mix_norm, mlp_norm, final_norm, w_up, w_down, w_in_ab, conv_w_ab, a_log_ab, dt_bias_ab,
           gdn_norm_ab, q_norm_ab, w_uq_ab, kv_norm_ab, w_uk_ab, w_uv_ab, w_out_ab, w_in_c, lb_logits_c,
           g_norm_c, w_out_c):
    assert mix_norm.shape[0] == 2 and w_in_ab.shape[0] == 1 and w_in_c.shape[0] == 1
    assert a_log_ab.shape == (1, H_A) and conv_w_ab.shape == (1, CONV_W, QKV_A)
    assert w_uq_ab.shape[1:] == (Q_LORA, H_B, D_NOPE + D_ROPE) and w_uk_ab.shape[1:] == (KV_LORA, H_B, D_NOPE)
    assert cache_mla_latent.shape[2] == PAGE and x_prompt.shape[1] >= CONV_W - 1 and x_sample.shape[1] >= CONV_W - 1
    w_uk, w_uv = w_uk_ab[0], w_uv_ab[0]
    wts = {
        "mix_norm": mix_norm, "mlp_norm": mlp_norm, "final_norm": final_norm,
        "w_up": w_up.astype(BF16), "w_down": w_down.astype(BF16),
        "w_in_ab": _layout_w_in_ab(w_in_ab[0]), "conv_w": conv_w_ab[0], "a_log": a_log_ab[0],
        "dt_bias": dt_bias_ab[0], "gdn_norm": gdn_norm_ab[0], "q_norm": q_norm_ab[0], "kv_norm": kv_norm_ab[0],
        "w_uq": _layout_w_uq(w_uq_ab[0]),
        "w_kv": jnp.concatenate([w_uk, w_uv], axis=-1).reshape(KV_LORA, H_B * 128).astype(BF16),
        "w_ukt": jnp.transpose(w_uk, (1, 2, 0)).astype(BF16),
        "w_uv": jnp.transpose(w_uv, (1, 0, 2)).astype(BF16),
        "w_out_a": w_out_ab[0, :H_A * DV_A].astype(BF16), "w_out_b": w_out_ab[0, H_A * DV_A:].astype(BF16),
        "w_in_c": w_in_c[0].astype(BF16), "lb_logits": lb_logits_c, "g_norm_c": g_norm_c[0],
        "w_out_c": w_out_c[0].astype(BF16),
    }
    s_len = x_prompt.shape[1]
    outs_p = _trunk(x_prompt, jnp.arange(s_len), None, None, None, None, wts)
    past_len = page_table.shape[1] * cache_mla_latent.shape[2]
    outs_s = _trunk(x_sample, past_len + jnp.arange(x_sample.shape[1]),
                    state_gdn_conv.reshape(state_gdn_conv.shape[1:]), state_gdn.reshape(state_gdn.shape[1:]),
                    state_hgrn.reshape(state_hgrn.shape[1:]),
                    (cache_mla_latent.reshape(cache_mla_latent.shape[1:]),
                     cache_mla_krope.reshape(cache_mla_krope.shape[1:]), page_table), wts)
    return (outs_p[0], outs_s[0]) + outs_p[1:] + outs_s[1:]
```

```python
import functools
import math

import jax
import jax.numpy as jnp
from jax import lax
from jax.experimental import pallas as pl
from jax.experimental.pallas import tpu as pltpu

F32 = jnp.float32
BF16 = jnp.bfloat16
EPS = 1e-6
NEG_INF = float("-inf")

H_A, DK_A, DV_A, CONV_W = 4, 128, 128, 4
QKV_A = H_A * (2 * DK_A + DV_A)
H_B, D_NOPE, D_ROPE, DV_B = 8, 64, 32, 64
Q_LORA, KV_LORA = 256, 256
ROPE_BASE = 10000.0
MLA_SCALE = (D_NOPE + D_ROPE) ** -0.5
H_C, DK_C, DV_C = 8, 128, 128
CHUNK = 64
SUB = 16
PAGE = 128
QW = 256
IN_AB_PAD = 2816
V7X_VMEM_LIMIT = 48 * 1024 * 1024


def _cparams(sem):
    return pltpu.CompilerParams(dimension_semantics=sem, vmem_limit_bytes=V7X_VMEM_LIMIT)


def _const_spec(shape):
    nd = len(shape)
    return pl.BlockSpec(shape, lambda *_: (0,) * nd)


def _dot(a, b):
    return jnp.dot(a.astype(BF16), b.astype(BF16), preferred_element_type=F32)


def _dot_nt(a, b):
    return lax.dot_general(a.astype(BF16), b.astype(BF16), (((1,), (1,)), ((), ())),
                           preferred_element_type=F32)


def _dot_tn(a, b):
    return lax.dot_general(a.astype(BF16), b.astype(BF16), (((0,), (0,)), ((), ())),
                           preferred_element_type=F32)


def _rms(x, g):
    return x * lax.rsqrt(jnp.mean(x * x, axis=-1, keepdims=True) + EPS) * g


def _silu(x):
    return x * jax.nn.sigmoid(x)


def _cumsum_rows(x):
    n = x.shape[0]
    row = lax.broadcasted_iota(jnp.int32, x.shape, 0)
    s = 1
    while s < n:
        x = x + jnp.where(row >= s, pltpu.roll(x, s, axis=0), 0.0)
        s *= 2
    return x


def _proj_kernel(x_ref, g_ref, w_ref, o_ref):
    h = _rms(x_ref[...], g_ref[...]).astype(BF16)
    o_ref[...] = jnp.dot(h, w_ref[...], preferred_element_type=F32)


def _proj(x2, g, w, tm):
    n, d = x2.shape
    nout = w.shape[1]
    return pl.pallas_call(
        _proj_kernel,
        grid=(n // tm,),
        in_specs=[pl.BlockSpec((tm, d), lambda i: (i, 0)), _const_spec((1, d)), _const_spec((d, nout))],
        out_specs=pl.BlockSpec((tm, nout), lambda i: (i, 0)),
        out_shape=jax.ShapeDtypeStruct((n, nout), F32),
        compiler_params=_cparams(("parallel",)),
        name="proj",
    )(x2, g.reshape(1, d), w)


def _mix_mlp_kernel(*refs, n_in, final_norm):
    r_ref, g_ref, gf_ref = refs[:3]
    a_refs = refs[3:3 + n_in]
    w_refs = refs[3 + n_in:3 + 2 * n_in]
    wu_ref, wd_ref, o_ref = refs[3 + 2 * n_in:]
    x = r_ref[...]
    for a_ref, w_ref in zip(a_refs, w_refs):
        x = x + jnp.dot(a_ref[...], w_ref[...], preferred_element_type=F32)
    h = _rms(x, g_ref[...]).astype(BF16)
    a = jnp.square(jnp.maximum(jnp.dot(h, wu_ref[...], preferred_element_type=F32), 0.0))
    y = x + jnp.dot(a.astype(BF16), wd_ref[...], preferred_element_type=F32)
    o_ref[...] = _rms(y, gf_ref[...]) if final_norm else y


def _mix_mlp(res, acts, ws, g, w_up, w_down, gf, final_norm, tm):
    n, d = res.shape
    n_in = len(acts)
    resident = lambda w: pl.BlockSpec(w.shape, lambda i: (0,) * w.ndim, pipeline_mode=pl.Buffered(1))
    in_specs = [pl.BlockSpec((tm, d), lambda i: (i, 0)), _const_spec((1, d)), _const_spec((1, d))]
    in_specs += [pl.BlockSpec((tm, a.shape[1]), lambda i: (i, 0)) for a in acts]
    in_specs += [resident(w) for w in ws] + [resident(w_up), resident(w_down)]
    return pl.pallas_call(
        functools.partial(_mix_mlp_kernel, n_in=n_in, final_norm=final_norm),
        grid=(n // tm,),
        in_specs=in_specs,
        out_specs=pl.BlockSpec((tm, d), lambda i: (i, 0)),
        out_shape=jax.ShapeDtypeStruct((n, d), F32),
        compiler_params=_cparams(("parallel",)),
        name="mix_mlp",
    )(res, g.reshape(1, d), gf.reshape(1, d), *acts, *ws, w_up, w_down)


def _split3_dot(a, b):
    a_hi = a.astype(BF16)
    a_lo = (a - a_hi.astype(F32)).astype(BF16)
    b_hi = b.astype(BF16)
    b_lo = (b - b_hi.astype(F32)).astype(BF16)
    return jnp.dot(jnp.concatenate([a_hi, a_lo, a_hi], axis=1), jnp.concatenate([b_hi, b_hi, b_lo], axis=0),
                   preferred_element_type=F32)


def _pick_dot(a, pick):
    a_hi = a.astype(BF16)
    a_lo = (a - a_hi.astype(F32)).astype(BF16)
    pick = pick.astype(BF16)
    return jnp.dot(jnp.concatenate([a_hi, a_lo], axis=1), jnp.concatenate([pick, pick], axis=0),
                   preferred_element_type=F32)


def _solve_unit_lower(lowers, rhss, c):
    nh = len(lowers)
    s = 8
    nb = c // s
    w = nh * c
    row = lax.broadcasted_iota(jnp.int32, (c, w), 0)
    col = lax.broadcasted_iota(jnp.int32, (c, w), 1) % c
    if nb == 1:
        diag_cols = lowers
    else:
        r1 = lax.broadcasted_iota(jnp.int32, (c, c), 0)
        c1 = lax.broadcasted_iota(jnp.int32, (c, c), 1)
        pick = (lax.broadcasted_iota(jnp.int32, (c, s), 0) % s
                == lax.broadcasted_iota(jnp.int32, (c, s), 1)).astype(F32)
        diag_cols = [_pick_dot(jnp.where(r1 // s == c1 // s, lo, 0.0), pick) for lo in lowers]
    x = (row == col).astype(F32)
    for j in range(s - 1):
        mult = jnp.concatenate([jnp.broadcast_to(dc[:, j:j + 1], (c, c)) for dc in diag_cols], axis=1)
        pivot_rows = jnp.broadcast_to(x.reshape(nb, s, w)[:, j:j + 1, :], (nb, s, w)).reshape(c, w)
        x = x - mult * pivot_rows

    def block_diag(m):
        r2 = lax.broadcasted_iota(jnp.int32, (w, w), 0)
        c2 = lax.broadcasted_iota(jnp.int32, (w, w), 1)
        return jnp.where(r2 // c == c2 // c, jnp.concatenate([m] * nh, axis=0), 0.0)

    lower_cat = jnp.concatenate(lowers, axis=1)
    size = s
    while size < c:
        off = jnp.where((row // (2 * size) == col // (2 * size)) & (row // size != col // size) & (row > col),
                        lower_cat, 0.0)
        x = x - _split3_dot(_split3_dot(x, block_diag(off)), block_diag(x))
        size *= 2
    out = _split3_dot(block_diag(x), jnp.concatenate(rhss, axis=0))
    return [out[h * c:(h + 1) * c] for h in range(nh)]


def _gdn_head_pre(q, k, v, b_col, a_col, alog, dtb, c):
    q = q * lax.rsqrt(jnp.sum(q * q, axis=-1, keepdims=True) + EPS) * (DK_A ** -0.5)
    k = k * lax.rsqrt(jnp.sum(k * k, axis=-1, keepdims=True) + EPS)
    beta = jax.nn.sigmoid(b_col)
    g = -jnp.exp(alog) * jax.nn.softplus(a_col + dtb)
    gc_b = _cumsum_rows(jnp.broadcast_to(g, (c, 128)))
    gc = gc_b[:, 0:1]
    diff = gc_b[:, :c] - gc_b.T[:c, :]
    row = lax.broadcasted_iota(jnp.int32, (c, c), 0)
    col = lax.broadcasted_iota(jnp.int32, (c, c), 1)
    incl = row >= col
    decay = jnp.where(incl, jnp.exp(jnp.where(incl, diff, 0.0)), 0.0)
    kb = k * beta
    lower = jnp.where(row > col, _dot_nt(kb, k) * decay, 0.0)
    rhs = jnp.concatenate([v * beta, kb * jnp.exp(gc)], axis=1)
    intra = _dot_nt(q, k) * decay
    return q, k, gc, lower, rhs, intra


def _gdn_head_post(q, k, gc, uw, intra, z, gnorm, s_prev, c):
    u, w = uw[:, :DV_A], uw[:, DV_A:]
    v_new = u - _dot(w, s_prev)
    o = _dot(q * jnp.exp(gc), s_prev) + _dot(intra, v_new)
    g_last = gc[c - 1:c, :]
    s_new = s_prev * jnp.exp(g_last) + _dot_tn(k * jnp.exp(g_last - gc), v_new)
    o = _rms(o, gnorm) * _silu(z)
    return o, s_new


def _gdn_kernel(*refs, c, n_chunks, has_state, bg):
    (q_ref, k_ref, v_ref, z_ref, gt_ref, cwq_ref, cwk_ref, cwv_ref, alog_ref, dtb_ref, gn_ref) = refs[:11]
    if has_state:
        cq_ref, ck_ref, cv_ref, s0_ref = refs[11:15]
        o_ref, sout_ref, s_scr, tail_scr = refs[15:]
    else:
        o_ref, sout_ref, s_scr, tail_scr = refs[11:]
    n = pl.program_id(1)

    @pl.when(n == 0)
    def _():
        if has_state:
            s_scr[...] = s0_ref[...]
            tail_scr[0] = cq_ref[...]
            tail_scr[1] = ck_ref[...]
            tail_scr[2] = cv_ref[...]
        else:
            s_scr[...] = jnp.zeros_like(s_scr)
            tail_scr[...] = jnp.zeros_like(tail_scr)

    def chunk(ci):
        rows = pl.ds(ci * c, c)
        pre = []
        zs = []
        for bi in range(bg):
            conv = []
            for idx, (x_ref, cw_ref) in enumerate(((q_ref, cwq_ref), (k_ref, cwk_ref), (v_ref, cwv_ref))):
                x = x_ref[bi, rows, :]
                ext = jnp.concatenate([tail_scr[idx, bi], x], axis=0)
                cw = cw_ref[...]
                y = x * cw[CONV_W - 1:CONV_W, :]
                for sh in range(1, CONV_W):
                    y = y + pltpu.roll(ext, sh, axis=0)[8:] * cw[CONV_W - 1 - sh:CONV_W - sh, :]
                tail_scr[idx, bi] = x[c - 8:, :]
                conv.append(_silu(y))
            zs.append(z_ref[bi, rows, :])
            gt = gt_ref[bi, rows, :]
            for h in range(H_A):
                hs = slice(h * 128, (h + 1) * 128)
                pre.append(_gdn_head_pre(conv[0][:, hs], conv[1][:, hs], conv[2][:, hs],
                                         gt[:, h:h + 1], gt[:, H_A + h:H_A + h + 1],
                                         alog_ref[:, h:h + 1], dtb_ref[:, h:h + 1], c))
        per_solve = max(H_A, min(bg * H_A, 128 // c))
        uws = []
        for g0 in range(0, bg * H_A, per_solve):
            grp = pre[g0:g0 + per_solve]
            uws += _solve_unit_lower([p[3] for p in grp], [p[4] for p in grp], c)
        for bi in range(bg):
            outs = []
            for h in range(H_A):
                q, k, gc, _, _, intra = pre[bi * H_A + h]
                o, s_new = _gdn_head_post(q, k, gc, uws[bi * H_A + h], intra, zs[bi][:, h * 128:(h + 1) * 128],
                                          gn_ref[...], s_scr[bi, h], c)
                s_scr[bi, h] = s_new
                outs.append(o)
            o_ref[bi, rows, :] = jnp.concatenate(outs, axis=1).astype(o_ref.dtype)

    for ci in range(n_chunks):
        chunk(ci)

    @pl.when(n == pl.num_programs(1) - 1)
    def _():
        sout_ref[...] = s_scr[...]


def _gdn(p3, conv_w, a_log, dt_bias, gdn_norm, conv0, s0, tb, bg):
    b, t, _ = p3.shape
    c = math.gcd(t, CHUNK)
    has_state = s0 is not None
    hw = H_A * 128
    cw = conv_w
    blk = lambda j: pl.BlockSpec((bg, tb, hw), lambda i, n: (i, n, j))
    in_specs = [blk(0), blk(1), blk(2), blk(3),
                pl.BlockSpec((bg, tb, 128), lambda i, n: (i, n, (IN_AB_PAD - 128) // 128)),
                pl.BlockSpec((CONV_W, hw), lambda i, n: (0, 0)), pl.BlockSpec((CONV_W, hw), lambda i, n: (0, 1)),
                pl.BlockSpec((CONV_W, hw), lambda i, n: (0, 2)),
                _const_spec((1, H_A)), _const_spec((1, H_A)), _const_spec((1, DV_A))]
    args = [p3, p3, p3, p3, p3, cw, cw, cw, a_log.reshape(1, H_A), dt_bias.reshape(1, H_A),
            gdn_norm.reshape(1, DV_A)]
    if has_state:
        conv8 = jnp.pad(conv0, ((0, 0), (8 - (CONV_W - 1), 0), (0, 0)))
        in_specs += [pl.BlockSpec((bg, 8, hw), lambda i, n, j=j: (i, 0, j)) for j in range(3)]
        in_specs += [pl.BlockSpec((bg, H_A, DK_A, DV_A), lambda i, n: (i, 0, 0, 0))]
        args += [conv8, conv8, conv8, s0]
    return pl.pallas_call(
        functools.partial(_gdn_kernel, c=c, n_chunks=tb // c, has_state=has_state, bg=bg),
        grid=(b // bg, t // tb),
        in_specs=in_specs,
        out_specs=[pl.BlockSpec((bg, tb, hw), lambda i, n: (i, n, 0)),
                   pl.BlockSpec((bg, H_A, DK_A, DV_A), lambda i, n: (i, 0, 0, 0))],
        out_shape=[jax.ShapeDtypeStruct((b, t, hw), BF16),
                   jax.ShapeDtypeStruct((b, H_A, DK_A, DV_A), F32)],
        scratch_shapes=[pltpu.VMEM((bg, H_A, DK_A, DV_A), F32), pltpu.VMEM((3, bg, 8, hw), F32)],
        compiler_params=_cparams(("parallel", "arbitrary")),
        name="gdn",
    )(*args)


def _hgrn_chunk(qr, fr, v, lb, s_prev, c):
    sub = min(SUB, c)
    q = _silu(qr) * (DK_C ** -0.5)
    f = lb + (1.0 - lb) * jax.nn.sigmoid(fr)
    k = 1.0 - f
    gc = _cumsum_rows(jnp.log(f))
    o_inter = _dot(q * jnp.exp(gc), s_prev)
    row = lax.broadcasted_iota(jnp.int32, (sub, sub), 0)
    col = lax.broadcasted_iota(jnp.int32, (sub, sub), 1)
    outs = []
    for blk in range(c // sub):
        r0 = blk * sub
        g_i = gc[r0:r0 + sub]
        q_i = q[r0:r0 + sub]
        a_parts = [jnp.zeros((8, sub), F32) for _ in range(sub // 8)]
        for j in range(sub):
            jj = r0 + j
            lo = 8 * (j // 8)
            e = jnp.exp(g_i[lo:] - gc[jj:jj + 1, :])
            col_j = jnp.sum(q_i[lo:] * k[jj:jj + 1, :] * e, axis=-1, keepdims=True)
            for p in range(j // 8, sub // 8):
                a_parts[p] = jnp.where(col[:8] == j, col_j[8 * p - lo:8 * p - lo + 8], a_parts[p])
        a_diag = jnp.concatenate(a_parts, axis=0) if len(a_parts) > 1 else a_parts[0]
        a_diag = jnp.where(row >= col, a_diag, 0.0)
        o_i = _dot(a_diag, v[r0:r0 + sub])
        if blk > 0:
            g_ref = gc[r0 - 1:r0, :]
            a_off = _dot_nt(q_i * jnp.exp(g_i - g_ref), k[:r0] * jnp.exp(g_ref - gc[:r0]))
            o_i = o_i + _dot(a_off, v[:r0])
        outs.append(o_i)
    o = o_inter + (jnp.concatenate(outs, axis=0) if len(outs) > 1 else outs[0])
    g_last = gc[c - 1:c, :]
    gl_col = gc.T[:, c - 1:c]
    s_new = s_prev * jnp.exp(gl_col) + _dot_tn(k * jnp.exp(g_last - gc), v)
    return o, s_new


def _hgrn_kernel(*refs, c, n_chunks, layer, has_state, hg, bg):
    q_ref, f_ref, i_ref, gate_ref, lbl_ref, gn_ref = refs[:6]
    if has_state:
        s0_ref = refs[6]
        o_ref, sout_ref, s_scr = refs[7:]
    else:
        o_ref, sout_ref, s_scr = refs[6:]
    n = pl.program_id(2)

    @pl.when(n == 0)
    def _():
        if has_state:
            s_scr[...] = s0_ref[...]
        else:
            s_scr[...] = jnp.zeros_like(s_scr)

    lbl = lbl_ref[...]
    ex = jnp.exp(lbl - jnp.max(lbl, axis=0, keepdims=True))
    p = ex / jnp.sum(ex, axis=0, keepdims=True)
    lb = jnp.sum(p[:layer + 1], axis=0, keepdims=True) - p[0:1]

    def chunk(ci, carry):
        r0 = ci * c if isinstance(ci, int) else pl.multiple_of(ci * c, c)
        rows = pl.ds(r0, c)
        for bi in range(bg):
            for h in range(hg):
                hs = slice(h * 128, (h + 1) * 128)
                o, s_new = _hgrn_chunk(q_ref[bi, rows, hs], f_ref[bi, rows, hs], i_ref[bi, rows, hs], lb[:, hs],
                                       s_scr[bi, h], c)
                s_scr[bi, h] = s_new
                o_ref[bi, rows, hs] = (_rms(o, gn_ref[...]) * _silu(gate_ref[bi, rows, hs])).astype(o_ref.dtype)
        return carry

    if n_chunks == 1:
        chunk(0, 0)
    else:
        lax.fori_loop(0, n_chunks, chunk, 0)

    @pl.when(n == pl.num_programs(2) - 1)
    def _():
        sout_ref[...] = s_scr[...]


def _hgrn(pc3, lb_logits, layer, g_norm, s0, tb, hg, bg):
    b, t, _ = pc3.shape
    c = math.gcd(t, CHUNK)
    has_state = s0 is not None
    depth = lb_logits.shape[0]
    ng = H_C // hg
    blk = lambda j: pl.BlockSpec((bg, tb, hg * 128), lambda i, h, n: (i, n, j * ng + h))
    in_specs = [blk(0), blk(1), blk(2), blk(3),
                pl.BlockSpec((depth, hg * 128), lambda i, h, n: (0, h)), _const_spec((1, DV_C))]
    args = [pc3, pc3, pc3, pc3, lb_logits, g_norm.reshape(1, DV_C)]
    if has_state:
        in_specs.append(pl.BlockSpec((bg, hg, DK_C, DV_C), lambda i, h, n: (i, h, 0, 0)))
        args.append(s0)
    return pl.pallas_call(
        functools.partial(_hgrn_kernel, c=c, n_chunks=tb // c, layer=layer, has_state=has_state, hg=hg, bg=bg),
        grid=(b // bg, ng, t // tb),
        in_specs=in_specs,
        out_specs=[pl.BlockSpec((bg, tb, hg * 128), lambda i, h, n: (i, n, h)),
                   pl.BlockSpec((bg, hg, DK_C, DV_C), lambda i, h, n: (i, h, 0, 0))],
        out_shape=[jax.ShapeDtypeStruct((b, t, H_C * DV_C), BF16),
                   jax.ShapeDtypeStruct((b, H_C, DK_C, DV_C), F32)],
        scratch_shapes=[pltpu.VMEM((bg, hg, DK_C, DV_C), F32)],
        compiler_params=_cparams(("parallel", "parallel", "arbitrary")),
        name="hgrn",
    )(*args)


def _mla_prep_kernel(*refs, absorb):
    cq_ref, ckv_ref, kp_ref, qn_ref, kvn_ref, tq_ref, tk_ref, wq_ref = refs[:8]
    if absorb:
        wukt_ref, lat_ref, kr_ref, qlat_ref, qpe_ref = refs[8:]
    else:
        wkv_ref, wkvt_ref, lat_ref, kr_ref, qt_ref, kv_ref, kvt_ref, kp4_ref = refs[8:]
    lat = _rms(ckv_ref[...], kvn_ref[...])
    lat_ref[...] = lat
    kp4 = kp_ref[...] * tk_ref[...]
    kr_ref[...] = kp4[:, 0:D_ROPE] + kp4[:, D_ROPE:2 * D_ROPE]
    cqn = _rms(cq_ref[...], qn_ref[...]).astype(BF16)
    tq = tq_ref[...]
    if absorb:
        for h in range(H_B):
            qh = jnp.dot(cqn, wq_ref[:, h * QW:(h + 1) * QW], preferred_element_type=F32) * tq
            qlat_ref[:, h * KV_LORA:(h + 1) * KV_LORA] = _dot(qh[:, 0:D_NOPE], wukt_ref[h])
            qpe_ref[:, h * D_ROPE:(h + 1) * D_ROPE] = (qh[:, 128:128 + D_ROPE]
                                                      + qh[:, 128 + 2 * D_ROPE:128 + 3 * D_ROPE])
    else:
        nt_dims = (((1,), (1,)), ((), ()))
        for h in range(H_B):
            qh = lax.dot_general(wq_ref[h * QW:(h + 1) * QW, :], cqn, nt_dims, preferred_element_type=F32) * tq
            qt_ref[0, h * QW:(h + 1) * QW, :] = qh.astype(BF16)
        lat_b = lat.astype(BF16)
        kv_ref[...] = jnp.dot(lat_b, wkv_ref[...], preferred_element_type=F32).astype(BF16)
        kvt_ref[0] = lax.dot_general(wkvt_ref[...], lat_b, nt_dims, preferred_element_type=F32).astype(BF16)
        kp4_ref[...] = kp4.astype(BF16)


def _mla_prep(p2, q_norm, kv_norm, tabq, tabk, wq, w_extra, absorb, tm, seq):
    n = p2.shape[0]
    nt = seq // tm if not absorb else 1
    tabq_spec = (pl.BlockSpec((tm, QW), lambda i: (0, 0)) if absorb
                 else pl.BlockSpec((QW, tm), lambda i: (0, i % nt)))
    in_specs = [pl.BlockSpec((tm, Q_LORA), lambda i: (i, 2048 // 256)),
                pl.BlockSpec((tm, KV_LORA), lambda i: (i, 2304 // 256)),
                pl.BlockSpec((tm, 128), lambda i: (i, 2560 // 128)),
                _const_spec((1, Q_LORA)), _const_spec((1, KV_LORA)),
                tabq_spec, pl.BlockSpec((tm, 128), lambda i: (i % nt, 0)),
                _const_spec(wq.shape)] + [_const_spec(w.shape) for w in w_extra]
    row = lambda w: pl.BlockSpec((tm, w), lambda i: (i, 0))
    out_specs = [row(KV_LORA), row(D_ROPE)]
    out_shape = [jax.ShapeDtypeStruct((n, KV_LORA), F32), jax.ShapeDtypeStruct((n, D_ROPE), F32)]
    if absorb:
        out_specs += [row(H_B * KV_LORA), row(H_B * D_ROPE)]
        out_shape += [jax.ShapeDtypeStruct((n, H_B * KV_LORA), F32), jax.ShapeDtypeStruct((n, H_B * D_ROPE), F32)]
    else:
        b = n // seq
        col = lambda w: pl.BlockSpec((1, w, tm), lambda i: (i // nt, 0, i % nt))
        out_specs += [col(H_B * QW), row(H_B * 128), col(H_B * 128), row(128)]
        out_shape += [jax.ShapeDtypeStruct((b, H_B * QW, seq), BF16), jax.ShapeDtypeStruct((n, H_B * 128), BF16),
                      jax.ShapeDtypeStruct((b, H_B * 128, seq), BF16), jax.ShapeDtypeStruct((n, 128), BF16)]
    return pl.pallas_call(
        functools.partial(_mla_prep_kernel, absorb=absorb),
        grid=(n // tm,),
        in_specs=in_specs, out_specs=out_specs, out_shape=out_shape,
        compiler_params=_cparams(("parallel",)),
        name="mla_prep",
    )(p2, p2, p2, q_norm.reshape(1, Q_LORA), kv_norm.reshape(1, KV_LORA), tabq, tabk, wq, *w_extra)


def _flash_kernel(qt_ref, kv_ref, kvt_ref, kp_ref, o_ref, m_scr, l_scr, acc_scr, *, tq):
    qi = pl.program_id(1)
    key = lax.broadcasted_iota(jnp.int32, (tq, tq), 0)
    qry = lax.broadcasted_iota(jnp.int32, (tq, tq), 1)
    m_scr[...] = jnp.full_like(m_scr, NEG_INF)
    l_scr[...] = jnp.zeros_like(l_scr)
    acc_scr[...] = jnp.zeros_like(acc_scr)

    def key_tile(k0, masked):
        kp = kp_ref[0, pl.ds(k0, tq), :]
        for h in range(H_B):
            keys = jnp.concatenate([kv_ref[0, pl.ds(k0, tq), h * 128:(h + 1) * 128], kp], axis=1)
            s = jnp.dot(keys, qt_ref[0, h * QW:(h + 1) * QW, :], preferred_element_type=F32)
            if masked:
                s = jnp.where(key <= qry, s, NEG_INF)
            m = m_scr[h]
            m_new = jnp.maximum(m, jnp.max(s, axis=0, keepdims=True))
            alpha = jnp.exp(m - m_new)
            p = jnp.exp(s - m_new)
            l_scr[h] = alpha * l_scr[h] + jnp.sum(p, axis=0, keepdims=True)
            acc_scr[h] = alpha * acc_scr[h] + jnp.dot(kvt_ref[0, h * 128:(h + 1) * 128, pl.ds(k0, tq)],
                                                      p.astype(BF16), preferred_element_type=F32)
            m_scr[h] = m_new

    def body(kt, carry):
        key_tile(pl.multiple_of(kt * tq, tq), False)
        return carry

    lax.fori_loop(0, qi, body, 0)
    key_tile(pl.multiple_of(qi * tq, tq), True)
    o_t = jnp.concatenate([(acc_scr[h] / l_scr[h])[D_NOPE:D_NOPE + DV_B] for h in range(H_B)], axis=0)
    o_ref[0] = o_t.T.astype(o_ref.dtype)


def _flash(qt3, kv3, kvt3, kp3, tq):
    b, t, _ = kv3.shape
    return pl.pallas_call(
        functools.partial(_flash_kernel, tq=tq),
        grid=(b, t // tq),
        in_specs=[pl.BlockSpec((1, H_B * QW, tq), lambda i, j: (i, 0, j)),
                  pl.BlockSpec((1, t, H_B * 128), lambda i, j: (i, 0, 0)),
                  pl.BlockSpec((1, H_B * 128, t), lambda i, j: (i, 0, 0)),
                  pl.BlockSpec((1, t, 128), lambda i, j: (i, 0, 0))],
        out_specs=pl.BlockSpec((1, tq, H_B * DV_B), lambda i, j: (i, j, 0)),
        out_shape=jax.ShapeDtypeStruct((b, t, H_B * DV_B), BF16),
        scratch_shapes=[pltpu.VMEM((H_B, 1, tq), F32), pltpu.VMEM((H_B, 1, tq), F32),
                        pltpu.VMEM((H_B, 128, tq), F32)],
        compiler_params=_cparams(("parallel", "arbitrary")),
        name="mla_flash",
    )(qt3, kv3, kvt3, kp3)


def _decode_kernel(pt_ref, qlat_ref, qpe_ref, latn_ref, krn_ref, wuv_ref, lat_hbm, kct_hbm, o_ref,
                   lat_buf, kct_buf, lat_sem, kct_sem, *, gp, gpe, be, n_slots, t_new):
    step = pl.program_id(0)
    n_steps = pl.num_programs(0)
    nq = H_B * t_new
    groups = [(e, g) for e in range(be) for g in range(gpe)]
    ahead = n_slots // 2

    def page_copies(bi, g, slot):
        copies = []
        for j in range(gp):
            page = pt_ref[bi, g * gp + j]
            copies.append(pltpu.make_async_copy(lat_hbm.at[page], lat_buf.at[slot, pl.ds(j * PAGE, PAGE), :],
                                                lat_sem.at[slot]))
            copies.append(pltpu.make_async_copy(kct_hbm.at[page], kct_buf.at[slot, j], kct_sem.at[slot]))
        return copies

    def start_group(st, idx):
        e, g = groups[idx % len(groups)]
        copies = page_copies(st * be + e, g, idx % n_slots)
        for k, cp in enumerate(copies):
            cp.start(priority=(k // 2 + k) % 2)

    @pl.when(step == 0)
    def _():
        for idx in range(ahead):
            start_group(0, idx)

    def group_softmax(s):
        m_g = jnp.max(s, axis=-1, keepdims=True)
        p = jnp.exp(s - m_g)
        return p.astype(BF16), m_g, jnp.sum(p, axis=-1, keepdims=True)

    def merge(state, group):
        m_old, l_old, acc_old = state
        p, m_g, l_g, values = group
        pv = jnp.dot(p, values, preferred_element_type=F32)
        m_new = jnp.maximum(m_old, m_g)
        a_old = jnp.exp(m_old - m_new)
        a_g = jnp.exp(m_g - m_new)
        return m_new, a_old * l_old + a_g * l_g, a_old * acc_old + a_g * pv

    nt_dims = (((1,), (1,)), ((), ()))
    for idx, (e, g) in enumerate(groups):
        slot = idx % n_slots
        for cp in page_copies(step * be + e, g, slot):
            cp.wait()
        nxt = idx + ahead
        if nxt < len(groups):
            start_group(step, nxt)
        else:
            @pl.when(step + 1 < n_steps)
            def _():
                start_group(step + 1, nxt)
        if g == 0:
            ql = jnp.concatenate([qlat_ref[e, :, h * KV_LORA:(h + 1) * KV_LORA] for h in range(H_B)],
                                 axis=0).astype(BF16)
            qp = jnp.concatenate([qpe_ref[e, :, h * D_ROPE:(h + 1) * D_ROPE] for h in range(H_B)],
                                 axis=0).astype(BF16)
            state = (jnp.full((nq, 1), NEG_INF, F32), jnp.zeros((nq, 1), F32), jnp.zeros((nq, KV_LORA), F32))
            pending = None
        lat = lat_buf[slot].astype(BF16)
        kct = jnp.concatenate([kct_buf[slot, j] for j in range(gp)], axis=1).astype(BF16)
        s = lax.dot_general(ql, lat, nt_dims, preferred_element_type=F32) + jnp.dot(qp, kct, preferred_element_type=F32)
        if pending is not None:
            state = merge(state, pending)
        pending = group_softmax(s) + (lat,)
        if g == gpe - 1:
            state = merge(state, pending)
            latn = latn_ref[e].astype(BF16)
            krn = krn_ref[e].astype(BF16)
            s = (lax.dot_general(ql, latn, nt_dims, preferred_element_type=F32)
                 + lax.dot_general(qp, krn, nt_dims, preferred_element_type=F32))
            tok = lax.broadcasted_iota(jnp.int32, (nq, t_new), 0) % t_new
            key = lax.broadcasted_iota(jnp.int32, (nq, t_new), 1)
            _, l, acc = merge(state, group_softmax(jnp.where(key <= tok, s, NEG_INF)) + (latn,))
            o_lat = acc / l
            outs = [_dot(o_lat[h * t_new:(h + 1) * t_new], wuv_ref[h]) for h in range(H_B)]
            o_ref[e] = jnp.concatenate(outs, axis=1).astype(o_ref.dtype)


def _decode(page_table, qlat3, qpe3, latn3, krn3, wuv, cache_lat, cache_kpe_t, gp, be):
    b, t_new, _ = qlat3.shape
    gpe = page_table.shape[1] // gp
    n_slots = min(4, be * gpe)
    assert n_slots >= 2 and (be * gpe) % n_slots == 0 and b % be == 0
    per_step = lambda w: pl.BlockSpec((be, t_new, w), lambda i, pt: (i, 0, 0))
    in_specs = [per_step(H_B * KV_LORA), per_step(H_B * D_ROPE), per_step(KV_LORA), per_step(D_ROPE),
                pl.BlockSpec(wuv.shape, lambda i, pt: (0, 0, 0)),
                pl.BlockSpec(memory_space=pl.ANY), pl.BlockSpec(memory_space=pl.ANY)]
    return pl.pallas_call(
        functools.partial(_decode_kernel, gp=gp, gpe=gpe, be=be, n_slots=n_slots, t_new=t_new),
        grid_spec=pltpu.PrefetchScalarGridSpec(
            num_scalar_prefetch=1,
            grid=(b // be,),
            in_specs=in_specs,
            out_specs=pl.BlockSpec((be, t_new, H_B * DV_B), lambda i, pt: (i, 0, 0)),
            scratch_shapes=[pltpu.VMEM((n_slots, gp * PAGE, KV_LORA), F32),
                            pltpu.VMEM((n_slots, gp, D_ROPE, PAGE), F32),
                            pltpu.SemaphoreType.DMA((n_slots,)), pltpu.SemaphoreType.DMA((n_slots,))]),
        out_shape=jax.ShapeDtypeStruct((b, t_new, H_B * DV_B), BF16),
        compiler_params=_cparams(("arbitrary",)),
        name="mla_decode",
    )(page_table, qlat3, qpe3, latn3, krn3, wuv, cache_lat, cache_kpe_t)


def _rot_cols(w):
    half = D_ROPE // 2
    return jnp.concatenate([-w[..., half:], w[..., :half]], axis=-1)


def _layout_w_in_ab(w):
    d = w.shape[0]
    o = QKV_A + H_A * DV_A
    qkv_z, b_w, a_w = w[:, :o], w[:, o:o + H_A], w[:, o + H_A:o + 2 * H_A]
    o += 2 * H_A
    cq, ckv, kpe = w[:, o:o + Q_LORA], w[:, o + Q_LORA:o + Q_LORA + KV_LORA], w[:, o + Q_LORA + KV_LORA:]
    kr = _rot_cols(kpe)
    pad = jnp.zeros((d, 128 - 2 * H_A), w.dtype)
    return jnp.concatenate([qkv_z, cq, ckv, kpe, kr, kpe, kr, b_w, a_w, pad], axis=1).astype(BF16)


def _layout_w_uq(w_uq):
    nope, pe = w_uq[..., :D_NOPE], w_uq[..., D_NOPE:]
    rot = _rot_cols(pe)
    z = jnp.zeros_like(nope)
    return jnp.concatenate([nope, z, pe, pe, rot, rot], axis=-1).reshape(w_uq.shape[0], H_B * QW).astype(BF16)


def _rope_tables(pos):
    half = D_ROPE // 2
    inv = ROPE_BASE ** (-jnp.arange(half, dtype=F32) / half)
    ang = pos.astype(F32)[:, None] * inv
    c = jnp.concatenate([jnp.cos(ang), jnp.cos(ang)], axis=1)
    s = jnp.concatenate([jnp.sin(ang), jnp.sin(ang)], axis=1)
    one = jnp.ones((pos.shape[0], 128), F32)
    tabq = jnp.concatenate([one, c, c, s, s], axis=1) * MLA_SCALE
    tabk = jnp.concatenate([c, s, c, s], axis=1)
    return tabq, tabk


def _row_tile(n):
    for tm in (512, 256, 128, 64, 32, 16, 8):
        if n % tm == 0:
            return tm
    raise ValueError(n)


def _trunk(x, pos, conv0, gdn0, hgrn0, paged, wts):
    b, t, d = x.shape
    n = b * t
    tm = min(256, _row_tile(n))
    tb = min(256, t)
    x2 = x.reshape(n, d)

    tmd = min(512, _row_tile(n))
    p2 = _proj(x2, wts["mix_norm"][0], wts["w_in_ab"], tmd)
    p3 = p2.reshape(b, t, IN_AB_PAD)
    conv_new = p3[:, t - (CONV_W - 1):, :QKV_A]
    o_a, gdn_new = _gdn(p3, wts["conv_w"], wts["a_log"], wts["dt_bias"], wts["gdn_norm"], conv0, gdn0, tb,
                        math.gcd(b, 2 if gdn0 is None else 4))
    tabq, tabk = _rope_tables(pos)
    if paged is None:
        tmp = math.gcd(tm, t)
        lat, kr, qt, kv, kvt, kp4 = _mla_prep(p2, wts["q_norm"], wts["kv_norm"], tabq.T, tabk, wts["w_uq"].T,
                                              [wts["w_kv"], wts["w_kv"].T], False, tmp, t)
        o_b = _flash(qt, kv.reshape(b, t, -1), kvt, kp4.reshape(b, t, -1), min(512, t))
    else:
        cache_lat, cache_kpe, page_table = paged
        reps = tm // t
        lat, kr, qlat, qpe = _mla_prep(p2, wts["q_norm"], wts["kv_norm"], jnp.tile(tabq, (reps, 1)),
                                       jnp.tile(tabk, (reps, 1)), wts["w_uq"], [wts["w_ukt"]], True, tm, t)
        n_pages = page_table.shape[1]
        gp = math.gcd(n_pages // 2, 32)
        o_b = _decode(page_table, qlat.reshape(b, t, -1), qpe.reshape(b, t, -1), lat.reshape(b, t, -1),
                      kr.reshape(b, t, -1), wts["w_uv"], cache_lat, jnp.swapaxes(cache_kpe, 1, 2), gp, math.gcd(b, 2))
    tmm = min(256, _row_tile(n))
    x2 = _mix_mlp(x2, [o_a.reshape(n, -1), o_b.reshape(n, -1)], [wts["w_out_a"], wts["w_out_b"]],
                  wts["mlp_norm"][0], wts["w_up"][0], wts["w_down"][0], wts["final_norm"], False, tmm)

    pc = _proj(x2, wts["mix_norm"][1], wts["w_in_c"], tmd)
    o_c, hgrn_new = _hgrn(pc.reshape(b, t, -1), wts["lb_logits"], 1, wts["g_norm_c"], hgrn0, tb,
                          H_C, 1)
    y2 = _mix_mlp(x2, [o_c.reshape(n, -1)], [wts["w_out_c"]],
                  wts["mlp_norm"][1], wts["w_up"][1], wts["w_down"][1], wts["final_norm"], True, tmm)
    return (y2.reshape(b, t, d), gdn_new[None], conv_new[None], lat.reshape(b, t, -1)[None],
            kr.reshape(b, t, -1)[None], hgrn_new[None])


def kernel(x_prompt, x_sample, state_gdn, state_gdn_conv, cache_mla_latent, cache_mla_krope, state_hgrn,
           page_table, mix_norm, mlp_norm, final_norm, w_up, w_down, w_in_ab, conv_w_ab, a_log_ab, dt_bias_ab,
           gdn_norm_ab, q_norm_ab, w_uq_ab, kv_norm_ab, w_uk_ab, w_uv_ab, w_out_ab, w_in_c, lb_logits_c,
           g_norm_c, w_out_c):
    assert mix_norm.shape[0] == 2 and w_in_ab.shape[0] == 1 and w_in_c.shape[0] == 1
    assert a_log_ab.shape == (1, H_A) and conv_w_ab.shape == (1, CONV_W, QKV_A)
    assert w_uq_ab.shape[1:] == (Q_LORA, H_B, D_NOPE + D_ROPE) and w_uk_ab.shape[1:] == (KV_LORA, H_B, D_NOPE)
    assert cache_mla_latent.shape[2] == PAGE and x_prompt.shape[1] >= CONV_W - 1 and x_sample.shape[1] >= CONV_W - 1
    w_uk, w_uv = w_uk_ab[0], w_uv_ab[0]
    wts = {
        "mix_norm": mix_norm, "mlp_norm": mlp_norm, "final_norm": final_norm,
        "w_up": w_up.astype(BF16), "w_down": w_down.astype(BF16),
        "w_in_ab": _layout_w_in_ab(w_in_ab[0]), "conv_w": conv_w_ab[0], "a_log": a_log_ab[0],
        "dt_bias": dt_bias_ab[0], "gdn_norm": gdn_norm_ab[0], "q_norm": q_norm_ab[0], "kv_norm": kv_norm_ab[0],
        "w_uq": _layout_w_uq(w_uq_ab[0]),
        "w_kv": jnp.concatenate([w_uk, w_uv], axis=-1).reshape(KV_LORA, H_B * 128).astype(BF16),
        "w_ukt": jnp.transpose(w_uk, (1, 2, 0)).astype(BF16),
        "w_uv": jnp.transpose(w_uv, (1, 0, 2)).astype(BF16),
        "w_out_a": w_out_ab[0, :H_A * DV_A].astype(BF16), "w_out_b": w_out_ab[0, H_A * DV_A:].astype(BF16),
        "w_in_c": w_in_c[0].astype(BF16), "lb_logits": lb_logits_c, "g_norm_c": g_norm_c[0],
        "w_out_c": w_out_c[0].astype(BF16),
    }
    s_len = x_prompt.shape[1]
    outs_p = _trunk(x_prompt, jnp.arange(s_len), None, None, None, None, wts)
    past_len = page_table.shape[1] * cache_mla_latent.shape[2]
    outs_s = _trunk(x_sample, past_len + jnp.arange(x_sample.shape[1]),
                    state_gdn_conv.reshape(state_gdn_conv.shape[1:]), state_gdn.reshape(state_gdn.shape[1:]),
                    state_hgrn.reshape(state_hgrn.shape[1:]),
                    (cache_mla_latent.reshape(cache_mla_latent.shape[1:]),
                     cache_mla_krope.reshape(cache_mla_krope.shape[1:]), page_table), wts)
    return (outs_p[0], outs_s[0]) + outs_p[1:] + outs_s[1:]
```

```python
import functools
import math

import jax
import jax.numpy as jnp
from jax import lax
from jax.experimental import pallas as pl
from jax.experimental.pallas import tpu as pltpu

F32 = jnp.float32
BF16 = jnp.bfloat16
EPS = 1e-6
NEG_INF = float("-inf")
LOG2_E = 1.0 / math.log(2.0)

H_A, DK_A, DV_A, CONV_W = 4, 128, 128, 4
QKV_A = H_A * (2 * DK_A + DV_A)
H_B, D_NOPE, D_ROPE, DV_B = 8, 64, 32, 64
Q_LORA, KV_LORA = 256, 256
ROPE_BASE = 10000.0
MLA_SCALE = (D_NOPE + D_ROPE) ** -0.5
H_C, DK_C, DV_C = 8, 128, 128
CHUNK = 64
SUB = 16
PAGE = 128
QW = 256
IN_AB_PAD = 2816
V7X_VMEM_LIMIT = 48 * 1024 * 1024


def _cparams(sem):
    return pltpu.CompilerParams(dimension_semantics=sem, vmem_limit_bytes=V7X_VMEM_LIMIT)


def _const_spec(shape):
    nd = len(shape)
    return pl.BlockSpec(shape, lambda *_: (0,) * nd)


def _dot(a, b):
    return jnp.dot(a.astype(BF16), b.astype(BF16), preferred_element_type=F32)


def _dot_nt(a, b):
    return lax.dot_general(a.astype(BF16), b.astype(BF16), (((1,), (1,)), ((), ())),
                           preferred_element_type=F32)


def _dot_tn(a, b):
    return lax.dot_general(a.astype(BF16), b.astype(BF16), (((0,), (0,)), ((), ())),
                           preferred_element_type=F32)


def _rms(x, g):
    return x * lax.rsqrt(jnp.mean(x * x, axis=-1, keepdims=True) + EPS) * g


def _silu(x):
    return x * jax.nn.sigmoid(x)


def _cumsum_rows(x):
    n = x.shape[0]
    row = lax.broadcasted_iota(jnp.int32, x.shape, 0)
    s = 1
    while s < n:
        x = x + jnp.where(row >= s, pltpu.roll(x, s, axis=0), 0.0)
        s *= 2
    return x


def _proj_kernel(x_ref, g_ref, w_ref, o_ref):
    h = _rms(x_ref[...], g_ref[...]).astype(BF16)
    o_ref[...] = jnp.dot(h, w_ref[...], preferred_element_type=F32)


def _proj(x2, g, w, tm):
    n, d = x2.shape
    nout = w.shape[1]
    return pl.pallas_call(
        _proj_kernel,
        grid=(n // tm,),
        in_specs=[pl.BlockSpec((tm, d), lambda i: (i, 0)), _const_spec((1, d)), _const_spec((d, nout))],
        out_specs=pl.BlockSpec((tm, nout), lambda i: (i, 0)),
        out_shape=jax.ShapeDtypeStruct((n, nout), F32),
        compiler_params=_cparams(("parallel",)),
        name="proj",
    )(x2, g.reshape(1, d), w)


def _mix_mlp_kernel(*refs, n_in, final_norm):
    r_ref, g_ref, gf_ref = refs[:3]
    a_refs = refs[3:3 + n_in]
    w_refs = refs[3 + n_in:3 + 2 * n_in]
    wu_ref, wd_ref, o_ref = refs[3 + 2 * n_in:]
    x = r_ref[...]
    for a_ref, w_ref in zip(a_refs, w_refs):
        x = x + jnp.dot(a_ref[...], w_ref[...], preferred_element_type=F32)
    h = _rms(x, g_ref[...]).astype(BF16)
    a = jnp.square(jnp.maximum(jnp.dot(h, wu_ref[...], preferred_element_type=F32), 0.0))
    y = x + jnp.dot(a.astype(BF16), wd_ref[...], preferred_element_type=F32)
    o_ref[...] = _rms(y, gf_ref[...]) if final_norm else y


def _mix_mlp(res, acts, ws, g, w_up, w_down, gf, final_norm, tm):
    n, d = res.shape
    n_in = len(acts)
    resident = lambda w: pl.BlockSpec(w.shape, lambda i: (0,) * w.ndim, pipeline_mode=pl.Buffered(1))
    in_specs = [pl.BlockSpec((tm, d), lambda i: (i, 0)), _const_spec((1, d)), _const_spec((1, d))]
    in_specs += [pl.BlockSpec((tm, a.shape[1]), lambda i: (i, 0)) for a in acts]
    in_specs += [resident(w) for w in ws] + [resident(w_up), resident(w_down)]
    return pl.pallas_call(
        functools.partial(_mix_mlp_kernel, n_in=n_in, final_norm=final_norm),
        grid=(n // tm,),
        in_specs=in_specs,
        out_specs=pl.BlockSpec((tm, d), lambda i: (i, 0)),
        out_shape=jax.ShapeDtypeStruct((n, d), F32),
        compiler_params=_cparams(("parallel",)),
        name="mix_mlp",
    )(res, g.reshape(1, d), gf.reshape(1, d), *acts, *ws, w_up, w_down)


def _split3_dot(a, b):
    a_hi = a.astype(BF16)
    a_lo = (a - a_hi.astype(F32)).astype(BF16)
    b_hi = b.astype(BF16)
    b_lo = (b - b_hi.astype(F32)).astype(BF16)
    return jnp.dot(jnp.concatenate([a_hi, a_lo, a_hi], axis=1), jnp.concatenate([b_hi, b_hi, b_lo], axis=0),
                   preferred_element_type=F32)


def _pick_dot(a, pick):
    a_hi = a.astype(BF16)
    a_lo = (a - a_hi.astype(F32)).astype(BF16)
    pick = pick.astype(BF16)
    return jnp.dot(jnp.concatenate([a_hi, a_lo], axis=1), jnp.concatenate([pick, pick], axis=0),
                   preferred_element_type=F32)


def _solve_unit_lower(lowers, rhss, c):
    nh = len(lowers)
    s = 8
    nb = c // s
    w = nh * c
    row = lax.broadcasted_iota(jnp.int32, (c, w), 0)
    col = lax.broadcasted_iota(jnp.int32, (c, w), 1) % c
    if nb == 1:
        diag_cols = lowers
    else:
        r1 = lax.broadcasted_iota(jnp.int32, (c, c), 0)
        c1 = lax.broadcasted_iota(jnp.int32, (c, c), 1)
        pick = (lax.broadcasted_iota(jnp.int32, (c, s), 0) % s
                == lax.broadcasted_iota(jnp.int32, (c, s), 1)).astype(F32)
        diag_cols = [_pick_dot(jnp.where(r1 // s == c1 // s, lo, 0.0), pick) for lo in lowers]
    x = (row == col).astype(F32)
    for j in range(s - 1):
        mult = jnp.concatenate([jnp.broadcast_to(dc[:, j:j + 1], (c, c)) for dc in diag_cols], axis=1)
        pivot_rows = jnp.broadcast_to(x.reshape(nb, s, w)[:, j:j + 1, :], (nb, s, w)).reshape(c, w)
        x = x - mult * pivot_rows

    def block_diag(m):
        r2 = lax.broadcasted_iota(jnp.int32, (w, w), 0)
        c2 = lax.broadcasted_iota(jnp.int32, (w, w), 1)
        return jnp.where(r2 // c == c2 // c, jnp.concatenate([m] * nh, axis=0), 0.0)

    lower_cat = jnp.concatenate(lowers, axis=1)
    size = s
    while size < c:
        off = jnp.where((row // (2 * size) == col // (2 * size)) & (row // size != col // size) & (row > col),
                        lower_cat, 0.0)
        x = x - _split3_dot(_split3_dot(x, block_diag(off)), block_diag(x))
        size *= 2
    out = _split3_dot(block_diag(x), jnp.concatenate(rhss, axis=0))
    return [out[h * c:(h + 1) * c] for h in range(nh)]


def _gdn_head_pre(q, k, v, b_col, a_col, alog, dtb, c):
    q = q * lax.rsqrt(jnp.sum(q * q, axis=-1, keepdims=True) + EPS) * (DK_A ** -0.5)
    k = k * lax.rsqrt(jnp.sum(k * k, axis=-1, keepdims=True) + EPS)
    beta = jax.nn.sigmoid(b_col)
    g = -jnp.exp(alog) * jax.nn.softplus(a_col + dtb)
    gc_b = _cumsum_rows(jnp.broadcast_to(g, (c, 128)))
    gc = gc_b[:, 0:1]
    diff = gc_b[:, :c] - gc_b.T[:c, :]
    row = lax.broadcasted_iota(jnp.int32, (c, c), 0)
    col = lax.broadcasted_iota(jnp.int32, (c, c), 1)
    incl = row >= col
    decay = jnp.where(incl, jnp.exp(jnp.where(incl, diff, 0.0)), 0.0)
    kb = k * beta
    lower = jnp.where(row > col, _dot_nt(kb, k) * decay, 0.0)
    rhs = jnp.concatenate([v * beta, kb * jnp.exp(gc)], axis=1)
    intra = _dot_nt(q, k) * decay
    return q, k, gc, lower, rhs, intra


def _gdn_head_post(q, k, gc, uw, intra, z, gnorm, s_prev, c):
    u, w = uw[:, :DV_A], uw[:, DV_A:]
    v_new = u - _dot(w, s_prev)
    o = _dot(q * jnp.exp(gc), s_prev) + _dot(intra, v_new)
    g_last = gc[c - 1:c, :]
    s_new = s_prev * jnp.exp(g_last) + _dot_tn(k * jnp.exp(g_last - gc), v_new)
    o = _rms(o, gnorm) * _silu(z)
    return o, s_new


def _gdn_kernel(*refs, c, n_chunks, has_state, bg):
    (q_ref, k_ref, v_ref, z_ref, gt_ref, cwq_ref, cwk_ref, cwv_ref, alog_ref, dtb_ref, gn_ref) = refs[:11]
    if has_state:
        cq_ref, ck_ref, cv_ref, s0_ref = refs[11:15]
        o_ref, sout_ref, s_scr, tail_scr = refs[15:]
    else:
        o_ref, sout_ref, s_scr, tail_scr = refs[11:]
    n = pl.program_id(1)

    @pl.when(n == 0)
    def _():
        if has_state:
            s_scr[...] = s0_ref[...]
            tail_scr[0] = cq_ref[...]
            tail_scr[1] = ck_ref[...]
            tail_scr[2] = cv_ref[...]
        else:
            s_scr[...] = jnp.zeros_like(s_scr)
            tail_scr[...] = jnp.zeros_like(tail_scr)

    def chunk(ci):
        rows = pl.ds(ci * c, c)
        pre = []
        zs = []
        for bi in range(bg):
            conv = []
            for idx, (x_ref, cw_ref) in enumerate(((q_ref, cwq_ref), (k_ref, cwk_ref), (v_ref, cwv_ref))):
                x = x_ref[bi, rows, :]
                ext = jnp.concatenate([tail_scr[idx, bi], x], axis=0)
                cw = cw_ref[...]
                y = x * cw[CONV_W - 1:CONV_W, :]
                for sh in range(1, CONV_W):
                    y = y + pltpu.roll(ext, sh, axis=0)[8:] * cw[CONV_W - 1 - sh:CONV_W - sh, :]
                tail_scr[idx, bi] = x[c - 8:, :]
                conv.append(_silu(y))
            zs.append(z_ref[bi, rows, :])
            gt = gt_ref[bi, rows, :]
            for h in range(H_A):
                hs = slice(h * 128, (h + 1) * 128)
                pre.append(_gdn_head_pre(conv[0][:, hs], conv[1][:, hs], conv[2][:, hs],
                                         gt[:, h:h + 1], gt[:, H_A + h:H_A + h + 1],
                                         alog_ref[:, h:h + 1], dtb_ref[:, h:h + 1], c))
        per_solve = max(H_A, min(bg * H_A, 128 // c))
        uws = []
        for g0 in range(0, bg * H_A, per_solve):
            grp = pre[g0:g0 + per_solve]
            uws += _solve_unit_lower([p[3] for p in grp], [p[4] for p in grp], c)
        for bi in range(bg):
            outs = []
            for h in range(H_A):
                q, k, gc, _, _, intra = pre[bi * H_A + h]
                o, s_new = _gdn_head_post(q, k, gc, uws[bi * H_A + h], intra, zs[bi][:, h * 128:(h + 1) * 128],
                                          gn_ref[...], s_scr[bi, h], c)
                s_scr[bi, h] = s_new
                outs.append(o)
            o_ref[bi, rows, :] = jnp.concatenate(outs, axis=1).astype(o_ref.dtype)

    for ci in range(n_chunks):
        chunk(ci)

    @pl.when(n == pl.num_programs(1) - 1)
    def _():
        sout_ref[...] = s_scr[...]


def _gdn(p3, conv_w, a_log, dt_bias, gdn_norm, conv0, s0, tb, bg):
    b, t, _ = p3.shape
    c = math.gcd(t, CHUNK)
    has_state = s0 is not None
    hw = H_A * 128
    cw = conv_w
    blk = lambda j: pl.BlockSpec((bg, tb, hw), lambda i, n: (i, n, j))
    in_specs = [blk(0), blk(1), blk(2), blk(3),
                pl.BlockSpec((bg, tb, 128), lambda i, n: (i, n, (IN_AB_PAD - 128) // 128)),
                pl.BlockSpec((CONV_W, hw), lambda i, n: (0, 0)), pl.BlockSpec((CONV_W, hw), lambda i, n: (0, 1)),
                pl.BlockSpec((CONV_W, hw), lambda i, n: (0, 2)),
                _const_spec((1, H_A)), _const_spec((1, H_A)), _const_spec((1, DV_A))]
    args = [p3, p3, p3, p3, p3, cw, cw, cw, a_log.reshape(1, H_A), dt_bias.reshape(1, H_A),
            gdn_norm.reshape(1, DV_A)]
    if has_state:
        conv8 = jnp.pad(conv0, ((0, 0), (8 - (CONV_W - 1), 0), (0, 0)))
        in_specs += [pl.BlockSpec((bg, 8, hw), lambda i, n, j=j: (i, 0, j)) for j in range(3)]
        in_specs += [pl.BlockSpec((bg, H_A, DK_A, DV_A), lambda i, n: (i, 0, 0, 0))]
        args += [conv8, conv8, conv8, s0]
    return pl.pallas_call(
        functools.partial(_gdn_kernel, c=c, n_chunks=tb // c, has_state=has_state, bg=bg),
        grid=(b // bg, t // tb),
        in_specs=in_specs,
        out_specs=[pl.BlockSpec((bg, tb, hw), lambda i, n: (i, n, 0)),
                   pl.BlockSpec((bg, H_A, DK_A, DV_A), lambda i, n: (i, 0, 0, 0))],
        out_shape=[jax.ShapeDtypeStruct((b, t, hw), BF16),
                   jax.ShapeDtypeStruct((b, H_A, DK_A, DV_A), F32)],
        scratch_shapes=[pltpu.VMEM((bg, H_A, DK_A, DV_A), F32), pltpu.VMEM((3, bg, 8, hw), F32)],
        compiler_params=_cparams(("parallel", "arbitrary")),
        name="gdn",
    )(*args)


def _hgrn_chunk(qr, fr, v, lb, s_prev, c):
    sub = min(SUB, c)
    q = _silu(qr) * (DK_C ** -0.5)
    f = lb + (1.0 - lb) * jax.nn.sigmoid(fr)
    k = 1.0 - f
    gc = _cumsum_rows(jnp.log(f))
    o_inter = _dot(q * jnp.exp(gc), s_prev)
    gc2 = gc * LOG2_E
    kg2 = gc2 - jnp.log2(k)
    row = lax.broadcasted_iota(jnp.int32, (sub, sub), 0)
    col = lax.broadcasted_iota(jnp.int32, (sub, sub), 1)
    outs = []
    for blk in range(c // sub):
        r0 = blk * sub
        g_i = gc[r0:r0 + sub]
        g2_i = gc2[r0:r0 + sub]
        q_i = q[r0:r0 + sub]
        a_parts = [jnp.zeros((8, sub), F32) for _ in range(sub // 8)]
        for j in range(sub):
            jj = r0 + j
            lo = 8 * (j // 8)
            ke = jnp.exp2(g2_i[lo:] - kg2[jj:jj + 1, :])
            col_j = jnp.sum(q_i[lo:] * ke, axis=-1, keepdims=True)
            for p in range(j // 8, sub // 8):
                a_parts[p] = jnp.where(col[:8] == j, col_j[8 * p - lo:8 * p - lo + 8], a_parts[p])
        a_diag = jnp.concatenate(a_parts, axis=0) if len(a_parts) > 1 else a_parts[0]
        a_diag = jnp.where(row >= col, a_diag, 0.0)
        o_i = _dot(a_diag, v[r0:r0 + sub])
        if blk > 0:
            g_ref = gc[r0 - 1:r0, :]
            a_off = _dot_nt(q_i * jnp.exp(g_i - g_ref), k[:r0] * jnp.exp(g_ref - gc[:r0]))
            o_i = o_i + _dot(a_off, v[:r0])
        outs.append(o_i)
    o = o_inter + (jnp.concatenate(outs, axis=0) if len(outs) > 1 else outs[0])
    g_last = gc[c - 1:c, :]
    gl_col = gc.T[:, c - 1:c]
    s_new = s_prev * jnp.exp(gl_col) + _dot_tn(k * jnp.exp(g_last - gc), v)
    return o, s_new


def _hgrn_kernel(*refs, c, n_chunks, layer, has_state, hg, bg):
    q_ref, f_ref, i_ref, gate_ref, lbl_ref, gn_ref = refs[:6]
    if has_state:
        s0_ref = refs[6]
        o_ref, sout_ref, s_scr = refs[7:]
    else:
        o_ref, sout_ref, s_scr = refs[6:]
    n = pl.program_id(2)

    @pl.when(n == 0)
    def _():
        if has_state:
            s_scr[...] = s0_ref[...]
        else:
            s_scr[...] = jnp.zeros_like(s_scr)

    lbl = lbl_ref[...]
    ex = jnp.exp(lbl - jnp.max(lbl, axis=0, keepdims=True))
    p = ex / jnp.sum(ex, axis=0, keepdims=True)
    lb = jnp.sum(p[:layer + 1], axis=0, keepdims=True) - p[0:1]

    def chunk(ci, carry):
        r0 = ci * c if isinstance(ci, int) else pl.multiple_of(ci * c, c)
        rows = pl.ds(r0, c)
        for bi in range(bg):
            for h in range(hg):
                hs = slice(h * 128, (h + 1) * 128)
                o, s_new = _hgrn_chunk(q_ref[bi, rows, hs], f_ref[bi, rows, hs], i_ref[bi, rows, hs], lb[:, hs],
                                       s_scr[bi, h], c)
                s_scr[bi, h] = s_new
                o_ref[bi, rows, hs] = (_rms(o, gn_ref[...]) * _silu(gate_ref[bi, rows, hs])).astype(o_ref.dtype)
        return carry

    if n_chunks == 1:
        chunk(0, 0)
    else:
        lax.fori_loop(0, n_chunks, chunk, 0)

    @pl.when(n == pl.num_programs(2) - 1)
    def _():
        sout_ref[...] = s_scr[...]


def _hgrn(pc3, lb_logits, layer, g_norm, s0, tb, hg, bg):
    b, t, _ = pc3.shape
    c = math.gcd(t, CHUNK)
    has_state = s0 is not None
    depth = lb_logits.shape[0]
    ng = H_C // hg
    blk = lambda j: pl.BlockSpec((bg, tb, hg * 128), lambda i, h, n: (i, n, j * ng + h))
    in_specs = [blk(0), blk(1), blk(2), blk(3),
                pl.BlockSpec((depth, hg * 128), lambda i, h, n: (0, h)), _const_spec((1, DV_C))]
    args = [pc3, pc3, pc3, pc3, lb_logits, g_norm.reshape(1, DV_C)]
    if has_state:
        in_specs.append(pl.BlockSpec((bg, hg, DK_C, DV_C), lambda i, h, n: (i, h, 0, 0)))
        args.append(s0)
    return pl.pallas_call(
        functools.partial(_hgrn_kernel, c=c, n_chunks=tb // c, layer=layer, has_state=has_state, hg=hg, bg=bg),
        grid=(b // bg, ng, t // tb),
        in_specs=in_specs,
        out_specs=[pl.BlockSpec((bg, tb, hg * 128), lambda i, h, n: (i, n, h)),
                   pl.BlockSpec((bg, hg, DK_C, DV_C), lambda i, h, n: (i, h, 0, 0))],
        out_shape=[jax.ShapeDtypeStruct((b, t, H_C * DV_C), BF16),
                   jax.ShapeDtypeStruct((b, H_C, DK_C, DV_C), F32)],
        scratch_shapes=[pltpu.VMEM((bg, hg, DK_C, DV_C), F32)],
        compiler_params=_cparams(("parallel", "parallel", "arbitrary")),
        name="hgrn",
    )(*args)


def _mla_prep_kernel(*refs, absorb):
    cq_ref, ckv_ref, kp_ref, qn_ref, kvn_ref, tq_ref, tk_ref, wq_ref = refs[:8]
    if absorb:
        wukt_ref, lat_ref, kr_ref, qlat_ref, qpe_ref = refs[8:]
    else:
        wkv_ref, wkvt_ref, lat_ref, kr_ref, qt_ref, kv_ref, kvt_ref, kp4_ref = refs[8:]
    lat = _rms(ckv_ref[...], kvn_ref[...])
    lat_ref[...] = lat
    kp4 = kp_ref[...] * tk_ref[...]
    kr_ref[...] = kp4[:, 0:D_ROPE] + kp4[:, D_ROPE:2 * D_ROPE]
    cqn = _rms(cq_ref[...], qn_ref[...]).astype(BF16)
    tq = tq_ref[...]
    if absorb:
        for h in range(H_B):
            qh = jnp.dot(cqn, wq_ref[:, h * QW:(h + 1) * QW], preferred_element_type=F32) * tq
            qlat_ref[:, h * KV_LORA:(h + 1) * KV_LORA] = _dot(qh[:, 0:D_NOPE], wukt_ref[h])
            qpe_ref[:, h * D_ROPE:(h + 1) * D_ROPE] = (qh[:, 128:128 + D_ROPE]
                                                      + qh[:, 128 + 2 * D_ROPE:128 + 3 * D_ROPE])
    else:
        nt_dims = (((1,), (1,)), ((), ()))
        for h in range(H_B):
            qh = lax.dot_general(wq_ref[h * QW:(h + 1) * QW, :], cqn, nt_dims, preferred_element_type=F32) * tq
            qt_ref[0, h * QW:(h + 1) * QW, :] = qh.astype(BF16)
        lat_b = lat.astype(BF16)
        kv_ref[...] = jnp.dot(lat_b, wkv_ref[...], preferred_element_type=F32).astype(BF16)
        kvt_ref[0] = lax.dot_general(wkvt_ref[...], lat_b, nt_dims, preferred_element_type=F32).astype(BF16)
        kp4_ref[...] = kp4.astype(BF16)


def _mla_prep(p2, q_norm, kv_norm, tabq, tabk, wq, w_extra, absorb, tm, seq):
    n = p2.shape[0]
    nt = seq // tm if not absorb else 1
    tabq_spec = (pl.BlockSpec((tm, QW), lambda i: (0, 0)) if absorb
                 else pl.BlockSpec((QW, tm), lambda i: (0, i % nt)))
    in_specs = [pl.BlockSpec((tm, Q_LORA), lambda i: (i, 2048 // 256)),
                pl.BlockSpec((tm, KV_LORA), lambda i: (i, 2304 // 256)),
                pl.BlockSpec((tm, 128), lambda i: (i, 2560 // 128)),
                _const_spec((1, Q_LORA)), _const_spec((1, KV_LORA)),
                tabq_spec, pl.BlockSpec((tm, 128), lambda i: (i % nt, 0)),
                _const_spec(wq.shape)] + [_const_spec(w.shape) for w in w_extra]
    row = lambda w: pl.BlockSpec((tm, w), lambda i: (i, 0))
    out_specs = [row(KV_LORA), row(D_ROPE)]
    out_shape = [jax.ShapeDtypeStruct((n, KV_LORA), F32), jax.ShapeDtypeStruct((n, D_ROPE), F32)]
    if absorb:
        out_specs += [row(H_B * KV_LORA), row(H_B * D_ROPE)]
        out_shape += [jax.ShapeDtypeStruct((n, H_B * KV_LORA), F32), jax.ShapeDtypeStruct((n, H_B * D_ROPE), F32)]
    else:
        b = n // seq
        col = lambda w: pl.BlockSpec((1, w, tm), lambda i: (i // nt, 0, i % nt))
        out_specs += [col(H_B * QW), row(H_B * 128), col(H_B * 128), row(128)]
        out_shape += [jax.ShapeDtypeStruct((b, H_B * QW, seq), BF16), jax.ShapeDtypeStruct((n, H_B * 128), BF16),
                      jax.ShapeDtypeStruct((b, H_B * 128, seq), BF16), jax.ShapeDtypeStruct((n, 128), BF16)]
    return pl.pallas_call(
        functools.partial(_mla_prep_kernel, absorb=absorb),
        grid=(n // tm,),
        in_specs=in_specs, out_specs=out_specs, out_shape=out_shape,
        compiler_params=_cparams(("parallel",)),
        name="mla_prep",
    )(p2, p2, p2, q_norm.reshape(1, Q_LORA), kv_norm.reshape(1, KV_LORA), tabq, tabk, wq, *w_extra)


def _flash_kernel(qt_ref, kv_ref, kvt_ref, kp_ref, o_ref, m_scr, l_scr, acc_scr, *, tq):
    qi = pl.program_id(1)
    key = lax.broadcasted_iota(jnp.int32, (tq, tq), 0)
    qry = lax.broadcasted_iota(jnp.int32, (tq, tq), 1)
    m_scr[...] = jnp.full_like(m_scr, NEG_INF)
    l_scr[...] = jnp.zeros_like(l_scr)
    acc_scr[...] = jnp.zeros_like(acc_scr)

    def key_tile(k0, masked):
        kp = kp_ref[0, pl.ds(k0, tq), :]
        for h in range(H_B):
            keys = jnp.concatenate([kv_ref[0, pl.ds(k0, tq), h * 128:(h + 1) * 128], kp], axis=1)
            s = jnp.dot(keys, qt_ref[0, h * QW:(h + 1) * QW, :], preferred_element_type=F32)
            if masked:
                s = jnp.where(key <= qry, s, NEG_INF)
            m = m_scr[h]
            m_new = jnp.maximum(m, jnp.max(s, axis=0, keepdims=True))
            alpha = jnp.exp(m - m_new)
            p = jnp.exp(s - m_new)
            l_scr[h] = alpha * l_scr[h] + jnp.sum(p, axis=0, keepdims=True)
            acc_scr[h] = alpha * acc_scr[h] + jnp.dot(kvt_ref[0, h * 128:(h + 1) * 128, pl.ds(k0, tq)],
                                                      p.astype(BF16), preferred_element_type=F32)
            m_scr[h] = m_new

    def body(kt, carry):
        key_tile(pl.multiple_of(kt * tq, tq), False)
        return carry

    lax.fori_loop(0, qi, body, 0)
    key_tile(pl.multiple_of(qi * tq, tq), True)
    o_t = jnp.concatenate([(acc_scr[h] / l_scr[h])[D_NOPE:D_NOPE + DV_B] for h in range(H_B)], axis=0)
    o_ref[0] = o_t.T.astype(o_ref.dtype)


def _flash(qt3, kv3, kvt3, kp3, tq):
    b, t, _ = kv3.shape
    return pl.pallas_call(
        functools.partial(_flash_kernel, tq=tq),
        grid=(b, t // tq),
        in_specs=[pl.BlockSpec((1, H_B * QW, tq), lambda i, j: (i, 0, j)),
                  pl.BlockSpec((1, t, H_B * 128), lambda i, j: (i, 0, 0)),
                  pl.BlockSpec((1, H_B * 128, t), lambda i, j: (i, 0, 0)),
                  pl.BlockSpec((1, t, 128), lambda i, j: (i, 0, 0))],
        out_specs=pl.BlockSpec((1, tq, H_B * DV_B), lambda i, j: (i, j, 0)),
        out_shape=jax.ShapeDtypeStruct((b, t, H_B * DV_B), BF16),
        scratch_shapes=[pltpu.VMEM((H_B, 1, tq), F32), pltpu.VMEM((H_B, 1, tq), F32),
                        pltpu.VMEM((H_B, 128, tq), F32)],
        compiler_params=_cparams(("parallel", "arbitrary")),
        name="mla_flash",
    )(qt3, kv3, kvt3, kp3)


def _decode_kernel(pt_ref, qlat_ref, qpe_ref, latn_ref, krn_ref, wuv_ref, lat_hbm, kct_hbm, o_ref,
                   lat_buf, kct_buf, lat_sem, kct_sem, *, gp, gpe, be, n_slots, t_new):
    step = pl.program_id(0)
    n_steps = pl.num_programs(0)
    nq = H_B * t_new
    groups = [(e, g) for e in range(be) for g in range(gpe)]
    ahead = n_slots // 2

    def page_copies(bi, g, slot):
        copies = []
        for j in range(gp):
            page = pt_ref[bi, g * gp + j]
            copies.append(pltpu.make_async_copy(lat_hbm.at[page], lat_buf.at[slot, pl.ds(j * PAGE, PAGE), :],
                                                lat_sem.at[slot]))
            copies.append(pltpu.make_async_copy(kct_hbm.at[page], kct_buf.at[slot, j], kct_sem.at[slot]))
        return copies

    def start_group(st, idx):
        e, g = groups[idx % len(groups)]
        for cp in page_copies(st * be + e, g, idx % n_slots):
            cp.start()

    @pl.when(step == 0)
    def _():
        for idx in range(ahead):
            start_group(0, idx)

    def group_softmax(s):
        m_g = jnp.max(s, axis=-1, keepdims=True)
        p = jnp.exp(s - m_g)
        return p.astype(BF16), m_g, jnp.sum(p, axis=-1, keepdims=True)

    def merge(state, group):
        m_old, l_old, acc_old = state
        p, m_g, l_g, values = group
        pv = jnp.dot(p, values, preferred_element_type=F32)
        m_new = jnp.maximum(m_old, m_g)
        a_old = jnp.exp(m_old - m_new)
        a_g = jnp.exp(m_g - m_new)
        return m_new, a_old * l_old + a_g * l_g, a_old * acc_old + a_g * pv

    nt_dims = (((1,), (1,)), ((), ()))
    for idx, (e, g) in enumerate(groups):
        slot = idx % n_slots
        for cp in page_copies(step * be + e, g, slot):
            cp.wait()
        nxt = idx + ahead
        if nxt < len(groups):
            start_group(step, nxt)
        else:
            @pl.when(step + 1 < n_steps)
            def _():
                start_group(step + 1, nxt)
        if g == 0:
            ql = jnp.concatenate([qlat_ref[e, :, h * KV_LORA:(h + 1) * KV_LORA] for h in range(H_B)],
                                 axis=0).astype(BF16)
            qp = jnp.concatenate([qpe_ref[e, :, h * D_ROPE:(h + 1) * D_ROPE] for h in range(H_B)],
                                 axis=0).astype(BF16)
            state = (jnp.full((nq, 1), NEG_INF, F32), jnp.zeros((nq, 1), F32), jnp.zeros((nq, KV_LORA), F32))
            pending = None
        lat = lat_buf[slot].astype(BF16)
        kct = jnp.concatenate([kct_buf[slot, j] for j in range(gp)], axis=1).astype(BF16)
        s = lax.dot_general(ql, lat, nt_dims, preferred_element_type=F32) + jnp.dot(qp, kct, preferred_element_type=F32)
        if pending is not None:
            state = merge(state, group_softmax(pending[0]) + (pending[1],))
        pending = (s, lat)
        if g == gpe - 1:
            state = merge(state, group_softmax(s) + (lat,))
            latn = latn_ref[e].astype(BF16)
            krn = krn_ref[e].astype(BF16)
            s = (lax.dot_general(ql, latn, nt_dims, preferred_element_type=F32)
                 + lax.dot_general(qp, krn, nt_dims, preferred_element_type=F32))
            tok = lax.broadcasted_iota(jnp.int32, (nq, t_new), 0) % t_new
            key = lax.broadcasted_iota(jnp.int32, (nq, t_new), 1)
            _, l, acc = merge(state, group_softmax(jnp.where(key <= tok, s, NEG_INF)) + (latn,))
            o_lat = acc / l
            outs = [_dot(o_lat[h * t_new:(h + 1) * t_new], wuv_ref[h]) for h in range(H_B)]
            o_ref[e] = jnp.concatenate(outs, axis=1).astype(o_ref.dtype)


def _decode(page_table, qlat3, qpe3, latn3, krn3, wuv, cache_lat, cache_kpe_t, gp, be):
    b, t_new, _ = qlat3.shape
    gpe = page_table.shape[1] // gp
    n_slots = min(4, be * gpe)
    assert n_slots >= 2 and (be * gpe) % n_slots == 0 and b % be == 0
    per_step = lambda w: pl.BlockSpec((be, t_new, w), lambda i, pt: (i, 0, 0))
    in_specs = [per_step(H_B * KV_LORA), per_step(H_B * D_ROPE), per_step(KV_LORA), per_step(D_ROPE),
                pl.BlockSpec(wuv.shape, lambda i, pt: (0, 0, 0)),
                pl.BlockSpec(memory_space=pl.ANY), pl.BlockSpec(memory_space=pl.ANY)]
    return pl.pallas_call(
        functools.partial(_decode_kernel, gp=gp, gpe=gpe, be=be, n_slots=n_slots, t_new=t_new),
        grid_spec=pltpu.PrefetchScalarGridSpec(
            num_scalar_prefetch=1,
            grid=(b // be,),
            in_specs=in_specs,
            out_specs=pl.BlockSpec((be, t_new, H_B * DV_B), lambda i, pt: (i, 0, 0)),
            scratch_shapes=[pltpu.VMEM((n_slots, gp * PAGE, KV_LORA), F32),
                            pltpu.VMEM((n_slots, gp, D_ROPE, PAGE), F32),
                            pltpu.SemaphoreType.DMA((n_slots,)), pltpu.SemaphoreType.DMA((n_slots,))]),
        out_shape=jax.ShapeDtypeStruct((b, t_new, H_B * DV_B), BF16),
        compiler_params=_cparams(("arbitrary",)),
        name="mla_decode",
    )(page_table, qlat3, qpe3, latn3, krn3, wuv, cache_lat, cache_kpe_t)


def _rot_cols(w):
    half = D_ROPE // 2
    return jnp.concatenate([-w[..., half:], w[..., :half]], axis=-1)


def _layout_w_in_ab(w):
    d = w.shape[0]
    o = QKV_A + H_A * DV_A
    qkv_z, b_w, a_w = w[:, :o], w[:, o:o + H_A], w[:, o + H_A:o + 2 * H_A]
    o += 2 * H_A
    cq, ckv, kpe = w[:, o:o + Q_LORA], w[:, o + Q_LORA:o + Q_LORA + KV_LORA], w[:, o + Q_LORA + KV_LORA:]
    kr = _rot_cols(kpe)
    pad = jnp.zeros((d, 128 - 2 * H_A), w.dtype)
    return jnp.concatenate([qkv_z, cq, ckv, kpe, kr, kpe, kr, b_w, a_w, pad], axis=1).astype(BF16)


def _layout_w_uq(w_uq):
    nope, pe = w_uq[..., :D_NOPE], w_uq[..., D_NOPE:]
    rot = _rot_cols(pe)
    z = jnp.zeros_like(nope)
    return jnp.concatenate([nope, z, pe, pe, rot, rot], axis=-1).reshape(w_uq.shape[0], H_B * QW).astype(BF16)


def _rope_tables(pos):
    half = D_ROPE // 2
    inv = ROPE_BASE ** (-jnp.arange(half, dtype=F32) / half)
    ang = pos.astype(F32)[:, None] * inv
    c = jnp.concatenate([jnp.cos(ang), jnp.cos(ang)], axis=1)
    s = jnp.concatenate([jnp.sin(ang), jnp.sin(ang)], axis=1)
    one = jnp.ones((pos.shape[0], 128), F32)
    tabq = jnp.concatenate([one, c, c, s, s], axis=1) * MLA_SCALE
    tabk = jnp.concatenate([c, s, c, s], axis=1)
    return tabq, tabk


def _row_tile(n):
    for tm in (512, 256, 128, 64, 32, 16, 8):
        if n % tm == 0:
            return tm
    raise ValueError(n)


def _trunk(x, pos, conv0, gdn0, hgrn0, paged, wts):
    b, t, d = x.shape
    n = b * t
    tm = min(256, _row_tile(n))
    tb = min(256, t)
    x2 = x.reshape(n, d)

    tmd = min(512, _row_tile(n))
    p2 = _proj(x2, wts["mix_norm"][0], wts["w_in_ab"], tmd)
    p3 = p2.reshape(b, t, IN_AB_PAD)
    conv_new = p3[:, t - (CONV_W - 1):, :QKV_A]
    o_a, gdn_new = _gdn(p3, wts["conv_w"], wts["a_log"], wts["dt_bias"], wts["gdn_norm"], conv0, gdn0, tb,
                        math.gcd(b, 2 if gdn0 is None else 4))
    tabq, tabk = _rope_tables(pos)
    if paged is None:
        tmp = math.gcd(tm, t)
        lat, kr, qt, kv, kvt, kp4 = _mla_prep(p2, wts["q_norm"], wts["kv_norm"], tabq.T, tabk, wts["w_uq"].T,
                                              [wts["w_kv"], wts["w_kv"].T], False, tmp, t)
        o_b = _flash(qt, kv.reshape(b, t, -1), kvt, kp4.reshape(b, t, -1), min(512, t))
    else:
        cache_lat, cache_kpe, page_table = paged
        reps = tm // t
        lat, kr, qlat, qpe = _mla_prep(p2, wts["q_norm"], wts["kv_norm"], jnp.tile(tabq, (reps, 1)),
                                       jnp.tile(tabk, (reps, 1)), wts["w_uq"], [wts["w_ukt"]], True, tm, t)
        n_pages = page_table.shape[1]
        gp = math.gcd(n_pages // 2, 32)
        o_b = _decode(page_table, qlat.reshape(b, t, -1), qpe.reshape(b, t, -1), lat.reshape(b, t, -1),
                      kr.reshape(b, t, -1), wts["w_uv"], cache_lat, jnp.swapaxes(cache_kpe, 1, 2), gp, math.gcd(b, 2))
    tmm = min(256, _row_tile(n))
    x2 = _mix_mlp(x2, [o_a.reshape(n, -1), o_b.reshape(n, -1)], [wts["w_out_a"], wts["w_out_b"]],
                  wts["mlp_norm"][0], wts["w_up"][0], wts["w_down"][0], wts["final_norm"], False, tmm)

    pc = _proj(x2, wts["mix_norm"][1], wts["w_in_c"], tmd)
    o_c, hgrn_new = _hgrn(pc.reshape(b, t, -1), wts["lb_logits"], 1, wts["g_norm_c"], hgrn0, tb,
                          H_C, 1)
    y2 = _mix_mlp(x2, [o_c.reshape(n, -1)], [wts["w_out_c"]],
                  wts["mlp_norm"][1], wts["w_up"][1], wts["w_down"][1], wts["final_norm"], True, tmm)
    return (y2.reshape(b, t, d), gdn_new[None], conv_new[None], lat.reshape(b, t, -1)[None],
            kr.reshape(b, t, -1)[None], hgrn_new[None])


def kernel(x_prompt, x_sample, state_gdn, state_gdn_conv, cache_mla_latent, cache_mla_krope, state_hgrn,
           page_table, mix_norm, mlp_norm, final_norm, w_up, w_down, w_in_ab, conv_w_ab, a_log_ab, dt_bias_ab,
           gdn_norm_ab, q_norm_ab, w_uq_ab, kv_norm_ab, w_uk_ab, w_uv_ab, w_out_ab, w_in_c, lb_logits_c,
           g_norm_c, w_out_c):
    assert mix_norm.shape[0] == 2 and w_in_ab.shape[0] == 1 and w_in_c.shape[0] == 1
    assert a_log_ab.shape == (1, H_A) and conv_w_ab.shape == (1, CONV_W, QKV_A)
    assert w_uq_ab.shape[1:] == (Q_LORA, H_B, D_NOPE + D_ROPE) and w_uk_ab.shape[1:] == (KV_LORA, H_B, D_NOPE)
    assert cache_mla_latent.shape[2] == PAGE and x_prompt.shape[1] >= CONV_W - 1 and x_sample.shape[1] >= CONV_W - 1
    w_uk, w_uv = w_uk_ab[0], w_uv_ab[0]
    wts = {
        "mix_norm": mix_norm, "mlp_norm": mlp_norm, "final_norm": final_norm,
        "w_up": w_up.astype(BF16), "w_down": w_down.astype(BF16),
        "w_in_ab": _layout_w_in_ab(w_in_ab[0]), "conv_w": conv_w_ab[0], "a_log": a_log_ab[0],
        "dt_bias": dt_bias_ab[0], "gdn_norm": gdn_norm_ab[0], "q_norm": q_norm_ab[0], "kv_norm": kv_norm_ab[0],
        "w_uq": _layout_w_uq(w_uq_ab[0]),
        "w_kv": jnp.concatenate([w_uk, w_uv], axis=-1).reshape(KV_LORA, H_B * 128).astype(BF16),
        "w_ukt": jnp.transpose(w_uk, (1, 2, 0)).astype(BF16),
        "w_uv": jnp.transpose(w_uv, (1, 0, 2)).astype(BF16),
        "w_out_a": w_out_ab[0, :H_A * DV_A].astype(BF16), "w_out_b": w_out_ab[0, H_A * DV_A:].astype(BF16),
        "w_in_c": w_in_c[0].astype(BF16), "lb_logits": lb_logits_c, "g_norm_c": g_norm_c[0],
        "w_out_c": w_out_c[0].astype(BF16),
    }
    s_len = x_prompt.shape[1]
    outs_p = _trunk(x_prompt, jnp.arange(s_len), None, None, None, None, wts)
    past_len = page_table.shape[1] * cache_mla_latent.shape[2]
    outs_s = _trunk(x_sample, past_len + jnp.arange(x_sample.shape[1]),
                    state_gdn_conv.reshape(state_gdn_conv.shape[1:]), state_gdn.reshape(state_gdn.shape[1:]),
                    state_hgrn.reshape(state_hgrn.shape[1:]),
                    (cache_mla_latent.reshape(cache_mla_latent.shape[1:]),
                     cache_mla_krope.reshape(cache_mla_krope.shape[1:]), page_table), wts)
    return (outs_p[0], outs_s[0]) + outs_p[1:] + outs_s[1:]
```

```python
import functools
import math

import jax
import jax.numpy as jnp
from jax import lax
from jax.experimental import pallas as pl
from jax.experimental.pallas import tpu as pltpu

F32 = jnp.float32
BF16 = jnp.bfloat16
EPS = 1e-6
NEG_INF = float("-inf")
LOG2_E = 1.0 / math.log(2.0)

H_A, DK_A, DV_A, CONV_W = 4, 128, 128, 4
QKV_A = H_A * (2 * DK_A + DV_A)
H_B, D_NOPE, D_ROPE, DV_B = 8, 64, 32, 64
Q_LORA, KV_LORA = 256, 256
ROPE_BASE = 10000.0
MLA_SCALE = (D_NOPE + D_ROPE) ** -0.5
H_C, DK_C, DV_C = 8, 128, 128
CHUNK = 64
SUB = 16
PAGE = 128
QW = 256
IN_AB_PAD = 2816
V7X_VMEM_LIMIT = 48 * 1024 * 1024


def _cparams(sem):
    return pltpu.CompilerParams(dimension_semantics=sem, vmem_limit_bytes=V7X_VMEM_LIMIT)


def _const_spec(shape):
    nd = len(shape)
    return pl.BlockSpec(shape, lambda *_: (0,) * nd)


def _dot(a, b):
    return jnp.dot(a.astype(BF16), b.astype(BF16), preferred_element_type=F32)


def _dot_nt(a, b):
    return lax.dot_general(a.astype(BF16), b.astype(BF16), (((1,), (1,)), ((), ())),
                           preferred_element_type=F32)


def _dot_tn(a, b):
    return lax.dot_general(a.astype(BF16), b.astype(BF16), (((0,), (0,)), ((), ())),
                           preferred_element_type=F32)


def _rms(x, g):
    return x * lax.rsqrt(jnp.mean(x * x, axis=-1, keepdims=True) + EPS) * g


def _silu(x):
    return x * jax.nn.sigmoid(x)


def _cumsum_rows(x):
    n = x.shape[0]
    row = lax.broadcasted_iota(jnp.int32, x.shape, 0)
    s = 1
    while s < n:
        x = x + jnp.where(row >= s, pltpu.roll(x, s, axis=0), 0.0)
        s *= 2
    return x


def _proj_kernel(x_ref, g_ref, w_ref, o_ref):
    h = _rms(x_ref[...], g_ref[...]).astype(BF16)
    o_ref[...] = jnp.dot(h, w_ref[...], preferred_element_type=F32)


def _proj(x2, g, w, tm):
    n, d = x2.shape
    nout = w.shape[1]
    return pl.pallas_call(
        _proj_kernel,
        grid=(n // tm,),
        in_specs=[pl.BlockSpec((tm, d), lambda i: (i, 0)), _const_spec((1, d)), _const_spec((d, nout))],
        out_specs=pl.BlockSpec((tm, nout), lambda i: (i, 0)),
        out_shape=jax.ShapeDtypeStruct((n, nout), F32),
        compiler_params=_cparams(("parallel",)),
        name="proj",
    )(x2, g.reshape(1, d), w)


def _mix_mlp_kernel(*refs, n_in, final_norm):
    r_ref, g_ref, gf_ref = refs[:3]
    a_refs = refs[3:3 + n_in]
    w_refs = refs[3 + n_in:3 + 2 * n_in]
    wu_ref, wd_ref, o_ref = refs[3 + 2 * n_in:]
    x = r_ref[...]
    for a_ref, w_ref in zip(a_refs, w_refs):
        x = x + jnp.dot(a_ref[...], w_ref[...], preferred_element_type=F32)
    h = _rms(x, g_ref[...]).astype(BF16)
    a = jnp.square(jnp.maximum(jnp.dot(h, wu_ref[...], preferred_element_type=F32), 0.0))
    y = x + jnp.dot(a.astype(BF16), wd_ref[...], preferred_element_type=F32)
    o_ref[...] = _rms(y, gf_ref[...]) if final_norm else y


def _mix_mlp(res, acts, ws, g, w_up, w_down, gf, final_norm, tm):
    n, d = res.shape
    n_in = len(acts)
    resident = lambda w: pl.BlockSpec(w.shape, lambda i: (0,) * w.ndim, pipeline_mode=pl.Buffered(1))
    in_specs = [pl.BlockSpec((tm, d), lambda i: (i, 0)), _const_spec((1, d)), _const_spec((1, d))]
    in_specs += [pl.BlockSpec((tm, a.shape[1]), lambda i: (i, 0)) for a in acts]
    in_specs += [resident(w) for w in ws] + [resident(w_up), resident(w_down)]
    return pl.pallas_call(
        functools.partial(_mix_mlp_kernel, n_in=n_in, final_norm=final_norm),
        grid=(n // tm,),
        in_specs=in_specs,
        out_specs=pl.BlockSpec((tm, d), lambda i: (i, 0)),
        out_shape=jax.ShapeDtypeStruct((n, d), F32),
        compiler_params=_cparams(("parallel",)),
        name="mix_mlp",
    )(res, g.reshape(1, d), gf.reshape(1, d), *acts, *ws, w_up, w_down)


def _split3_dot(a, b):
    a_hi = a.astype(BF16)
    a_lo = (a - a_hi.astype(F32)).astype(BF16)
    b_hi = b.astype(BF16)
    b_lo = (b - b_hi.astype(F32)).astype(BF16)
    return jnp.dot(jnp.concatenate([a_hi, a_lo, a_hi], axis=1), jnp.concatenate([b_hi, b_hi, b_lo], axis=0),
                   preferred_element_type=F32)


def _pick_dot(a, pick):
    a_hi = a.astype(BF16)
    a_lo = (a - a_hi.astype(F32)).astype(BF16)
    pick = pick.astype(BF16)
    return jnp.dot(jnp.concatenate([a_hi, a_lo], axis=1), jnp.concatenate([pick, pick], axis=0),
                   preferred_element_type=F32)


def _solve_unit_lower_steps(lowers, rhss, c):
    nh = len(lowers)
    s = 8
    nb = c // s
    w = nh * c
    row = lax.broadcasted_iota(jnp.int32, (c, w), 0)
    col = lax.broadcasted_iota(jnp.int32, (c, w), 1) % c
    if nb == 1:
        diag_cols = lowers
    else:
        r1 = lax.broadcasted_iota(jnp.int32, (c, c), 0)
        c1 = lax.broadcasted_iota(jnp.int32, (c, c), 1)
        pick = (lax.broadcasted_iota(jnp.int32, (c, s), 0) % s
                == lax.broadcasted_iota(jnp.int32, (c, s), 1)).astype(F32)
        diag_cols = [_pick_dot(jnp.where(r1 // s == c1 // s, lo, 0.0), pick) for lo in lowers]
    yield None
    x = (row == col).astype(F32)
    for j in range(s - 1):
        mult = jnp.concatenate([jnp.broadcast_to(dc[:, j:j + 1], (c, c)) for dc in diag_cols], axis=1)
        pivot_rows = jnp.broadcast_to(x.reshape(nb, s, w)[:, j:j + 1, :], (nb, s, w)).reshape(c, w)
        x = x - mult * pivot_rows
        yield None

    def block_diag(m):
        r2 = lax.broadcasted_iota(jnp.int32, (w, w), 0)
        c2 = lax.broadcasted_iota(jnp.int32, (w, w), 1)
        return jnp.where(r2 // c == c2 // c, jnp.concatenate([m] * nh, axis=0), 0.0)

    lower_cat = jnp.concatenate(lowers, axis=1)
    size = s
    while size < c:
        off = jnp.where((row // (2 * size) == col // (2 * size)) & (row // size != col // size) & (row > col),
                        lower_cat, 0.0)
        y = _split3_dot(x, block_diag(off))
        yield None
        x = x - _split3_dot(y, block_diag(x))
        yield None
        size *= 2
    out = _split3_dot(block_diag(x), jnp.concatenate(rhss, axis=0))
    yield [out[h * c:(h + 1) * c] for h in range(nh)]


def _run_lockstep(gens, lockstep=True):
    if not lockstep:
        return [[val for val in gen if val is not None][-1] for gen in gens]
    results = [None] * len(gens)
    active = list(range(len(gens)))
    while active:
        for i in list(active):
            try:
                val = next(gens[i])
                if val is not None:
                    results[i] = val
            except StopIteration:
                active.remove(i)
    return results


def _gdn_head_pre(q, k, v, b_col, a_col, alog, dtb, c):
    q = q * lax.rsqrt(jnp.sum(q * q, axis=-1, keepdims=True) + EPS) * (DK_A ** -0.5)
    k = k * lax.rsqrt(jnp.sum(k * k, axis=-1, keepdims=True) + EPS)
    beta = jax.nn.sigmoid(b_col)
    g = -jnp.exp(alog) * jax.nn.softplus(a_col + dtb)
    gc_b = _cumsum_rows(jnp.broadcast_to(g, (c, 128)))
    gc = gc_b[:, 0:1]
    diff = gc_b[:, :c] - gc_b.T[:c, :]
    row = lax.broadcasted_iota(jnp.int32, (c, c), 0)
    col = lax.broadcasted_iota(jnp.int32, (c, c), 1)
    incl = row >= col
    decay = jnp.where(incl, jnp.exp(jnp.where(incl, diff, 0.0)), 0.0)
    kb = k * beta
    lower = jnp.where(row > col, _dot_nt(kb, k) * decay, 0.0)
    rhs = jnp.concatenate([v * beta, kb * jnp.exp(gc)], axis=1)
    intra = _dot_nt(q, k) * decay
    return q, k, gc, lower, rhs, intra


def _gdn_head_post_steps(q, k, gc, uw, intra, z, gnorm, s_prev, c):
    u, w = uw[:, :DV_A], uw[:, DV_A:]
    qs = _dot(q * jnp.exp(gc), s_prev)
    v_new = u - _dot(w, s_prev)
    yield None
    g_last = gc[c - 1:c, :]
    s_new = s_prev * jnp.exp(g_last) + _dot_tn(k * jnp.exp(g_last - gc), v_new)
    o = qs + _dot(intra, v_new)
    yield None
    yield _rms(o, gnorm) * _silu(z), s_new


def _gdn_kernel(*refs, c, n_chunks, has_state, bg):
    (q_ref, k_ref, v_ref, z_ref, gt_ref, cwq_ref, cwk_ref, cwv_ref, alog_ref, dtb_ref, gn_ref) = refs[:11]
    if has_state:
        cq_ref, ck_ref, cv_ref, s0_ref = refs[11:15]
        o_ref, sout_ref, s_scr, tail_scr = refs[15:]
    else:
        o_ref, sout_ref, s_scr, tail_scr = refs[11:]
    n = pl.program_id(1)

    @pl.when(n == 0)
    def _():
        if has_state:
            s_scr[...] = s0_ref[...]
            tail_scr[0] = cq_ref[...]
            tail_scr[1] = ck_ref[...]
            tail_scr[2] = cv_ref[...]
        else:
            s_scr[...] = jnp.zeros_like(s_scr)
            tail_scr[...] = jnp.zeros_like(tail_scr)

    pre = {}
    for ci in range(n_chunks):
        rows = pl.ds(ci * c, c)
        for bi in range(bg):
            conv = []
            for idx, (x_ref, cw_ref) in enumerate(((q_ref, cwq_ref), (k_ref, cwk_ref), (v_ref, cwv_ref))):
                x = x_ref[bi, rows, :]
                prev = tail_scr[idx, bi] if ci == 0 else x_ref[bi, pl.ds(ci * c - 8, 8), :]
                ext = jnp.concatenate([prev, x], axis=0)
                cw = cw_ref[...]
                y = x * cw[CONV_W - 1:CONV_W, :]
                for sh in range(1, CONV_W):
                    y = y + pltpu.roll(ext, sh, axis=0)[8:] * cw[CONV_W - 1 - sh:CONV_W - sh, :]
                conv.append(_silu(y))
            gt = gt_ref[bi, rows, :]
            for h in range(H_A):
                hs = slice(h * 128, (h + 1) * 128)
                pre[ci, bi, h] = _gdn_head_pre(conv[0][:, hs], conv[1][:, hs], conv[2][:, hs],
                                               gt[:, h:h + 1], gt[:, H_A + h:H_A + h + 1],
                                               alog_ref[:, h:h + 1], dtb_ref[:, h:h + 1], c)
    last = pl.ds(n_chunks * c - 8, 8)
    for idx, x_ref in enumerate((q_ref, k_ref, v_ref)):
        tail_scr[idx] = x_ref[:, last, :]

    per_solve = max(H_A, min(bg * H_A, 128 // c))
    keys = [(ci, bi, h) for ci in range(n_chunks) for bi in range(bg) for h in range(H_A)]
    groups = [keys[g0:g0 + per_solve] for g0 in range(0, len(keys), per_solve)]
    solved = _run_lockstep([_solve_unit_lower_steps([pre[k][3] for k in grp], [pre[k][4] for k in grp], c)
                            for grp in groups])
    uws = {k: uw for grp, res in zip(groups, solved) for k, uw in zip(grp, res)}

    for ci in range(n_chunks):
        rows = pl.ds(ci * c, c)
        heads = [(bi, h) for bi in range(bg) for h in range(H_A)]
        res = _run_lockstep([_gdn_head_post_steps(pre[ci, bi, h][0], pre[ci, bi, h][1], pre[ci, bi, h][2],
                                                  uws[ci, bi, h], pre[ci, bi, h][5],
                                                  z_ref[bi, rows, h * 128:(h + 1) * 128], gn_ref[...],
                                                  s_scr[bi, h], c) for bi, h in heads])
        for (bi, h), (_, s_new) in zip(heads, res):
            s_scr[bi, h] = s_new
        for bi in range(bg):
            o_ref[bi, rows, :] = jnp.concatenate([res[bi * H_A + h][0] for h in range(H_A)],
                                                 axis=1).astype(o_ref.dtype)

    @pl.when(n == pl.num_programs(1) - 1)
    def _():
        sout_ref[...] = s_scr[...]


def _gdn(p3, conv_w, a_log, dt_bias, gdn_norm, conv0, s0, tb, bg):
    b, t, _ = p3.shape
    c = math.gcd(t, CHUNK)
    has_state = s0 is not None
    hw = H_A * 128
    cw = conv_w
    blk = lambda j: pl.BlockSpec((bg, tb, hw), lambda i, n: (i, n, j))
    in_specs = [blk(0), blk(1), blk(2), blk(3),
                pl.BlockSpec((bg, tb, 128), lambda i, n: (i, n, (IN_AB_PAD - 128) // 128)),
                pl.BlockSpec((CONV_W, hw), lambda i, n: (0, 0)), pl.BlockSpec((CONV_W, hw), lambda i, n: (0, 1)),
                pl.BlockSpec((CONV_W, hw), lambda i, n: (0, 2)),
                _const_spec((1, H_A)), _const_spec((1, H_A)), _const_spec((1, DV_A))]
    args = [p3, p3, p3, p3, p3, cw, cw, cw, a_log.reshape(1, H_A), dt_bias.reshape(1, H_A),
            gdn_norm.reshape(1, DV_A)]
    if has_state:
        conv8 = jnp.pad(conv0, ((0, 0), (8 - (CONV_W - 1), 0), (0, 0)))
        in_specs += [pl.BlockSpec((bg, 8, hw), lambda i, n, j=j: (i, 0, j)) for j in range(3)]
        in_specs += [pl.BlockSpec((bg, H_A, DK_A, DV_A), lambda i, n: (i, 0, 0, 0))]
        args += [conv8, conv8, conv8, s0]
    return pl.pallas_call(
        functools.partial(_gdn_kernel, c=c, n_chunks=tb // c, has_state=has_state, bg=bg),
        grid=(b // bg, t // tb),
        in_specs=in_specs,
        out_specs=[pl.BlockSpec((bg, tb, hw), lambda i, n: (i, n, 0)),
                   pl.BlockSpec((bg, H_A, DK_A, DV_A), lambda i, n: (i, 0, 0, 0))],
        out_shape=[jax.ShapeDtypeStruct((b, t, hw), BF16),
                   jax.ShapeDtypeStruct((b, H_A, DK_A, DV_A), F32)],
        scratch_shapes=[pltpu.VMEM((bg, H_A, DK_A, DV_A), F32), pltpu.VMEM((3, bg, 8, hw), F32)],
        compiler_params=_cparams(("parallel", "arbitrary")),
        name="gdn",
    )(*args)


def _hgrn_chunk_steps(qr, fr, v, gate, gnorm, lb, s_prev, c):
    sub = min(SUB, c)
    q = _silu(qr) * (DK_C ** -0.5)
    f = lb + (1.0 - lb) * jax.nn.sigmoid(fr)
    k = 1.0 - f
    gc = _cumsum_rows(jnp.log(f))
    o_inter = _dot(q * jnp.exp(gc), s_prev)
    gc2 = gc * LOG2_E
    kg2 = gc2 - jnp.log2(k)
    yield None
    row = lax.broadcasted_iota(jnp.int32, (sub, sub), 0)
    col = lax.broadcasted_iota(jnp.int32, (sub, sub), 1)
    outs = []
    for blk in range(c // sub):
        r0 = blk * sub
        g_i = gc[r0:r0 + sub]
        g2_i = gc2[r0:r0 + sub]
        q_i = q[r0:r0 + sub]
        a_parts = [jnp.zeros((8, sub), F32) for _ in range(sub // 8)]
        for j in range(sub):
            jj = r0 + j
            lo = 8 * (j // 8)
            ke = jnp.exp2(g2_i[lo:] - kg2[jj:jj + 1, :])
            col_j = jnp.sum(q_i[lo:] * ke, axis=-1, keepdims=True)
            for p in range(j // 8, sub // 8):
                a_parts[p] = jnp.where(col[:8] == j, col_j[8 * p - lo:8 * p - lo + 8], a_parts[p])
        a_diag = jnp.concatenate(a_parts, axis=0) if len(a_parts) > 1 else a_parts[0]
        a_diag = jnp.where(row >= col, a_diag, 0.0)
        o_i = _dot(a_diag, v[r0:r0 + sub])
        if blk > 0:
            g_ref = gc[r0 - 1:r0, :]
            a_off = _dot_nt(q_i * jnp.exp(g_i - g_ref), k[:r0] * jnp.exp(g_ref - gc[:r0]))
            o_i = o_i + _dot(a_off, v[:r0])
        outs.append(o_i)
        yield None
    o = o_inter + (jnp.concatenate(outs, axis=0) if len(outs) > 1 else outs[0])
    g_last = gc[c - 1:c, :]
    gl_col = gc.T[:, c - 1:c]
    s_new = s_prev * jnp.exp(gl_col) + _dot_tn(k * jnp.exp(g_last - gc), v)
    yield None
    yield _rms(o, gnorm) * _silu(gate), s_new


def _hgrn_kernel(*refs, c, n_chunks, layer, has_state, hg, bg):
    q_ref, f_ref, i_ref, gate_ref, lbl_ref, gn_ref = refs[:6]
    if has_state:
        s0_ref = refs[6]
        o_ref, sout_ref, s_scr = refs[7:]
    else:
        o_ref, sout_ref, s_scr = refs[6:]
    n = pl.program_id(2)

    @pl.when(n == 0)
    def _():
        if has_state:
            s_scr[...] = s0_ref[...]
        else:
            s_scr[...] = jnp.zeros_like(s_scr)

    lbl = lbl_ref[...]
    ex = jnp.exp(lbl - jnp.max(lbl, axis=0, keepdims=True))
    p = ex / jnp.sum(ex, axis=0, keepdims=True)
    lb = jnp.sum(p[:layer + 1], axis=0, keepdims=True) - p[0:1]

    def chunk(ci, carry):
        r0 = ci * c if isinstance(ci, int) else pl.multiple_of(ci * c, c)
        rows = pl.ds(r0, c)
        heads = [(bi, h) for bi in range(bg) for h in range(hg)]
        res = _run_lockstep([_hgrn_chunk_steps(q_ref[bi, rows, h * 128:(h + 1) * 128],
                                               f_ref[bi, rows, h * 128:(h + 1) * 128],
                                               i_ref[bi, rows, h * 128:(h + 1) * 128],
                                               gate_ref[bi, rows, h * 128:(h + 1) * 128], gn_ref[...],
                                               lb[:, h * 128:(h + 1) * 128], s_scr[bi, h], c) for bi, h in heads],
                            lockstep=c < CHUNK)
        for (bi, h), (o, s_new) in zip(heads, res):
            s_scr[bi, h] = s_new
            o_ref[bi, rows, h * 128:(h + 1) * 128] = o.astype(o_ref.dtype)
        return carry

    if n_chunks == 1:
        chunk(0, 0)
    else:
        lax.fori_loop(0, n_chunks, chunk, 0)

    @pl.when(n == pl.num_programs(2) - 1)
    def _():
        sout_ref[...] = s_scr[...]


def _hgrn(pc3, lb_logits, layer, g_norm, s0, tb, hg, bg):
    b, t, _ = pc3.shape
    c = math.gcd(t, CHUNK)
    has_state = s0 is not None
    depth = lb_logits.shape[0]
    ng = H_C // hg
    blk = lambda j: pl.BlockSpec((bg, tb, hg * 128), lambda i, h, n: (i, n, j * ng + h))
    in_specs = [blk(0), blk(1), blk(2), blk(3),
                pl.BlockSpec((depth, hg * 128), lambda i, h, n: (0, h)), _const_spec((1, DV_C))]
    args = [pc3, pc3, pc3, pc3, lb_logits, g_norm.reshape(1, DV_C)]
    if has_state:
        in_specs.append(pl.BlockSpec((bg, hg, DK_C, DV_C), lambda i, h, n: (i, h, 0, 0)))
        args.append(s0)
    return pl.pallas_call(
        functools.partial(_hgrn_kernel, c=c, n_chunks=tb // c, layer=layer, has_state=has_state, hg=hg, bg=bg),
        grid=(b // bg, ng, t // tb),
        in_specs=in_specs,
        out_specs=[pl.BlockSpec((bg, tb, hg * 128), lambda i, h, n: (i, n, h)),
                   pl.BlockSpec((bg, hg, DK_C, DV_C), lambda i, h, n: (i, h, 0, 0))],
        out_shape=[jax.ShapeDtypeStruct((b, t, H_C * DV_C), BF16),
                   jax.ShapeDtypeStruct((b, H_C, DK_C, DV_C), F32)],
        scratch_shapes=[pltpu.VMEM((bg, hg, DK_C, DV_C), F32)],
        compiler_params=_cparams(("parallel", "parallel", "arbitrary")),
        name="hgrn",
    )(*args)


def _mla_prep_kernel(*refs, absorb):
    cq_ref, ckv_ref, kp_ref, qn_ref, kvn_ref, tq_ref, tk_ref, wq_ref = refs[:8]
    if absorb:
        wukt_ref, lat_ref, kr_ref, qlat_ref, qpe_ref = refs[8:]
    else:
        wkv_ref, wkvt_ref, lat_ref, kr_ref, qt_ref, kv_ref, kvt_ref, kp4_ref = refs[8:]
    lat = _rms(ckv_ref[...], kvn_ref[...])
    lat_ref[...] = lat
    kp4 = kp_ref[...] * tk_ref[...]
    kr_ref[...] = kp4[:, 0:D_ROPE] + kp4[:, D_ROPE:2 * D_ROPE]
    cqn = _rms(cq_ref[...], qn_ref[...]).astype(BF16)
    tq = tq_ref[...]
    if absorb:
        for h in range(H_B):
            qh = jnp.dot(cqn, wq_ref[:, h * QW:(h + 1) * QW], preferred_element_type=F32) * tq
            qlat_ref[:, h * KV_LORA:(h + 1) * KV_LORA] = _dot(qh[:, 0:D_NOPE], wukt_ref[h])
            qpe_ref[:, h * D_ROPE:(h + 1) * D_ROPE] = (qh[:, 128:128 + D_ROPE]
                                                      + qh[:, 128 + 2 * D_ROPE:128 + 3 * D_ROPE])
    else:
        nt_dims = (((1,), (1,)), ((), ()))
        for h in range(H_B):
            qh = lax.dot_general(wq_ref[h * QW:(h + 1) * QW, :], cqn, nt_dims, preferred_element_type=F32) * tq
            qt_ref[0, h * QW:(h + 1) * QW, :] = qh.astype(BF16)
        lat_b = lat.astype(BF16)
        kv_ref[...] = jnp.dot(lat_b, wkv_ref[...], preferred_element_type=F32).astype(BF16)
        kvt_ref[0] = lax.dot_general(wkvt_ref[...], lat_b, nt_dims, preferred_element_type=F32).astype(BF16)
        kp4_ref[...] = kp4.astype(BF16)


def _mla_prep(p2, q_norm, kv_norm, tabq, tabk, wq, w_extra, absorb, tm, seq):
    n = p2.shape[0]
    nt = seq // tm if not absorb else 1
    tabq_spec = (pl.BlockSpec((tm, QW), lambda i: (0, 0)) if absorb
                 else pl.BlockSpec((QW, tm), lambda i: (0, i % nt)))
    in_specs = [pl.BlockSpec((tm, Q_LORA), lambda i: (i, 2048 // 256)),
                pl.BlockSpec((tm, KV_LORA), lambda i: (i, 2304 // 256)),
                pl.BlockSpec((tm, 128), lambda i: (i, 2560 // 128)),
                _const_spec((1, Q_LORA)), _const_spec((1, KV_LORA)),
                tabq_spec, pl.BlockSpec((tm, 128), lambda i: (i % nt, 0)),
                _const_spec(wq.shape)] + [_const_spec(w.shape) for w in w_extra]
    row = lambda w: pl.BlockSpec((tm, w), lambda i: (i, 0))
    out_specs = [row(KV_LORA), row(D_ROPE)]
    out_shape = [jax.ShapeDtypeStruct((n, KV_LORA), F32), jax.ShapeDtypeStruct((n, D_ROPE), F32)]
    if absorb:
        out_specs += [row(H_B * KV_LORA), row(H_B * D_ROPE)]
        out_shape += [jax.ShapeDtypeStruct((n, H_B * KV_LORA), F32), jax.ShapeDtypeStruct((n, H_B * D_ROPE), F32)]
    else:
        b = n // seq
        col = lambda w: pl.BlockSpec((1, w, tm), lambda i: (i // nt, 0, i % nt))
        out_specs += [col(H_B * QW), row(H_B * 128), col(H_B * 128), row(128)]
        out_shape += [jax.ShapeDtypeStruct((b, H_B * QW, seq), BF16), jax.ShapeDtypeStruct((n, H_B * 128), BF16),
                      jax.ShapeDtypeStruct((b, H_B * 128, seq), BF16), jax.ShapeDtypeStruct((n, 128), BF16)]
    return pl.pallas_call(
        functools.partial(_mla_prep_kernel, absorb=absorb),
        grid=(n // tm,),
        in_specs=in_specs, out_specs=out_specs, out_shape=out_shape,
        compiler_params=_cparams(("parallel",)),
        name="mla_prep",
    )(p2, p2, p2, q_norm.reshape(1, Q_LORA), kv_norm.reshape(1, KV_LORA), tabq, tabk, wq, *w_extra)


def _flash_kernel(qt_ref, kv_ref, kvt_ref, kp_ref, o_ref, m_scr, l_scr, acc_scr, *, tq):
    qi = pl.program_id(1)
    key = lax.broadcasted_iota(jnp.int32, (tq, tq), 0)
    qry = lax.broadcasted_iota(jnp.int32, (tq, tq), 1)
    m_scr[...] = jnp.full_like(m_scr, NEG_INF)
    l_scr[...] = jnp.zeros_like(l_scr)
    acc_scr[...] = jnp.zeros_like(acc_scr)

    def head_steps(h, k0, kp, masked):
        keys = jnp.concatenate([kv_ref[0, pl.ds(k0, tq), h * 128:(h + 1) * 128], kp], axis=1)
        s = jnp.dot(keys, qt_ref[0, h * QW:(h + 1) * QW, :], preferred_element_type=F32)
        yield None
        if masked:
            s = jnp.where(key <= qry, s, NEG_INF)
        m = m_scr[h]
        m_new = jnp.maximum(m, jnp.max(s, axis=0, keepdims=True))
        alpha = jnp.exp(m - m_new)
        p = jnp.exp(s - m_new)
        l_scr[h] = alpha * l_scr[h] + jnp.sum(p, axis=0, keepdims=True)
        m_scr[h] = m_new
        yield None
        acc_scr[h] = alpha * acc_scr[h] + jnp.dot(kvt_ref[0, h * 128:(h + 1) * 128, pl.ds(k0, tq)],
                                                  p.astype(BF16), preferred_element_type=F32)
        yield True

    def key_tile(k0, masked):
        kp = kp_ref[0, pl.ds(k0, tq), :]
        _run_lockstep([head_steps(h, k0, kp, masked) for h in range(H_B)])

    def body(kt, carry):
        key_tile(pl.multiple_of(kt * tq, tq), False)
        return carry

    lax.fori_loop(0, qi, body, 0)
    key_tile(pl.multiple_of(qi * tq, tq), True)
    o_t = jnp.concatenate([(acc_scr[h] / l_scr[h])[D_NOPE:D_NOPE + DV_B] for h in range(H_B)], axis=0)
    o_ref[0] = o_t.T.astype(o_ref.dtype)


def _flash(qt3, kv3, kvt3, kp3, tq):
    b, t, _ = kv3.shape
    return pl.pallas_call(
        functools.partial(_flash_kernel, tq=tq),
        grid=(b, t // tq),
        in_specs=[pl.BlockSpec((1, H_B * QW, tq), lambda i, j: (i, 0, j)),
                  pl.BlockSpec((1, t, H_B * 128), lambda i, j: (i, 0, 0)),
                  pl.BlockSpec((1, H_B * 128, t), lambda i, j: (i, 0, 0)),
                  pl.BlockSpec((1, t, 128), lambda i, j: (i, 0, 0))],
        out_specs=pl.BlockSpec((1, tq, H_B * DV_B), lambda i, j: (i, j, 0)),
        out_shape=jax.ShapeDtypeStruct((b, t, H_B * DV_B), BF16),
        scratch_shapes=[pltpu.VMEM((H_B, 1, tq), F32), pltpu.VMEM((H_B, 1, tq), F32),
                        pltpu.VMEM((H_B, 128, tq), F32)],
        compiler_params=_cparams(("parallel", "arbitrary")),
        name="mla_flash",
    )(qt3, kv3, kvt3, kp3)


def _decode_kernel(pt_ref, qlat_ref, qpe_ref, latn_ref, krn_ref, wuv_ref, lat_hbm, kct_hbm, o_ref,
                   lat_buf, kct_buf, lat_sem, kct_sem, *, gp, gpe, be, n_slots, t_new):
    step = pl.program_id(0)
    n_steps = pl.num_programs(0)
    nq = H_B * t_new
    groups = [(e, g) for e in range(be) for g in range(gpe)]
    ahead = n_slots // 2

    def page_copies(bi, g, slot):
        copies = []
        for j in range(gp):
            page = pt_ref[bi, g * gp + j]
            copies.append(pltpu.make_async_copy(lat_hbm.at[page], lat_buf.at[slot, pl.ds(j * PAGE, PAGE), :],
                                                lat_sem.at[slot]))
            copies.append(pltpu.make_async_copy(kct_hbm.at[page], kct_buf.at[slot, j], kct_sem.at[slot]))
        return copies

    def start_group(st, idx):
        e, g = groups[idx % len(groups)]
        for cp in page_copies(st * be + e, g, idx % n_slots):
            cp.start()

    @pl.when(step == 0)
    def _():
        for idx in range(ahead):
            start_group(0, idx)

    def group_softmax(s):
        m_g = jnp.max(s, axis=-1, keepdims=True)
        p = jnp.exp(s - m_g)
        return p.astype(BF16), m_g, jnp.sum(p, axis=-1, keepdims=True)

    def merge(state, group):
        m_old, l_old, acc_old = state
        p, m_g, l_g, values = group
        pv = jnp.dot(p, values, preferred_element_type=F32)
        m_new = jnp.maximum(m_old, m_g)
        a_old = jnp.exp(m_old - m_new)
        a_g = jnp.exp(m_g - m_new)
        return m_new, a_old * l_old + a_g * l_g, a_old * acc_old + a_g * pv

    nt_dims = (((1,), (1,)), ((), ()))
    for idx, (e, g) in enumerate(groups):
        slot = idx % n_slots
        for cp in page_copies(step * be + e, g, slot):
            cp.wait()
        nxt = idx + ahead
        if nxt < len(groups):
            start_group(step, nxt)
        else:
            @pl.when(step + 1 < n_steps)
            def _():
                start_group(step + 1, nxt)
        if g == 0:
            ql = jnp.concatenate([qlat_ref[e, :, h * KV_LORA:(h + 1) * KV_LORA] for h in range(H_B)],
                                 axis=0).astype(BF16)
            qp = jnp.concatenate([qpe_ref[e, :, h * D_ROPE:(h + 1) * D_ROPE] for h in range(H_B)],
                                 axis=0).astype(BF16)
            state = (jnp.full((nq, 1), NEG_INF, F32), jnp.zeros((nq, 1), F32), jnp.zeros((nq, KV_LORA), F32))
            pending = None
        lat = lat_buf[slot].astype(BF16)
        kct = jnp.concatenate([kct_buf[slot, j] for j in range(gp)], axis=1).astype(BF16)
        s = lax.dot_general(ql, lat, nt_dims, preferred_element_type=F32) + jnp.dot(qp, kct, preferred_element_type=F32)
        if pending is not None:
            state = merge(state, group_softmax(pending[0]) + (pending[1],))
        pending = (s, lat)
        if g == gpe - 1:
            state = merge(state, group_softmax(s) + (lat,))
            latn = latn_ref[e].astype(BF16)
            krn = krn_ref[e].astype(BF16)
            s = (lax.dot_general(ql, latn, nt_dims, preferred_element_type=F32)
                 + lax.dot_general(qp, krn, nt_dims, preferred_element_type=F32))
            tok = lax.broadcasted_iota(jnp.int32, (nq, t_new), 0) % t_new
            key = lax.broadcasted_iota(jnp.int32, (nq, t_new), 1)
            _, l, acc = merge(state, group_softmax(jnp.where(key <= tok, s, NEG_INF)) + (latn,))
            o_lat = acc / l
            outs = [_dot(o_lat[h * t_new:(h + 1) * t_new], wuv_ref[h]) for h in range(H_B)]
            o_ref[e] = jnp.concatenate(outs, axis=1).astype(o_ref.dtype)


def _decode(page_table, qlat3, qpe3, latn3, krn3, wuv, cache_lat, cache_kpe_t, gp, be):
    b, t_new, _ = qlat3.shape
    gpe = page_table.shape[1] // gp
    n_slots = min(4, be * gpe)
    assert n_slots >= 2 and (be * gpe) % n_slots == 0 and b % be == 0
    per_step = lambda w: pl.BlockSpec((be, t_new, w), lambda i, pt: (i, 0, 0))
    in_specs = [per_step(H_B * KV_LORA), per_step(H_B * D_ROPE), per_step(KV_LORA), per_step(D_ROPE),
                pl.BlockSpec(wuv.shape, lambda i, pt: (0, 0, 0)),
                pl.BlockSpec(memory_space=pl.ANY), pl.BlockSpec(memory_space=pl.ANY)]
    return pl.pallas_call(
        functools.partial(_decode_kernel, gp=gp, gpe=gpe, be=be, n_slots=n_slots, t_new=t_new),
        grid_spec=pltpu.PrefetchScalarGridSpec(
            num_scalar_prefetch=1,
            grid=(b // be,),
            in_specs=in_specs,
            out_specs=pl.BlockSpec((be, t_new, H_B * DV_B), lambda i, pt: (i, 0, 0)),
            scratch_shapes=[pltpu.VMEM((n_slots, gp * PAGE, KV_LORA), F32),
                            pltpu.VMEM((n_slots, gp, D_ROPE, PAGE), F32),
                            pltpu.SemaphoreType.DMA((n_slots,)), pltpu.SemaphoreType.DMA((n_slots,))]),
        out_shape=jax.ShapeDtypeStruct((b, t_new, H_B * DV_B), BF16),
        compiler_params=_cparams(("arbitrary",)),
        name="mla_decode",
    )(page_table, qlat3, qpe3, latn3, krn3, wuv, cache_lat, cache_kpe_t)


def _rot_cols(w):
    half = D_ROPE // 2
    return jnp.concatenate([-w[..., half:], w[..., :half]], axis=-1)


def _layout_w_in_ab(w):
    d = w.shape[0]
    o = QKV_A + H_A * DV_A
    qkv_z, b_w, a_w = w[:, :o], w[:, o:o + H_A], w[:, o + H_A:o + 2 * H_A]
    o += 2 * H_A
    cq, ckv, kpe = w[:, o:o + Q_LORA], w[:, o + Q_LORA:o + Q_LORA + KV_LORA], w[:, o + Q_LORA + KV_LORA:]
    kr = _rot_cols(kpe)
    pad = jnp.zeros((d, 128 - 2 * H_A), w.dtype)
    return jnp.concatenate([qkv_z, cq, ckv, kpe, kr, kpe, kr, b_w, a_w, pad], axis=1).astype(BF16)


def _layout_w_uq(w_uq):
    nope, pe = w_uq[..., :D_NOPE], w_uq[..., D_NOPE:]
    rot = _rot_cols(pe)
    z = jnp.zeros_like(nope)
    return jnp.concatenate([nope, z, pe, pe, rot, rot], axis=-1).reshape(w_uq.shape[0], H_B * QW).astype(BF16)


def _rope_tables(pos):
    half = D_ROPE // 2
    inv = ROPE_BASE ** (-jnp.arange(half, dtype=F32) / half)
    ang = pos.astype(F32)[:, None] * inv
    c = jnp.concatenate([jnp.cos(ang), jnp.cos(ang)], axis=1)
    s = jnp.concatenate([jnp.sin(ang), jnp.sin(ang)], axis=1)
    one = jnp.ones((pos.shape[0], 128), F32)
    tabq = jnp.concatenate([one, c, c, s, s], axis=1) * MLA_SCALE
    tabk = jnp.concatenate([c, s, c, s], axis=1)
    return tabq, tabk


def _row_tile(n):
    for tm in (512, 256, 128, 64, 32, 16, 8):
        if n % tm == 0:
            return tm
    raise ValueError(n)


def _trunk(x, pos, conv0, gdn0, hgrn0, paged, wts):
    b, t, d = x.shape
    n = b * t
    tm = min(256, _row_tile(n))
    tb = min(256, t)
    x2 = x.reshape(n, d)

    tmd = min(512, _row_tile(n))
    p2 = _proj(x2, wts["mix_norm"][0], wts["w_in_ab"], tmd)
    p3 = p2.reshape(b, t, IN_AB_PAD)
    conv_new = p3[:, t - (CONV_W - 1):, :QKV_A]
    o_a, gdn_new = _gdn(p3, wts["conv_w"], wts["a_log"], wts["dt_bias"], wts["gdn_norm"], conv0, gdn0, tb,
                        math.gcd(b, 2 if gdn0 is None else 4))
    tabq, tabk = _rope_tables(pos)
    if paged is None:
        tmp = math.gcd(tm, t)
        lat, kr, qt, kv, kvt, kp4 = _mla_prep(p2, wts["q_norm"], wts["kv_norm"], tabq.T, tabk, wts["w_uq"].T,
                                              [wts["w_kv"], wts["w_kv"].T], False, tmp, t)
        o_b = _flash(qt, kv.reshape(b, t, -1), kvt, kp4.reshape(b, t, -1), min(512, t))
    else:
        cache_lat, cache_kpe, page_table = paged
        reps = tm // t
        lat, kr, qlat, qpe = _mla_prep(p2, wts["q_norm"], wts["kv_norm"], jnp.tile(tabq, (reps, 1)),
                                       jnp.tile(tabk, (reps, 1)), wts["w_uq"], [wts["w_ukt"]], True, tm, t)
        n_pages = page_table.shape[1]
        gp = math.gcd(n_pages // 2, 32)
        o_b = _decode(page_table, qlat.reshape(b, t, -1), qpe.reshape(b, t, -1), lat.reshape(b, t, -1),
                      kr.reshape(b, t, -1), wts["w_uv"], cache_lat, jnp.swapaxes(cache_kpe, 1, 2), gp, math.gcd(b, 2))
    tmm = min(256, _row_tile(n))
    x2 = _mix_mlp(x2, [o_a.reshape(n, -1), o_b.reshape(n, -1)], [wts["w_out_a"], wts["w_out_b"]],
                  wts["mlp_norm"][0], wts["w_up"][0], wts["w_down"][0], wts["final_norm"], False, tmm)

    pc = _proj(x2, wts["mix_norm"][1], wts["w_in_c"], tmd)
    o_c, hgrn_new = _hgrn(pc.reshape(b, t, -1), wts["lb_logits"], 1, wts["g_norm_c"], hgrn0, tb,
                          H_C, 1)
    y2 = _mix_mlp(x2, [o_c.reshape(n, -1)], [wts["w_out_c"]],
                  wts["mlp_norm"][1], wts["w_up"][1], wts["w_down"][1], wts["final_norm"], True, tmm)
    return (y2.reshape(b, t, d), gdn_new[None], conv_new[None], lat.reshape(b, t, -1)[None],
            kr.reshape(b, t, -1)[None], hgrn_new[None])


def kernel(x_prompt, x_sample, state_gdn, state_gdn_conv, cache_mla_latent, cache_mla_krope, state_hgrn,
           page_table, mix_norm, mlp_norm, final_norm, w_up, w_down, w_in_ab, conv_w_ab, a_log_ab, dt_bias_ab,
           gdn_norm_ab, q_norm_ab, w_uq_ab, kv_norm_ab, w_uk_ab, w_uv_ab, w_out_ab, w_in_c, lb_logits_c,
           g_norm_c, w_out_c):
    assert mix_norm.shape[0] == 2 and w_in_ab.shape[0] == 1 and w_in_c.shape[0] == 1
    assert a_log_ab.shape == (1, H_A) and conv_w_ab.shape == (1, CONV_W, QKV_A)
    assert w_uq_ab.shape[1:] == (Q_LORA, H_B, D_NOPE + D_ROPE) and w_uk_ab.shape[1:] == (KV_LORA, H_B, D_NOPE)
    assert cache_mla_latent.shape[2] == PAGE and x_prompt.shape[1] >= CONV_W - 1 and x_sample.shape[1] >= CONV_W - 1
    w_uk, w_uv = w_uk_ab[0], w_uv_ab[0]
    wts = {
        "mix_norm": mix_norm, "mlp_norm": mlp_norm, "final_norm": final_norm,
        "w_up": w_up.astype(BF16), "w_down": w_down.astype(BF16),
        "w_in_ab": _layout_w_in_ab(w_in_ab[0]), "conv_w": conv_w_ab[0], "a_log": a_log_ab[0],
        "dt_bias": dt_bias_ab[0], "gdn_norm": gdn_norm_ab[0], "q_norm": q_norm_ab[0], "kv_norm": kv_norm_ab[0],
        "w_uq": _layout_w_uq(w_uq_ab[0]),
        "w_kv": jnp.concatenate([w_uk, w_uv], axis=-1).reshape(KV_LORA, H_B * 128).astype(BF16),
        "w_ukt": jnp.transpose(w_uk, (1, 2, 0)).astype(BF16),
        "w_uv": jnp.transpose(w_uv, (1, 0, 2)).astype(BF16),
        "w_out_a": w_out_ab[0, :H_A * DV_A].astype(BF16), "w_out_b": w_out_ab[0, H_A * DV_A:].astype(BF16),
        "w_in_c": w_in_c[0].astype(BF16), "lb_logits": lb_logits_c, "g_norm_c": g_norm_c[0],
        "w_out_c": w_out_c[0].astype(BF16),
    }
    s_len = x_prompt.shape[1]
    outs_p = _trunk(x_prompt, jnp.arange(s_len), None, None, None, None, wts)
    past_len = page_table.shape[1] * cache_mla_latent.shape[2]
    outs_s = _trunk(x_sample, past_len + jnp.arange(x_sample.shape[1]),
                    state_gdn_conv.reshape(state_gdn_conv.shape[1:]), state_gdn.reshape(state_gdn.shape[1:]),
                    state_hgrn.reshape(state_hgrn.shape[1:]),
                    (cache_mla_latent.reshape(cache_mla_latent.shape[1:]),
                     cache_mla_krope.reshape(cache_mla_krope.shape[1:]), page_table), wts)
    return (outs_p[0], outs_s[0]) + outs_p[1:] + outs_s[1:]
```

```python
import functools
import math

import jax
import jax.numpy as jnp
from jax import lax
from jax.experimental import pallas as pl
from jax.experimental.pallas import tpu as pltpu

F32 = jnp.float32
BF16 = jnp.bfloat16
EPS = 1e-6
NEG_INF = float("-inf")
LOG2_E = 1.0 / math.log(2.0)

H_A, DK_A, DV_A, CONV_W = 4, 128, 128, 4
QKV_A = H_A * (2 * DK_A + DV_A)
H_B, D_NOPE, D_ROPE, DV_B = 8, 64, 32, 64
Q_LORA, KV_LORA = 256, 256
ROPE_BASE = 10000.0
MLA_SCALE = (D_NOPE + D_ROPE) ** -0.5
H_C, DK_C, DV_C = 8, 128, 128
CHUNK = 64
SUB = 16
PAGE = 128
QW = 256
IN_AB_PAD = 2816

V7X_VMEM_LIMIT = 48 * 1024 * 1024
ROW_TILE_PROJ = 512
ROW_TILE_MIX = 256
ROW_TILE_PREP = 256
TIME_BLOCK = 256
HGRN_TIME_BLOCK = 512
FLASH_TILE = 512
DECODE_GROUP_PAGES = 32
DECODE_BATCH = 2
DECODE_SLOTS = 4
GDN_BATCH_PROMPT, GDN_BATCH_STATE = 2, 4
HGRN_LOCKSTEP = 4


def _cparams(sem):
    return pltpu.CompilerParams(dimension_semantics=sem, vmem_limit_bytes=V7X_VMEM_LIMIT)


def _const_spec(shape):
    nd = len(shape)
    return pl.BlockSpec(shape, lambda *_: (0,) * nd)


def _dot(a, b):
    return jnp.dot(a.astype(BF16), b.astype(BF16), preferred_element_type=F32)


def _dot_nt(a, b):
    return lax.dot_general(a.astype(BF16), b.astype(BF16), (((1,), (1,)), ((), ())),
                           preferred_element_type=F32)


def _dot_tn(a, b):
    return lax.dot_general(a.astype(BF16), b.astype(BF16), (((0,), (0,)), ((), ())),
                           preferred_element_type=F32)


def _rms(x, g):
    return x * lax.rsqrt(jnp.mean(x * x, axis=-1, keepdims=True) + EPS) * g


def _silu(x):
    return x * jax.nn.sigmoid(x)


def _cumsum_rows(x):
    n = x.shape[0]
    row = lax.broadcasted_iota(jnp.int32, x.shape, 0)
    s = 1
    while s < n:
        x = x + jnp.where(row >= s, pltpu.roll(x, s, axis=0), 0.0)
        s *= 2
    return x


def _proj_kernel(x_ref, g_ref, w_ref, o_ref):
    h = _rms(x_ref[...], g_ref[...]).astype(BF16)
    o_ref[...] = jnp.dot(h, w_ref[...], preferred_element_type=F32)


def _proj(x2, g, w, tm):
    n, d = x2.shape
    nout = w.shape[1]
    return pl.pallas_call(
        _proj_kernel,
        grid=(n // tm,),
        in_specs=[pl.BlockSpec((tm, d), lambda i: (i, 0)), _const_spec((1, d)), _const_spec((d, nout))],
        out_specs=pl.BlockSpec((tm, nout), lambda i: (i, 0)),
        out_shape=jax.ShapeDtypeStruct((n, nout), F32),
        compiler_params=_cparams(("parallel",)),
        name="proj",
    )(x2, g.reshape(1, d), w)


def _mix_mlp_kernel(*refs, n_in, final_norm):
    r_ref, g_ref, gf_ref = refs[:3]
    a_refs = refs[3:3 + n_in]
    w_refs = refs[3 + n_in:3 + 2 * n_in]
    wu_ref, wd_ref, o_ref = refs[3 + 2 * n_in:]
    x = r_ref[...]
    for a_ref, w_ref in zip(a_refs, w_refs):
        x = x + jnp.dot(a_ref[...], w_ref[...], preferred_element_type=F32)
    h = _rms(x, g_ref[...]).astype(BF16)
    a = jnp.square(jnp.maximum(jnp.dot(h, wu_ref[...], preferred_element_type=F32), 0.0))
    y = x + jnp.dot(a.astype(BF16), wd_ref[...], preferred_element_type=F32)
    o_ref[...] = _rms(y, gf_ref[...]) if final_norm else y


def _mix_mlp(res, acts, ws, g, w_up, w_down, gf, final_norm, tm):
    n, d = res.shape
    n_in = len(acts)
    resident = lambda w: pl.BlockSpec(w.shape, lambda i: (0,) * w.ndim, pipeline_mode=pl.Buffered(1))
    in_specs = [pl.BlockSpec((tm, d), lambda i: (i, 0)), _const_spec((1, d)), _const_spec((1, d))]
    in_specs += [pl.BlockSpec((tm, a.shape[1]), lambda i: (i, 0)) for a in acts]
    in_specs += [resident(w) for w in ws] + [resident(w_up), resident(w_down)]
    return pl.pallas_call(
        functools.partial(_mix_mlp_kernel, n_in=n_in, final_norm=final_norm),
        grid=(n // tm,),
        in_specs=in_specs,
        out_specs=pl.BlockSpec((tm, d), lambda i: (i, 0)),
        out_shape=jax.ShapeDtypeStruct((n, d), F32),
        compiler_params=_cparams(("parallel",)),
        name="mix_mlp",
    )(res, g.reshape(1, d), gf.reshape(1, d), *acts, *ws, w_up, w_down)


def _hi_lo(a):
    hi = a.astype(BF16)
    return hi, (a - hi.astype(F32)).astype(BF16)


def _split3_dot(a_parts, b_parts):
    (a_hi, a_lo), (b_hi, b_lo) = a_parts, b_parts
    return jnp.dot(jnp.concatenate([a_hi, a_lo, a_hi], axis=1), jnp.concatenate([b_hi, b_hi, b_lo], axis=0),
                   preferred_element_type=F32)


def _pick_dot(a, pick):
    a_hi = a.astype(BF16)
    a_lo = (a - a_hi.astype(F32)).astype(BF16)
    pick = pick.astype(BF16)
    return jnp.dot(jnp.concatenate([a_hi, a_lo], axis=1), jnp.concatenate([pick, pick], axis=0),
                   preferred_element_type=F32)


def _solve_unit_lower_steps(lowers, rhss, c, block_mask):
    nh = len(lowers)
    s = 8
    nb = c // s
    w = nh * c
    row = lax.broadcasted_iota(jnp.int32, (c, w), 0)
    col = lax.broadcasted_iota(jnp.int32, (c, w), 1) % c
    if nb == 1:
        diag_cols = lowers
    else:
        r1 = lax.broadcasted_iota(jnp.int32, (c, c), 0)
        c1 = lax.broadcasted_iota(jnp.int32, (c, c), 1)
        pick = (lax.broadcasted_iota(jnp.int32, (c, s), 0) % s
                == lax.broadcasted_iota(jnp.int32, (c, s), 1)).astype(F32)
        diag_cols = [_pick_dot(jnp.where(r1 // s == c1 // s, lo, 0.0), pick) for lo in lowers]
    yield None
    x = (row == col).astype(F32)
    for j in range(s - 1):
        mult = jnp.concatenate([jnp.broadcast_to(dc[:, j:j + 1], (c, c)) for dc in diag_cols], axis=1)
        pivot_rows = jnp.broadcast_to(x.reshape(nb, s, w)[:, j:j + 1, :], (nb, s, w)).reshape(c, w)
        x = x - mult * pivot_rows
        yield None

    def block_diag(m):
        return tuple(jnp.concatenate([part] * nh, axis=0) * block_mask for part in _hi_lo(m))

    lower_cat = jnp.concatenate(lowers, axis=1)
    size = s
    while size < c:
        off = jnp.where((row // (2 * size) == col // (2 * size)) & (row // size != col // size) & (row > col),
                        lower_cat, 0.0)
        y = _split3_dot(_hi_lo(x), block_diag(off))
        yield None
        x = x - _split3_dot(_hi_lo(y), block_diag(x))
        yield None
        size *= 2
    out = _split3_dot(block_diag(x), _hi_lo(jnp.concatenate(rhss, axis=0)))
    yield [out[h * c:(h + 1) * c] for h in range(nh)]


def _run_lockstep(gens, group=None):
    if group is not None and group < len(gens):
        return [val for g0 in range(0, len(gens), group) for val in _run_lockstep(gens[g0:g0 + group])]
    results = [None] * len(gens)
    active = list(range(len(gens)))
    while active:
        for i in list(active):
            try:
                val = next(gens[i])
                if val is not None:
                    results[i] = val
            except StopIteration:
                active.remove(i)
    return results


def _gdn_head_pre(q, k, v, b_col, a_col, alog, dtb, c):
    q = q * lax.rsqrt(jnp.sum(q * q, axis=-1, keepdims=True) + EPS) * (DK_A ** -0.5)
    k = k * lax.rsqrt(jnp.sum(k * k, axis=-1, keepdims=True) + EPS)
    beta = jax.nn.sigmoid(b_col)
    g = -jnp.exp(alog) * jax.nn.softplus(a_col + dtb)
    gc_b = _cumsum_rows(jnp.broadcast_to(g, (c, 128)))
    gc = gc_b[:, 0:1]
    diff = gc_b[:, :c] - gc_b.T[:c, :]
    row = lax.broadcasted_iota(jnp.int32, (c, c), 0)
    col = lax.broadcasted_iota(jnp.int32, (c, c), 1)
    incl = row >= col
    decay = jnp.where(incl, jnp.exp(jnp.where(incl, diff, 0.0)), 0.0)
    kb = k * beta
    lower = jnp.where(row > col, _dot_nt(kb, k) * decay, 0.0)
    rhs = jnp.concatenate([v * beta, kb * jnp.exp(gc)], axis=1)
    intra = _dot_nt(q, k) * decay
    return q, k, gc, lower, rhs, intra


def _gdn_head_post_steps(q, k, gc, uw, intra, z, gnorm, s_prev, c):
    u, w = uw[:, :DV_A], uw[:, DV_A:]
    qs = _dot(q * jnp.exp(gc), s_prev)
    v_new = u - _dot(w, s_prev)
    yield None
    g_last = gc[c - 1:c, :]
    s_new = s_prev * jnp.exp(g_last) + _dot_tn(k * jnp.exp(g_last - gc), v_new)
    o = qs + _dot(intra, v_new)
    yield None
    yield _rms(o, gnorm) * _silu(z), s_new


def _gdn_kernel(*refs, c, n_chunks, has_state, bg):
    (q_ref, k_ref, v_ref, z_ref, gt_ref, cwq_ref, cwk_ref, cwv_ref, alog_ref, dtb_ref, gn_ref) = refs[:11]
    if has_state:
        cq_ref, ck_ref, cv_ref, s0_ref = refs[11:15]
        o_ref, sout_ref, s_scr, tail_scr = refs[15:]
    else:
        o_ref, sout_ref, s_scr, tail_scr = refs[11:]
    n = pl.program_id(1)

    @pl.when(n == 0)
    def _():
        if has_state:
            s_scr[...] = s0_ref[...]
            tail_scr[0] = cq_ref[...]
            tail_scr[1] = ck_ref[...]
            tail_scr[2] = cv_ref[...]
        else:
            s_scr[...] = jnp.zeros_like(s_scr)
            tail_scr[...] = jnp.zeros_like(tail_scr)

    pre = {}
    for ci in range(n_chunks):
        rows = pl.ds(ci * c, c)
        for bi in range(bg):
            conv = []
            for idx, (x_ref, cw_ref) in enumerate(((q_ref, cwq_ref), (k_ref, cwk_ref), (v_ref, cwv_ref))):
                x = x_ref[bi, rows, :]
                prev = tail_scr[idx, bi] if ci == 0 else x_ref[bi, pl.ds(ci * c - 8, 8), :]
                ext = jnp.concatenate([prev, x], axis=0)
                cw = cw_ref[...]
                y = x * cw[CONV_W - 1:CONV_W, :]
                for sh in range(1, CONV_W):
                    y = y + pltpu.roll(ext, sh, axis=0)[8:] * cw[CONV_W - 1 - sh:CONV_W - sh, :]
                conv.append(_silu(y))
            gt = gt_ref[bi, rows, :]
            for h in range(H_A):
                hs = slice(h * 128, (h + 1) * 128)
                pre[ci, bi, h] = _gdn_head_pre(conv[0][:, hs], conv[1][:, hs], conv[2][:, hs],
                                               gt[:, h:h + 1], gt[:, H_A + h:H_A + h + 1],
                                               alog_ref[:, h:h + 1], dtb_ref[:, h:h + 1], c)
    last = pl.ds(n_chunks * c - 8, 8)
    for idx, x_ref in enumerate((q_ref, k_ref, v_ref)):
        tail_scr[idx] = x_ref[:, last, :]

    per_solve = max(H_A, min(bg * H_A, 128 // c))
    keys = [(ci, bi, h) for ci in range(n_chunks) for bi in range(bg) for h in range(H_A)]
    groups = [keys[g0:g0 + per_solve] for g0 in range(0, len(keys), per_solve)]
    w = per_solve * c
    block_mask = (lax.broadcasted_iota(jnp.int32, (w, w), 0) // c
                  == lax.broadcasted_iota(jnp.int32, (w, w), 1) // c).astype(BF16)
    solved = _run_lockstep([_solve_unit_lower_steps([pre[k][3] for k in grp], [pre[k][4] for k in grp], c, block_mask)
                            for grp in groups])
    uws = {k: uw for grp, res in zip(groups, solved) for k, uw in zip(grp, res)}

    for ci in range(n_chunks):
        rows = pl.ds(ci * c, c)
        heads = [(bi, h) for bi in range(bg) for h in range(H_A)]
        res = _run_lockstep([_gdn_head_post_steps(pre[ci, bi, h][0], pre[ci, bi, h][1], pre[ci, bi, h][2],
                                                  uws[ci, bi, h], pre[ci, bi, h][5],
                                                  z_ref[bi, rows, h * 128:(h + 1) * 128], gn_ref[...],
                                                  s_scr[bi, h], c) for bi, h in heads])
        for (bi, h), (_, s_new) in zip(heads, res):
            s_scr[bi, h] = s_new
        for bi in range(bg):
            o_ref[bi, rows, :] = jnp.concatenate([res[bi * H_A + h][0] for h in range(H_A)],
                                                 axis=1).astype(o_ref.dtype)

    @pl.when(n == pl.num_programs(1) - 1)
    def _():
        sout_ref[...] = s_scr[...]


def _gdn(p3, conv_w, a_log, dt_bias, gdn_norm, conv0, s0, tb, bg):
    b, t, _ = p3.shape
    c = math.gcd(t, CHUNK)
    has_state = s0 is not None
    hw = H_A * 128
    cw = conv_w
    blk = lambda j: pl.BlockSpec((bg, tb, hw), lambda i, n: (i, n, j))
    in_specs = [blk(0), blk(1), blk(2), blk(3),
                pl.BlockSpec((bg, tb, 128), lambda i, n: (i, n, (IN_AB_PAD - 128) // 128)),
                pl.BlockSpec((CONV_W, hw), lambda i, n: (0, 0)), pl.BlockSpec((CONV_W, hw), lambda i, n: (0, 1)),
                pl.BlockSpec((CONV_W, hw), lambda i, n: (0, 2)),
                _const_spec((1, H_A)), _const_spec((1, H_A)), _const_spec((1, DV_A))]
    args = [p3, p3, p3, p3, p3, cw, cw, cw, a_log.reshape(1, H_A), dt_bias.reshape(1, H_A),
            gdn_norm.reshape(1, DV_A)]
    if has_state:
        conv8 = jnp.pad(conv0, ((0, 0), (8 - (CONV_W - 1), 0), (0, 0)))
        in_specs += [pl.BlockSpec((bg, 8, hw), lambda i, n, j=j: (i, 0, j)) for j in range(3)]
        in_specs += [pl.BlockSpec((bg, H_A, DK_A, DV_A), lambda i, n: (i, 0, 0, 0))]
        args += [conv8, conv8, conv8, s0]
    return pl.pallas_call(
        functools.partial(_gdn_kernel, c=c, n_chunks=tb // c, has_state=has_state, bg=bg),
        grid=(b // bg, t // tb),
        in_specs=in_specs,
        out_specs=[pl.BlockSpec((bg, tb, hw), lambda i, n: (i, n, 0)),
                   pl.BlockSpec((bg, H_A, DK_A, DV_A), lambda i, n: (i, 0, 0, 0))],
        out_shape=[jax.ShapeDtypeStruct((b, t, hw), BF16),
                   jax.ShapeDtypeStruct((b, H_A, DK_A, DV_A), F32)],
        scratch_shapes=[pltpu.VMEM((bg, H_A, DK_A, DV_A), F32), pltpu.VMEM((3, bg, 8, hw), F32)],
        compiler_params=_cparams(("parallel", "arbitrary")),
        name="gdn",
    )(*args)


def _hgrn_chunk_steps(qr, fr, v, gate, gnorm, lb, s_prev, c):
    sub = min(SUB, c)
    q = _silu(qr) * (DK_C ** -0.5)
    f = lb + (1.0 - lb) * jax.nn.sigmoid(fr)
    k = 1.0 - f
    gc = _cumsum_rows(jnp.log(f))
    o_inter = _dot(q * jnp.exp(gc), s_prev)
    gc2 = gc * LOG2_E
    kg2 = gc2 - jnp.log2(k)
    yield None
    row = lax.broadcasted_iota(jnp.int32, (sub, sub), 0)
    col = lax.broadcasted_iota(jnp.int32, (sub, sub), 1)
    outs = []
    for blk in range(c // sub):
        r0 = blk * sub
        g_i = gc[r0:r0 + sub]
        g2_i = gc2[r0:r0 + sub]
        q_i = q[r0:r0 + sub]
        a_parts = [jnp.zeros((8, sub), F32) for _ in range(sub // 8)]
        for j in range(sub):
            jj = r0 + j
            lo = 8 * (j // 8)
            ke = jnp.exp2(g2_i[lo:] - kg2[jj:jj + 1, :])
            col_j = jnp.sum(q_i[lo:] * ke, axis=-1, keepdims=True)
            for p in range(j // 8, sub // 8):
                a_parts[p] = jnp.where(col[:8] == j, col_j[8 * p - lo:8 * p - lo + 8], a_parts[p])
        a_diag = jnp.concatenate(a_parts, axis=0) if len(a_parts) > 1 else a_parts[0]
        a_diag = jnp.where(row >= col, a_diag, 0.0)
        o_i = _dot(a_diag, v[r0:r0 + sub])
        if blk > 0:
            g_ref = gc[r0 - 1:r0, :]
            a_off = _dot_nt(q_i * jnp.exp(g_i - g_ref), k[:r0] * jnp.exp(g_ref - gc[:r0]))
            o_i = o_i + _dot(a_off, v[:r0])
        outs.append(o_i)
        yield None
    o = o_inter + (jnp.concatenate(outs, axis=0) if len(outs) > 1 else outs[0])
    g_last = gc[c - 1:c, :]
    gl_col = gc.T[:, c - 1:c]
    s_new = s_prev * jnp.exp(gl_col) + _dot_tn(k * jnp.exp(g_last - gc), v)
    yield None
    yield _rms(o, gnorm) * _silu(gate), s_new


def _hgrn_kernel(*refs, c, n_chunks, layer, has_state, hg, bg):
    q_ref, f_ref, i_ref, gate_ref, lbl_ref, gn_ref = refs[:6]
    if has_state:
        s0_ref = refs[6]
        o_ref, sout_ref, s_scr = refs[7:]
    else:
        o_ref, sout_ref, s_scr = refs[6:]
    n = pl.program_id(2)

    @pl.when(n == 0)
    def _():
        if has_state:
            s_scr[...] = s0_ref[...]
        else:
            s_scr[...] = jnp.zeros_like(s_scr)

    lbl = lbl_ref[...]
    ex = jnp.exp(lbl - jnp.max(lbl, axis=0, keepdims=True))
    p = ex / jnp.sum(ex, axis=0, keepdims=True)
    lb = jnp.sum(p[:layer + 1], axis=0, keepdims=True) - p[0:1]

    def chunk(ci, carry):
        r0 = ci * c if isinstance(ci, int) else pl.multiple_of(ci * c, c)
        rows = pl.ds(r0, c)
        heads = [(bi, h) for bi in range(bg) for h in range(hg)]
        res = _run_lockstep([_hgrn_chunk_steps(q_ref[bi, rows, h * 128:(h + 1) * 128],
                                               f_ref[bi, rows, h * 128:(h + 1) * 128],
                                               i_ref[bi, rows, h * 128:(h + 1) * 128],
                                               gate_ref[bi, rows, h * 128:(h + 1) * 128], gn_ref[...],
                                               lb[:, h * 128:(h + 1) * 128], s_scr[bi, h], c) for bi, h in heads],
                            group=len(heads) if c < CHUNK else HGRN_LOCKSTEP)
        for (bi, h), (o, s_new) in zip(heads, res):
            s_scr[bi, h] = s_new
            o_ref[bi, rows, h * 128:(h + 1) * 128] = o.astype(o_ref.dtype)
        return carry

    if n_chunks == 1:
        chunk(0, 0)
    else:
        lax.fori_loop(0, n_chunks, chunk, 0)

    @pl.when(n == pl.num_programs(2) - 1)
    def _():
        sout_ref[...] = s_scr[...]


def _hgrn(pc3, lb_logits, layer, g_norm, s0, tb, hg, bg):
    b, t, _ = pc3.shape
    c = math.gcd(t, CHUNK)
    has_state = s0 is not None
    depth = lb_logits.shape[0]
    ng = H_C // hg
    blk = lambda j: pl.BlockSpec((bg, tb, hg * 128), lambda i, h, n: (i, n, j * ng + h))
    in_specs = [blk(0), blk(1), blk(2), blk(3),
                pl.BlockSpec((depth, hg * 128), lambda i, h, n: (0, h)), _const_spec((1, DV_C))]
    args = [pc3, pc3, pc3, pc3, lb_logits, g_norm.reshape(1, DV_C)]
    if has_state:
        in_specs.append(pl.BlockSpec((bg, hg, DK_C, DV_C), lambda i, h, n: (i, h, 0, 0)))
        args.append(s0)
    return pl.pallas_call(
        functools.partial(_hgrn_kernel, c=c, n_chunks=tb // c, layer=layer, has_state=has_state, hg=hg, bg=bg),
        grid=(b // bg, ng, t // tb),
        in_specs=in_specs,
        out_specs=[pl.BlockSpec((bg, tb, hg * 128), lambda i, h, n: (i, n, h)),
                   pl.BlockSpec((bg, hg, DK_C, DV_C), lambda i, h, n: (i, h, 0, 0))],
        out_shape=[jax.ShapeDtypeStruct((b, t, H_C * DV_C), BF16),
                   jax.ShapeDtypeStruct((b, H_C, DK_C, DV_C), F32)],
        scratch_shapes=[pltpu.VMEM((bg, hg, DK_C, DV_C), F32)],
        compiler_params=_cparams(("parallel", "parallel", "arbitrary")),
        name="hgrn",
    )(*args)


def _mla_prep_kernel(*refs, absorb):
    cq_ref, ckv_ref, kp_ref, qn_ref, kvn_ref, tq_ref, tk_ref, wq_ref = refs[:8]
    if absorb:
        wukt_ref, lat_ref, kr_ref, qlat_ref, qpe_ref = refs[8:]
    else:
        wkv_ref, wkvt_ref, lat_ref, kr_ref, qt_ref, kv_ref, kvt_ref, kp4_ref = refs[8:]
    lat = _rms(ckv_ref[...], kvn_ref[...])
    lat_ref[...] = lat
    kp4 = kp_ref[...] * tk_ref[...]
    kr_ref[...] = kp4[:, 0:D_ROPE] + kp4[:, D_ROPE:2 * D_ROPE]
    cqn = _rms(cq_ref[...], qn_ref[...]).astype(BF16)
    tq = tq_ref[...]
    if absorb:
        for h in range(H_B):
            qh = jnp.dot(cqn, wq_ref[:, h * QW:(h + 1) * QW], preferred_element_type=F32) * tq
            qlat_ref[:, h * KV_LORA:(h + 1) * KV_LORA] = _dot(qh[:, 0:D_NOPE], wukt_ref[h])
            qpe_ref[:, h * D_ROPE:(h + 1) * D_ROPE] = (qh[:, 128:128 + D_ROPE]
                                                      + qh[:, 128 + 2 * D_ROPE:128 + 3 * D_ROPE])
    else:
        nt_dims = (((1,), (1,)), ((), ()))
        for h in range(H_B):
            qh = lax.dot_general(wq_ref[h * QW:(h + 1) * QW, :], cqn, nt_dims, preferred_element_type=F32) * tq
            qt_ref[0, h * QW:(h + 1) * QW, :] = qh.astype(BF16)
        lat_b = lat.astype(BF16)
        kv_ref[...] = jnp.dot(lat_b, wkv_ref[...], preferred_element_type=F32).astype(BF16)
        kvt_ref[0] = lax.dot_general(wkvt_ref[...], lat_b, nt_dims, preferred_element_type=F32).astype(BF16)
        kp4_ref[...] = kp4.astype(BF16)


def _mla_prep(p2, q_norm, kv_norm, tabq, tabk, wq, w_extra, absorb, tm, seq):
    n = p2.shape[0]
    nt = seq // tm if not absorb else 1
    tabq_spec = (pl.BlockSpec((tm, QW), lambda i: (0, 0)) if absorb
                 else pl.BlockSpec((QW, tm), lambda i: (0, i % nt)))
    in_specs = [pl.BlockSpec((tm, Q_LORA), lambda i: (i, 2048 // 256)),
                pl.BlockSpec((tm, KV_LORA), lambda i: (i, 2304 // 256)),
                pl.BlockSpec((tm, 128), lambda i: (i, 2560 // 128)),
                _const_spec((1, Q_LORA)), _const_spec((1, KV_LORA)),
                tabq_spec, pl.BlockSpec((tm, 128), lambda i: (i % nt, 0)),
                _const_spec(wq.shape)] + [_const_spec(w.shape) for w in w_extra]
    row = lambda w: pl.BlockSpec((tm, w), lambda i: (i, 0))
    out_specs = [row(KV_LORA), row(D_ROPE)]
    out_shape = [jax.ShapeDtypeStruct((n, KV_LORA), F32), jax.ShapeDtypeStruct((n, D_ROPE), F32)]
    if absorb:
        out_specs += [row(H_B * KV_LORA), row(H_B * D_ROPE)]
        out_shape += [jax.ShapeDtypeStruct((n, H_B * KV_LORA), F32), jax.ShapeDtypeStruct((n, H_B * D_ROPE), F32)]
    else:
        b = n // seq
        col = lambda w: pl.BlockSpec((1, w, tm), lambda i: (i // nt, 0, i % nt))
        out_specs += [col(H_B * QW), row(H_B * 128), col(H_B * 128), row(128)]
        out_shape += [jax.ShapeDtypeStruct((b, H_B * QW, seq), BF16), jax.ShapeDtypeStruct((n, H_B * 128), BF16),
                      jax.ShapeDtypeStruct((b, H_B * 128, seq), BF16), jax.ShapeDtypeStruct((n, 128), BF16)]
    return pl.pallas_call(
        functools.partial(_mla_prep_kernel, absorb=absorb),
        grid=(n // tm,),
        in_specs=in_specs, out_specs=out_specs, out_shape=out_shape,
        compiler_params=_cparams(("parallel",)),
        name="mla_prep",
    )(p2, p2, p2, q_norm.reshape(1, Q_LORA), kv_norm.reshape(1, KV_LORA), tabq, tabk, wq, *w_extra)


def _flash_kernel(qt_ref, kv_ref, kvt_ref, kp_ref, o_ref, m_scr, l_scr, acc_scr, *, tq):
    qi = pl.program_id(1)
    key = lax.broadcasted_iota(jnp.int32, (tq, tq), 0)
    qry = lax.broadcasted_iota(jnp.int32, (tq, tq), 1)
    m_scr[...] = jnp.full_like(m_scr, NEG_INF)
    l_scr[...] = jnp.zeros_like(l_scr)
    acc_scr[...] = jnp.zeros_like(acc_scr)

    def head_steps(h, k0, kp, masked):
        keys = jnp.concatenate([kv_ref[0, pl.ds(k0, tq), h * 128:(h + 1) * 128], kp], axis=1)
        s = jnp.dot(keys, qt_ref[0, h * QW:(h + 1) * QW, :], preferred_element_type=F32)
        yield None
        if masked:
            s = jnp.where(key <= qry, s, NEG_INF)
        m = m_scr[h]
        m_new = jnp.maximum(m, jnp.max(s, axis=0, keepdims=True))
        alpha = jnp.exp(m - m_new)
        p = jnp.exp(s - m_new)
        l_scr[h] = alpha * l_scr[h] + jnp.sum(p, axis=0, keepdims=True)
        m_scr[h] = m_new
        yield None
        acc_scr[h] = alpha * acc_scr[h] + jnp.dot(kvt_ref[0, h * 128:(h + 1) * 128, pl.ds(k0, tq)],
                                                  p.astype(BF16), preferred_element_type=F32)
        yield True

    def key_tile(k0, masked):
        kp = kp_ref[0, pl.ds(k0, tq), :]
        _run_lockstep([head_steps(h, k0, kp, masked) for h in range(H_B)])

    def body(kt, carry):
        key_tile(pl.multiple_of(kt * tq, tq), False)
        return carry

    lax.fori_loop(0, qi, body, 0)
    key_tile(pl.multiple_of(qi * tq, tq), True)
    o_t = jnp.concatenate([(acc_scr[h] / l_scr[h])[D_NOPE:D_NOPE + DV_B] for h in range(H_B)], axis=0)
    o_ref[0] = o_t.T.astype(o_ref.dtype)


def _flash(qt3, kv3, kvt3, kp3, tq):
    b, t, _ = kv3.shape
    return pl.pallas_call(
        functools.partial(_flash_kernel, tq=tq),
        grid=(b, t // tq),
        in_specs=[pl.BlockSpec((1, H_B * QW, tq), lambda i, j: (i, 0, j)),
                  pl.BlockSpec((1, t, H_B * 128), lambda i, j: (i, 0, 0)),
                  pl.BlockSpec((1, H_B * 128, t), lambda i, j: (i, 0, 0)),
                  pl.BlockSpec((1, t, 128), lambda i, j: (i, 0, 0))],
        out_specs=pl.BlockSpec((1, tq, H_B * DV_B), lambda i, j: (i, j, 0)),
        out_shape=jax.ShapeDtypeStruct((b, t, H_B * DV_B), BF16),
        scratch_shapes=[pltpu.VMEM((H_B, 1, tq), F32), pltpu.VMEM((H_B, 1, tq), F32),
                        pltpu.VMEM((H_B, 128, tq), F32)],
        compiler_params=_cparams(("parallel", "arbitrary")),
        name="mla_flash",
    )(qt3, kv3, kvt3, kp3)


def _decode_kernel(pt_ref, qlat_ref, qpe_ref, latn_ref, krn_ref, wuv_ref, lat_hbm, kct_hbm, o_ref,
                   lat_buf, kct_buf, lat_sem, kct_sem, *, gp, gpe, be, n_slots, t_new):
    step = pl.program_id(0)
    n_steps = pl.num_programs(0)
    nq = H_B * t_new
    groups = [(e, g) for e in range(be) for g in range(gpe)]
    ahead = n_slots // 2

    def page_copies(bi, g, slot):
        copies = []
        for j in range(gp):
            page = pt_ref[bi, g * gp + j]
            copies.append(pltpu.make_async_copy(lat_hbm.at[page], lat_buf.at[slot, pl.ds(j * PAGE, PAGE), :],
                                                lat_sem.at[slot]))
            copies.append(pltpu.make_async_copy(kct_hbm.at[page], kct_buf.at[slot, j], kct_sem.at[slot]))
        return copies

    def start_group(st, idx):
        e, g = groups[idx % len(groups)]
        for cp in page_copies(st * be + e, g, idx % n_slots):
            cp.start()

    @pl.when(step == 0)
    def _():
        for idx in range(ahead):
            start_group(0, idx)

    def group_softmax(s):
        m_g = jnp.max(s, axis=-1, keepdims=True)
        p = jnp.exp(s - m_g)
        return p.astype(BF16), m_g, jnp.sum(p, axis=-1, keepdims=True)

    def merge(state, group):
        m_old, l_old, acc_old = state
        p, m_g, l_g, values = group
        pv = jnp.dot(p, values, preferred_element_type=F32)
        m_new = jnp.maximum(m_old, m_g)
        a_old = jnp.exp(m_old - m_new)
        a_g = jnp.exp(m_g - m_new)
        return m_new, a_old * l_old + a_g * l_g, a_old * acc_old + a_g * pv

    nt_dims = (((1,), (1,)), ((), ()))
    for idx, (e, g) in enumerate(groups):
        slot = idx % n_slots
        for cp in page_copies(step * be + e, g, slot):
            cp.wait()
        nxt = idx + ahead
        if nxt < len(groups):
            start_group(step, nxt)
        else:
            @pl.when(step + 1 < n_steps)
            def _():
                start_group(step + 1, nxt)
        if g == 0:
            ql = jnp.concatenate([qlat_ref[e, :, h * KV_LORA:(h + 1) * KV_LORA] for h in range(H_B)],
                                 axis=0).astype(BF16)
            qp = jnp.concatenate([qpe_ref[e, :, h * D_ROPE:(h + 1) * D_ROPE] for h in range(H_B)],
                                 axis=0).astype(BF16)
            state = (jnp.full((nq, 1), NEG_INF, F32), jnp.zeros((nq, 1), F32), jnp.zeros((nq, KV_LORA), F32))
            pending = None
        lat = lat_buf[slot].astype(BF16)
        kct = jnp.concatenate([kct_buf[slot, j] for j in range(gp)], axis=1).astype(BF16)
        s = lax.dot_general(ql, lat, nt_dims, preferred_element_type=F32) + jnp.dot(qp, kct, preferred_element_type=F32)
        if pending is not None:
            state = merge(state, group_softmax(pending[0]) + (pending[1],))
        pending = (s, lat)
        if g == gpe - 1:
            state = merge(state, group_softmax(s) + (lat,))
            latn = latn_ref[e].astype(BF16)
            krn = krn_ref[e].astype(BF16)
            s = (lax.dot_general(ql, latn, nt_dims, preferred_element_type=F32)
                 + lax.dot_general(qp, krn, nt_dims, preferred_element_type=F32))
            tok = lax.broadcasted_iota(jnp.int32, (nq, t_new), 0) % t_new
            key = lax.broadcasted_iota(jnp.int32, (nq, t_new), 1)
            _, l, acc = merge(state, group_softmax(jnp.where(key <= tok, s, NEG_INF)) + (latn,))
            o_lat = acc / l
            outs = [_dot(o_lat[h * t_new:(h + 1) * t_new], wuv_ref[h]) for h in range(H_B)]
            o_ref[e] = jnp.concatenate(outs, axis=1).astype(o_ref.dtype)


def _decode(page_table, qlat3, qpe3, latn3, krn3, wuv, cache_lat, cache_kpe_t, gp, be):
    b, t_new, _ = qlat3.shape
    gpe = page_table.shape[1] // gp
    n_slots = min(DECODE_SLOTS, be * gpe)
    assert n_slots >= 2 and (be * gpe) % n_slots == 0 and b % be == 0
    per_step = lambda w: pl.BlockSpec((be, t_new, w), lambda i, pt: (i, 0, 0))
    in_specs = [per_step(H_B * KV_LORA), per_step(H_B * D_ROPE), per_step(KV_LORA), per_step(D_ROPE),
                pl.BlockSpec(wuv.shape, lambda i, pt: (0, 0, 0)),
                pl.BlockSpec(memory_space=pl.ANY), pl.BlockSpec(memory_space=pl.ANY)]
    return pl.pallas_call(
        functools.partial(_decode_kernel, gp=gp, gpe=gpe, be=be, n_slots=n_slots, t_new=t_new),
        grid_spec=pltpu.PrefetchScalarGridSpec(
            num_scalar_prefetch=1,
            grid=(b // be,),
            in_specs=in_specs,
            out_specs=pl.BlockSpec((be, t_new, H_B * DV_B), lambda i, pt: (i, 0, 0)),
            scratch_shapes=[pltpu.VMEM((n_slots, gp * PAGE, KV_LORA), F32),
                            pltpu.VMEM((n_slots, gp, D_ROPE, PAGE), F32),
                            pltpu.SemaphoreType.DMA((n_slots,)), pltpu.SemaphoreType.DMA((n_slots,))]),
        out_shape=jax.ShapeDtypeStruct((b, t_new, H_B * DV_B), BF16),
        compiler_params=_cparams(("arbitrary",)),
        name="mla_decode",
    )(page_table, qlat3, qpe3, latn3, krn3, wuv, cache_lat, cache_kpe_t)


def _rot_cols(w):
    half = D_ROPE // 2
    return jnp.concatenate([-w[..., half:], w[..., :half]], axis=-1)


def _layout_w_in_ab(w):
    d = w.shape[0]
    o = QKV_A + H_A * DV_A
    qkv_z, b_w, a_w = w[:, :o], w[:, o:o + H_A], w[:, o + H_A:o + 2 * H_A]
    o += 2 * H_A
    cq, ckv, kpe = w[:, o:o + Q_LORA], w[:, o + Q_LORA:o + Q_LORA + KV_LORA], w[:, o + Q_LORA + KV_LORA:]
    kr = _rot_cols(kpe)
    pad = jnp.zeros((d, 128 - 2 * H_A), w.dtype)
    return jnp.concatenate([qkv_z, cq, ckv, kpe, kr, kpe, kr, b_w, a_w, pad], axis=1).astype(BF16)


def _layout_w_uq(w_uq):
    nope, pe = w_uq[..., :D_NOPE], w_uq[..., D_NOPE:]
    rot = _rot_cols(pe)
    z = jnp.zeros_like(nope)
    return jnp.concatenate([nope, z, pe, pe, rot, rot], axis=-1).reshape(w_uq.shape[0], H_B * QW).astype(BF16)


def _rope_tables(pos):
    half = D_ROPE // 2
    inv = ROPE_BASE ** (-jnp.arange(half, dtype=F32) / half)
    ang = pos.astype(F32)[:, None] * inv
    c = jnp.concatenate([jnp.cos(ang), jnp.cos(ang)], axis=1)
    s = jnp.concatenate([jnp.sin(ang), jnp.sin(ang)], axis=1)
    one = jnp.ones((pos.shape[0], 128), F32)
    tabq = jnp.concatenate([one, c, c, s, s], axis=1) * MLA_SCALE
    tabk = jnp.concatenate([c, s, c, s], axis=1)
    return tabq, tabk


def _row_tile(n):
    for tm in (512, 256, 128, 64, 32, 16, 8):
        if n % tm == 0:
            return tm
    raise ValueError(n)


def _trunk(x, pos, conv0, gdn0, hgrn0, paged, wts):
    b, t, d = x.shape
    n = b * t
    tm = min(ROW_TILE_PREP, _row_tile(n))
    tb = min(TIME_BLOCK, t)
    x2 = x.reshape(n, d)

    tmd = min(ROW_TILE_PROJ, _row_tile(n))
    p2 = _proj(x2, wts["mix_norm"][0], wts["w_in_ab"], tmd)
    p3 = p2.reshape(b, t, IN_AB_PAD)
    conv_new = p3[:, t - (CONV_W - 1):, :QKV_A]
    o_a, gdn_new = _gdn(p3, wts["conv_w"], wts["a_log"], wts["dt_bias"], wts["gdn_norm"], conv0, gdn0, tb,
                        math.gcd(b, GDN_BATCH_PROMPT if gdn0 is None else GDN_BATCH_STATE))
    tabq, tabk = _rope_tables(pos)
    if paged is None:
        tmp = math.gcd(tm, t)
        lat, kr, qt, kv, kvt, kp4 = _mla_prep(p2, wts["q_norm"], wts["kv_norm"], tabq.T, tabk, wts["w_uq"].T,
                                              [wts["w_kv"], wts["w_kv"].T], False, tmp, t)
        o_b = _flash(qt, kv.reshape(b, t, -1), kvt, kp4.reshape(b, t, -1), min(FLASH_TILE, t))
    else:
        cache_lat, cache_kpe, page_table = paged
        reps = tm // t
        lat, kr, qlat, qpe = _mla_prep(p2, wts["q_norm"], wts["kv_norm"], jnp.tile(tabq, (reps, 1)),
                                       jnp.tile(tabk, (reps, 1)), wts["w_uq"], [wts["w_ukt"]], True, tm, t)
        n_pages = page_table.shape[1]
        gp = math.gcd(n_pages // 2, DECODE_GROUP_PAGES)
        o_b = _decode(page_table, qlat.reshape(b, t, -1), qpe.reshape(b, t, -1), lat.reshape(b, t, -1),
                      kr.reshape(b, t, -1), wts["w_uv"], cache_lat, jnp.swapaxes(cache_kpe, 1, 2), gp,
                      math.gcd(b, DECODE_BATCH))
    tmm = min(ROW_TILE_MIX, _row_tile(n))
    x2 = _mix_mlp(x2, [o_a.reshape(n, -1), o_b.reshape(n, -1)], [wts["w_out_a"], wts["w_out_b"]],
                  wts["mlp_norm"][0], wts["w_up"][0], wts["w_down"][0], wts["final_norm"], False, tmm)

    pc = _proj(x2, wts["mix_norm"][1], wts["w_in_c"], tmd)
    o_c, hgrn_new = _hgrn(pc.reshape(b, t, -1), wts["lb_logits"], 1, wts["g_norm_c"], hgrn0, min(HGRN_TIME_BLOCK, t),
                          H_C, 1)
    y2 = _mix_mlp(x2, [o_c.reshape(n, -1)], [wts["w_out_c"]],
                  wts["mlp_norm"][1], wts["w_up"][1], wts["w_down"][1], wts["final_norm"], True, tmm)
    return (y2.reshape(b, t, d), gdn_new[None], conv_new[None], lat.reshape(b, t, -1)[None],
            kr.reshape(b, t, -1)[None], hgrn_new[None])


def kernel(x_prompt, x_sample, state_gdn, state_gdn_conv, cache_mla_latent, cache_mla_krope, state_hgrn,
           page_table, mix_norm, mlp_norm, final_norm, w_up, w_down, w_in_ab, conv_w_ab, a_log_ab, dt_bias_ab,
           gdn_norm_ab, q_norm_ab, w_uq_ab, kv_norm_ab, w_uk_ab, w_uv_ab, w_out_ab, w_in_c, lb_logits_c,
           g_norm_c, w_out_c):
    assert mix_norm.shape[0] == 2 and w_in_ab.shape[0] == 1 and w_in_c.shape[0] == 1
    assert a_log_ab.shape == (1, H_A) and conv_w_ab.shape == (1, CONV_W, QKV_A)
    assert w_uq_ab.shape[1:] == (Q_LORA, H_B, D_NOPE + D_ROPE) and w_uk_ab.shape[1:] == (KV_LORA, H_B, D_NOPE)
    assert cache_mla_latent.shape[2] == PAGE and x_prompt.shape[1] >= CONV_W - 1 and x_sample.shape[1] >= CONV_W - 1
    w_uk, w_uv = w_uk_ab[0], w_uv_ab[0]
    wts = {
        "mix_norm": mix_norm, "mlp_norm": mlp_norm, "final_norm": final_norm,
        "w_up": [w_up[l].astype(BF16) for l in range(w_up.shape[0])],
        "w_down": [w_down[l].astype(BF16) for l in range(w_down.shape[0])],
        "w_in_ab": _layout_w_in_ab(w_in_ab[0]), "conv_w": conv_w_ab[0], "a_log": a_log_ab[0],
        "dt_bias": dt_bias_ab[0], "gdn_norm": gdn_norm_ab[0], "q_norm": q_norm_ab[0], "kv_norm": kv_norm_ab[0],
        "w_uq": _layout_w_uq(w_uq_ab[0]),
        "w_kv": jnp.concatenate([w_uk, w_uv], axis=-1).reshape(KV_LORA, H_B * 128).astype(BF16),
        "w_ukt": jnp.transpose(w_uk, (1, 2, 0)).astype(BF16),
        "w_uv": jnp.transpose(w_uv, (1, 0, 2)).astype(BF16),
        "w_out_a": w_out_ab[0, :H_A * DV_A].astype(BF16), "w_out_b": w_out_ab[0, H_A * DV_A:].astype(BF16),
        "w_in_c": w_in_c[0].astype(BF16), "lb_logits": lb_logits_c, "g_norm_c": g_norm_c[0],
        "w_out_c": w_out_c[0].astype(BF16),
    }
    s_len = x_prompt.shape[1]
    outs_p = _trunk(x_prompt, jnp.arange(s_len), None, None, None, None, wts)
    past_len = page_table.shape[1] * cache_mla_latent.shape[2]
    outs_s = _trunk(x_sample, past_len + jnp.arange(x_sample.shape[1]),
                    state_gdn_conv.reshape(state_gdn_conv.shape[1:]), state_gdn.reshape(state_gdn.shape[1:]),
                    state_hgrn.reshape(state_hgrn.shape[1:]),
                    (cache_mla_latent.reshape(cache_mla_latent.shape[1:]),
                     cache_mla_krope.reshape(cache_mla_krope.shape[1:]), page_table), wts)
    return (outs_p[0], outs_s[0]) + outs_p[1:] + outs_s[1:]
```

```python
import functools
import math

import jax
import jax.numpy as jnp
from jax import lax
from jax.experimental import pallas as pl
from jax.experimental.pallas import tpu as pltpu

F32 = jnp.float32
BF16 = jnp.bfloat16
EPS = 1e-6
NEG_INF = float("-inf")
LOG2_E = 1.0 / math.log(2.0)

H_A, DK_A, DV_A, CONV_W = 4, 128, 128, 4
QKV_A = H_A * (2 * DK_A + DV_A)
H_B, D_NOPE, D_ROPE, DV_B = 8, 64, 32, 64
Q_LORA, KV_LORA = 256, 256
ROPE_BASE = 10000.0
MLA_SCALE = (D_NOPE + D_ROPE) ** -0.5
H_C, DK_C, DV_C = 8, 128, 128
CHUNK = 64
SUB = 16
PAGE = 128
QW = 256
IN_AB_PAD = 2816

V7X_VMEM_LIMIT = 48 * 1024 * 1024
ROW_TILE_PROJ = 512
ROW_TILE_MIX = 512
ROW_TILE_PREP = 256
TIME_BLOCK = 256
HGRN_TIME_BLOCK = 512
FLASH_TILE = 512
DECODE_GROUP_PAGES = 32
DECODE_BATCH = 2
DECODE_SLOTS = 4
GDN_BATCH_PROMPT, GDN_BATCH_STATE = 2, 4
HGRN_LOCKSTEP = 4


def _cparams(sem):
    return pltpu.CompilerParams(dimension_semantics=sem, vmem_limit_bytes=V7X_VMEM_LIMIT)


def _const_spec(shape):
    nd = len(shape)
    return pl.BlockSpec(shape, lambda *_: (0,) * nd)


def _dot(a, b):
    return jnp.dot(a.astype(BF16), b.astype(BF16), preferred_element_type=F32)


def _dot_nt(a, b):
    return lax.dot_general(a.astype(BF16), b.astype(BF16), (((1,), (1,)), ((), ())),
                           preferred_element_type=F32)


def _dot_tn(a, b):
    return lax.dot_general(a.astype(BF16), b.astype(BF16), (((0,), (0,)), ((), ())),
                           preferred_element_type=F32)


def _rms(x, g):
    return x * lax.rsqrt(jnp.mean(x * x, axis=-1, keepdims=True) + EPS) * g


def _silu(x):
    return x * jax.nn.sigmoid(x)


def _cumsum_rows(x):
    n = x.shape[0]
    row = lax.broadcasted_iota(jnp.int32, x.shape, 0)
    s = 1
    while s < n:
        x = x + jnp.where(row >= s, pltpu.roll(x, s, axis=0), 0.0)
        s *= 2
    return x


def _proj_kernel(x_ref, g_ref, w_ref, o_ref):
    h = _rms(x_ref[...], g_ref[...]).astype(BF16)
    o_ref[...] = jnp.dot(h, w_ref[...], preferred_element_type=F32)


def _proj(x2, g, w, tm):
    n, d = x2.shape
    nout = w.shape[1]
    return pl.pallas_call(
        _proj_kernel,
        grid=(n // tm,),
        in_specs=[pl.BlockSpec((tm, d), lambda i: (i, 0)), _const_spec((1, d)), _const_spec((d, nout))],
        out_specs=pl.BlockSpec((tm, nout), lambda i: (i, 0)),
        out_shape=jax.ShapeDtypeStruct((n, nout), F32),
        compiler_params=_cparams(("parallel",)),
        name="proj",
    )(x2, g.reshape(1, d), w)


def _mix_mlp_kernel(*refs, n_in, final_norm):
    r_ref, g_ref, gf_ref = refs[:3]
    a_refs = refs[3:3 + n_in]
    w_refs = refs[3 + n_in:3 + 2 * n_in]
    wu_ref, wd_ref, o_ref = refs[3 + 2 * n_in:]
    x = r_ref[...]
    for a_ref, w_ref in zip(a_refs, w_refs):
        x = x + jnp.dot(a_ref[...], w_ref[...], preferred_element_type=F32)
    h = _rms(x, g_ref[...]).astype(BF16)
    a = jnp.square(jnp.maximum(jnp.dot(h, wu_ref[...], preferred_element_type=F32), 0.0))
    y = x + jnp.dot(a.astype(BF16), wd_ref[...], preferred_element_type=F32)
    o_ref[...] = _rms(y, gf_ref[...]) if final_norm else y


def _mix_mlp(res, acts, ws, g, w_up, w_down, gf, final_norm, tm):
    n, d = res.shape
    n_in = len(acts)
    resident = lambda w: pl.BlockSpec(w.shape, lambda i: (0,) * w.ndim, pipeline_mode=pl.Buffered(1))
    in_specs = [pl.BlockSpec((tm, d), lambda i: (i, 0)), _const_spec((1, d)), _const_spec((1, d))]
    in_specs += [pl.BlockSpec((tm, a.shape[1]), lambda i: (i, 0)) for a in acts]
    in_specs += [resident(w) for w in ws] + [resident(w_up), resident(w_down)]
    return pl.pallas_call(
        functools.partial(_mix_mlp_kernel, n_in=n_in, final_norm=final_norm),
        grid=(n // tm,),
        in_specs=in_specs,
        out_specs=pl.BlockSpec((tm, d), lambda i: (i, 0)),
        out_shape=jax.ShapeDtypeStruct((n, d), F32),
        compiler_params=_cparams(("parallel",)),
        name="mix_mlp",
    )(res, g.reshape(1, d), gf.reshape(1, d), *acts, *ws, w_up, w_down)


def _hi_lo(a):
    hi = a.astype(BF16)
    return hi, (a - hi.astype(F32)).astype(BF16)


def _split3_dot(a_parts, b_parts):
    (a_hi, a_lo), (b_hi, b_lo) = a_parts, b_parts
    return jnp.dot(jnp.concatenate([a_hi, a_lo, a_hi], axis=1), jnp.concatenate([b_hi, b_hi, b_lo], axis=0),
                   preferred_element_type=F32)


def _pick_dot(a, pick):
    a_hi = a.astype(BF16)
    a_lo = (a - a_hi.astype(F32)).astype(BF16)
    pick = pick.astype(BF16)
    return jnp.dot(jnp.concatenate([a_hi, a_lo], axis=1), jnp.concatenate([pick, pick], axis=0),
                   preferred_element_type=F32)


def _solve_unit_lower_steps(lowers, rhss, c, block_mask):
    nh = len(lowers)
    s = 8
    nb = c // s
    w = nh * c
    row = lax.broadcasted_iota(jnp.int32, (c, w), 0)
    col = lax.broadcasted_iota(jnp.int32, (c, w), 1) % c
    if nb == 1:
        diag_cols = lowers
    else:
        r1 = lax.broadcasted_iota(jnp.int32, (c, c), 0)
        c1 = lax.broadcasted_iota(jnp.int32, (c, c), 1)
        pick = (lax.broadcasted_iota(jnp.int32, (c, s), 0) % s
                == lax.broadcasted_iota(jnp.int32, (c, s), 1)).astype(F32)
        diag_cols = [_pick_dot(jnp.where(r1 // s == c1 // s, lo, 0.0), pick) for lo in lowers]
    yield None
    x = (row == col).astype(F32)
    for j in range(s - 1):
        mult = jnp.concatenate([jnp.broadcast_to(dc[:, j:j + 1], (c, c)) for dc in diag_cols], axis=1)
        pivot_rows = jnp.broadcast_to(x.reshape(nb, s, w)[:, j:j + 1, :], (nb, s, w)).reshape(c, w)
        x = x - mult * pivot_rows
        yield None

    def block_diag(m):
        return tuple(jnp.concatenate([part] * nh, axis=0) * block_mask for part in _hi_lo(m))

    lower_cat = jnp.concatenate(lowers, axis=1)
    size = s
    while size < c:
        off = jnp.where((row // (2 * size) == col // (2 * size)) & (row // size != col // size) & (row > col),
                        lower_cat, 0.0)
        y = _split3_dot(_hi_lo(x), block_diag(off))
        yield None
        x = x - _split3_dot(_hi_lo(y), block_diag(x))
        yield None
        size *= 2
    out = _split3_dot(block_diag(x), _hi_lo(jnp.concatenate(rhss, axis=0)))
    yield [out[h * c:(h + 1) * c] for h in range(nh)]


def _run_lockstep(gens, group=None):
    if group is not None and group < len(gens):
        return [val for g0 in range(0, len(gens), group) for val in _run_lockstep(gens[g0:g0 + group])]
    results = [None] * len(gens)
    active = list(range(len(gens)))
    while active:
        for i in list(active):
            try:
                val = next(gens[i])
                if val is not None:
                    results[i] = val
            except StopIteration:
                active.remove(i)
    return results


def _gdn_head_pre(q, k, v, b_col, a_col, alog, dtb, c):
    q = q * lax.rsqrt(jnp.sum(q * q, axis=-1, keepdims=True) + EPS) * (DK_A ** -0.5)
    k = k * lax.rsqrt(jnp.sum(k * k, axis=-1, keepdims=True) + EPS)
    beta = jax.nn.sigmoid(b_col)
    g = -jnp.exp(alog) * jax.nn.softplus(a_col + dtb)
    gc_b = _cumsum_rows(jnp.broadcast_to(g, (c, 128)))
    gc = gc_b[:, 0:1]
    diff = gc_b[:, :c] - gc_b.T[:c, :]
    row = lax.broadcasted_iota(jnp.int32, (c, c), 0)
    col = lax.broadcasted_iota(jnp.int32, (c, c), 1)
    incl = row >= col
    decay = jnp.where(incl, jnp.exp(jnp.where(incl, diff, 0.0)), 0.0)
    kb = k * beta
    lower = jnp.where(row > col, _dot_nt(kb, k) * decay, 0.0)
    rhs = jnp.concatenate([v * beta, kb * jnp.exp(gc)], axis=1)
    intra = _dot_nt(q, k) * decay
    return q, k, gc, lower, rhs, intra


def _gdn_head_post_steps(q, k, gc, uw, intra, z, gnorm, s_prev, c):
    u, w = uw[:, :DV_A], uw[:, DV_A:]
    qs = _dot(q * jnp.exp(gc), s_prev)
    v_new = u - _dot(w, s_prev)
    yield None
    g_last = gc[c - 1:c, :]
    s_new = s_prev * jnp.exp(g_last) + _dot_tn(k * jnp.exp(g_last - gc), v_new)
    o = qs + _dot(intra, v_new)
    yield None
    yield _rms(o, gnorm) * _silu(z), s_new


def _gdn_kernel(*refs, c, n_chunks, has_state, bg):
    (q_ref, k_ref, v_ref, z_ref, gt_ref, cwq_ref, cwk_ref, cwv_ref, alog_ref, dtb_ref, gn_ref) = refs[:11]
    if has_state:
        cq_ref, ck_ref, cv_ref, s0_ref = refs[11:15]
        o_ref, sout_ref, s_scr, tail_scr = refs[15:]
    else:
        o_ref, sout_ref, s_scr, tail_scr = refs[11:]
    n = pl.program_id(1)

    @pl.when(n == 0)
    def _():
        if has_state:
            s_scr[...] = s0_ref[...]
            tail_scr[0] = cq_ref[...]
            tail_scr[1] = ck_ref[...]
            tail_scr[2] = cv_ref[...]
        else:
            s_scr[...] = jnp.zeros_like(s_scr)
            tail_scr[...] = jnp.zeros_like(tail_scr)

    pre = {}
    for ci in range(n_chunks):
        rows = pl.ds(ci * c, c)
        for bi in range(bg):
            conv = []
            for idx, (x_ref, cw_ref) in enumerate(((q_ref, cwq_ref), (k_ref, cwk_ref), (v_ref, cwv_ref))):
                x = x_ref[bi, rows, :]
                prev = tail_scr[idx, bi] if ci == 0 else x_ref[bi, pl.ds(ci * c - 8, 8), :]
                ext = jnp.concatenate([prev, x], axis=0)
                cw = cw_ref[...]
                y = x * cw[CONV_W - 1:CONV_W, :]
                for sh in range(1, CONV_W):
                    y = y + pltpu.roll(ext, sh, axis=0)[8:] * cw[CONV_W - 1 - sh:CONV_W - sh, :]
                conv.append(_silu(y))
            gt = gt_ref[bi, rows, :]
            for h in range(H_A):
                hs = slice(h * 128, (h + 1) * 128)
                pre[ci, bi, h] = _gdn_head_pre(conv[0][:, hs], conv[1][:, hs], conv[2][:, hs],
                                               gt[:, h:h + 1], gt[:, H_A + h:H_A + h + 1],
                                               alog_ref[:, h:h + 1], dtb_ref[:, h:h + 1], c)
    last = pl.ds(n_chunks * c - 8, 8)
    for idx, x_ref in enumerate((q_ref, k_ref, v_ref)):
        tail_scr[idx] = x_ref[:, last, :]

    per_solve = max(H_A, min(bg * H_A, 128 // c))
    keys = [(ci, bi, h) for ci in range(n_chunks) for bi in range(bg) for h in range(H_A)]
    groups = [keys[g0:g0 + per_solve] for g0 in range(0, len(keys), per_solve)]
    w = per_solve * c
    block_mask = (lax.broadcasted_iota(jnp.int32, (w, w), 0) // c
                  == lax.broadcasted_iota(jnp.int32, (w, w), 1) // c).astype(BF16)
    solved = _run_lockstep([_solve_unit_lower_steps([pre[k][3] for k in grp], [pre[k][4] for k in grp], c, block_mask)
                            for grp in groups])
    uws = {k: uw for grp, res in zip(groups, solved) for k, uw in zip(grp, res)}

    for ci in range(n_chunks):
        rows = pl.ds(ci * c, c)
        heads = [(bi, h) for bi in range(bg) for h in range(H_A)]
        res = _run_lockstep([_gdn_head_post_steps(pre[ci, bi, h][0], pre[ci, bi, h][1], pre[ci, bi, h][2],
                                                  uws[ci, bi, h], pre[ci, bi, h][5],
                                                  z_ref[bi, rows, h * 128:(h + 1) * 128], gn_ref[...],
                                                  s_scr[bi, h], c) for bi, h in heads])
        for (bi, h), (_, s_new) in zip(heads, res):
            s_scr[bi, h] = s_new
        for bi in range(bg):
            o_ref[bi, rows, :] = jnp.concatenate([res[bi * H_A + h][0] for h in range(H_A)],
                                                 axis=1).astype(o_ref.dtype)

    @pl.when(n == pl.num_programs(1) - 1)
    def _():
        sout_ref[...] = s_scr[...]


def _gdn(p3, conv_w, a_log, dt_bias, gdn_norm, conv0, s0, tb, bg):
    b, t, _ = p3.shape
    c = math.gcd(t, CHUNK)
    has_state = s0 is not None
    hw = H_A * 128
    cw = conv_w
    blk = lambda j: pl.BlockSpec((bg, tb, hw), lambda i, n: (i, n, j))
    in_specs = [blk(0), blk(1), blk(2), blk(3),
                pl.BlockSpec((bg, tb, 128), lambda i, n: (i, n, (IN_AB_PAD - 128) // 128)),
                pl.BlockSpec((CONV_W, hw), lambda i, n: (0, 0)), pl.BlockSpec((CONV_W, hw), lambda i, n: (0, 1)),
                pl.BlockSpec((CONV_W, hw), lambda i, n: (0, 2)),
                _const_spec((1, H_A)), _const_spec((1, H_A)), _const_spec((1, DV_A))]
    args = [p3, p3, p3, p3, p3, cw, cw, cw, a_log.reshape(1, H_A), dt_bias.reshape(1, H_A),
            gdn_norm.reshape(1, DV_A)]
    if has_state:
        conv8 = jnp.pad(conv0, ((0, 0), (8 - (CONV_W - 1), 0), (0, 0)))
        in_specs += [pl.BlockSpec((bg, 8, hw), lambda i, n, j=j: (i, 0, j)) for j in range(3)]
        in_specs += [pl.BlockSpec((bg, H_A, DK_A, DV_A), lambda i, n: (i, 0, 0, 0))]
        args += [conv8, conv8, conv8, s0]
    return pl.pallas_call(
        functools.partial(_gdn_kernel, c=c, n_chunks=tb // c, has_state=has_state, bg=bg),
        grid=(b // bg, t // tb),
        in_specs=in_specs,
        out_specs=[pl.BlockSpec((bg, tb, hw), lambda i, n: (i, n, 0)),
                   pl.BlockSpec((bg, H_A, DK_A, DV_A), lambda i, n: (i, 0, 0, 0))],
        out_shape=[jax.ShapeDtypeStruct((b, t, hw), BF16),
                   jax.ShapeDtypeStruct((b, H_A, DK_A, DV_A), F32)],
        scratch_shapes=[pltpu.VMEM((bg, H_A, DK_A, DV_A), F32), pltpu.VMEM((3, bg, 8, hw), F32)],
        compiler_params=_cparams(("parallel", "arbitrary")),
        name="gdn",
    )(*args)


def _hgrn_chunk_steps(qr, fr, v, gate, gnorm, lb, s_prev, c):
    sub = min(SUB, c)
    q = _silu(qr) * (DK_C ** -0.5)
    f = lb + (1.0 - lb) * jax.nn.sigmoid(fr)
    k = 1.0 - f
    gc = _cumsum_rows(jnp.log(f))
    o_inter = _dot(q * jnp.exp(gc), s_prev)
    gc2 = gc * LOG2_E
    kg2 = gc2 - jnp.log2(k)
    yield None
    row = lax.broadcasted_iota(jnp.int32, (sub, sub), 0)
    col = lax.broadcasted_iota(jnp.int32, (sub, sub), 1)
    outs = []
    for blk in range(c // sub):
        r0 = blk * sub
        g_i = gc[r0:r0 + sub]
        g2_i = gc2[r0:r0 + sub]
        q_i = q[r0:r0 + sub]
        a_parts = [jnp.zeros((8, sub), F32) for _ in range(sub // 8)]
        for j in range(sub):
            jj = r0 + j
            lo = 8 * (j // 8)
            ke = jnp.exp2(g2_i[lo:] - kg2[jj:jj + 1, :])
            col_j = jnp.sum(q_i[lo:] * ke, axis=-1, keepdims=True)
            for p in range(j // 8, sub // 8):
                a_parts[p] = jnp.where(col[:8] == j, col_j[8 * p - lo:8 * p - lo + 8], a_parts[p])
        a_diag = jnp.concatenate(a_parts, axis=0) if len(a_parts) > 1 else a_parts[0]
        a_diag = jnp.where(row >= col, a_diag, 0.0)
        o_i = _dot(a_diag, v[r0:r0 + sub])
        if blk > 0:
            g_ref = gc[r0 - 1:r0, :]
            a_off = _dot_nt(q_i * jnp.exp(g_i - g_ref), k[:r0] * jnp.exp(g_ref - gc[:r0]))
            o_i = o_i + _dot(a_off, v[:r0])
        outs.append(o_i)
        yield None
    o = o_inter + (jnp.concatenate(outs, axis=0) if len(outs) > 1 else outs[0])
    g_last = gc[c - 1:c, :]
    gl_col = gc.T[:, c - 1:c]
    s_new = s_prev * jnp.exp(gl_col) + _dot_tn(k * jnp.exp(g_last - gc), v)
    yield None
    yield _rms(o, gnorm) * _silu(gate), s_new


def _hgrn_kernel(*refs, c, n_chunks, layer, has_state, hg, bg):
    q_ref, f_ref, i_ref, gate_ref, lbl_ref, gn_ref = refs[:6]
    if has_state:
        s0_ref = refs[6]
        o_ref, sout_ref, s_scr = refs[7:]
    else:
        o_ref, sout_ref, s_scr = refs[6:]
    n = pl.program_id(2)

    @pl.when(n == 0)
    def _():
        if has_state:
            s_scr[...] = s0_ref[...]
        else:
            s_scr[...] = jnp.zeros_like(s_scr)

    lbl = lbl_ref[...]
    ex = jnp.exp(lbl - jnp.max(lbl, axis=0, keepdims=True))
    p = ex / jnp.sum(ex, axis=0, keepdims=True)
    lb = jnp.sum(p[:layer + 1], axis=0, keepdims=True) - p[0:1]

    def chunk(ci, carry):
        r0 = ci * c if isinstance(ci, int) else pl.multiple_of(ci * c, c)
        rows = pl.ds(r0, c)
        heads = [(bi, h) for bi in range(bg) for h in range(hg)]
        res = _run_lockstep([_hgrn_chunk_steps(q_ref[bi, rows, h * 128:(h + 1) * 128],
                                               f_ref[bi, rows, h * 128:(h + 1) * 128],
                                               i_ref[bi, rows, h * 128:(h + 1) * 128],
                                               gate_ref[bi, rows, h * 128:(h + 1) * 128], gn_ref[...],
                                               lb[:, h * 128:(h + 1) * 128], s_scr[bi, h], c) for bi, h in heads],
                            group=len(heads) if c < CHUNK else HGRN_LOCKSTEP)
        for (bi, h), (o, s_new) in zip(heads, res):
            s_scr[bi, h] = s_new
            o_ref[bi, rows, h * 128:(h + 1) * 128] = o.astype(o_ref.dtype)
        return carry

    if n_chunks == 1:
        chunk(0, 0)
    else:
        lax.fori_loop(0, n_chunks, chunk, 0)

    @pl.when(n == pl.num_programs(2) - 1)
    def _():
        sout_ref[...] = s_scr[...]


def _hgrn(pc3, lb_logits, layer, g_norm, s0, tb, hg, bg):
    b, t, _ = pc3.shape
    c = math.gcd(t, CHUNK)
    has_state = s0 is not None
    depth = lb_logits.shape[0]
    ng = H_C // hg
    blk = lambda j: pl.BlockSpec((bg, tb, hg * 128), lambda i, h, n: (i, n, j * ng + h))
    in_specs = [blk(0), blk(1), blk(2), blk(3),
                pl.BlockSpec((depth, hg * 128), lambda i, h, n: (0, h)), _const_spec((1, DV_C))]
    args = [pc3, pc3, pc3, pc3, lb_logits, g_norm.reshape(1, DV_C)]
    if has_state:
        in_specs.append(pl.BlockSpec((bg, hg, DK_C, DV_C), lambda i, h, n: (i, h, 0, 0)))
        args.append(s0)
    return pl.pallas_call(
        functools.partial(_hgrn_kernel, c=c, n_chunks=tb // c, layer=layer, has_state=has_state, hg=hg, bg=bg),
        grid=(b // bg, ng, t // tb),
        in_specs=in_specs,
        out_specs=[pl.BlockSpec((bg, tb, hg * 128), lambda i, h, n: (i, n, h)),
                   pl.BlockSpec((bg, hg, DK_C, DV_C), lambda i, h, n: (i, h, 0, 0))],
        out_shape=[jax.ShapeDtypeStruct((b, t, H_C * DV_C), BF16),
                   jax.ShapeDtypeStruct((b, H_C, DK_C, DV_C), F32)],
        scratch_shapes=[pltpu.VMEM((bg, hg, DK_C, DV_C), F32)],
        compiler_params=_cparams(("parallel", "parallel", "arbitrary")),
        name="hgrn",
    )(*args)


def _mla_prep_kernel(*refs, absorb):
    cq_ref, ckv_ref, kp_ref, qn_ref, kvn_ref, tq_ref, tk_ref, wq_ref = refs[:8]
    if absorb:
        wukt_ref, lat_ref, kr_ref, qlat_ref, qpe_ref = refs[8:]
    else:
        wkv_ref, wkvt_ref, lat_ref, kr_ref, qt_ref, kv_ref, kvt_ref, kp4_ref = refs[8:]
    lat = _rms(ckv_ref[...], kvn_ref[...])
    lat_ref[...] = lat
    kp4 = kp_ref[...] * tk_ref[...]
    kr_ref[...] = kp4[:, 0:D_ROPE] + kp4[:, D_ROPE:2 * D_ROPE]
    cqn = _rms(cq_ref[...], qn_ref[...]).astype(BF16)
    tq = tq_ref[...]
    if absorb:
        for h in range(H_B):
            qh = jnp.dot(cqn, wq_ref[:, h * QW:(h + 1) * QW], preferred_element_type=F32) * tq
            qlat_ref[:, h * KV_LORA:(h + 1) * KV_LORA] = _dot(qh[:, 0:D_NOPE], wukt_ref[h])
            qpe_ref[:, h * D_ROPE:(h + 1) * D_ROPE] = (qh[:, 128:128 + D_ROPE]
                                                      + qh[:, 128 + 2 * D_ROPE:128 + 3 * D_ROPE])
    else:
        nt_dims = (((1,), (1,)), ((), ()))
        for h in range(H_B):
            qh = lax.dot_general(wq_ref[h * QW:(h + 1) * QW, :], cqn, nt_dims, preferred_element_type=F32) * tq
            qt_ref[0, h * QW:(h + 1) * QW, :] = qh.astype(BF16)
        lat_b = lat.astype(BF16)
        kv_ref[...] = jnp.dot(lat_b, wkv_ref[...], preferred_element_type=F32).astype(BF16)
        kvt_ref[0] = lax.dot_general(wkvt_ref[...], lat_b, nt_dims, preferred_element_type=F32).astype(BF16)
        kp4_ref[...] = kp4.astype(BF16)


def _mla_prep(p2, q_norm, kv_norm, tabq, tabk, wq, w_extra, absorb, tm, seq):
    n = p2.shape[0]
    nt = seq // tm if not absorb else 1
    tabq_spec = (pl.BlockSpec((tm, QW), lambda i: (0, 0)) if absorb
                 else pl.BlockSpec((QW, tm), lambda i: (0, i % nt)))
    in_specs = [pl.BlockSpec((tm, Q_LORA), lambda i: (i, 2048 // 256)),
                pl.BlockSpec((tm, KV_LORA), lambda i: (i, 2304 // 256)),
                pl.BlockSpec((tm, 128), lambda i: (i, 2560 // 128)),
                _const_spec((1, Q_LORA)), _const_spec((1, KV_LORA)),
                tabq_spec, pl.BlockSpec((tm, 128), lambda i: (i % nt, 0)),
                _const_spec(wq.shape)] + [_const_spec(w.shape) for w in w_extra]
    row = lambda w: pl.BlockSpec((tm, w), lambda i: (i, 0))
    out_specs = [row(KV_LORA), row(D_ROPE)]
    out_shape = [jax.ShapeDtypeStruct((n, KV_LORA), F32), jax.ShapeDtypeStruct((n, D_ROPE), F32)]
    if absorb:
        out_specs += [row(H_B * KV_LORA), row(H_B * D_ROPE)]
        out_shape += [jax.ShapeDtypeStruct((n, H_B * KV_LORA), F32), jax.ShapeDtypeStruct((n, H_B * D_ROPE), F32)]
    else:
        b = n // seq
        col = lambda w: pl.BlockSpec((1, w, tm), lambda i: (i // nt, 0, i % nt))
        out_specs += [col(H_B * QW), row(H_B * 128), col(H_B * 128), row(128)]
        out_shape += [jax.ShapeDtypeStruct((b, H_B * QW, seq), BF16), jax.ShapeDtypeStruct((n, H_B * 128), BF16),
                      jax.ShapeDtypeStruct((b, H_B * 128, seq), BF16), jax.ShapeDtypeStruct((n, 128), BF16)]
    return pl.pallas_call(
        functools.partial(_mla_prep_kernel, absorb=absorb),
        grid=(n // tm,),
        in_specs=in_specs, out_specs=out_specs, out_shape=out_shape,
        compiler_params=_cparams(("parallel",)),
        name="mla_prep",
    )(p2, p2, p2, q_norm.reshape(1, Q_LORA), kv_norm.reshape(1, KV_LORA), tabq, tabk, wq, *w_extra)


def _flash_kernel(qt_ref, kv_ref, kvt_ref, kp_ref, o_ref, m_scr, l_scr, acc_scr, *, tq):
    qi = pl.program_id(1)
    key = lax.broadcasted_iota(jnp.int32, (tq, tq), 0)
    qry = lax.broadcasted_iota(jnp.int32, (tq, tq), 1)
    m_scr[...] = jnp.full_like(m_scr, NEG_INF)
    l_scr[...] = jnp.zeros_like(l_scr)
    acc_scr[...] = jnp.zeros_like(acc_scr)

    def head_steps(h, k0, kp, masked):
        keys = jnp.concatenate([kv_ref[0, pl.ds(k0, tq), h * 128:(h + 1) * 128], kp], axis=1)
        s = jnp.dot(keys, qt_ref[0, h * QW:(h + 1) * QW, :], preferred_element_type=F32)
        yield None
        if masked:
            s = jnp.where(key <= qry, s, NEG_INF)
        m = m_scr[h]
        m_new = jnp.maximum(m, jnp.max(s, axis=0, keepdims=True))
        alpha = jnp.exp(m - m_new)
        p = jnp.exp(s - m_new)
        l_scr[h] = alpha * l_scr[h] + jnp.sum(p, axis=0, keepdims=True)
        m_scr[h] = m_new
        yield None
        acc_scr[h] = alpha * acc_scr[h] + jnp.dot(kvt_ref[0, h * 128:(h + 1) * 128, pl.ds(k0, tq)],
                                                  p.astype(BF16), preferred_element_type=F32)
        yield True

    def key_tile(k0, masked):
        kp = kp_ref[0, pl.ds(k0, tq), :]
        _run_lockstep([head_steps(h, k0, kp, masked) for h in range(H_B)])

    def body(kt, carry):
        key_tile(pl.multiple_of(kt * tq, tq), False)
        return carry

    lax.fori_loop(0, qi, body, 0)
    key_tile(pl.multiple_of(qi * tq, tq), True)
    o_t = jnp.concatenate([(acc_scr[h] / l_scr[h])[D_NOPE:D_NOPE + DV_B] for h in range(H_B)], axis=0)
    o_ref[0] = o_t.T.astype(o_ref.dtype)


def _flash(qt3, kv3, kvt3, kp3, tq):
    b, t, _ = kv3.shape
    return pl.pallas_call(
        functools.partial(_flash_kernel, tq=tq),
        grid=(b, t // tq),
        in_specs=[pl.BlockSpec((1, H_B * QW, tq), lambda i, j: (i, 0, j)),
                  pl.BlockSpec((1, t, H_B * 128), lambda i, j: (i, 0, 0)),
                  pl.BlockSpec((1, H_B * 128, t), lambda i, j: (i, 0, 0)),
                  pl.BlockSpec((1, t, 128), lambda i, j: (i, 0, 0))],
        out_specs=pl.BlockSpec((1, tq, H_B * DV_B), lambda i, j: (i, j, 0)),
        out_shape=jax.ShapeDtypeStruct((b, t, H_B * DV_B), BF16),
        scratch_shapes=[pltpu.VMEM((H_B, 1, tq), F32), pltpu.VMEM((H_B, 1, tq), F32),
                        pltpu.VMEM((H_B, 128, tq), F32)],
        compiler_params=_cparams(("parallel", "arbitrary")),
        name="mla_flash",
    )(qt3, kv3, kvt3, kp3)


def _decode_kernel(pt_ref, qlat_ref, qpe_ref, latn_ref, krn_ref, wuv_ref, lat_hbm, kct_hbm, o_ref,
                   lat_buf, kct_buf, lat_sem, kct_sem, *, gp, gpe, be, n_slots, t_new):
    step = pl.program_id(0)
    n_steps = pl.num_programs(0)
    nq = H_B * t_new
    groups = [(e, g) for e in range(be) for g in range(gpe)]
    ahead = n_slots // 2

    def page_copies(bi, g, slot):
        copies = []
        for j in range(gp):
            page = pt_ref[bi, g * gp + j]
            copies.append(pltpu.make_async_copy(lat_hbm.at[page], lat_buf.at[slot, pl.ds(j * PAGE, PAGE), :],
                                                lat_sem.at[slot]))
            copies.append(pltpu.make_async_copy(kct_hbm.at[page], kct_buf.at[slot, j], kct_sem.at[slot]))
        return copies

    def start_group(st, idx):
        e, g = groups[idx % len(groups)]
        for cp in page_copies(st * be + e, g, idx % n_slots):
            cp.start()

    @pl.when(step == 0)
    def _():
        for idx in range(ahead):
            start_group(0, idx)

    def group_softmax(s):
        m_g = jnp.max(s, axis=-1, keepdims=True)
        p = jnp.exp(s - m_g)
        return p.astype(BF16), m_g, jnp.sum(p, axis=-1, keepdims=True)

    def merge(state, group):
        m_old, l_old, acc_old = state
        p, m_g, l_g, values = group
        pv = jnp.dot(p, values, preferred_element_type=F32)
        m_new = jnp.maximum(m_old, m_g)
        a_old = jnp.exp(m_old - m_new)
        a_g = jnp.exp(m_g - m_new)
        return m_new, a_old * l_old + a_g * l_g, a_old * acc_old + a_g * pv

    nt_dims = (((1,), (1,)), ((), ()))
    for idx, (e, g) in enumerate(groups):
        slot = idx % n_slots
        for cp in page_copies(step * be + e, g, slot):
            cp.wait()
        nxt = idx + ahead
        if nxt < len(groups):
            start_group(step, nxt)
        else:
            @pl.when(step + 1 < n_steps)
            def _():
                start_group(step + 1, nxt)
        if g == 0:
            ql = jnp.concatenate([qlat_ref[e, :, h * KV_LORA:(h + 1) * KV_LORA] for h in range(H_B)],
                                 axis=0).astype(BF16)
            qp = jnp.concatenate([qpe_ref[e, :, h * D_ROPE:(h + 1) * D_ROPE] for h in range(H_B)],
                                 axis=0).astype(BF16)
            state = (jnp.full((nq, 1), NEG_INF, F32), jnp.zeros((nq, 1), F32), jnp.zeros((nq, KV_LORA), F32))
            pending = None
        lat = lat_buf[slot].astype(BF16)
        kct = jnp.concatenate([kct_buf[slot, j] for j in range(gp)], axis=1).astype(BF16)
        s = lax.dot_general(ql, lat, nt_dims, preferred_element_type=F32) + jnp.dot(qp, kct, preferred_element_type=F32)
        if pending is not None:
            state = merge(state, group_softmax(pending[0]) + (pending[1],))
        pending = (s, lat)
        if g == gpe - 1:
            state = merge(state, group_softmax(s) + (lat,))
            latn = latn_ref[e].astype(BF16)
            krn = krn_ref[e].astype(BF16)
            s = (lax.dot_general(ql, latn, nt_dims, preferred_element_type=F32)
                 + lax.dot_general(qp, krn, nt_dims, preferred_element_type=F32))
            tok = lax.broadcasted_iota(jnp.int32, (nq, t_new), 0) % t_new
            key = lax.broadcasted_iota(jnp.int32, (nq, t_new), 1)
            _, l, acc = merge(state, group_softmax(jnp.where(key <= tok, s, NEG_INF)) + (latn,))
            o_lat = acc / l
            outs = [_dot(o_lat[h * t_new:(h + 1) * t_new], wuv_ref[h]) for h in range(H_B)]
            o_ref[e] = jnp.concatenate(outs, axis=1).astype(o_ref.dtype)


def _decode(page_table, qlat3, qpe3, latn3, krn3, wuv, cache_lat, cache_kpe_t, gp, be):
    b, t_new, _ = qlat3.shape
    gpe = page_table.shape[1] // gp
    n_slots = min(DECODE_SLOTS, be * gpe)
    assert n_slots >= 2 and (be * gpe) % n_slots == 0 and b % be == 0
    per_step = lambda w: pl.BlockSpec((be, t_new, w), lambda i, pt: (i, 0, 0))
    in_specs = [per_step(H_B * KV_LORA), per_step(H_B * D_ROPE), per_step(KV_LORA), per_step(D_ROPE),
                pl.BlockSpec(wuv.shape, lambda i, pt: (0, 0, 0)),
                pl.BlockSpec(memory_space=pl.ANY), pl.BlockSpec(memory_space=pl.ANY)]
    return pl.pallas_call(
        functools.partial(_decode_kernel, gp=gp, gpe=gpe, be=be, n_slots=n_slots, t_new=t_new),
        grid_spec=pltpu.PrefetchScalarGridSpec(
            num_scalar_prefetch=1,
            grid=(b // be,),
            in_specs=in_specs,
            out_specs=pl.BlockSpec((be, t_new, H_B * DV_B), lambda i, pt: (i, 0, 0)),
            scratch_shapes=[pltpu.VMEM((n_slots, gp * PAGE, KV_LORA), F32),
                            pltpu.VMEM((n_slots, gp, D_ROPE, PAGE), F32),
                            pltpu.SemaphoreType.DMA((n_slots,)), pltpu.SemaphoreType.DMA((n_slots,))]),
        out_shape=jax.ShapeDtypeStruct((b, t_new, H_B * DV_B), BF16),
        compiler_params=_cparams(("arbitrary",)),
        name="mla_decode",
    )(page_table, qlat3, qpe3, latn3, krn3, wuv, cache_lat, cache_kpe_t)


def _rot_cols(w):
    half = D_ROPE // 2
    return jnp.concatenate([-w[..., half:], w[..., :half]], axis=-1)


def _layout_w_in_ab(w):
    d = w.shape[0]
    o = QKV_A + H_A * DV_A
    qkv_z, b_w, a_w = w[:, :o], w[:, o:o + H_A], w[:, o + H_A:o + 2 * H_A]
    o += 2 * H_A
    cq, ckv, kpe = w[:, o:o + Q_LORA], w[:, o + Q_LORA:o + Q_LORA + KV_LORA], w[:, o + Q_LORA + KV_LORA:]
    kr = _rot_cols(kpe)
    pad = jnp.zeros((d, 128 - 2 * H_A), w.dtype)
    return jnp.concatenate([qkv_z, cq, ckv, kpe, kr, kpe, kr, b_w, a_w, pad], axis=1).astype(BF16)


def _layout_w_uq(w_uq):
    nope, pe = w_uq[..., :D_NOPE], w_uq[..., D_NOPE:]
    rot = _rot_cols(pe)
    z = jnp.zeros_like(nope)
    return jnp.concatenate([nope, z, pe, pe, rot, rot], axis=-1).reshape(w_uq.shape[0], H_B * QW).astype(BF16)


def _rope_tables(pos):
    half = D_ROPE // 2
    inv = ROPE_BASE ** (-jnp.arange(half, dtype=F32) / half)
    ang = pos.astype(F32)[:, None] * inv
    c = jnp.concatenate([jnp.cos(ang), jnp.cos(ang)], axis=1)
    s = jnp.concatenate([jnp.sin(ang), jnp.sin(ang)], axis=1)
    one = jnp.ones((pos.shape[0], 128), F32)
    tabq = jnp.concatenate([one, c, c, s, s], axis=1) * MLA_SCALE
    tabk = jnp.concatenate([c, s, c, s], axis=1)
    return tabq, tabk


def _row_tile(n):
    for tm in (512, 256, 128, 64, 32, 16, 8):
        if n % tm == 0:
            return tm
    raise ValueError(n)


def _trunk(x, pos, conv0, gdn0, hgrn0, paged, wts):
    b, t, d = x.shape
    n = b * t
    tm = min(ROW_TILE_PREP, _row_tile(n))
    tb = min(TIME_BLOCK, t)
    x2 = x.reshape(n, d)

    tmd = min(ROW_TILE_PROJ, _row_tile(n))
    p2 = _proj(x2, wts["mix_norm"][0], wts["w_in_ab"], tmd)
    p3 = p2.reshape(b, t, IN_AB_PAD)
    conv_new = p3[:, t - (CONV_W - 1):, :QKV_A]
    o_a, gdn_new = _gdn(p3, wts["conv_w"], wts["a_log"], wts["dt_bias"], wts["gdn_norm"], conv0, gdn0, tb,
                        math.gcd(b, GDN_BATCH_PROMPT if gdn0 is None else GDN_BATCH_STATE))
    tabq, tabk = _rope_tables(pos)
    if paged is None:
        tmp = math.gcd(tm, t)
        lat, kr, qt, kv, kvt, kp4 = _mla_prep(p2, wts["q_norm"], wts["kv_norm"], tabq.T, tabk, wts["w_uq"].T,
                                              [wts["w_kv"], wts["w_kv"].T], False, tmp, t)
        o_b = _flash(qt, kv.reshape(b, t, -1), kvt, kp4.reshape(b, t, -1), min(FLASH_TILE, t))
    else:
        cache_lat, cache_kpe, page_table = paged
        reps = tm // t
        lat, kr, qlat, qpe = _mla_prep(p2, wts["q_norm"], wts["kv_norm"], jnp.tile(tabq, (reps, 1)),
                                       jnp.tile(tabk, (reps, 1)), wts["w_uq"], [wts["w_ukt"]], True, tm, t)
        n_pages = page_table.shape[1]
        gp = math.gcd(n_pages // 2, DECODE_GROUP_PAGES)
        o_b = _decode(page_table, qlat.reshape(b, t, -1), qpe.reshape(b, t, -1), lat.reshape(b, t, -1),
                      kr.reshape(b, t, -1), wts["w_uv"], cache_lat, jnp.swapaxes(cache_kpe, 1, 2), gp,
                      math.gcd(b, DECODE_BATCH))
    tmm = min(ROW_TILE_MIX, _row_tile(n))
    x2 = _mix_mlp(x2, [o_a.reshape(n, -1), o_b.reshape(n, -1)], [wts["w_out_a"], wts["w_out_b"]],
                  wts["mlp_norm"][0], wts["w_up"][0], wts["w_down"][0], wts["final_norm"], False, tmm)

    pc = _proj(x2, wts["mix_norm"][1], wts["w_in_c"], tmd)
    o_c, hgrn_new = _hgrn(pc.reshape(b, t, -1), wts["lb_logits"], 1, wts["g_norm_c"], hgrn0, min(HGRN_TIME_BLOCK, t),
                          H_C, 1)
    y2 = _mix_mlp(x2, [o_c.reshape(n, -1)], [wts["w_out_c"]],
                  wts["mlp_norm"][1], wts["w_up"][1], wts["w_down"][1], wts["final_norm"], True, tmm)
    return (y2.reshape(b, t, d), gdn_new[None], conv_new[None], lat.reshape(b, t, -1)[None],
            kr.reshape(b, t, -1)[None], hgrn_new[None])


def kernel(x_prompt, x_sample, state_gdn, state_gdn_conv, cache_mla_latent, cache_mla_krope, state_hgrn,
           page_table, mix_norm, mlp_norm, final_norm, w_up, w_down, w_in_ab, conv_w_ab, a_log_ab, dt_bias_ab,
           gdn_norm_ab, q_norm_ab, w_uq_ab, kv_norm_ab, w_uk_ab, w_uv_ab, w_out_ab, w_in_c, lb_logits_c,
           g_norm_c, w_out_c):
    assert mix_norm.shape[0] == 2 and w_in_ab.shape[0] == 1 and w_in_c.shape[0] == 1
    assert a_log_ab.shape == (1, H_A) and conv_w_ab.shape == (1, CONV_W, QKV_A)
    assert w_uq_ab.shape[1:] == (Q_LORA, H_B, D_NOPE + D_ROPE) and w_uk_ab.shape[1:] == (KV_LORA, H_B, D_NOPE)
    assert cache_mla_latent.shape[2] == PAGE and x_prompt.shape[1] >= CONV_W - 1 and x_sample.shape[1] >= CONV_W - 1
    w_uk, w_uv = w_uk_ab[0], w_uv_ab[0]
    wts = {
        "mix_norm": mix_norm, "mlp_norm": mlp_norm, "final_norm": final_norm,
        "w_up": [w_up[l].astype(BF16) for l in range(w_up.shape[0])],
        "w_down": [w_down[l].astype(BF16) for l in range(w_down.shape[0])],
        "w_in_ab": _layout_w_in_ab(w_in_ab[0]), "conv_w": conv_w_ab[0], "a_log": a_log_ab[0],
        "dt_bias": dt_bias_ab[0], "gdn_norm": gdn_norm_ab[0], "q_norm": q_norm_ab[0], "kv_norm": kv_norm_ab[0],
        "w_uq": _layout_w_uq(w_uq_ab[0]),
        "w_kv": jnp.concatenate([w_uk, w_uv], axis=-1).reshape(KV_LORA, H_B * 128).astype(BF16),
        "w_ukt": jnp.transpose(w_uk, (1, 2, 0)).astype(BF16),
        "w_uv": jnp.transpose(w_uv, (1, 0, 2)).astype(BF16),
        "w_out_a": w_out_ab[0, :H_A * DV_A].astype(BF16), "w_out_b": w_out_ab[0, H_A * DV_A:].astype(BF16),
        "w_in_c": w_in_c[0].astype(BF16), "lb_logits": lb_logits_c, "g_norm_c": g_norm_c[0],
        "w_out_c": w_out_c[0].astype(BF16),
    }
    s_len = x_prompt.shape[1]
    outs_p = _trunk(x_prompt, jnp.arange(s_len), None, None, None, None, wts)
    past_len = page_table.shape[1] * cache_mla_latent.shape[2]
    outs_s = _trunk(x_sample, past_len + jnp.arange(x_sample.shape[1]),
                    state_gdn_conv.reshape(state_gdn_conv.shape[1:]), state_gdn.reshape(state_gdn.shape[1:]),
                    state_hgrn.reshape(state_hgrn.shape[1:]),
                    (cache_mla_latent.reshape(cache_mla_latent.shape[1:]),
                     cache_mla_krope.reshape(cache_mla_krope.shape[1:]), page_table), wts)
    return (outs_p[0], outs_s[0]) + outs_p[1:] + outs_s[1:]
```

```python
import functools
import math

import jax
import jax.numpy as jnp
from jax import lax
from jax.experimental import pallas as pl
from jax.experimental.pallas import tpu as pltpu

F32 = jnp.float32
BF16 = jnp.bfloat16
EPS = 1e-6
NEG_INF = float("-inf")
LOG2_E = 1.0 / math.log(2.0)

H_A, DK_A, DV_A, CONV_W = 4, 128, 128, 4
QKV_A = H_A * (2 * DK_A + DV_A)
H_B, D_NOPE, D_ROPE, DV_B = 8, 64, 32, 64
Q_LORA, KV_LORA = 256, 256
ROPE_BASE = 10000.0
MLA_SCALE = (D_NOPE + D_ROPE) ** -0.5
H_C, DK_C, DV_C = 8, 128, 128
CHUNK = 64
SUB = 16
PAGE = 128
QW = 256
IN_AB_PAD = 2816

V7X_VMEM_LIMIT = 48 * 1024 * 1024
ROW_TILE_PROJ = 512
ROW_TILE_MIX = 512
ROW_TILE_MIX_PROJ = 256
ROW_TILE_PREP = 256
TIME_BLOCK = 256
HGRN_TIME_BLOCK = 512
FLASH_TILE = 512
DECODE_GROUP_PAGES = 32
DECODE_BATCH = 2
DECODE_SLOTS = 4
GDN_BATCH_PROMPT, GDN_BATCH_STATE = 2, 4
HGRN_LOCKSTEP = 4


def _cparams(sem):
    return pltpu.CompilerParams(dimension_semantics=sem, vmem_limit_bytes=V7X_VMEM_LIMIT)


def _const_spec(shape):
    nd = len(shape)
    return pl.BlockSpec(shape, lambda *_: (0,) * nd)


def _dot(a, b):
    return jnp.dot(a.astype(BF16), b.astype(BF16), preferred_element_type=F32)


def _dot_nt(a, b):
    return lax.dot_general(a.astype(BF16), b.astype(BF16), (((1,), (1,)), ((), ())),
                           preferred_element_type=F32)


def _dot_tn(a, b):
    return lax.dot_general(a.astype(BF16), b.astype(BF16), (((0,), (0,)), ((), ())),
                           preferred_element_type=F32)


def _rms(x, g):
    return x * lax.rsqrt(jnp.mean(x * x, axis=-1, keepdims=True) + EPS) * g


def _silu(x):
    return x * jax.nn.sigmoid(x)


def _cumsum_rows(x):
    n = x.shape[0]
    row = lax.broadcasted_iota(jnp.int32, x.shape, 0)
    s = 1
    while s < n:
        x = x + jnp.where(row >= s, pltpu.roll(x, s, axis=0), 0.0)
        s *= 2
    return x


def _proj_kernel(x_ref, g_ref, w_ref, o_ref):
    h = _rms(x_ref[...], g_ref[...]).astype(BF16)
    o_ref[...] = jnp.dot(h, w_ref[...], preferred_element_type=F32)


def _proj(x2, g, w, tm):
    n, d = x2.shape
    nout = w.shape[1]
    return pl.pallas_call(
        _proj_kernel,
        grid=(n // tm,),
        in_specs=[pl.BlockSpec((tm, d), lambda i: (i, 0)), _const_spec((1, d)), _const_spec((d, nout))],
        out_specs=pl.BlockSpec((tm, nout), lambda i: (i, 0)),
        out_shape=jax.ShapeDtypeStruct((n, nout), F32),
        compiler_params=_cparams(("parallel",)),
        name="proj",
    )(x2, g.reshape(1, d), w)


def _mix_mlp_kernel(*refs, n_in, final_norm, next_proj):
    r_ref, g_ref, gf_ref = refs[:3]
    a_refs = refs[3:3 + n_in]
    w_refs = refs[3 + n_in:3 + 2 * n_in]
    rest = refs[3 + 2 * n_in:]
    if next_proj:
        wu_ref, wd_ref, gn_ref, wn_ref, o_ref, p_ref = rest
    else:
        wu_ref, wd_ref, o_ref = rest
    x = r_ref[...]
    for a_ref, w_ref in zip(a_refs, w_refs):
        x = x + jnp.dot(a_ref[...], w_ref[...], preferred_element_type=F32)
    h = _rms(x, g_ref[...]).astype(BF16)
    a = jnp.square(jnp.maximum(jnp.dot(h, wu_ref[...], preferred_element_type=F32), 0.0))
    y = x + jnp.dot(a.astype(BF16), wd_ref[...], preferred_element_type=F32)
    o_ref[...] = _rms(y, gf_ref[...]) if final_norm else y
    if next_proj:
        p_ref[...] = jnp.dot(_rms(y, gn_ref[...]).astype(BF16), wn_ref[...], preferred_element_type=F32)


def _mix_mlp(res, acts, ws, g, w_up, w_down, gf, final_norm, tm, next_norm=None, next_w=None):
    n, d = res.shape
    n_in = len(acts)
    next_proj = next_w is not None
    resident = lambda w: pl.BlockSpec(w.shape, lambda i: (0,) * w.ndim, pipeline_mode=pl.Buffered(1))
    in_specs = [pl.BlockSpec((tm, d), lambda i: (i, 0)), _const_spec((1, d)), _const_spec((1, d))]
    in_specs += [pl.BlockSpec((tm, a.shape[1]), lambda i: (i, 0)) for a in acts]
    in_specs += [resident(w) for w in ws] + [resident(w_up), resident(w_down)]
    args = [res, g.reshape(1, d), gf.reshape(1, d), *acts, *ws, w_up, w_down]
    out_specs = pl.BlockSpec((tm, d), lambda i: (i, 0))
    out_shape = jax.ShapeDtypeStruct((n, d), F32)
    if next_proj:
        in_specs += [_const_spec((1, d)), resident(next_w)]
        args += [next_norm.reshape(1, d), next_w]
        out_specs = [out_specs, pl.BlockSpec((tm, next_w.shape[1]), lambda i: (i, 0))]
        out_shape = [out_shape, jax.ShapeDtypeStruct((n, next_w.shape[1]), F32)]
    return pl.pallas_call(
        functools.partial(_mix_mlp_kernel, n_in=n_in, final_norm=final_norm, next_proj=next_proj),
        grid=(n // tm,),
        in_specs=in_specs,
        out_specs=out_specs,
        out_shape=out_shape,
        compiler_params=_cparams(("parallel",)),
        name="mix_mlp",
    )(*args)


def _hi_lo(a):
    hi = a.astype(BF16)
    return hi, (a - hi.astype(F32)).astype(BF16)


def _split3_dot(a_parts, b_parts):
    (a_hi, a_lo), (b_hi, b_lo) = a_parts, b_parts
    return jnp.dot(jnp.concatenate([a_hi, a_lo, a_hi], axis=1), jnp.concatenate([b_hi, b_hi, b_lo], axis=0),
                   preferred_element_type=F32)


def _pick_dot(a, pick):
    a_hi = a.astype(BF16)
    a_lo = (a - a_hi.astype(F32)).astype(BF16)
    pick = pick.astype(BF16)
    return jnp.dot(jnp.concatenate([a_hi, a_lo], axis=1), jnp.concatenate([pick, pick], axis=0),
                   preferred_element_type=F32)


def _solve_unit_lower_steps(lowers, rhss, c, block_mask):
    nh = len(lowers)
    s = 8
    nb = c // s
    w = nh * c
    row = lax.broadcasted_iota(jnp.int32, (c, w), 0)
    col = lax.broadcasted_iota(jnp.int32, (c, w), 1) % c
    if nb == 1:
        diag_cols = lowers
    else:
        r1 = lax.broadcasted_iota(jnp.int32, (c, c), 0)
        c1 = lax.broadcasted_iota(jnp.int32, (c, c), 1)
        pick = (lax.broadcasted_iota(jnp.int32, (c, s), 0) % s
                == lax.broadcasted_iota(jnp.int32, (c, s), 1)).astype(F32)
        diag_cols = [_pick_dot(jnp.where(r1 // s == c1 // s, lo, 0.0), pick) for lo in lowers]
    yield None
    x = (row == col).astype(F32)
    for j in range(s - 1):
        mult = jnp.concatenate([jnp.broadcast_to(dc[:, j:j + 1], (c, c)) for dc in diag_cols], axis=1)
        pivot_rows = jnp.broadcast_to(x.reshape(nb, s, w)[:, j:j + 1, :], (nb, s, w)).reshape(c, w)
        x = x - mult * pivot_rows
        yield None

    def block_diag(m):
        return tuple(jnp.concatenate([part] * nh, axis=0) * block_mask for part in _hi_lo(m))

    lower_cat = jnp.concatenate(lowers, axis=1)
    size = s
    while size < c:
        off = jnp.where((row // (2 * size) == col // (2 * size)) & (row // size != col // size) & (row > col),
                        lower_cat, 0.0)
        y = _split3_dot(_hi_lo(x), block_diag(off))
        yield None
        x = x - _split3_dot(_hi_lo(y), block_diag(x))
        yield None
        size *= 2
    out = _split3_dot(block_diag(x), _hi_lo(jnp.concatenate(rhss, axis=0)))
    yield [out[h * c:(h + 1) * c] for h in range(nh)]


def _run_lockstep(gens, group=None):
    if group is not None and group < len(gens):
        return [val for g0 in range(0, len(gens), group) for val in _run_lockstep(gens[g0:g0 + group])]
    results = [None] * len(gens)
    active = list(range(len(gens)))
    while active:
        for i in list(active):
            try:
                val = next(gens[i])
                if val is not None:
                    results[i] = val
            except StopIteration:
                active.remove(i)
    return results


def _gdn_head_pre(q, k, v, b_col, a_col, alog, dtb, c):
    q = q * lax.rsqrt(jnp.sum(q * q, axis=-1, keepdims=True) + EPS) * (DK_A ** -0.5)
    k = k * lax.rsqrt(jnp.sum(k * k, axis=-1, keepdims=True) + EPS)
    beta = jax.nn.sigmoid(b_col)
    g = -jnp.exp(alog) * jax.nn.softplus(a_col + dtb)
    gc_b = _cumsum_rows(jnp.broadcast_to(g, (c, 128)))
    gc = gc_b[:, 0:1]
    diff = gc_b[:, :c] - gc_b.T[:c, :]
    row = lax.broadcasted_iota(jnp.int32, (c, c), 0)
    col = lax.broadcasted_iota(jnp.int32, (c, c), 1)
    incl = row >= col
    decay = jnp.where(incl, jnp.exp(jnp.where(incl, diff, 0.0)), 0.0)
    kb = k * beta
    lower = jnp.where(row > col, _dot_nt(kb, k) * decay, 0.0)
    rhs = jnp.concatenate([v * beta, kb * jnp.exp(gc)], axis=1)
    intra = _dot_nt(q, k) * decay
    return q, k, gc, lower, rhs, intra


def _gdn_head_post_steps(q, k, gc, uw, intra, z, gnorm, s_prev, c):
    u, w = uw[:, :DV_A], uw[:, DV_A:]
    qs = _dot(q * jnp.exp(gc), s_prev)
    v_new = u - _dot(w, s_prev)
    yield None
    g_last = gc[c - 1:c, :]
    s_new = s_prev * jnp.exp(g_last) + _dot_tn(k * jnp.exp(g_last - gc), v_new)
    o = qs + _dot(intra, v_new)
    yield None
    yield _rms(o, gnorm) * _silu(z), s_new


def _gdn_kernel(*refs, c, n_chunks, has_state, bg):
    (q_ref, k_ref, v_ref, z_ref, gt_ref, cwq_ref, cwk_ref, cwv_ref, alog_ref, dtb_ref, gn_ref) = refs[:11]
    if has_state:
        cq_ref, ck_ref, cv_ref, s0_ref = refs[11:15]
        o_ref, sout_ref, s_scr, tail_scr = refs[15:]
    else:
        o_ref, sout_ref, s_scr, tail_scr = refs[11:]
    n = pl.program_id(1)

    @pl.when(n == 0)
    def _():
        if has_state:
            s_scr[...] = s0_ref[...]
            tail_scr[0] = cq_ref[...]
            tail_scr[1] = ck_ref[...]
            tail_scr[2] = cv_ref[...]
        else:
            s_scr[...] = jnp.zeros_like(s_scr)
            tail_scr[...] = jnp.zeros_like(tail_scr)

    pre = {}
    for ci in range(n_chunks):
        rows = pl.ds(ci * c, c)
        for bi in range(bg):
            conv = []
            for idx, (x_ref, cw_ref) in enumerate(((q_ref, cwq_ref), (k_ref, cwk_ref), (v_ref, cwv_ref))):
                x = x_ref[bi, rows, :]
                prev = tail_scr[idx, bi] if ci == 0 else x_ref[bi, pl.ds(ci * c - 8, 8), :]
                ext = jnp.concatenate([prev, x], axis=0)
                cw = cw_ref[...]
                y = x * cw[CONV_W - 1:CONV_W, :]
                for sh in range(1, CONV_W):
                    y = y + pltpu.roll(ext, sh, axis=0)[8:] * cw[CONV_W - 1 - sh:CONV_W - sh, :]
                conv.append(_silu(y))
            gt = gt_ref[bi, rows, :]
            for h in range(H_A):
                hs = slice(h * 128, (h + 1) * 128)
                pre[ci, bi, h] = _gdn_head_pre(conv[0][:, hs], conv[1][:, hs], conv[2][:, hs],
                                               gt[:, h:h + 1], gt[:, H_A + h:H_A + h + 1],
                                               alog_ref[:, h:h + 1], dtb_ref[:, h:h + 1], c)
    last = pl.ds(n_chunks * c - 8, 8)
    for idx, x_ref in enumerate((q_ref, k_ref, v_ref)):
        tail_scr[idx] = x_ref[:, last, :]

    per_solve = max(H_A, min(bg * H_A, 128 // c))
    keys = [(ci, bi, h) for ci in range(n_chunks) for bi in range(bg) for h in range(H_A)]
    groups = [keys[g0:g0 + per_solve] for g0 in range(0, len(keys), per_solve)]
    w = per_solve * c
    block_mask = (lax.broadcasted_iota(jnp.int32, (w, w), 0) // c
                  == lax.broadcasted_iota(jnp.int32, (w, w), 1) // c).astype(BF16)
    solved = _run_lockstep([_solve_unit_lower_steps([pre[k][3] for k in grp], [pre[k][4] for k in grp], c, block_mask)
                            for grp in groups])
    uws = {k: uw for grp, res in zip(groups, solved) for k, uw in zip(grp, res)}

    for ci in range(n_chunks):
        rows = pl.ds(ci * c, c)
        heads = [(bi, h) for bi in range(bg) for h in range(H_A)]
        res = _run_lockstep([_gdn_head_post_steps(pre[ci, bi, h][0], pre[ci, bi, h][1], pre[ci, bi, h][2],
                                                  uws[ci, bi, h], pre[ci, bi, h][5],
                                                  z_ref[bi, rows, h * 128:(h + 1) * 128], gn_ref[...],
                                                  s_scr[bi, h], c) for bi, h in heads])
        for (bi, h), (_, s_new) in zip(heads, res):
            s_scr[bi, h] = s_new
        for bi in range(bg):
            o_ref[bi, rows, :] = jnp.concatenate([res[bi * H_A + h][0] for h in range(H_A)],
                                                 axis=1).astype(o_ref.dtype)

    @pl.when(n == pl.num_programs(1) - 1)
    def _():
        sout_ref[...] = s_scr[...]


def _gdn(p3, conv_w, a_log, dt_bias, gdn_norm, conv0, s0, tb, bg):
    b, t, _ = p3.shape
    c = math.gcd(t, CHUNK)
    has_state = s0 is not None
    hw = H_A * 128
    cw = conv_w
    blk = lambda j: pl.BlockSpec((bg, tb, hw), lambda i, n: (i, n, j))
    in_specs = [blk(0), blk(1), blk(2), blk(3),
                pl.BlockSpec((bg, tb, 128), lambda i, n: (i, n, (IN_AB_PAD - 128) // 128)),
                pl.BlockSpec((CONV_W, hw), lambda i, n: (0, 0)), pl.BlockSpec((CONV_W, hw), lambda i, n: (0, 1)),
                pl.BlockSpec((CONV_W, hw), lambda i, n: (0, 2)),
                _const_spec((1, H_A)), _const_spec((1, H_A)), _const_spec((1, DV_A))]
    args = [p3, p3, p3, p3, p3, cw, cw, cw, a_log.reshape(1, H_A), dt_bias.reshape(1, H_A),
            gdn_norm.reshape(1, DV_A)]
    if has_state:
        conv8 = jnp.pad(conv0, ((0, 0), (8 - (CONV_W - 1), 0), (0, 0)))
        in_specs += [pl.BlockSpec((bg, 8, hw), lambda i, n, j=j: (i, 0, j)) for j in range(3)]
        in_specs += [pl.BlockSpec((bg, H_A, DK_A, DV_A), lambda i, n: (i, 0, 0, 0))]
        args += [conv8, conv8, conv8, s0]
    return pl.pallas_call(
        functools.partial(_gdn_kernel, c=c, n_chunks=tb // c, has_state=has_state, bg=bg),
        grid=(b // bg, t // tb),
        in_specs=in_specs,
        out_specs=[pl.BlockSpec((bg, tb, hw), lambda i, n: (i, n, 0)),
                   pl.BlockSpec((bg, H_A, DK_A, DV_A), lambda i, n: (i, 0, 0, 0))],
        out_shape=[jax.ShapeDtypeStruct((b, t, hw), BF16),
                   jax.ShapeDtypeStruct((b, H_A, DK_A, DV_A), F32)],
        scratch_shapes=[pltpu.VMEM((bg, H_A, DK_A, DV_A), F32), pltpu.VMEM((3, bg, 8, hw), F32)],
        compiler_params=_cparams(("parallel", "arbitrary")),
        name="gdn",
    )(*args)


def _hgrn_chunk_steps(qr, fr, v, gate, gnorm, lb, s_prev, c):
    sub = min(SUB, c)
    q = _silu(qr) * (DK_C ** -0.5)
    f = lb + (1.0 - lb) * jax.nn.sigmoid(fr)
    k = 1.0 - f
    gc = _cumsum_rows(jnp.log(f))
    o_inter = _dot(q * jnp.exp(gc), s_prev)
    gc2 = gc * LOG2_E
    kg2 = gc2 - jnp.log2(k)
    yield None
    row = lax.broadcasted_iota(jnp.int32, (sub, sub), 0)
    col = lax.broadcasted_iota(jnp.int32, (sub, sub), 1)
    outs = []
    for blk in range(c // sub):
        r0 = blk * sub
        g_i = gc[r0:r0 + sub]
        g2_i = gc2[r0:r0 + sub]
        q_i = q[r0:r0 + sub]
        a_parts = [jnp.zeros((8, sub), F32) for _ in range(sub // 8)]
        for j in range(sub):
            jj = r0 + j
            lo = 8 * (j // 8)
            ke = jnp.exp2(g2_i[lo:] - kg2[jj:jj + 1, :])
            col_j = jnp.sum(q_i[lo:] * ke, axis=-1, keepdims=True)
            for p in range(j // 8, sub // 8):
                a_parts[p] = jnp.where(col[:8] == j, col_j[8 * p - lo:8 * p - lo + 8], a_parts[p])
        a_diag = jnp.concatenate(a_parts, axis=0) if len(a_parts) > 1 else a_parts[0]
        a_diag = jnp.where(row >= col, a_diag, 0.0)
        o_i = _dot(a_diag, v[r0:r0 + sub])
        if blk > 0:
            g_ref = gc[r0 - 1:r0, :]
            a_off = _dot_nt(q_i * jnp.exp(g_i - g_ref), k[:r0] * jnp.exp(g_ref - gc[:r0]))
            o_i = o_i + _dot(a_off, v[:r0])
        outs.append(o_i)
        yield None
    o = o_inter + (jnp.concatenate(outs, axis=0) if len(outs) > 1 else outs[0])
    g_last = gc[c - 1:c, :]
    gl_col = gc.T[:, c - 1:c]
    s_new = s_prev * jnp.exp(gl_col) + _dot_tn(k * jnp.exp(g_last - gc), v)
    yield None
    yield _rms(o, gnorm) * _silu(gate), s_new


def _hgrn_kernel(*refs, c, n_chunks, layer, has_state, hg, bg):
    q_ref, f_ref, i_ref, gate_ref, lbl_ref, gn_ref = refs[:6]
    if has_state:
        s0_ref = refs[6]
        o_ref, sout_ref, s_scr = refs[7:]
    else:
        o_ref, sout_ref, s_scr = refs[6:]
    n = pl.program_id(2)

    @pl.when(n == 0)
    def _():
        if has_state:
            s_scr[...] = s0_ref[...]
        else:
            s_scr[...] = jnp.zeros_like(s_scr)

    lbl = lbl_ref[...]
    ex = jnp.exp(lbl - jnp.max(lbl, axis=0, keepdims=True))
    p = ex / jnp.sum(ex, axis=0, keepdims=True)
    lb = jnp.sum(p[:layer + 1], axis=0, keepdims=True) - p[0:1]

    def chunk(ci, carry):
        r0 = ci * c if isinstance(ci, int) else pl.multiple_of(ci * c, c)
        rows = pl.ds(r0, c)
        heads = [(bi, h) for bi in range(bg) for h in range(hg)]
        res = _run_lockstep([_hgrn_chunk_steps(q_ref[bi, rows, h * 128:(h + 1) * 128],
                                               f_ref[bi, rows, h * 128:(h + 1) * 128],
                                               i_ref[bi, rows, h * 128:(h + 1) * 128],
                                               gate_ref[bi, rows, h * 128:(h + 1) * 128], gn_ref[...],
                                               lb[:, h * 128:(h + 1) * 128], s_scr[bi, h], c) for bi, h in heads],
                            group=len(heads) if c < CHUNK else HGRN_LOCKSTEP)
        for (bi, h), (o, s_new) in zip(heads, res):
            s_scr[bi, h] = s_new
            o_ref[bi, rows, h * 128:(h + 1) * 128] = o.astype(o_ref.dtype)
        return carry

    if n_chunks == 1:
        chunk(0, 0)
    else:
        lax.fori_loop(0, n_chunks, chunk, 0)

    @pl.when(n == pl.num_programs(2) - 1)
    def _():
        sout_ref[...] = s_scr[...]


def _hgrn(pc3, lb_logits, layer, g_norm, s0, tb, hg, bg):
    b, t, _ = pc3.shape
    c = math.gcd(t, CHUNK)
    has_state = s0 is not None
    depth = lb_logits.shape[0]
    ng = H_C // hg
    blk = lambda j: pl.BlockSpec((bg, tb, hg * 128), lambda i, h, n: (i, n, j * ng + h))
    in_specs = [blk(0), blk(1), blk(2), blk(3),
                pl.BlockSpec((depth, hg * 128), lambda i, h, n: (0, h)), _const_spec((1, DV_C))]
    args = [pc3, pc3, pc3, pc3, lb_logits, g_norm.reshape(1, DV_C)]
    if has_state:
        in_specs.append(pl.BlockSpec((bg, hg, DK_C, DV_C), lambda i, h, n: (i, h, 0, 0)))
        args.append(s0)
    return pl.pallas_call(
        functools.partial(_hgrn_kernel, c=c, n_chunks=tb // c, layer=layer, has_state=has_state, hg=hg, bg=bg),
        grid=(b // bg, ng, t // tb),
        in_specs=in_specs,
        out_specs=[pl.BlockSpec((bg, tb, hg * 128), lambda i, h, n: (i, n, h)),
                   pl.BlockSpec((bg, hg, DK_C, DV_C), lambda i, h, n: (i, h, 0, 0))],
        out_shape=[jax.ShapeDtypeStruct((b, t, H_C * DV_C), BF16),
                   jax.ShapeDtypeStruct((b, H_C, DK_C, DV_C), F32)],
        scratch_shapes=[pltpu.VMEM((bg, hg, DK_C, DV_C), F32)],
        compiler_params=_cparams(("parallel", "parallel", "arbitrary")),
        name="hgrn",
    )(*args)


def _mla_prep_kernel(*refs, absorb):
    cq_ref, ckv_ref, kp_ref, qn_ref, kvn_ref, tq_ref, tk_ref, wq_ref = refs[:8]
    if absorb:
        wukt_ref, lat_ref, kr_ref, qlat_ref, qpe_ref = refs[8:]
    else:
        wkv_ref, wkvt_ref, lat_ref, kr_ref, qt_ref, kv_ref, kvt_ref, kp4_ref = refs[8:]
    lat = _rms(ckv_ref[...], kvn_ref[...])
    lat_ref[...] = lat
    kp4 = kp_ref[...] * tk_ref[...]
    kr_ref[...] = kp4[:, 0:D_ROPE] + kp4[:, D_ROPE:2 * D_ROPE]
    cqn = _rms(cq_ref[...], qn_ref[...]).astype(BF16)
    tq = tq_ref[...]
    if absorb:
        for h in range(H_B):
            qh = jnp.dot(cqn, wq_ref[:, h * QW:(h + 1) * QW], preferred_element_type=F32) * tq
            qlat_ref[:, h * KV_LORA:(h + 1) * KV_LORA] = _dot(qh[:, 0:D_NOPE], wukt_ref[h])
            qpe_ref[:, h * D_ROPE:(h + 1) * D_ROPE] = (qh[:, 128:128 + D_ROPE]
                                                      + qh[:, 128 + 2 * D_ROPE:128 + 3 * D_ROPE])
    else:
        nt_dims = (((1,), (1,)), ((), ()))
        for h in range(H_B):
            qh = lax.dot_general(wq_ref[h * QW:(h + 1) * QW, :], cqn, nt_dims, preferred_element_type=F32) * tq
            qt_ref[0, h * QW:(h + 1) * QW, :] = qh.astype(BF16)
        lat_b = lat.astype(BF16)
        kv_ref[...] = jnp.dot(lat_b, wkv_ref[...], preferred_element_type=F32).astype(BF16)
        kvt_ref[0] = lax.dot_general(wkvt_ref[...], lat_b, nt_dims, preferred_element_type=F32).astype(BF16)
        kp4_ref[...] = kp4.astype(BF16)


def _mla_prep(p2, q_norm, kv_norm, tabq, tabk, wq, w_extra, absorb, tm, seq):
    n = p2.shape[0]
    nt = seq // tm if not absorb else 1
    tabq_spec = (pl.BlockSpec((tm, QW), lambda i: (0, 0)) if absorb
                 else pl.BlockSpec((QW, tm), lambda i: (0, i % nt)))
    in_specs = [pl.BlockSpec((tm, Q_LORA), lambda i: (i, 2048 // 256)),
                pl.BlockSpec((tm, KV_LORA), lambda i: (i, 2304 // 256)),
                pl.BlockSpec((tm, 128), lambda i: (i, 2560 // 128)),
                _const_spec((1, Q_LORA)), _const_spec((1, KV_LORA)),
                tabq_spec, pl.BlockSpec((tm, 128), lambda i: (i % nt, 0)),
                _const_spec(wq.shape)] + [_const_spec(w.shape) for w in w_extra]
    row = lambda w: pl.BlockSpec((tm, w), lambda i: (i, 0))
    out_specs = [row(KV_LORA), row(D_ROPE)]
    out_shape = [jax.ShapeDtypeStruct((n, KV_LORA), F32), jax.ShapeDtypeStruct((n, D_ROPE), F32)]
    if absorb:
        out_specs += [row(H_B * KV_LORA), row(H_B * D_ROPE)]
        out_shape += [jax.ShapeDtypeStruct((n, H_B * KV_LORA), F32), jax.ShapeDtypeStruct((n, H_B * D_ROPE), F32)]
    else:
        b = n // seq
        col = lambda w: pl.BlockSpec((1, w, tm), lambda i: (i // nt, 0, i % nt))
        out_specs += [col(H_B * QW), row(H_B * 128), col(H_B * 128), row(128)]
        out_shape += [jax.ShapeDtypeStruct((b, H_B * QW, seq), BF16), jax.ShapeDtypeStruct((n, H_B * 128), BF16),
                      jax.ShapeDtypeStruct((b, H_B * 128, seq), BF16), jax.ShapeDtypeStruct((n, 128), BF16)]
    return pl.pallas_call(
        functools.partial(_mla_prep_kernel, absorb=absorb),
        grid=(n // tm,),
        in_specs=in_specs, out_specs=out_specs, out_shape=out_shape,
        compiler_params=_cparams(("parallel",)),
        name="mla_prep",
    )(p2, p2, p2, q_norm.reshape(1, Q_LORA), kv_norm.reshape(1, KV_LORA), tabq, tabk, wq, *w_extra)


def _flash_kernel(qt_ref, kv_ref, kvt_ref, kp_ref, o_ref, m_scr, l_scr, acc_scr, *, tq):
    qi = pl.program_id(1)
    key = lax.broadcasted_iota(jnp.int32, (tq, tq), 0)
    qry = lax.broadcasted_iota(jnp.int32, (tq, tq), 1)
    m_scr[...] = jnp.full_like(m_scr, NEG_INF)
    l_scr[...] = jnp.zeros_like(l_scr)
    acc_scr[...] = jnp.zeros_like(acc_scr)

    def head_steps(h, k0, kp, masked):
        keys = jnp.concatenate([kv_ref[0, pl.ds(k0, tq), h * 128:(h + 1) * 128], kp], axis=1)
        s = jnp.dot(keys, qt_ref[0, h * QW:(h + 1) * QW, :], preferred_element_type=F32)
        yield None
        if masked:
            s = jnp.where(key <= qry, s, NEG_INF)
        m = m_scr[h]
        m_new = jnp.maximum(m, jnp.max(s, axis=0, keepdims=True))
        alpha = jnp.exp(m - m_new)
        p = jnp.exp(s - m_new)
        l_scr[h] = alpha * l_scr[h] + jnp.sum(p, axis=0, keepdims=True)
        m_scr[h] = m_new
        yield None
        acc_scr[h] = alpha * acc_scr[h] + jnp.dot(kvt_ref[0, h * 128:(h + 1) * 128, pl.ds(k0, tq)],
                                                  p.astype(BF16), preferred_element_type=F32)
        yield True

    def key_tile(k0, masked):
        kp = kp_ref[0, pl.ds(k0, tq), :]
        _run_lockstep([head_steps(h, k0, kp, masked) for h in range(H_B)])

    def body(kt, carry):
        key_tile(pl.multiple_of(kt * tq, tq), False)
        return carry

    lax.fori_loop(0, qi, body, 0)
    key_tile(pl.multiple_of(qi * tq, tq), True)
    o_t = jnp.concatenate([(acc_scr[h] / l_scr[h])[D_NOPE:D_NOPE + DV_B] for h in range(H_B)], axis=0)
    o_ref[0] = o_t.T.astype(o_ref.dtype)


def _flash(qt3, kv3, kvt3, kp3, tq):
    b, t, _ = kv3.shape
    return pl.pallas_call(
        functools.partial(_flash_kernel, tq=tq),
        grid=(b, t // tq),
        in_specs=[pl.BlockSpec((1, H_B * QW, tq), lambda i, j: (i, 0, j)),
                  pl.BlockSpec((1, t, H_B * 128), lambda i, j: (i, 0, 0)),
                  pl.BlockSpec((1, H_B * 128, t), lambda i, j: (i, 0, 0)),
                  pl.BlockSpec((1, t, 128), lambda i, j: (i, 0, 0))],
        out_specs=pl.BlockSpec((1, tq, H_B * DV_B), lambda i, j: (i, j, 0)),
        out_shape=jax.ShapeDtypeStruct((b, t, H_B * DV_B), BF16),
        scratch_shapes=[pltpu.VMEM((H_B, 1, tq), F32), pltpu.VMEM((H_B, 1, tq), F32),
                        pltpu.VMEM((H_B, 128, tq), F32)],
        compiler_params=_cparams(("parallel", "arbitrary")),
        name="mla_flash",
    )(qt3, kv3, kvt3, kp3)


def _decode_kernel(pt_ref, qlat_ref, qpe_ref, latn_ref, krn_ref, wuv_ref, lat_hbm, kct_hbm, o_ref,
                   lat_buf, kct_buf, lat_sem, kct_sem, *, gp, gpe, be, n_slots, t_new):
    step = pl.program_id(0)
    n_steps = pl.num_programs(0)
    nq = H_B * t_new
    groups = [(e, g) for e in range(be) for g in range(gpe)]
    ahead = n_slots // 2

    def page_copies(bi, g, slot):
        copies = []
        for j in range(gp):
            page = pt_ref[bi, g * gp + j]
            copies.append(pltpu.make_async_copy(lat_hbm.at[page], lat_buf.at[slot, pl.ds(j * PAGE, PAGE), :],
                                                lat_sem.at[slot]))
            copies.append(pltpu.make_async_copy(kct_hbm.at[page], kct_buf.at[slot, j], kct_sem.at[slot]))
        return copies

    def start_group(st, idx):
        e, g = groups[idx % len(groups)]
        for cp in page_copies(st * be + e, g, idx % n_slots):
            cp.start()

    @pl.when(step == 0)
    def _():
        for idx in range(ahead):
            start_group(0, idx)

    def group_softmax(s):
        m_g = jnp.max(s, axis=-1, keepdims=True)
        p = jnp.exp(s - m_g)
        return p.astype(BF16), m_g, jnp.sum(p, axis=-1, keepdims=True)

    def merge(state, group):
        m_old, l_old, acc_old = state
        p, m_g, l_g, values = group
        pv = jnp.dot(p, values, preferred_element_type=F32)
        m_new = jnp.maximum(m_old, m_g)
        a_old = jnp.exp(m_old - m_new)
        a_g = jnp.exp(m_g - m_new)
        return m_new, a_old * l_old + a_g * l_g, a_old * acc_old + a_g * pv

    nt_dims = (((1,), (1,)), ((), ()))
    for idx, (e, g) in enumerate(groups):
        slot = idx % n_slots
        for cp in page_copies(step * be + e, g, slot):
            cp.wait()
        nxt = idx + ahead
        if nxt < len(groups):
            start_group(step, nxt)
        else:
            @pl.when(step + 1 < n_steps)
            def _():
                start_group(step + 1, nxt)
        if g == 0:
            ql = jnp.concatenate([qlat_ref[e, :, h * KV_LORA:(h + 1) * KV_LORA] for h in range(H_B)],
                                 axis=0).astype(BF16)
            qp = jnp.concatenate([qpe_ref[e, :, h * D_ROPE:(h + 1) * D_ROPE] for h in range(H_B)],
                                 axis=0).astype(BF16)
            state = (jnp.full((nq, 1), NEG_INF, F32), jnp.zeros((nq, 1), F32), jnp.zeros((nq, KV_LORA), F32))
            pending = None
        lat = lat_buf[slot].astype(BF16)
        kct = jnp.concatenate([kct_buf[slot, j] for j in range(gp)], axis=1).astype(BF16)
        s = lax.dot_general(ql, lat, nt_dims, preferred_element_type=F32) + jnp.dot(qp, kct, preferred_element_type=F32)
        if pending is not None:
            state = merge(state, group_softmax(pending[0]) + (pending[1],))
        pending = (s, lat)
        if g == gpe - 1:
            state = merge(state, group_softmax(s) + (lat,))
            latn = latn_ref[e].astype(BF16)
            krn = krn_ref[e].astype(BF16)
            s = (lax.dot_general(ql, latn, nt_dims, preferred_element_type=F32)
                 + lax.dot_general(qp, krn, nt_dims, preferred_element_type=F32))
            tok = lax.broadcasted_iota(jnp.int32, (nq, t_new), 0) % t_new
            key = lax.broadcasted_iota(jnp.int32, (nq, t_new), 1)
            _, l, acc = merge(state, group_softmax(jnp.where(key <= tok, s, NEG_INF)) + (latn,))
            o_lat = acc / l
            outs = [_dot(o_lat[h * t_new:(h + 1) * t_new], wuv_ref[h]) for h in range(H_B)]
            o_ref[e] = jnp.concatenate(outs, axis=1).astype(o_ref.dtype)


def _decode(page_table, qlat3, qpe3, latn3, krn3, wuv, cache_lat, cache_kpe_t, gp, be):
    b, t_new, _ = qlat3.shape
    gpe = page_table.shape[1] // gp
    n_slots = min(DECODE_SLOTS, be * gpe)
    assert n_slots >= 2 and (be * gpe) % n_slots == 0 and b % be == 0
    per_step = lambda w: pl.BlockSpec((be, t_new, w), lambda i, pt: (i, 0, 0))
    in_specs = [per_step(H_B * KV_LORA), per_step(H_B * D_ROPE), per_step(KV_LORA), per_step(D_ROPE),
                pl.BlockSpec(wuv.shape, lambda i, pt: (0, 0, 0)),
                pl.BlockSpec(memory_space=pl.ANY), pl.BlockSpec(memory_space=pl.ANY)]
    return pl.pallas_call(
        functools.partial(_decode_kernel, gp=gp, gpe=gpe, be=be, n_slots=n_slots, t_new=t_new),
        grid_spec=pltpu.PrefetchScalarGridSpec(
            num_scalar_prefetch=1,
            grid=(b // be,),
            in_specs=in_specs,
            out_specs=pl.BlockSpec((be, t_new, H_B * DV_B), lambda i, pt: (i, 0, 0)),
            scratch_shapes=[pltpu.VMEM((n_slots, gp * PAGE, KV_LORA), F32),
                            pltpu.VMEM((n_slots, gp, D_ROPE, PAGE), F32),
                            pltpu.SemaphoreType.DMA((n_slots,)), pltpu.SemaphoreType.DMA((n_slots,))]),
        out_shape=jax.ShapeDtypeStruct((b, t_new, H_B * DV_B), BF16),
        compiler_params=_cparams(("arbitrary",)),
        name="mla_decode",
    )(page_table, qlat3, qpe3, latn3, krn3, wuv, cache_lat, cache_kpe_t)


def _rot_cols(w):
    half = D_ROPE // 2
    return jnp.concatenate([-w[..., half:], w[..., :half]], axis=-1)


def _layout_w_in_ab(w):
    d = w.shape[0]
    o = QKV_A + H_A * DV_A
    qkv_z, b_w, a_w = w[:, :o], w[:, o:o + H_A], w[:, o + H_A:o + 2 * H_A]
    o += 2 * H_A
    cq, ckv, kpe = w[:, o:o + Q_LORA], w[:, o + Q_LORA:o + Q_LORA + KV_LORA], w[:, o + Q_LORA + KV_LORA:]
    kr = _rot_cols(kpe)
    pad = jnp.zeros((d, 128 - 2 * H_A), w.dtype)
    return jnp.concatenate([qkv_z, cq, ckv, kpe, kr, kpe, kr, b_w, a_w, pad], axis=1).astype(BF16)


def _layout_w_uq(w_uq):
    nope, pe = w_uq[..., :D_NOPE], w_uq[..., D_NOPE:]
    rot = _rot_cols(pe)
    z = jnp.zeros_like(nope)
    return jnp.concatenate([nope, z, pe, pe, rot, rot], axis=-1).reshape(w_uq.shape[0], H_B * QW).astype(BF16)


def _rope_tables(pos):
    half = D_ROPE // 2
    inv = ROPE_BASE ** (-jnp.arange(half, dtype=F32) / half)
    ang = pos.astype(F32)[:, None] * inv
    c = jnp.concatenate([jnp.cos(ang), jnp.cos(ang)], axis=1)
    s = jnp.concatenate([jnp.sin(ang), jnp.sin(ang)], axis=1)
    one = jnp.ones((pos.shape[0], 128), F32)
    tabq = jnp.concatenate([one, c, c, s, s], axis=1) * MLA_SCALE
    tabk = jnp.concatenate([c, s, c, s], axis=1)
    return tabq, tabk


def _row_tile(n):
    for tm in (512, 256, 128, 64, 32, 16, 8):
        if n % tm == 0:
            return tm
    raise ValueError(n)


def _trunk(x, pos, conv0, gdn0, hgrn0, paged, wts):
    b, t, d = x.shape
    n = b * t
    tm = min(ROW_TILE_PREP, _row_tile(n))
    tb = min(TIME_BLOCK, t)
    x2 = x.reshape(n, d)

    tmd = min(ROW_TILE_PROJ, _row_tile(n))
    p2 = _proj(x2, wts["mix_norm"][0], wts["w_in_ab"], tmd)
    p3 = p2.reshape(b, t, IN_AB_PAD)
    conv_new = p3[:, t - (CONV_W - 1):, :QKV_A]
    o_a, gdn_new = _gdn(p3, wts["conv_w"], wts["a_log"], wts["dt_bias"], wts["gdn_norm"], conv0, gdn0, tb,
                        math.gcd(b, GDN_BATCH_PROMPT if gdn0 is None else GDN_BATCH_STATE))
    tabq, tabk = _rope_tables(pos)
    if paged is None:
        tmp = math.gcd(tm, t)
        lat, kr, qt, kv, kvt, kp4 = _mla_prep(p2, wts["q_norm"], wts["kv_norm"], tabq.T, tabk, wts["w_uq"].T,
                                              [wts["w_kv"], wts["w_kv"].T], False, tmp, t)
        o_b = _flash(qt, kv.reshape(b, t, -1), kvt, kp4.reshape(b, t, -1), min(FLASH_TILE, t))
    else:
        cache_lat, cache_kpe, page_table = paged
        reps = tm // t
        lat, kr, qlat, qpe = _mla_prep(p2, wts["q_norm"], wts["kv_norm"], jnp.tile(tabq, (reps, 1)),
                                       jnp.tile(tabk, (reps, 1)), wts["w_uq"], [wts["w_ukt"]], True, tm, t)
        n_pages = page_table.shape[1]
        gp = math.gcd(n_pages // 2, DECODE_GROUP_PAGES)
        o_b = _decode(page_table, qlat.reshape(b, t, -1), qpe.reshape(b, t, -1), lat.reshape(b, t, -1),
                      kr.reshape(b, t, -1), wts["w_uv"], cache_lat, jnp.swapaxes(cache_kpe, 1, 2), gp,
                      math.gcd(b, DECODE_BATCH))
    tmm = min(ROW_TILE_MIX, _row_tile(n))
    x2, pc = _mix_mlp(x2, [o_a.reshape(n, -1), o_b.reshape(n, -1)], [wts["w_out_a"], wts["w_out_b"]],
                      wts["mlp_norm"][0], wts["w_up"][0], wts["w_down"][0], wts["final_norm"], False,
                      min(ROW_TILE_MIX_PROJ, _row_tile(n)), wts["mix_norm"][1], wts["w_in_c"])

    o_c, hgrn_new = _hgrn(pc.reshape(b, t, -1), wts["lb_logits"], 1, wts["g_norm_c"], hgrn0, min(HGRN_TIME_BLOCK, t),
                          H_C, 1)
    y2 = _mix_mlp(x2, [o_c.reshape(n, -1)], [wts["w_out_c"]],
                  wts["mlp_norm"][1], wts["w_up"][1], wts["w_down"][1], wts["final_norm"], True, tmm)
    return (y2.reshape(b, t, d), gdn_new[None], conv_new[None], lat.reshape(b, t, -1)[None],
            kr.reshape(b, t, -1)[None], hgrn_new[None])


def kernel(x_prompt, x_sample, state_gdn, state_gdn_conv, cache_mla_latent, cache_mla_krope, state_hgrn,
           page_table, mix_norm, mlp_norm, final_norm, w_up, w_down, w_in_ab, conv_w_ab, a_log_ab, dt_bias_ab,
           gdn_norm_ab, q_norm_ab, w_uq_ab, kv_norm_ab, w_uk_ab, w_uv_ab, w_out_ab, w_in_c, lb_logits_c,
           g_norm_c, w_out_c):
    assert mix_norm.shape[0] == 2 and w_in_ab.shape[0] == 1 and w_in_c.shape[0] == 1
    assert a_log_ab.shape == (1, H_A) and conv_w_ab.shape == (1, CONV_W, QKV_A)
    assert w_uq_ab.shape[1:] == (Q_LORA, H_B, D_NOPE + D_ROPE) and w_uk_ab.shape[1:] == (KV_LORA, H_B, D_NOPE)
    assert cache_mla_latent.shape[2] == PAGE and x_prompt.shape[1] >= CONV_W - 1 and x_sample.shape[1] >= CONV_W - 1
    w_uk, w_uv = w_uk_ab[0], w_uv_ab[0]
    wts = {
        "mix_norm": mix_norm, "mlp_norm": mlp_norm, "final_norm": final_norm,
        "w_up": [w_up[l].astype(BF16) for l in range(w_up.shape[0])],
        "w_down": [w_down[l].astype(BF16) for l in range(w_down.shape[0])],
        "w_in_ab": _layout_w_in_ab(w_in_ab[0]), "conv_w": conv_w_ab[0], "a_log": a_log_ab[0],
        "dt_bias": dt_bias_ab[0], "gdn_norm": gdn_norm_ab[0], "q_norm": q_norm_ab[0], "kv_norm": kv_norm_ab[0],
        "w_uq": _layout_w_uq(w_uq_ab[0]),
        "w_kv": jnp.concatenate([w_uk, w_uv], axis=-1).reshape(KV_LORA, H_B * 128).astype(BF16),
        "w_ukt": jnp.transpose(w_uk, (1, 2, 0)).astype(BF16),
        "w_uv": jnp.transpose(w_uv, (1, 0, 2)).astype(BF16),
        "w_out_a": w_out_ab[0, :H_A * DV_A].astype(BF16), "w_out_b": w_out_ab[0, H_A * DV_A:].astype(BF16),
        "w_in_c": w_in_c[0].astype(BF16), "lb_logits": lb_logits_c, "g_norm_c": g_norm_c[0],
        "w_out_c": w_out_c[0].astype(BF16),
    }
    s_len = x_prompt.shape[1]
    outs_p = _trunk(x_prompt, jnp.arange(s_len), None, None, None, None, wts)
    past_len = page_table.shape[1] * cache_mla_latent.shape[2]
    outs_s = _trunk(x_sample, past_len + jnp.arange(x_sample.shape[1]),
                    state_gdn_conv.reshape(state_gdn_conv.shape[1:]), state_gdn.reshape(state_gdn.shape[1:]),
                    state_hgrn.reshape(state_hgrn.shape[1:]),
                    (cache_mla_latent.reshape(cache_mla_latent.shape[1:]),
                     cache_mla_krope.reshape(cache_mla_krope.shape[1:]), page_table), wts)
    return (outs_p[0], outs_s[0]) + outs_p[1:] + outs_s[1:]
```
